```python
import math
import jax, jax.numpy as jnp
from jax import lax
import numpy as np


D_MODEL = 2048
BATCH = 8
SEQ = 2048
DEPTH = 1

CHUNK = 64
Q_BLOCK = 128
N_MEM = 256
HG_HEADS = 16
HG_KDIM = 128
HG_VDIM = D_MODEL // HG_HEADS
HG_K = HG_HEADS * HG_KDIM
HG_V = HG_HEADS * HG_VDIM
MLA_HEADS = 16
Q_LORA = 512
KV_LORA = 512
QK_NOPE = 128
QK_ROPE = 64
V_HEAD = 128
MLA_QK = QK_NOPE + QK_ROPE
MLA_V = MLA_HEADS * V_HEAD
ROPE_THETA = 10000.0
XA_HEADS = 4
XA_HEAD_DIM = 128
XA_WIDTH = XA_HEADS * XA_HEAD_DIM
D_FF = 5632
FFN_RESIDUAL_WEIGHT = 0.5
EPS = 1e-6
IN_SPLITS = (HG_K, HG_K, HG_V, HG_V, Q_LORA, KV_LORA, QK_ROPE, D_MODEL, D_MODEL)
IN_DIM = HG_K + HG_K + HG_V + HG_V + Q_LORA + KV_LORA + QK_ROPE + D_MODEL + D_MODEL

kernel_name = 'hybrid_hgrn2_mla_macaron_sandwich_memory_block'


def rms_norm(x, g):
    xf = x.astype(jnp.float32)
    y = xf * lax.rsqrt(jnp.mean(xf * xf, axis=-1, keepdims=True) + EPS)
    return (y * g.astype(jnp.float32)).astype(x.dtype)


def rotary(x, cos, sin):
    x1, x2 = jnp.split(x, 2, axis=-1)
    return jnp.concatenate([x1 * cos - x2 * sin, x2 * cos + x1 * sin], axis=-1)


def split_columns(u):
    parts, start = [], 0
    for width in IN_SPLITS:
        parts.append(u[..., start:start + width])
        start += width
    return parts


def swiglu_half_step(x, pre_g, w_gate, w_up, w_down, post_g):
    h = rms_norm(x, pre_g)
    y = (jax.nn.silu(h @ w_gate) * (h @ w_up)) @ w_down
    return x + FFN_RESIDUAL_WEIGHT * rms_norm(y, post_g)


def hgrn2_chunk_scan(q, k, log_f, v):
    B, S, H, K = q.shape
    V = v.shape[-1]
    n = S // CHUNK

    def to_chunks(t):
        return t.reshape(B, n, CHUNK, H, t.shape[-1]).transpose(1, 0, 3, 2, 4)

    causal = jnp.tril(jnp.ones((CHUNK, CHUNK), dtype=bool))

    def step(state, inp):
        qc, kc, gc, vc = inp
        b = jnp.cumsum(gc, axis=2)
        diff = b[:, :, :, None, :] - b[:, :, None, :, :]
        decay = jnp.exp(jnp.where(causal[None, None, :, :, None], diff, -jnp.inf))
        scores = jnp.einsum('bhtk,bhsk,bhtsk->bhts', qc, kc, decay)
        o = (jnp.einsum('bhts,bhsv->bhtv', scores, vc)
             + jnp.einsum('bhtk,bhkv->bhtv', qc * jnp.exp(b), state))
        b_last = b[:, :, -1:, :]
        new_state = (jnp.exp(b_last[:, :, 0, :])[..., None] * state
                     + jnp.einsum('bhsk,bhsv->bhkv', kc * jnp.exp(b_last - b), vc))
        return new_state, o

    state0 = jnp.zeros((B, H, K, V), jnp.float32)
    _, o = lax.scan(step, state0, (to_chunks(q), to_chunks(k), to_chunks(log_f), to_chunks(v)))
    return o.transpose(1, 0, 3, 2, 4).reshape(B, S, H, V)


def chunk_causal_attention(q, k, v, scale):
    B, S, H, Dqk = q.shape
    Dv = v.shape[-1]
    nblk = S // Q_BLOCK
    qb = q.reshape(B, nblk, Q_BLOCK, H, Dqk).transpose(1, 0, 2, 3, 4)
    key_chunk = jnp.arange(S) // CHUNK

    def one_block(args):
        blk, q_blk = args
        query_chunk = (blk * Q_BLOCK + jnp.arange(Q_BLOCK)) // CHUNK
        mask = key_chunk[None, :] <= query_chunk[:, None]
        s = jnp.einsum('bqhd,bkhd->bhqk', q_blk, k, preferred_element_type=jnp.float32) * scale
        s = jnp.where(mask[None, None], s, -jnp.inf)
        p = jax.nn.softmax(s, axis=-1).astype(v.dtype)
        return jnp.einsum('bhqk,bkhd->bqhd', p, v)

    out = lax.map(one_block, (jnp.arange(nblk), qb))
    return out.transpose(1, 0, 2, 3, 4).reshape(B, S, H, Dv)


def hybrid_mixer(x, cos, sin, lb, pre_g, w_in, hg_norm_g, q_norm_g, w_q_up, kv_norm_g,
                 w_kv_up, w_branch_a, w_branch_b, w_out, post_g):
    B, S, _ = x.shape
    f32 = jnp.float32
    h = rms_norm(x, pre_g)
    u = h @ w_in
    q_hg, f_hg, i_hg, og_hg, c_q, c_kv, k_pe, gate_a, gate_b = split_columns(u)

    f_raw = f_hg.astype(f32).reshape(B, S, HG_HEADS, HG_KDIM)
    lb_h = lb.reshape(HG_HEADS, HG_KDIM)
    log_f = jnp.logaddexp(jnp.log(lb_h), jnp.log1p(-lb_h) + jax.nn.log_sigmoid(f_raw))
    k_in = (1.0 - lb_h) * jax.nn.sigmoid(-f_raw)
    q_in = jax.nn.silu(q_hg.astype(f32)).reshape(B, S, HG_HEADS, HG_KDIM)
    v_in = i_hg.astype(f32).reshape(B, S, HG_HEADS, HG_VDIM)
    o_a = hgrn2_chunk_scan(q_in, k_in, log_f, v_in).astype(x.dtype)
    o_a = (rms_norm(o_a, hg_norm_g.reshape(HG_HEADS, HG_VDIM))
           * jax.nn.silu(og_hg).reshape(B, S, HG_HEADS, HG_VDIM))
    y_a = o_a.reshape(B, S, HG_V) @ w_branch_a

    q = (rms_norm(c_q, q_norm_g) @ w_q_up).reshape(B, S, MLA_HEADS, MLA_QK)
    q_nope, q_pe = q[..., :QK_NOPE], q[..., QK_NOPE:]
    q_pe = rotary(q_pe, cos[:, :, None, :], sin[:, :, None, :])
    kv = (rms_norm(c_kv, kv_norm_g) @ w_kv_up).reshape(B, S, MLA_HEADS, QK_NOPE + V_HEAD)
    k_nope, v = kv[..., :QK_NOPE], kv[..., QK_NOPE:]
    k_pe = rotary(k_pe, cos, sin)
    q_full = jnp.concatenate([q_nope, q_pe], axis=-1)
    k_full = jnp.concatenate(
        [k_nope, jnp.broadcast_to(k_pe[:, :, None, :], (B, S, MLA_HEADS, QK_ROPE))], axis=-1)
    o_b = chunk_causal_attention(q_full, k_full, v, MLA_QK ** -0.5)
    y_b = o_b.reshape(B, S, MLA_V) @ w_branch_b

    y = jax.nn.sigmoid(gate_a) * y_a + jax.nn.sigmoid(gate_b) * y_b
    return x + rms_norm(y @ w_out, post_g)


def memory_cross_attention(x, mem, pre_g, mem_g, w_q, w_k, w_v, w_o, post_g):
    B, S, _ = x.shape
    M = mem.shape[1]
    h = rms_norm(x, pre_g)
    m = rms_norm(mem, mem_g)
    q = (h @ w_q).reshape(B, S, XA_HEADS, XA_HEAD_DIM)
    k = (m @ w_k).reshape(B, M, XA_HEADS, XA_HEAD_DIM)
    v = (m @ w_v).reshape(B, M, XA_HEADS, XA_HEAD_DIM)
    s = jnp.einsum('bqhd,bkhd->bhqk', q, k, preferred_element_type=jnp.float32) * XA_HEAD_DIM ** -0.5
    p = jax.nn.softmax(s, axis=-1).astype(v.dtype)
    o = jnp.einsum('bhqk,bkhd->bqhd', p, v).reshape(B, S, XA_WIDTH)
    return x + rms_norm(o @ w_o, post_g)


def _fwd_setup_inputs(seed: int = 0) -> dict:
    key = jax.random.key(seed)
    ks = iter(jax.random.split(key, 40))
    L = DEPTH

    def w(shape, fan_in):
        return jax.random.normal(next(ks), shape, jnp.float32) * fan_in ** -0.5

    def gain(n):
        return 1.0 + 0.02 * jax.random.normal(next(ks), (L, n), jnp.float32)

    x = jax.random.normal(next(ks), (BATCH, SEQ, D_MODEL), jnp.float32)
    mem = jax.random.normal(next(ks), (BATCH, N_MEM, D_MODEL), jnp.float32)
    offset = jax.random.randint(next(ks), (BATCH, 1), 0, 64, dtype=jnp.int32) * CHUNK
    positions = (offset + jnp.arange(SEQ, dtype=jnp.int32)[None, :]).astype(jnp.int32)
    hgrn_lb_logits = 0.5 * jax.random.normal(next(ks), (L + 1, HG_K), jnp.float32)
    return {
        'x': x, 'mem': mem, 'positions': positions, 'hgrn_lb_logits': hgrn_lb_logits,
        'ffn1_pre_g': gain(D_MODEL),
        'ffn1_w_gate': w((L, D_MODEL, D_FF), D_MODEL),
        'ffn1_w_up': w((L, D_MODEL, D_FF), D_MODEL),
        'ffn1_w_down': w((L, D_FF, D_MODEL), D_FF),
        'ffn1_post_g': gain(D_MODEL),
        'mix_pre_g': gain(D_MODEL),
        'w_in': w((L, D_MODEL, IN_DIM), D_MODEL),
        'hg_norm_g': gain(HG_V),
        'mla_q_norm_g': gain(Q_LORA),
        'mla_w_q_up': w((L, Q_LORA, MLA_HEADS * MLA_QK), Q_LORA),
        'mla_kv_norm_g': gain(KV_LORA),
        'mla_w_kv_up': w((L, KV_LORA, MLA_HEADS * (QK_NOPE + V_HEAD)), KV_LORA),
        'w_branch_a': w((L, HG_V, D_MODEL), HG_V),
        'w_branch_b': w((L, MLA_V, D_MODEL), MLA_V),
        'w_out': w((L, D_MODEL, D_MODEL), D_MODEL),
        'mix_post_g': gain(D_MODEL),
        'xa_pre_g': gain(D_MODEL),
        'xa_mem_g': gain(D_MODEL),
        'xa_w_q': w((L, D_MODEL, XA_WIDTH), D_MODEL),
        'xa_w_k': w((L, D_MODEL, XA_WIDTH), D_MODEL),
        'xa_w_v': w((L, D_MODEL, XA_WIDTH), D_MODEL),
        'xa_w_o': w((L, XA_WIDTH, D_MODEL), XA_WIDTH),
        'xa_post_g': gain(D_MODEL),
        'ffn2_pre_g': gain(D_MODEL),
        'ffn2_w_gate': w((L, D_MODEL, D_FF), D_MODEL),
        'ffn2_w_up': w((L, D_MODEL, D_FF), D_MODEL),
        'ffn2_w_down': w((L, D_FF, D_MODEL), D_FF),
        'ffn2_post_g': gain(D_MODEL),
    }


def _fwd_reference(x, mem, positions, hgrn_lb_logits,
              ffn1_pre_g, ffn1_w_gate, ffn1_w_up, ffn1_w_down, ffn1_post_g,
              mix_pre_g, w_in, hg_norm_g, mla_q_norm_g, mla_w_q_up, mla_kv_norm_g, mla_w_kv_up,
              w_branch_a, w_branch_b, w_out, mix_post_g,
              xa_pre_g, xa_mem_g, xa_w_q, xa_w_k, xa_w_v, xa_w_o, xa_post_g,
              ffn2_pre_g, ffn2_w_gate, ffn2_w_up, ffn2_w_down, ffn2_post_g):
    f32 = jnp.float32
    inv_freq = 1.0 / (ROPE_THETA ** (jnp.arange(0, QK_ROPE, 2, dtype=f32) / QK_ROPE))
    ang = positions.astype(f32)[..., None] * inv_freq
    cos = jnp.cos(ang).astype(x.dtype)
    sin = jnp.sin(ang).astype(x.dtype)
    lower_bounds = jnp.cumsum(jax.nn.softmax(hgrn_lb_logits.astype(f32), axis=0), axis=0)

    for l in range(DEPTH):
        x = swiglu_half_step(x, ffn1_pre_g[l], ffn1_w_gate[l], ffn1_w_up[l], ffn1_w_down[l], ffn1_post_g[l])
        x = hybrid_mixer(x, cos, sin, lower_bounds[l], mix_pre_g[l], w_in[l], hg_norm_g[l],
                         mla_q_norm_g[l], mla_w_q_up[l], mla_kv_norm_g[l], mla_w_kv_up[l],
                         w_branch_a[l], w_branch_b[l], w_out[l], mix_post_g[l])
        x = memory_cross_attention(x, mem, xa_pre_g[l], xa_mem_g[l], xa_w_q[l], xa_w_k[l],
                                   xa_w_v[l], xa_w_o[l], xa_post_g[l])
        x = swiglu_half_step(x, ffn2_pre_g[l], ffn2_w_gate[l], ffn2_w_up[l], ffn2_w_down[l], ffn2_post_g[l])
    return x


import jax as _jax
import jax.numpy as _jnp

TWIN_FORMAT = 'train_step'
FWD_PARAMS = ['x', 'mem', 'positions', 'hgrn_lb_logits', 'ffn1_pre_g', 'ffn1_w_gate', 'ffn1_w_up', 'ffn1_w_down', 'ffn1_post_g', 'mix_pre_g', 'w_in', 'hg_norm_g', 'mla_q_norm_g', 'mla_w_q_up', 'mla_kv_norm_g', 'mla_w_kv_up', 'w_branch_a', 'w_branch_b', 'w_out', 'mix_post_g', 'xa_pre_g', 'xa_mem_g', 'xa_w_q', 'xa_w_k', 'xa_w_v', 'xa_w_o', 'xa_post_g', 'ffn2_pre_g', 'ffn2_w_gate', 'ffn2_w_up', 'ffn2_w_down', 'ffn2_post_g']
TWIN_WEIGHTS = ['hgrn_lb_logits', 'ffn1_pre_g', 'ffn1_w_gate', 'ffn1_w_up', 'ffn1_w_down', 'ffn1_post_g', 'mix_pre_g', 'w_in', 'hg_norm_g', 'mla_q_norm_g', 'mla_w_q_up', 'mla_kv_norm_g', 'mla_w_kv_up', 'w_branch_a', 'w_branch_b', 'w_out', 'mix_post_g', 'xa_pre_g', 'xa_mem_g', 'xa_w_q', 'xa_w_k', 'xa_w_v', 'xa_w_o', 'xa_post_g', 'ffn2_pre_g', 'ffn2_w_gate', 'ffn2_w_up', 'ffn2_w_down', 'ffn2_post_g']
TWIN_DIFF_INPUT = 'x'
TWIN_INPUTS = ['x', 'mem', 'positions', 'hgrn_lb_logits', 'ffn1_pre_g', 'ffn1_w_gate', 'ffn1_w_up', 'ffn1_w_down', 'ffn1_post_g', 'mix_pre_g', 'w_in', 'hg_norm_g', 'mla_q_norm_g', 'mla_w_q_up', 'mla_kv_norm_g', 'mla_w_kv_up', 'w_branch_a', 'w_branch_b', 'w_out', 'mix_post_g', 'xa_pre_g', 'xa_mem_g', 'xa_w_q', 'xa_w_k', 'xa_w_v', 'xa_w_o', 'xa_post_g', 'ffn2_pre_g', 'ffn2_w_gate', 'ffn2_w_up', 'ffn2_w_down', 'ffn2_post_g', 'loss_target', 'm_hgrn_lb_logits', 'm_ffn1_pre_g', 'm_ffn1_w_gate', 'm_ffn1_w_up', 'm_ffn1_w_down', 'm_ffn1_post_g', 'm_mix_pre_g', 'm_w_in', 'm_hg_norm_g', 'm_mla_q_norm_g', 'm_mla_w_q_up', 'm_mla_kv_norm_g', 'm_mla_w_kv_up', 'm_w_branch_a', 'm_w_branch_b', 'm_w_out', 'm_mix_post_g', 'm_xa_pre_g', 'm_xa_mem_g', 'm_xa_w_q', 'm_xa_w_k', 'm_xa_w_v', 'm_xa_w_o', 'm_xa_post_g', 'm_ffn2_pre_g', 'm_ffn2_w_gate', 'm_ffn2_w_up', 'm_ffn2_w_down', 'm_ffn2_post_g', 'v_hgrn_lb_logits', 'v_ffn1_pre_g', 'v_ffn1_w_gate', 'v_ffn1_w_up', 'v_ffn1_w_down', 'v_ffn1_post_g', 'v_mix_pre_g', 'v_w_in', 'v_hg_norm_g', 'v_mla_q_norm_g', 'v_mla_w_q_up', 'v_mla_kv_norm_g', 'v_mla_w_kv_up', 'v_w_branch_a', 'v_w_branch_b', 'v_w_out', 'v_mix_post_g', 'v_xa_pre_g', 'v_xa_mem_g', 'v_xa_w_q', 'v_xa_w_k', 'v_xa_w_v', 'v_xa_w_o', 'v_xa_post_g', 'v_ffn2_pre_g', 'v_ffn2_w_gate', 'v_ffn2_w_up', 'v_ffn2_w_down', 'v_ffn2_post_g']
TWIN_OUTPUTS = ['loss', 'grad_x', 'grad_hgrn_lb_logits', 'grad_ffn1_pre_g', 'grad_ffn1_w_gate', 'grad_ffn1_w_up', 'grad_ffn1_w_down', 'grad_ffn1_post_g', 'grad_mix_pre_g', 'grad_w_in', 'grad_hg_norm_g', 'grad_mla_q_norm_g', 'grad_mla_w_q_up', 'grad_mla_kv_norm_g', 'grad_mla_w_kv_up', 'grad_w_branch_a', 'grad_w_branch_b', 'grad_w_out', 'grad_mix_post_g', 'grad_xa_pre_g', 'grad_xa_mem_g', 'grad_xa_w_q', 'grad_xa_w_k', 'grad_xa_w_v', 'grad_xa_w_o', 'grad_xa_post_g', 'grad_ffn2_pre_g', 'grad_ffn2_w_gate', 'grad_ffn2_w_up', 'grad_ffn2_w_down', 'grad_ffn2_post_g', 'delta_hgrn_lb_logits', 'delta_ffn1_pre_g', 'delta_ffn1_w_gate', 'delta_ffn1_w_up', 'delta_ffn1_w_down', 'delta_ffn1_post_g', 'delta_mix_pre_g', 'delta_w_in', 'delta_hg_norm_g', 'delta_mla_q_norm_g', 'delta_mla_w_q_up', 'delta_mla_kv_norm_g', 'delta_mla_w_kv_up', 'delta_w_branch_a', 'delta_w_branch_b', 'delta_w_out', 'delta_mix_post_g', 'delta_xa_pre_g', 'delta_xa_mem_g', 'delta_xa_w_q', 'delta_xa_w_k', 'delta_xa_w_v', 'delta_xa_w_o', 'delta_xa_post_g', 'delta_ffn2_pre_g', 'delta_ffn2_w_gate', 'delta_ffn2_w_up', 'delta_ffn2_w_down', 'delta_ffn2_post_g', 'new_m_hgrn_lb_logits', 'new_m_ffn1_pre_g', 'new_m_ffn1_w_gate', 'new_m_ffn1_w_up', 'new_m_ffn1_w_down', 'new_m_ffn1_post_g', 'new_m_mix_pre_g', 'new_m_w_in', 'new_m_hg_norm_g', 'new_m_mla_q_norm_g', 'new_m_mla_w_q_up', 'new_m_mla_kv_norm_g', 'new_m_mla_w_kv_up', 'new_m_w_branch_a', 'new_m_w_branch_b', 'new_m_w_out', 'new_m_mix_post_g', 'new_m_xa_pre_g', 'new_m_xa_mem_g', 'new_m_xa_w_q', 'new_m_xa_w_k', 'new_m_xa_w_v', 'new_m_xa_w_o', 'new_m_xa_post_g', 'new_m_ffn2_pre_g', 'new_m_ffn2_w_gate', 'new_m_ffn2_w_up', 'new_m_ffn2_w_down', 'new_m_ffn2_post_g', 'new_v_hgrn_lb_logits', 'new_v_ffn1_pre_g', 'new_v_ffn1_w_gate', 'new_v_ffn1_w_up', 'new_v_ffn1_w_down', 'new_v_ffn1_post_g', 'new_v_mix_pre_g', 'new_v_w_in', 'new_v_hg_norm_g', 'new_v_mla_q_norm_g', 'new_v_mla_w_q_up', 'new_v_mla_kv_norm_g', 'new_v_mla_w_kv_up', 'new_v_w_branch_a', 'new_v_w_branch_b', 'new_v_w_out', 'new_v_mix_post_g', 'new_v_xa_pre_g', 'new_v_xa_mem_g', 'new_v_xa_w_q', 'new_v_xa_w_k', 'new_v_xa_w_v', 'new_v_xa_w_o', 'new_v_xa_post_g', 'new_v_ffn2_pre_g', 'new_v_ffn2_w_gate', 'new_v_ffn2_w_up', 'new_v_ffn2_w_down', 'new_v_ffn2_post_g']
TWIN_LEAF_KINDS = {'loss': 'loss', 'grad_x': 'grad_x', 'grad_hgrn_lb_logits': 'grad_w', 'grad_ffn1_pre_g': 'grad_w', 'grad_ffn1_w_gate': 'grad_w', 'grad_ffn1_w_up': 'grad_w', 'grad_ffn1_w_down': 'grad_w', 'grad_ffn1_post_g': 'grad_w', 'grad_mix_pre_g': 'grad_w', 'grad_w_in': 'grad_w', 'grad_hg_norm_g': 'grad_w', 'grad_mla_q_norm_g': 'grad_w', 'grad_mla_w_q_up': 'grad_w', 'grad_mla_kv_norm_g': 'grad_w', 'grad_mla_w_kv_up': 'grad_w', 'grad_w_branch_a': 'grad_w', 'grad_w_branch_b': 'grad_w', 'grad_w_out': 'grad_w', 'grad_mix_post_g': 'grad_w', 'grad_xa_pre_g': 'grad_w', 'grad_xa_mem_g': 'grad_w', 'grad_xa_w_q': 'grad_w', 'grad_xa_w_k': 'grad_w', 'grad_xa_w_v': 'grad_w', 'grad_xa_w_o': 'grad_w', 'grad_xa_post_g': 'grad_w', 'grad_ffn2_pre_g': 'grad_w', 'grad_ffn2_w_gate': 'grad_w', 'grad_ffn2_w_up': 'grad_w', 'grad_ffn2_w_down': 'grad_w', 'grad_ffn2_post_g': 'grad_w', 'delta_hgrn_lb_logits': 'delta_w', 'delta_ffn1_pre_g': 'delta_w', 'delta_ffn1_w_gate': 'delta_w', 'delta_ffn1_w_up': 'delta_w', 'delta_ffn1_w_down': 'delta_w', 'delta_ffn1_post_g': 'delta_w', 'delta_mix_pre_g': 'delta_w', 'delta_w_in': 'delta_w', 'delta_hg_norm_g': 'delta_w', 'delta_mla_q_norm_g': 'delta_w', 'delta_mla_w_q_up': 'delta_w', 'delta_mla_kv_norm_g': 'delta_w', 'delta_mla_w_kv_up': 'delta_w', 'delta_w_branch_a': 'delta_w', 'delta_w_branch_b': 'delta_w', 'delta_w_out': 'delta_w', 'delta_mix_post_g': 'delta_w', 'delta_xa_pre_g': 'delta_w', 'delta_xa_mem_g': 'delta_w', 'delta_xa_w_q': 'delta_w', 'delta_xa_w_k': 'delta_w', 'delta_xa_w_v': 'delta_w', 'delta_xa_w_o': 'delta_w', 'delta_xa_post_g': 'delta_w', 'delta_ffn2_pre_g': 'delta_w', 'delta_ffn2_w_gate': 'delta_w', 'delta_ffn2_w_up': 'delta_w', 'delta_ffn2_w_down': 'delta_w', 'delta_ffn2_post_g': 'delta_w', 'new_m_hgrn_lb_logits': 'new_m', 'new_m_ffn1_pre_g': 'new_m', 'new_m_ffn1_w_gate': 'new_m', 'new_m_ffn1_w_up': 'new_m', 'new_m_ffn1_w_down': 'new_m', 'new_m_ffn1_post_g': 'new_m', 'new_m_mix_pre_g': 'new_m', 'new_m_w_in': 'new_m', 'new_m_hg_norm_g': 'new_m', 'new_m_mla_q_norm_g': 'new_m', 'new_m_mla_w_q_up': 'new_m', 'new_m_mla_kv_norm_g': 'new_m', 'new_m_mla_w_kv_up': 'new_m', 'new_m_w_branch_a': 'new_m', 'new_m_w_branch_b': 'new_m', 'new_m_w_out': 'new_m', 'new_m_mix_post_g': 'new_m', 'new_m_xa_pre_g': 'new_m', 'new_m_xa_mem_g': 'new_m', 'new_m_xa_w_q': 'new_m', 'new_m_xa_w_k': 'new_m', 'new_m_xa_w_v': 'new_m', 'new_m_xa_w_o': 'new_m', 'new_m_xa_post_g': 'new_m', 'new_m_ffn2_pre_g': 'new_m', 'new_m_ffn2_w_gate': 'new_m', 'new_m_ffn2_w_up': 'new_m', 'new_m_ffn2_w_down': 'new_m', 'new_m_ffn2_post_g': 'new_m', 'new_v_hgrn_lb_logits': 'new_v', 'new_v_ffn1_pre_g': 'new_v', 'new_v_ffn1_w_gate': 'new_v', 'new_v_ffn1_w_up': 'new_v', 'new_v_ffn1_w_down': 'new_v', 'new_v_ffn1_post_g': 'new_v', 'new_v_mix_pre_g': 'new_v', 'new_v_w_in': 'new_v', 'new_v_hg_norm_g': 'new_v', 'new_v_mla_q_norm_g': 'new_v', 'new_v_mla_w_q_up': 'new_v', 'new_v_mla_kv_norm_g': 'new_v', 'new_v_mla_w_kv_up': 'new_v', 'new_v_w_branch_a': 'new_v', 'new_v_w_branch_b': 'new_v', 'new_v_w_out': 'new_v', 'new_v_mix_post_g': 'new_v', 'new_v_xa_pre_g': 'new_v', 'new_v_xa_mem_g': 'new_v', 'new_v_xa_w_q': 'new_v', 'new_v_xa_w_k': 'new_v', 'new_v_xa_w_v': 'new_v', 'new_v_xa_w_o': 'new_v', 'new_v_xa_post_g': 'new_v', 'new_v_ffn2_pre_g': 'new_v', 'new_v_ffn2_w_gate': 'new_v', 'new_v_ffn2_w_up': 'new_v', 'new_v_ffn2_w_down': 'new_v', 'new_v_ffn2_post_g': 'new_v'}


def _forward(args):
    return _fwd_reference(*[args[k] for k in FWD_PARAMS])


def _output_shape():
    out = _jax.eval_shape(lambda: _forward(_fwd_setup_inputs(0)))
    return out.shape, out.dtype

N_MICROBATCH = 1
ADAM_LR = 0.001
ADAM_B1 = 0.9
ADAM_B2 = 0.999
ADAM_EPS = 1e-08
ADAM_WD = 0.01
ADAM_STEP = 10
PER_EXAMPLE_BATCH_AXIS = {'x': 0, 'mem': 0, 'positions': 0, 'loss_target': 0}
SHARED_INPUTS = []
_WEIGHT_DTYPES = {'hgrn_lb_logits': _jnp.float32, 'ffn1_pre_g': _jnp.float32, 'ffn1_w_gate': _jnp.float32, 'ffn1_w_up': _jnp.float32, 'ffn1_w_down': _jnp.float32, 'ffn1_post_g': _jnp.float32, 'mix_pre_g': _jnp.float32, 'w_in': _jnp.float32, 'hg_norm_g': _jnp.float32, 'mla_q_norm_g': _jnp.float32, 'mla_w_q_up': _jnp.float32, 'mla_kv_norm_g': _jnp.float32, 'mla_w_kv_up': _jnp.float32, 'w_branch_a': _jnp.float32, 'w_branch_b': _jnp.float32, 'w_out': _jnp.float32, 'mix_post_g': _jnp.float32, 'xa_pre_g': _jnp.float32, 'xa_mem_g': _jnp.float32, 'xa_w_q': _jnp.float32, 'xa_w_k': _jnp.float32, 'xa_w_v': _jnp.float32, 'xa_w_o': _jnp.float32, 'xa_post_g': _jnp.float32, 'ffn2_pre_g': _jnp.float32, 'ffn2_w_gate': _jnp.float32, 'ffn2_w_up': _jnp.float32, 'ffn2_w_down': _jnp.float32, 'ffn2_post_g': _jnp.float32}
MOMENT_SCALE = {'hgrn_lb_logits': 1.268725e-02, 'ffn1_pre_g': 2.070923e-01, 'ffn1_w_gate': 8.700348e-02, 'ffn1_w_up': 8.554668e-02, 'ffn1_w_down': 1.422869e-01, 'ffn1_post_g': 1.956993e+00, 'mix_pre_g': 2.501304e-01, 'w_in': 9.920621e-02, 'hg_norm_g': 1.680495e-01, 'mla_q_norm_g': 7.433482e-02, 'mla_w_q_up': 2.958464e-02, 'mla_kv_norm_g': 2.101159e-01, 'mla_w_kv_up': 6.718815e-02, 'w_branch_a': 1.659966e-01, 'w_branch_b': 9.341754e-02, 'w_out': 1.952126e-01, 'mix_post_g': 7.987762e+00, 'xa_pre_g': 1.506289e-01, 'xa_mem_g': 3.291420e-01, 'xa_w_q': 2.963994e-01, 'xa_w_k': 3.005435e-01, 'xa_w_v': 6.001049e-01, 'xa_w_o': 2.995121e-01, 'xa_post_g': 8.255893e+00, 'ffn2_pre_g': 2.186742e-01, 'ffn2_w_gate': 7.260674e-02, 'ffn2_w_up': 1.084075e-01, 'ffn2_w_down': 1.837554e-01, 'ffn2_post_g': 1.999695e+00}


def _to_microbatches(a, axis):
    t = _jnp.moveaxis(a, axis, 0)
    t = t.reshape((N_MICROBATCH, t.shape[0] // N_MICROBATCH) + t.shape[1:])
    return _jnp.moveaxis(t, 1, axis + 1)


def setup_inputs(seed: int = 0) -> dict:
    inp = _fwd_setup_inputs(seed)
    key = _jax.random.fold_in(_jax.random.key(seed), 7919)
    shape, _ = _output_shape()
    out = dict(inp)
    out["loss_target"] = _jax.random.normal(_jax.random.fold_in(key, 0), shape, _jnp.float32)
    for i, name in enumerate(TWIN_WEIGHTS):
        w = inp[name].astype(_jnp.float32)
        if MOMENT_SCALE is None:
            s = _jnp.sqrt(_jnp.mean(_jnp.square(w)) + 1e-30)
        else:
            s = MOMENT_SCALE[name]
        km, kv = _jax.random.split(_jax.random.fold_in(key, i + 1))
        out[name] = w
        out["m_" + name] = s * _jax.random.normal(km, w.shape, _jnp.float32)
        out["v_" + name] = (s * s) * _jax.random.uniform(kv, w.shape, _jnp.float32, 0.5, 1.5)
    if N_MICROBATCH > 1:
        for name, axis in PER_EXAMPLE_BATCH_AXIS.items():
            out[name] = _to_microbatches(out[name], axis)
    return {'x': out['x'], 'mem': out['mem'], 'positions': out['positions'], 'hgrn_lb_logits': out['hgrn_lb_logits'], 'ffn1_pre_g': out['ffn1_pre_g'], 'ffn1_w_gate': out['ffn1_w_gate'], 'ffn1_w_up': out['ffn1_w_up'], 'ffn1_w_down': out['ffn1_w_down'], 'ffn1_post_g': out['ffn1_post_g'], 'mix_pre_g': out['mix_pre_g'], 'w_in': out['w_in'], 'hg_norm_g': out['hg_norm_g'], 'mla_q_norm_g': out['mla_q_norm_g'], 'mla_w_q_up': out['mla_w_q_up'], 'mla_kv_norm_g': out['mla_kv_norm_g'], 'mla_w_kv_up': out['mla_w_kv_up'], 'w_branch_a': out['w_branch_a'], 'w_branch_b': out['w_branch_b'], 'w_out': out['w_out'], 'mix_post_g': out['mix_post_g'], 'xa_pre_g': out['xa_pre_g'], 'xa_mem_g': out['xa_mem_g'], 'xa_w_q': out['xa_w_q'], 'xa_w_k': out['xa_w_k'], 'xa_w_v': out['xa_w_v'], 'xa_w_o': out['xa_w_o'], 'xa_post_g': out['xa_post_g'], 'ffn2_pre_g': out['ffn2_pre_g'], 'ffn2_w_gate': out['ffn2_w_gate'], 'ffn2_w_up': out['ffn2_w_up'], 'ffn2_w_down': out['ffn2_w_down'], 'ffn2_post_g': out['ffn2_post_g'], 'loss_target': out['loss_target'], 'm_hgrn_lb_logits': out['m_hgrn_lb_logits'], 'm_ffn1_pre_g': out['m_ffn1_pre_g'], 'm_ffn1_w_gate': out['m_ffn1_w_gate'], 'm_ffn1_w_up': out['m_ffn1_w_up'], 'm_ffn1_w_down': out['m_ffn1_w_down'], 'm_ffn1_post_g': out['m_ffn1_post_g'], 'm_mix_pre_g': out['m_mix_pre_g'], 'm_w_in': out['m_w_in'], 'm_hg_norm_g': out['m_hg_norm_g'], 'm_mla_q_norm_g': out['m_mla_q_norm_g'], 'm_mla_w_q_up': out['m_mla_w_q_up'], 'm_mla_kv_norm_g': out['m_mla_kv_norm_g'], 'm_mla_w_kv_up': out['m_mla_w_kv_up'], 'm_w_branch_a': out['m_w_branch_a'], 'm_w_branch_b': out['m_w_branch_b'], 'm_w_out': out['m_w_out'], 'm_mix_post_g': out['m_mix_post_g'], 'm_xa_pre_g': out['m_xa_pre_g'], 'm_xa_mem_g': out['m_xa_mem_g'], 'm_xa_w_q': out['m_xa_w_q'], 'm_xa_w_k': out['m_xa_w_k'], 'm_xa_w_v': out['m_xa_w_v'], 'm_xa_w_o': out['m_xa_w_o'], 'm_xa_post_g': out['m_xa_post_g'], 'm_ffn2_pre_g': out['m_ffn2_pre_g'], 'm_ffn2_w_gate': out['m_ffn2_w_gate'], 'm_ffn2_w_up': out['m_ffn2_w_up'], 'm_ffn2_w_down': out['m_ffn2_w_down'], 'm_ffn2_post_g': out['m_ffn2_post_g'], 'v_hgrn_lb_logits': out['v_hgrn_lb_logits'], 'v_ffn1_pre_g': out['v_ffn1_pre_g'], 'v_ffn1_w_gate': out['v_ffn1_w_gate'], 'v_ffn1_w_up': out['v_ffn1_w_up'], 'v_ffn1_w_down': out['v_ffn1_w_down'], 'v_ffn1_post_g': out['v_ffn1_post_g'], 'v_mix_pre_g': out['v_mix_pre_g'], 'v_w_in': out['v_w_in'], 'v_hg_norm_g': out['v_hg_norm_g'], 'v_mla_q_norm_g': out['v_mla_q_norm_g'], 'v_mla_w_q_up': out['v_mla_w_q_up'], 'v_mla_kv_norm_g': out['v_mla_kv_norm_g'], 'v_mla_w_kv_up': out['v_mla_w_kv_up'], 'v_w_branch_a': out['v_w_branch_a'], 'v_w_branch_b': out['v_w_branch_b'], 'v_w_out': out['v_w_out'], 'v_mix_post_g': out['v_mix_post_g'], 'v_xa_pre_g': out['v_xa_pre_g'], 'v_xa_mem_g': out['v_xa_mem_g'], 'v_xa_w_q': out['v_xa_w_q'], 'v_xa_w_k': out['v_xa_w_k'], 'v_xa_w_v': out['v_xa_w_v'], 'v_xa_w_o': out['v_xa_w_o'], 'v_xa_post_g': out['v_xa_post_g'], 'v_ffn2_pre_g': out['v_ffn2_pre_g'], 'v_ffn2_w_gate': out['v_ffn2_w_gate'], 'v_ffn2_w_up': out['v_ffn2_w_up'], 'v_ffn2_w_down': out['v_ffn2_w_down'], 'v_ffn2_post_g': out['v_ffn2_post_g']}


def _loss(weights, diff, rest, loss_target):
    with _jax.named_scope("forward"):
        args = {**rest, TWIN_DIFF_INPUT: diff, **{k: w.astype(_WEIGHT_DTYPES[k]) for k, w in weights.items()}}
        y = _forward(args)
    with _jax.named_scope("loss_head"):
        err = _jnp.square(y.astype(_jnp.float32) - loss_target)
        return 0.5 * _jnp.sum(_jnp.mean(err, axis=-1)) if err.ndim else 0.5 * err


def _adamw(w, g, m, v):
    m = ADAM_B1 * m + (1.0 - ADAM_B1) * g
    v = ADAM_B2 * v + (1.0 - ADAM_B2) * _jnp.square(g)
    m_hat = m / (1.0 - ADAM_B1 ** ADAM_STEP)
    v_hat = v / (1.0 - ADAM_B2 ** ADAM_STEP)
    delta = -ADAM_LR * (m_hat / (_jnp.sqrt(v_hat) + ADAM_EPS) + ADAM_WD * w)
    return delta, m, v


def reference(x, mem, positions, hgrn_lb_logits, ffn1_pre_g, ffn1_w_gate, ffn1_w_up, ffn1_w_down, ffn1_post_g, mix_pre_g, w_in, hg_norm_g, mla_q_norm_g, mla_w_q_up, mla_kv_norm_g, mla_w_kv_up, w_branch_a, w_branch_b, w_out, mix_post_g, xa_pre_g, xa_mem_g, xa_w_q, xa_w_k, xa_w_v, xa_w_o, xa_post_g, ffn2_pre_g, ffn2_w_gate, ffn2_w_up, ffn2_w_down, ffn2_post_g, loss_target, m_hgrn_lb_logits, m_ffn1_pre_g, m_ffn1_w_gate, m_ffn1_w_up, m_ffn1_w_down, m_ffn1_post_g, m_mix_pre_g, m_w_in, m_hg_norm_g, m_mla_q_norm_g, m_mla_w_q_up, m_mla_kv_norm_g, m_mla_w_kv_up, m_w_branch_a, m_w_branch_b, m_w_out, m_mix_post_g, m_xa_pre_g, m_xa_mem_g, m_xa_w_q, m_xa_w_k, m_xa_w_v, m_xa_w_o, m_xa_post_g, m_ffn2_pre_g, m_ffn2_w_gate, m_ffn2_w_up, m_ffn2_w_down, m_ffn2_post_g, v_hgrn_lb_logits, v_ffn1_pre_g, v_ffn1_w_gate, v_ffn1_w_up, v_ffn1_w_down, v_ffn1_post_g, v_mix_pre_g, v_w_in, v_hg_norm_g, v_mla_q_norm_g, v_mla_w_q_up, v_mla_kv_norm_g, v_mla_w_kv_up, v_w_branch_a, v_w_branch_b, v_w_out, v_mix_post_g, v_xa_pre_g, v_xa_mem_g, v_xa_w_q, v_xa_w_k, v_xa_w_v, v_xa_w_o, v_xa_post_g, v_ffn2_pre_g, v_ffn2_w_gate, v_ffn2_w_up, v_ffn2_w_down, v_ffn2_post_g):
    given = dict(x=x, mem=mem, positions=positions, hgrn_lb_logits=hgrn_lb_logits, ffn1_pre_g=ffn1_pre_g, ffn1_w_gate=ffn1_w_gate, ffn1_w_up=ffn1_w_up, ffn1_w_down=ffn1_w_down, ffn1_post_g=ffn1_post_g, mix_pre_g=mix_pre_g, w_in=w_in, hg_norm_g=hg_norm_g, mla_q_norm_g=mla_q_norm_g, mla_w_q_up=mla_w_q_up, mla_kv_norm_g=mla_kv_norm_g, mla_w_kv_up=mla_w_kv_up, w_branch_a=w_branch_a, w_branch_b=w_branch_b, w_out=w_out, mix_post_g=mix_post_g, xa_pre_g=xa_pre_g, xa_mem_g=xa_mem_g, xa_w_q=xa_w_q, xa_w_k=xa_w_k, xa_w_v=xa_w_v, xa_w_o=xa_w_o, xa_post_g=xa_post_g, ffn2_pre_g=ffn2_pre_g, ffn2_w_gate=ffn2_w_gate, ffn2_w_up=ffn2_w_up, ffn2_w_down=ffn2_w_down, ffn2_post_g=ffn2_post_g, loss_target=loss_target, m_hgrn_lb_logits=m_hgrn_lb_logits, m_ffn1_pre_g=m_ffn1_pre_g, m_ffn1_w_gate=m_ffn1_w_gate, m_ffn1_w_up=m_ffn1_w_up, m_ffn1_w_down=m_ffn1_w_down, m_ffn1_post_g=m_ffn1_post_g, m_mix_pre_g=m_mix_pre_g, m_w_in=m_w_in, m_hg_norm_g=m_hg_norm_g, m_mla_q_norm_g=m_mla_q_norm_g, m_mla_w_q_up=m_mla_w_q_up, m_mla_kv_norm_g=m_mla_kv_norm_g, m_mla_w_kv_up=m_mla_w_kv_up, m_w_branch_a=m_w_branch_a, m_w_branch_b=m_w_branch_b, m_w_out=m_w_out, m_mix_post_g=m_mix_post_g, m_xa_pre_g=m_xa_pre_g, m_xa_mem_g=m_xa_mem_g, m_xa_w_q=m_xa_w_q, m_xa_w_k=m_xa_w_k, m_xa_w_v=m_xa_w_v, m_xa_w_o=m_xa_w_o, m_xa_post_g=m_xa_post_g, m_ffn2_pre_g=m_ffn2_pre_g, m_ffn2_w_gate=m_ffn2_w_gate, m_ffn2_w_up=m_ffn2_w_up, m_ffn2_w_down=m_ffn2_w_down, m_ffn2_post_g=m_ffn2_post_g, v_hgrn_lb_logits=v_hgrn_lb_logits, v_ffn1_pre_g=v_ffn1_pre_g, v_ffn1_w_gate=v_ffn1_w_gate, v_ffn1_w_up=v_ffn1_w_up, v_ffn1_w_down=v_ffn1_w_down, v_ffn1_post_g=v_ffn1_post_g, v_mix_pre_g=v_mix_pre_g, v_w_in=v_w_in, v_hg_norm_g=v_hg_norm_g, v_mla_q_norm_g=v_mla_q_norm_g, v_mla_w_q_up=v_mla_w_q_up, v_mla_kv_norm_g=v_mla_kv_norm_g, v_mla_w_kv_up=v_mla_w_kv_up, v_w_branch_a=v_w_branch_a, v_w_branch_b=v_w_branch_b, v_w_out=v_w_out, v_mix_post_g=v_mix_post_g, v_xa_pre_g=v_xa_pre_g, v_xa_mem_g=v_xa_mem_g, v_xa_w_q=v_xa_w_q, v_xa_w_k=v_xa_w_k, v_xa_w_v=v_xa_w_v, v_xa_w_o=v_xa_w_o, v_xa_post_g=v_xa_post_g, v_ffn2_pre_g=v_ffn2_pre_g, v_ffn2_w_gate=v_ffn2_w_gate, v_ffn2_w_up=v_ffn2_w_up, v_ffn2_w_down=v_ffn2_w_down, v_ffn2_post_g=v_ffn2_post_g)
    weights = {n: given[n] for n in TWIN_WEIGHTS}
    shared = {n: given[n] for n in SHARED_INPUTS}
    per_example = {n: given[n] for n in ['x', 'mem', 'positions']}
    grad_fn = _jax.value_and_grad(_loss, argnums=(0, 1))

    def one_microbatch(ex, loss_target):
        ex = dict(ex)
        diff = ex.pop(TWIN_DIFF_INPUT)
        return grad_fn(weights, diff, {**shared, **ex}, loss_target)

    if N_MICROBATCH == 1:
        loss, (grad_w, grad_x) = one_microbatch(per_example, given["loss_target"])
    else:
        def body(carry, xs):
            loss_sum, grad_sum = carry
            l_k, (gw_k, gx_k) = one_microbatch(xs[0], xs[1])
            with _jax.named_scope("update"):
                return (loss_sum + l_k, _jax.tree.map(_jnp.add, grad_sum, gw_k)), gx_k

        init = (_jnp.zeros((), _jnp.float32), _jax.tree.map(_jnp.zeros_like, weights))
        (loss, grad_w), grad_x = _jax.lax.scan(body, init, (per_example, given["loss_target"]))
    with _jax.named_scope("update"):
        delta_w, new_m, new_v = {}, {}, {}
        for n in TWIN_WEIGHTS:
            delta_w[n], new_m[n], new_v[n] = _adamw(weights[n], grad_w[n], given["m_" + n], given["v_" + n])
    return (loss, grad_x, *[grad_w[n] for n in TWIN_WEIGHTS], *[delta_w[n] for n in TWIN_WEIGHTS],
            *[new_m[n] for n in TWIN_WEIGHTS], *[new_v[n] for n in TWIN_WEIGHTS])
```

```python
import functools

import jax
import jax.numpy as jnp
from jax import lax
from jax.experimental import pallas as pl
from jax.experimental.pallas import tpu as pltpu

F32 = jnp.float32
BF16 = jnp.bfloat16
EPS = 1e-6
HEAD = 128
ROPE = 64
CHUNK = 64
ROPE_THETA = 10000.0
FFN_RESIDUAL_WEIGHT = 0.5
ADAM_LR, ADAM_B1, ADAM_B2, ADAM_EPS, ADAM_WD, ADAM_STEP = 0.001, 0.9, 0.999, 1e-08, 0.01, 10
VMEM_LIMIT = 56 * 2**20
ROW_BUDGET = 20 * 2**20
NEG = -1e30
MESH = pl.DeviceIdType.MESH
PACK_W = 1024
N_CHIP = 4
N_DEV = 8


def _params(sem):
    return pltpu.CompilerParams(dimension_semantics=sem, vmem_limit_bytes=VMEM_LIMIT)


def _tile(n, cap, mult):
    if n <= cap:
        return n
    t = (cap // mult) * mult
    while t >= mult:
        if n % t == 0:
            return t
        t -= mult
    raise ValueError(f"no tile for {n} under {cap}")


def _sigmoid(x):
    return 1.0 / (1.0 + jnp.exp(-x))


def _dot(a, b, dims):
    return lax.dot_general(a, b, (dims, ((), ())), preferred_element_type=F32)


NN = ((1,), (0,))
NT = ((1,), (1,))
TN = ((0,), (0,))


def _mm(a, b, mode, out_dtype, name, add=None, tm_cap=1024, tn_cap=512, tk_cap=2816):
    if mode == "nn":
        (M, K), (K2, N) = a.shape, b.shape
    elif mode == "nt":
        (M, K), (N, K2) = a.shape, b.shape
    else:
        (K, M), (K2, N) = a.shape, b.shape
    assert K == K2, (a.shape, b.shape, mode)
    tm = _tile(M, tm_cap, 128)
    tn = _tile(N, tn_cap, 128)
    tk = _tile(K, tk_cap, 128)
    nk = K // tk
    a_spec = pl.BlockSpec((tk, tm), lambda i, j, k: (k, i)) if mode == "tn" else pl.BlockSpec((tm, tk), lambda i, j, k: (i, k))
    b_spec = pl.BlockSpec((tn, tk), lambda i, j, k: (j, k)) if mode == "nt" else pl.BlockSpec((tk, tn), lambda i, j, k: (k, j))
    o_spec = pl.BlockSpec((tm, tn), lambda i, j, k: (i, j))
    dims = {"nn": NN, "nt": NT, "tn": TN}[mode]
    has_add = add is not None

    def body(*refs):
        a_ref, b_ref = refs[0], refs[1]
        add_ref = refs[2] if has_add else None
        o_ref = refs[3] if has_add else refs[2]
        p = _dot(a_ref[...].astype(BF16), b_ref[...].astype(BF16), dims)

        def finish(val):
            if has_add:
                val = val + add_ref[...]
            o_ref[...] = val.astype(out_dtype)

        if nk == 1:
            finish(p)
        else:
            acc_ref = refs[-1]
            k = pl.program_id(2)

            @pl.when(k == 0)
            def _():
                acc_ref[...] = p

            @pl.when(k > 0)
            def _():
                acc_ref[...] += p

            @pl.when(k == nk - 1)
            def _():
                finish(acc_ref[...])

    in_specs = [a_spec, b_spec] + ([o_spec] if has_add else [])
    args = (a, b) + ((add,) if has_add else ())
    return pl.pallas_call(
        body,
        grid=(M // tm, N // tn, nk),
        in_specs=in_specs,
        out_specs=o_spec,
        out_shape=jax.ShapeDtypeStruct((M, N), out_dtype),
        scratch_shapes=[pltpu.VMEM((tm, tn), F32)] if nk > 1 else [],
        compiler_params=_params(("parallel", "parallel", "arbitrary")),
        name=name,
    )(*args)


def _rows(body, ins, outs, name):
    T = next(a.shape[0] for a, k in ins if k == "row")
    per_row = sum(a.shape[1] * a.dtype.itemsize for a, k in ins if k == "row")
    per_row += sum(s[1] * jnp.dtype(d).itemsize for s, d, k in outs if k == "row")
    tr = next(t for t in (512, 256, 128, 64, 32, 16, 8) if T % t == 0 and 2 * t * per_row <= ROW_BUDGET)
    in_specs = [
        pl.BlockSpec((tr, a.shape[1]), lambda i: (i, 0)) if k == "row" else pl.BlockSpec(a.shape, lambda i: (0, 0))
        for a, k in ins
    ]
    out_specs = [
        pl.BlockSpec((tr, s[1]), lambda i: (i, 0)) if k == "row" else pl.BlockSpec(s, lambda i: (0, 0))
        for s, d, k in outs
    ]
    has_acc = any(k == "acc" for _, _, k in outs)
    res = pl.pallas_call(
        body,
        grid=(T // tr,),
        in_specs=in_specs,
        out_specs=out_specs,
        out_shape=[jax.ShapeDtypeStruct(s, d) for s, d, k in outs],
        compiler_params=_params(("arbitrary",) if has_acc else ("parallel",)),
        name=name,
    )(*[a for a, _ in ins])
    return res


def _rstd(x):
    return lax.rsqrt(jnp.mean(x * x, axis=-1, keepdims=True) + EPS)


def _norm_fwd(x, g, name):
    def body(x_ref, g_ref, o_ref):
        x = x_ref[...]
        o_ref[...] = (x * _rstd(x) * g_ref[...]).astype(BF16)

    return _rows(body, [(x, "row"), (g, "vec")], [(x.shape, BF16, "row")], name)[0]


def _postnorm_fwd(x, y, g, weight, name):
    def body(x_ref, y_ref, g_ref, o_ref):
        y = y_ref[...]
        o_ref[...] = x_ref[...] + weight * (y * _rstd(y) * g_ref[...])

    return _rows(body, [(x, "row"), (y, "row"), (g, "vec")], [(x.shape, F32, "row")], name)[0]


def _norm_bwd(x, g, dy, weight, out_dtype, name, res=None):
    has_res = res is not None

    def body(*refs):
        x_ref, g_ref, dy_ref = refs[:3]
        res_ref = refs[3] if has_res else None
        dx_ref, dg_ref = refs[-2], refs[-1]

        @pl.when(pl.program_id(0) == 0)
        def _():
            dg_ref[...] = jnp.zeros_like(dg_ref)

        x = x_ref[...]
        dn = dy_ref[...].astype(F32) * weight
        r = _rstd(x)
        xhat = x * r
        dg_ref[...] += jnp.sum(dn * xhat, axis=0, keepdims=True)
        dxh = dn * g_ref[...]
        dx = r * (dxh - xhat * jnp.mean(dxh * xhat, axis=-1, keepdims=True))
        if has_res:
            dx = dx + res_ref[...]
        dx_ref[...] = dx.astype(out_dtype)

    ins = [(x, "row"), (g, "vec"), (dy, "row")] + ([(res, "row")] if has_res else [])
    return _rows(body, ins, [(x.shape, out_dtype, "row"), (g.shape, F32, "acc")], name)


def _swiglu_fwd(a, b, name):
    def body(a_ref, b_ref, o_ref):
        a = a_ref[...]
        o_ref[...] = (a * _sigmoid(a) * b_ref[...]).astype(BF16)

    return _rows(body, [(a, "row"), (b, "row")], [(a.shape, BF16, "row")], name)[0]


def _swiglu_bwd(a, b, ds, name):
    def body(a_ref, b_ref, ds_ref, da_ref, db_ref):
        a, ds = a_ref[...], ds_ref[...]
        sg = _sigmoid(a)
        da_ref[...] = (ds * b_ref[...] * (sg * (1.0 + a * (1.0 - sg)))).astype(BF16)
        db_ref[...] = (ds * (a * sg)).astype(BF16)

    return _rows(body, [(a, "row"), (b, "row"), (ds, "row")], [(a.shape, BF16, "row"), (a.shape, BF16, "row")], name)


def _merge_fwd(ga, gb, ya, yb, name):
    def body(ga_ref, gb_ref, ya_ref, yb_ref, o_ref):
        o_ref[...] = (_sigmoid(ga_ref[...]) * ya_ref[...] + _sigmoid(gb_ref[...]) * yb_ref[...]).astype(BF16)

    return _rows(body, [(ga, "row"), (gb, "row"), (ya, "row"), (yb, "row")], [(ga.shape, BF16, "row")], name)[0]


def _merge_bwd(ga, gb, ya, yb, dy, name):
    def body(ga_ref, gb_ref, ya_ref, yb_ref, dy_ref, dya_ref, dyb_ref, dga_ref, dgb_ref):
        dy = dy_ref[...]
        sa, sb = _sigmoid(ga_ref[...]), _sigmoid(gb_ref[...])
        dya_ref[...] = (dy * sa).astype(BF16)
        dyb_ref[...] = (dy * sb).astype(BF16)
        dga_ref[...] = (dy * ya_ref[...] * (sa * (1.0 - sa))).astype(BF16)
        dgb_ref[...] = (dy * yb_ref[...] * (sb * (1.0 - sb))).astype(BF16)

    ins = [(ga, "row"), (gb, "row"), (ya, "row"), (yb, "row"), (dy, "row")]
    return _rows(body, ins, [(ga.shape, BF16, "row")] * 4, name)


def _loss_head(y, target, name):
    D = y.shape[1]

    def body(y_ref, t_ref, dy_ref, acc_ref):
        @pl.when(pl.program_id(0) == 0)
        def _():
            acc_ref[...] = jnp.zeros_like(acc_ref)

        err = y_ref[...] - t_ref[...]
        dy_ref[...] = err * (1.0 / D)
        acc_ref[...] += jnp.sum(err * err, axis=0, keepdims=True)

    return _rows(body, [(y, "row"), (target, "row")], [(y.shape, F32, "row"), ((1, D), F32, "acc")], name)


def _rot(x):
    lane = lax.broadcasted_iota(jnp.int32, x.shape, 1)
    return jnp.where((lane % ROPE) < ROPE // 2, -pltpu.roll(x, 128 - ROPE // 2, 1), pltpu.roll(x, ROPE // 2, 1))


def _rope_q_fwd(qpe, cos, sin, name):
    T, W = qpe.shape
    tr = min(T, 512)
    blk = pl.BlockSpec((tr, 128), lambda i, j: (i, j))
    tab = pl.BlockSpec((tr, 128), lambda i, j: (i, 0))

    def body(x_ref, c_ref, s_ref, o_ref):
        x = x_ref[...]
        o_ref[...] = (x * c_ref[...] + _rot(x) * s_ref[...]).astype(BF16)

    return pl.pallas_call(
        body, grid=(T // tr, W // 128), in_specs=[blk, tab, tab], out_specs=blk,
        out_shape=jax.ShapeDtypeStruct((T, W), BF16), compiler_params=_params(("parallel", "parallel")), name=name,
    )(qpe, cos, sin)


def _rope_q_bwd(dq_heads, cos, sin, name):
    T, W = dq_heads.shape
    tr = min(T, 512)
    even = pl.BlockSpec((tr, 128), lambda i, j: (i, 2 * j))
    odd = pl.BlockSpec((tr, 128), lambda i, j: (i, 2 * j + 1))
    tab = pl.BlockSpec((tr, 128), lambda i, j: (i, 0))

    def body(a_ref, b_ref, c_ref, s_ref, o_ref):
        d = a_ref[...] + b_ref[...]
        o_ref[...] = (d * c_ref[...] - _rot(d * s_ref[...])).astype(BF16)

    return pl.pallas_call(
        body, grid=(T // tr, W // 256), in_specs=[even, odd, tab, tab],
        out_specs=pl.BlockSpec((tr, 128), lambda i, j: (i, j)),
        out_shape=jax.ShapeDtypeStruct((T, W // 2), BF16), compiler_params=_params(("parallel", "parallel")), name=name,
    )(dq_heads, dq_heads, cos, sin)


def _rope_k_fwd(kpe, cos, sin, name):
    def body(x_ref, c_ref, s_ref, o_ref):
        x = x_ref[...]
        y = x * c_ref[...] + _rot(x) * s_ref[...]
        o_ref[...] = (y + pltpu.roll(y, ROPE, 1)).astype(BF16)

    return _rows(body, [(kpe, "row"), (cos, "row"), (sin, "row")], [(kpe.shape, BF16, "row")], name)[0]


def _rope_k_bwd(dk_heads, cos, sin, name):
    T, W = dk_heads.shape

    def body(d_ref, c_ref, s_ref, o_ref):
        d = d_ref[:, 0:128]
        for h in range(1, W // 128):
            d = d + d_ref[:, h * 128:(h + 1) * 128]
        d = d + pltpu.roll(d, ROPE, 1)
        dx = d * c_ref[...] - _rot(d * s_ref[...])
        lane = lax.broadcasted_iota(jnp.int32, dx.shape, 1)
        o_ref[...] = jnp.where(lane < ROPE, dx, 0.0).astype(BF16)

    return _rows(body, [(dk_heads, "row"), (cos, "row"), (sin, "row")], [((T, 128), BF16, "row")], name)[0]


def _attn_probs(q, k, qpe, kpe, scale, causal, q0):
    s = _dot(q, k, NT)
    if qpe is not None:
        s = s + _dot(qpe, kpe, NT)
    s = s * scale
    if causal:
        row = q0 + lax.broadcasted_iota(jnp.int32, s.shape, 0)
        col = lax.broadcasted_iota(jnp.int32, s.shape, 1)
        s = jnp.where((col // CHUNK) <= (row // CHUNK), s, NEG)
    p = jnp.exp(s - jnp.max(s, axis=-1, keepdims=True))
    return p / jnp.sum(p, axis=-1, keepdims=True)


def _pe_mask(x, h):
    lane = lax.broadcasted_iota(jnp.int32, x.shape, 1)
    return jnp.where((lane // ROPE) == (h % 2), x, jnp.zeros_like(x))


def _attn_fwd(q, k, v, scale, name, qpe=None, kpe=None, causal=False):
    T, W = q.shape
    Tk = k.shape[0]
    H = W // HEAD
    tq = min(T, 256)
    has_pe = qpe is not None
    qs = pl.BlockSpec((tq, HEAD), lambda h, i: (i, h))
    ks = pl.BlockSpec((Tk, HEAD), lambda h, i: (0, h))
    in_specs, args = [qs, ks, ks], [q, k, v]
    if has_pe:
        in_specs += [pl.BlockSpec((tq, HEAD), lambda h, i: (i, h // 2)), pl.BlockSpec((Tk, HEAD), lambda h, i: (0, 0))]
        args += [qpe, kpe]

    def body(*refs):
        q_ref, k_ref, v_ref = refs[:3]
        o_ref = refs[-1]
        h, i = pl.program_id(0), pl.program_id(1)
        qp = _pe_mask(refs[3][...], h) if has_pe else None
        kp = refs[4][...] if has_pe else None
        p = _attn_probs(q_ref[...], k_ref[...], qp, kp, scale, causal, i * tq)
        o_ref[...] = _dot(p.astype(BF16), v_ref[...], NN).astype(BF16)

    return pl.pallas_call(
        body, grid=(H, T // tq), in_specs=in_specs, out_specs=qs, out_shape=jax.ShapeDtypeStruct((T, W), BF16),
        compiler_params=_params(("parallel", "parallel")), name=name,
    )(*args)


def _attn_bwd(q, k, v, do, scale, name, qpe=None, kpe=None, causal=False):
    T, W = q.shape
    Tk = k.shape[0]
    H = W // HEAD
    tq = min(T, 256)
    nq = T // tq
    has_pe = qpe is not None
    qs = pl.BlockSpec((tq, HEAD), lambda h, i: (i, h))
    ks = pl.BlockSpec((Tk, HEAD), lambda h, i: (0, h))
    in_specs, args = [qs, ks, ks, qs], [q, k, v, do]
    out_specs = [qs, ks, ks]
    out_shape = [jax.ShapeDtypeStruct((T, W), BF16), jax.ShapeDtypeStruct((Tk, W), BF16), jax.ShapeDtypeStruct((Tk, W), BF16)]
    scratch = [pltpu.VMEM((Tk, HEAD), F32), pltpu.VMEM((Tk, HEAD), F32)]
    if has_pe:
        in_specs += [pl.BlockSpec((tq, HEAD), lambda h, i: (i, h // 2)), pl.BlockSpec((Tk, HEAD), lambda h, i: (0, 0))]
        args += [qpe, kpe]
        out_specs += [qs, ks]
        out_shape += [jax.ShapeDtypeStruct((T, W), F32), jax.ShapeDtypeStruct((Tk, W), F32)]
        scratch += [pltpu.VMEM((Tk, HEAD), F32)]
    n_in = len(in_specs)

    def body(*refs):
        q_ref, k_ref, v_ref, do_ref = refs[:4]
        outs = refs[n_in:n_in + len(out_specs)]
        accs = refs[n_in + len(out_specs):]
        dq_ref, dk_ref, dv_ref = outs[:3]
        h, i = pl.program_id(0), pl.program_id(1)
        qp = _pe_mask(refs[4][...], h) if has_pe else None
        kp = refs[5][...] if has_pe else None
        qv, kv, vv, dov = q_ref[...], k_ref[...], v_ref[...], do_ref[...]
        p = _attn_probs(qv, kv, qp, kp, scale, causal, i * tq)
        dp = _dot(dov, vv, NT)
        ds = (p * (dp - jnp.sum(p * dp, axis=-1, keepdims=True)) * scale).astype(BF16)
        dq_ref[...] = _dot(ds, kv, NN).astype(BF16)
        dk_part = _dot(ds, qv, TN)
        dv_part = _dot(p.astype(BF16), dov, TN)
        if has_pe:
            outs[3][...] = _pe_mask(_dot(ds, kp, NN), h)
            dkp_part = _dot(ds, qp, TN)

        @pl.when(i == 0)
        def _():
            accs[0][...] = dk_part
            accs[1][...] = dv_part
            if has_pe:
                accs[2][...] = dkp_part

        @pl.when(i > 0)
        def _():
            accs[0][...] += dk_part
            accs[1][...] += dv_part
            if has_pe:
                accs[2][...] += dkp_part

        @pl.when(i == nq - 1)
        def _():
            dk_ref[...] = accs[0][...].astype(BF16)
            dv_ref[...] = accs[1][...].astype(BF16)
            if has_pe:
                outs[4][...] = accs[2][...]

    return pl.pallas_call(
        body, grid=(H, nq), in_specs=in_specs, out_specs=out_specs, out_shape=out_shape, scratch_shapes=scratch,
        compiler_params=_params(("parallel", "arbitrary")), name=name,
    )(*args)


def _split3(x):
    hi = x.astype(BF16)
    r1 = x - hi.astype(F32)
    mid = r1.astype(BF16)
    lo = (r1 - mid.astype(F32)).astype(BF16)
    return hi, mid, lo


def _tri_dot(tri, x):
    hi, mid, lo = _split3(x)
    return _dot(tri, hi, NN) + _dot(tri, mid, NN) + _dot(tri, lo, NN)


def _hg_gates(u, lb):
    q, fr, v = u[:, 0:HEAD], u[:, HEAD:2 * HEAD], u[:, 2 * HEAD:3 * HEAD]
    sg = 1.0 / (1.0 + jnp.exp(-fr))
    sgm = 1.0 / (1.0 + jnp.exp(fr))
    f = lb + (1.0 - lb) * sg
    kin = (1.0 - lb) * sgm
    sq = _sigmoid(q)
    return q, v, sg, sgm, f, kin, sq, q * sq


def _hg_cumsum(logf):
    t = lax.broadcasted_iota(jnp.int32, (CHUNK, CHUNK), 0)
    s = lax.broadcasted_iota(jnp.int32, (CHUNK, CHUNK), 1)
    return _tri_dot((s <= t).astype(BF16), logf)


def _hg_scan_fwd(u, lb, name):
    T, W = u.shape
    H = W // (4 * HEAD)
    NC = T // CHUNK

    def body(u_ref, lb_ref, o_ref, st_ref, state, b_sc, k_sc, v_sc):
        c = pl.program_id(1)

        @pl.when(c == 0)
        def _():
            state[...] = jnp.zeros_like(state)

        q, v, sg, sgm, f, kin, sq, qin = _hg_gates(u_ref[...], lb_ref[...])
        b = _hg_cumsum(jnp.log(f))
        b_sc[...] = b
        k_sc[...] = kin
        v_sc[...] = v
        rows = lax.broadcasted_iota(jnp.int32, (CHUNK, HEAD), 0)

        def step(s, o):
            bs = b_sc[pl.ds(s, 1), :]
            ks = k_sc[pl.ds(s, 1), :]
            vs = v_sc[pl.ds(s, 1), :]
            e = jnp.exp(jnp.where(rows >= s, b - bs, NEG))
            col = jnp.sum(qin * e * ks, axis=-1, keepdims=True)
            return o + col * vs

        o = lax.fori_loop(0, CHUNK, step, jnp.zeros((CHUNK, HEAD), F32))
        st = state[...]
        st_ref[...] = st
        o_ref[...] = o + _dot((qin * jnp.exp(b)).astype(BF16), st.astype(BF16), NT)
        bl = b[CHUNK - 1:CHUNK, :]
        kd = kin * jnp.exp(bl - b)
        state[...] = jnp.exp(bl) * st + _dot(v.astype(BF16), kd.astype(BF16), TN)

    return pl.pallas_call(
        body, grid=(H, NC),
        in_specs=[pl.BlockSpec((CHUNK, 4 * HEAD), lambda h, c: (c, h)), pl.BlockSpec((1, HEAD), lambda h, c: (0, h))],
        out_specs=[pl.BlockSpec((CHUNK, HEAD), lambda h, c: (c, h)), pl.BlockSpec((None, None, HEAD, HEAD), lambda h, c: (h, c, 0, 0))],
        out_shape=[jax.ShapeDtypeStruct((T, H * HEAD), F32), jax.ShapeDtypeStruct((H, NC, HEAD, HEAD), F32)],
        scratch_shapes=[pltpu.VMEM((HEAD, HEAD), F32)] + [pltpu.VMEM((CHUNK, HEAD), F32)] * 3,
        compiler_params=_params(("parallel", "arbitrary")), name=name,
    )(u, lb)


def _hg_scan_bwd(u, lb, do, dog, states, name):
    T, W = u.shape
    H = W // (4 * HEAD)
    NC = T // CHUNK

    def body(u_ref, lb_ref, do_ref, dog_ref, st_ref, du_ref, dlb_ref, dstate, b_sc, k_sc, v_sc, dk_sc, dbn_sc, dv_sc):
        c = pl.program_id(1)

        @pl.when(c == 0)
        def _():
            dstate[...] = jnp.zeros_like(dstate)
            dlb_ref[...] = jnp.zeros_like(dlb_ref)

        lb = lb_ref[...]
        q, v, sg, sgm, f, kin, sq, qin = _hg_gates(u_ref[...], lb)
        b = _hg_cumsum(jnp.log(f))
        b_sc[...] = b
        k_sc[...] = kin
        v_sc[...] = v
        do = do_ref[...]
        dob, vb = do.astype(BF16), v.astype(BF16)
        st = st_ref[...]
        dsn = dstate[...]
        bl = b[CHUNK - 1:CHUNK, :]
        ebl, eb, ekd = jnp.exp(bl), jnp.exp(b), jnp.exp(bl - b)
        qe, kd = qin * eb, kin * ekd
        dqe = _dot(dob, st.astype(BF16), NN)
        dsnb = dsn.astype(BF16)
        dv = _dot(kd.astype(BF16), dsnb, NT)
        dkd = _dot(vb, dsnb, NN)
        dbl = jnp.sum(dsn * st, axis=0, keepdims=True) * ebl
        dstate[...] = ebl * dsn + _dot(dob, qe.astype(BF16), TN)
        t1 = dkd * kd
        dbl = dbl + jnp.sum(t1, axis=0, keepdims=True)
        dkin = dkd * ekd
        dqin = dqe * eb
        db = dqe * qe - t1
        rows = lax.broadcasted_iota(jnp.int32, (CHUNK, HEAD), 0)

        def step(s, carry):
            dq_acc, db_acc = carry
            bs = b_sc[pl.ds(s, 1), :]
            ks = k_sc[pl.ds(s, 1), :]
            vs = v_sc[pl.ds(s, 1), :]
            e = jnp.exp(jnp.where(rows >= s, b - bs, NEG))
            qe_s = qin * e
            col = jnp.sum(qe_s * ks, axis=-1, keepdims=True)
            dcol = jnp.sum(do * vs, axis=-1, keepdims=True)
            dqe_s = dcol * qe_s
            g = dqe_s * ks
            dk_sc[pl.ds(s, 1), :] = jnp.sum(dqe_s, axis=0, keepdims=True)
            dbn_sc[pl.ds(s, 1), :] = jnp.sum(g, axis=0, keepdims=True)
            dv_sc[pl.ds(s, 1), :] = jnp.sum(col * do, axis=0, keepdims=True)
            return dq_acc + dcol * (e * ks), db_acc + g

        zero = jnp.zeros((CHUNK, HEAD), F32)
        dq_acc, db_acc = lax.fori_loop(0, CHUNK, step, (zero, zero))
        dqin = dqin + dq_acc
        dkin = dkin + dk_sc[...]
        dv = dv + dv_sc[...]
        db = db + db_acc - dbn_sc[...] + jnp.where(rows == CHUNK - 1, dbl, 0.0)
        t = lax.broadcasted_iota(jnp.int32, (CHUNK, CHUNK), 0)
        s_ = lax.broadcasted_iota(jnp.int32, (CHUNK, CHUNK), 1)
        dlogf = _tri_dot((s_ >= t).astype(BF16), db)
        diff = dlogf / f - dkin
        dlb_ref[...] += jnp.sum(sgm * diff, axis=0, keepdims=True)
        du_ref[:, 0:HEAD] = (dqin * (sq * (1.0 + q * (1.0 - sq)))).astype(BF16)
        du_ref[:, HEAD:2 * HEAD] = ((1.0 - lb) * sg * sgm * diff).astype(BF16)
        du_ref[:, 2 * HEAD:3 * HEAD] = dv.astype(BF16)
        du_ref[:, 3 * HEAD:4 * HEAD] = dog_ref[...]

    rev = lambda h, c: (NC - 1 - c, h)
    return pl.pallas_call(
        body, grid=(H, NC),
        in_specs=[
            pl.BlockSpec((CHUNK, 4 * HEAD), rev), pl.BlockSpec((1, HEAD), lambda h, c: (0, h)),
            pl.BlockSpec((CHUNK, HEAD), rev), pl.BlockSpec((CHUNK, HEAD), rev),
            pl.BlockSpec((None, None, HEAD, HEAD), lambda h, c: (h, NC - 1 - c, 0, 0)),
        ],
        out_specs=[pl.BlockSpec((CHUNK, 4 * HEAD), rev), pl.BlockSpec((1, HEAD), lambda h, c: (0, h))],
        out_shape=[jax.ShapeDtypeStruct((T, W), BF16), jax.ShapeDtypeStruct((1, H * HEAD), F32)],
        scratch_shapes=[pltpu.VMEM((HEAD, HEAD), F32)] + [pltpu.VMEM((CHUNK, HEAD), F32)] * 6,
        compiler_params=_params(("parallel", "arbitrary")), name=name,
    )(u, lb, do, dog, states)


def _hg_tail_fwd(o_raw, u, g, name):
    T, D = o_raw.shape
    H = D // HEAD
    tr = min(T, 512)
    blk = pl.BlockSpec((tr, HEAD), lambda h, i: (i, h))

    def body(o_ref, og_ref, g_ref, out_ref):
        o, og = o_ref[...], og_ref[...]
        out_ref[...] = (o * _rstd(o) * g_ref[...] * (og * _sigmoid(og))).astype(BF16)

    return pl.pallas_call(
        body, grid=(H, T // tr),
        in_specs=[blk, pl.BlockSpec((tr, HEAD), lambda h, i: (i, 4 * h + 3)), pl.BlockSpec((1, HEAD), lambda h, i: (0, h))],
        out_specs=blk, out_shape=jax.ShapeDtypeStruct((T, D), BF16),
        compiler_params=_params(("parallel", "parallel")), name=name,
    )(o_raw, u, g)


def _hg_tail_bwd(o_raw, u, g, doa, name):
    T, D = o_raw.shape
    H = D // HEAD
    tr = min(T, 512)
    blk = pl.BlockSpec((tr, HEAD), lambda h, i: (i, h))
    vec = pl.BlockSpec((1, HEAD), lambda h, i: (0, h))

    def body(o_ref, og_ref, g_ref, doa_ref, do_ref, dog_ref, dg_ref):
        @pl.when(pl.program_id(1) == 0)
        def _():
            dg_ref[...] = jnp.zeros_like(dg_ref)

        o, og, doa, g = o_ref[...], og_ref[...], doa_ref[...], g_ref[...]
        sg = _sigmoid(og)
        r = _rstd(o)
        xhat = o * r
        dog_ref[...] = (doa * (xhat * g) * (sg * (1.0 + og * (1.0 - sg)))).astype(BF16)
        dn = doa * (og * sg)
        dg_ref[...] += jnp.sum(dn * xhat, axis=0, keepdims=True)
        dxh = dn * g
        do_ref[...] = r * (dxh - xhat * jnp.mean(dxh * xhat, axis=-1, keepdims=True))

    return pl.pallas_call(
        body, grid=(H, T // tr),
        in_specs=[blk, pl.BlockSpec((tr, HEAD), lambda h, i: (i, 4 * h + 3)), vec, blk],
        out_specs=[blk, blk, vec],
        out_shape=[jax.ShapeDtypeStruct((T, D), F32), jax.ShapeDtypeStruct((T, D), BF16), jax.ShapeDtypeStruct((1, D), F32)],
        compiler_params=_params(("parallel", "arbitrary")), name=name,
    )(o_raw, u, g, doa)


def _lb_fwd(logits, name):
    def body(l_ref, o_ref):
        l0, l1 = l_ref[0:1, :], l_ref[1:2, :]
        m = jnp.maximum(l0, l1)
        e0, e1 = jnp.exp(l0 - m), jnp.exp(l1 - m)
        o_ref[...] = e0 / (e0 + e1)

    D = logits.shape[1]
    return pl.pallas_call(body, out_shape=jax.ShapeDtypeStruct((1, D), F32), name=name)(logits)


def _lb_bwd(lb, dlb, name):
    def body(lb_ref, d_ref, o_ref):
        lb = lb_ref[...]
        d0 = d_ref[...] * lb * (1.0 - lb)
        o_ref[0:1, :] = d0
        o_ref[1:2, :] = -d0

    D = lb.shape[1]
    return pl.pallas_call(body, out_shape=jax.ShapeDtypeStruct((2, D), F32), name=name)(lb, dlb)


def _ffn_fwd(x, p, tag):
    hb = _norm_fwd(x, p["pre_g"], f"{tag}_pre_norm")
    a = _mm(hb, p["w_gate"], "nn", F32, f"{tag}_gate")
    b = _mm(hb, p["w_up"], "nn", F32, f"{tag}_up")
    sb = _swiglu_fwd(a, b, f"{tag}_swiglu")
    y = _mm(sb, p["w_down"], "nn", F32, f"{tag}_down")
    xo = _postnorm_fwd(x, y, p["post_g"], FFN_RESIDUAL_WEIGHT, f"{tag}_post_norm")
    return xo, (x, hb, a, b, sb, y)


def _ffn_bwd(dxo, p, saved, tag):
    x, hb, a, b, sb, y = saved
    dyb, dpost = _norm_bwd(y, p["post_g"], dxo, FFN_RESIDUAL_WEIGHT, BF16, f"{tag}_post_norm_bwd")
    dw_down = _mm(sb, dyb, "tn", BF16, f"{tag}_down_dw")
    ds = _mm(dyb, p["w_down"], "nt", F32, f"{tag}_down_dx")
    dab, dbb = _swiglu_bwd(a, b, ds, f"{tag}_swiglu_bwd")
    dw_gate = _mm(hb, dab, "tn", BF16, f"{tag}_gate_dw")
    dw_up = _mm(hb, dbb, "tn", BF16, f"{tag}_up_dw")
    dh = _mm(dab, p["w_gate"], "nt", F32, f"{tag}_gate_dx")
    dh = _mm(dbb, p["w_up"], "nt", F32, f"{tag}_up_dx", add=dh)
    dx, dpre = _norm_bwd(x, p["pre_g"], dh, 1.0, F32, f"{tag}_pre_norm_bwd", res=dxo)
    return dx, {"pre_g": dpre, "w_gate": dw_gate, "w_up": dw_up, "w_down": dw_down, "post_g": dpost}


def _mixer_fwd(x, cos, sin, p):
    scale = (HEAD + ROPE) ** -0.5
    hb = _norm_fwd(x, p["pre_g"], "mix_pre_norm")
    u = _mm(hb, p["w_hg"], "nn", F32, "mix_in_hg")
    cq = _mm(hb, p["w_cq"], "nn", F32, "mix_in_cq")
    ckv = _mm(hb, p["w_ckv"], "nn", F32, "mix_in_ckv")
    kpe = _mm(hb, p["w_kpe"], "nn", F32, "mix_in_kpe")
    ga = _mm(hb, p["w_ga"], "nn", F32, "mix_in_ga")
    gb = _mm(hb, p["w_gb"], "nn", F32, "mix_in_gb")
    lb = _lb_fwd(p["lb_logits"], "hg_lb")
    o_raw, states = _hg_scan_fwd(u, lb, "hg_scan")
    oa = _hg_tail_fwd(o_raw, u, p["hg_norm_g"], "hg_tail")
    ya = _mm(oa, p["w_branch_a"], "nn", F32, "mix_branch_a")
    cqn = _norm_fwd(cq, p["q_norm_g"], "mla_q_norm")
    qn = _mm(cqn, p["w_qn"], "nn", BF16, "mla_q_up_nope")
    qpe = _rope_q_fwd(_mm(cqn, p["w_qpe"], "nn", F32, "mla_q_up_pe"), cos, sin, "mla_rope_q")
    ckvn = _norm_fwd(ckv, p["kv_norm_g"], "mla_kv_norm")
    kn = _mm(ckvn, p["w_kn"], "nn", BF16, "mla_k_up")
    vv = _mm(ckvn, p["w_vv"], "nn", BF16, "mla_v_up")
    kpe2 = _rope_k_fwd(kpe, cos, sin, "mla_rope_k")
    ob = _attn_fwd(qn, kn, vv, scale, "mla_attn", qpe=qpe, kpe=kpe2, causal=True)
    yb = _mm(ob, p["w_branch_b"], "nn", F32, "mix_branch_b")
    ym = _merge_fwd(ga, gb, ya, yb, "mix_merge")
    z = _mm(ym, p["w_out"], "nn", F32, "mix_out")
    xo = _postnorm_fwd(x, z, p["post_g"], 1.0, "mix_post_norm")
    saved = (x, hb, u, cq, ckv, ga, gb, lb, o_raw, states, oa, ya, cqn, qn, qpe, ckvn, kn, vv, kpe2, ob, yb, ym, z)
    return xo, saved


def _mixer_bwd(dxo, cos, sin, p, saved):
    x, hb, u, cq, ckv, ga, gb, lb, o_raw, states, oa, ya, cqn, qn, qpe, ckvn, kn, vv, kpe2, ob, yb, ym, z = saved
    scale = (HEAD + ROPE) ** -0.5
    g = {}
    dzb, g["post_g"] = _norm_bwd(z, p["post_g"], dxo, 1.0, BF16, "mix_post_norm_bwd")
    g["w_out"] = _mm(ym, dzb, "tn", BF16, "mix_out_dw")
    dym = _mm(dzb, p["w_out"], "nt", F32, "mix_out_dx")
    dya, dyb, dga, dgb = _merge_bwd(ga, gb, ya, yb, dym, "mix_merge_bwd")
    g["w_branch_a"] = _mm(oa, dya, "tn", BF16, "mix_branch_a_dw")
    doa = _mm(dya, p["w_branch_a"], "nt", F32, "mix_branch_a_dx")
    do_raw, dog, g["hg_norm_g"] = _hg_tail_bwd(o_raw, u, p["hg_norm_g"], doa, "hg_tail_bwd")
    du, dlb = _hg_scan_bwd(u, lb, do_raw, dog, states, "hg_scan_bwd")
    g["lb_logits"] = _lb_bwd(lb, dlb, "hg_lb_bwd")
    g["w_branch_b"] = _mm(ob, dyb, "tn", BF16, "mix_branch_b_dw")
    dob = _mm(dyb, p["w_branch_b"], "nt", BF16, "mix_branch_b_dx")
    dqn, dkn, dvv, dqpe_h, dkpe_h = _attn_bwd(qn, kn, vv, dob, scale, "mla_attn_bwd", qpe=qpe, kpe=kpe2, causal=True)
    dqpe = _rope_q_bwd(dqpe_h, cos, sin, "mla_rope_q_bwd")
    dkpe = _rope_k_bwd(dkpe_h, cos, sin, "mla_rope_k_bwd")
    g["w_qn"] = _mm(cqn, dqn, "tn", BF16, "mla_q_up_nope_dw")
    g["w_qpe"] = _mm(cqn, dqpe, "tn", BF16, "mla_q_up_pe_dw")
    dcqn = _mm(dqn, p["w_qn"], "nt", F32, "mla_q_up_nope_dx")
    dcqn = _mm(dqpe, p["w_qpe"], "nt", F32, "mla_q_up_pe_dx", add=dcqn)
    dcq, g["q_norm_g"] = _norm_bwd(cq, p["q_norm_g"], dcqn, 1.0, BF16, "mla_q_norm_bwd")
    g["w_kn"] = _mm(ckvn, dkn, "tn", BF16, "mla_k_up_dw")
    g["w_vv"] = _mm(ckvn, dvv, "tn", BF16, "mla_v_up_dw")
    dckvn = _mm(dkn, p["w_kn"], "nt", F32, "mla_k_up_dx")
    dckvn = _mm(dvv, p["w_vv"], "nt", F32, "mla_v_up_dx", add=dckvn)
    dckv, g["kv_norm_g"] = _norm_bwd(ckv, p["kv_norm_g"], dckvn, 1.0, BF16, "mla_kv_norm_bwd")
    dh = None
    for key, d in (("w_hg", du), ("w_cq", dcq), ("w_ckv", dckv), ("w_kpe", dkpe), ("w_ga", dga), ("w_gb", dgb)):
        g[key] = _mm(hb, d, "tn", BF16, f"mix_in_{key}_dw")
        dh = _mm(d, p[key], "nt", F32, f"mix_in_{key}_dx", add=dh)
    dx, g["pre_g"] = _norm_bwd(x, p["pre_g"], dh, 1.0, F32, "mix_pre_norm_bwd", res=dxo)
    return dx, g


def _xa_fwd(x, mem, p):
    scale = HEAD ** -0.5
    hb = _norm_fwd(x, p["pre_g"], "xa_pre_norm")
    mb = _norm_fwd(mem, p["mem_g"], "xa_mem_norm")
    q = _mm(hb, p["w_q"], "nn", BF16, "xa_q")
    k = _mm(mb, p["w_k"], "nn", BF16, "xa_k")
    v = _mm(mb, p["w_v"], "nn", BF16, "xa_v")
    o = _attn_fwd(q, k, v, scale, "xa_attn")
    z = _mm(o, p["w_o"], "nn", F32, "xa_o")
    xo = _postnorm_fwd(x, z, p["post_g"], 1.0, "xa_post_norm")
    return xo, (x, mem, hb, mb, q, k, v, o, z)


def _xa_bwd(dxo, p, saved):
    x, mem, hb, mb, q, k, v, o, z = saved
    scale = HEAD ** -0.5
    g = {}
    dzb, g["post_g"] = _norm_bwd(z, p["post_g"], dxo, 1.0, BF16, "xa_post_norm_bwd")
    g["w_o"] = _mm(o, dzb, "tn", BF16, "xa_o_dw")
    do = _mm(dzb, p["w_o"], "nt", BF16, "xa_o_dx")
    dq, dk, dv = _attn_bwd(q, k, v, do, scale, "xa_attn_bwd")
    g["w_q"] = _mm(hb, dq, "tn", BF16, "xa_q_dw")
    g["w_k"] = _mm(mb, dk, "tn", BF16, "xa_k_dw")
    g["w_v"] = _mm(mb, dv, "tn", BF16, "xa_v_dw")
    dh = _mm(dq, p["w_q"], "nt", F32, "xa_q_dx")
    dm = _mm(dk, p["w_k"], "nt", F32, "xa_k_dx")
    dm = _mm(dv, p["w_v"], "nt", F32, "xa_v_dx", add=dm)
    _, g["mem_g"] = _norm_bwd(mem, p["mem_g"], dm, 1.0, BF16, "xa_mem_norm_bwd")
    dx, g["pre_g"] = _norm_bwd(x, p["pre_g"], dh, 1.0, F32, "xa_pre_norm_bwd", res=dxo)
    return dx, g


def _local_step(x, mem, cos, sin, target, params):
    x1, s1 = _ffn_fwd(x, params["ffn1"], "ffn1")
    x2, s2 = _mixer_fwd(x1, cos, sin, params["mix"])
    x3, s3 = _xa_fwd(x2, mem, params["xa"])
    x4, s4 = _ffn_fwd(x3, params["ffn2"], "ffn2")
    dy, sq_err = _loss_head(x4, target, "loss_head")
    loss = 0.5 / x.shape[1] * jnp.sum(sq_err)
    dx, g4 = _ffn_bwd(dy, params["ffn2"], s4, "ffn2")
    dx, g3 = _xa_bwd(dx, params["xa"], s3)
    dx, g2 = _mixer_bwd(dx, cos, sin, params["mix"], s2)
    dx, g1 = _ffn_bwd(dx, params["ffn1"], s1, "ffn1")
    return loss, dx, {"ffn1": g1, "mix": g2, "xa": g3, "ffn2": g4}


def _split_w_in(w_in):
    D = w_in.shape[0]
    H = D // HEAD
    lora = (w_in.shape[1] - 6 * D - ROPE) // 2
    o = 4 * D
    w_hg = w_in[:, :o].reshape(D, 4, H, HEAD).transpose(0, 2, 1, 3).reshape(D, 4 * D)
    w_cq, w_ckv = w_in[:, o:o + lora], w_in[:, o + lora:o + 2 * lora]
    o += 2 * lora
    w_kpe = jnp.pad(w_in[:, o:o + ROPE], ((0, 0), (0, HEAD - ROPE)))
    o += ROPE
    return {"w_hg": w_hg, "w_cq": w_cq, "w_ckv": w_ckv, "w_kpe": w_kpe, "w_ga": w_in[:, o:o + D], "w_gb": w_in[:, o + D:o + 2 * D]}


def _merge_w_in(g):
    D = g["w_ga"].shape[0]
    H = D // HEAD
    hg = g["w_hg"].reshape(D, H, 4, HEAD).transpose(0, 2, 1, 3).reshape(D, 4 * D)
    return jnp.concatenate([hg, g["w_cq"], g["w_ckv"], g["w_kpe"][:, :ROPE], g["w_ga"], g["w_gb"]], axis=1)


def _split_heads(w, rest):
    K, N = w.shape
    w3 = w.reshape(K, N // (HEAD + rest), HEAD + rest)
    return w3[:, :, :HEAD].reshape(K, -1), w3[:, :, HEAD:].reshape(K, -1)


def _merge_heads(a, b, rest):
    K = a.shape[0]
    H = a.shape[1] // HEAD
    return jnp.concatenate([a.reshape(K, H, HEAD), b.reshape(K, H, rest)], axis=2).reshape(K, H * (HEAD + rest))


def _layer_params(w, small):
    ffn = lambda t: {"pre_g": small[f"{t}_pre_g"], "w_gate": w[f"{t}_w_gate"], "w_up": w[f"{t}_w_up"],
                     "w_down": w[f"{t}_w_down"], "post_g": small[f"{t}_post_g"]}
    mix = _split_w_in(w["w_in"])
    mix["w_qn"], mix["w_qpe"] = _split_heads(w["mla_w_q_up"], ROPE)
    mix["w_kn"], mix["w_vv"] = _split_heads(w["mla_w_kv_up"], HEAD)
    mix.update(w_branch_a=w["w_branch_a"], w_branch_b=w["w_branch_b"], w_out=w["w_out"], pre_g=small["mix_pre_g"],
               post_g=small["mix_post_g"], hg_norm_g=small["hg_norm_g"], q_norm_g=small["mla_q_norm_g"],
               kv_norm_g=small["mla_kv_norm_g"], lb_logits=small["hgrn_lb_logits"])
    xa = {"pre_g": small["xa_pre_g"], "mem_g": small["xa_mem_g"], "post_g": small["xa_post_g"],
          "w_q": w["xa_w_q"], "w_k": w["xa_w_k"], "w_v": w["xa_w_v"], "w_o": w["xa_w_o"]}
    return {"ffn1": ffn("ffn1"), "mix": mix, "xa": xa, "ffn2": ffn("ffn2")}


def _natural_grads(g):
    out = {}
    for t in ("ffn1", "ffn2"):
        for k in ("pre_g", "w_gate", "w_up", "w_down", "post_g"):
            out[f"{t}_{k}"] = g[t][k]
    m = g["mix"]
    out["w_in"] = _merge_w_in(m)
    out["mla_w_q_up"] = _merge_heads(m["w_qn"], m["w_qpe"], ROPE)
    out["mla_w_kv_up"] = _merge_heads(m["w_kn"], m["w_vv"], HEAD)
    out.update(w_branch_a=m["w_branch_a"], w_branch_b=m["w_branch_b"], w_out=m["w_out"], mix_pre_g=m["pre_g"],
               mix_post_g=m["post_g"], hg_norm_g=m["hg_norm_g"], mla_q_norm_g=m["q_norm_g"],
               mla_kv_norm_g=m["kv_norm_g"], hgrn_lb_logits=m["lb_logits"])
    x = g["xa"]
    out.update(xa_pre_g=x["pre_g"], xa_mem_g=x["mem_g"], xa_post_g=x["post_g"], xa_w_q=x["w_q"], xa_w_k=x["w_k"],
               xa_w_v=x["w_v"], xa_w_o=x["w_o"])
    return out


def _rope_tables(positions):
    inv_freq = 1.0 / (ROPE_THETA ** (jnp.arange(0, ROPE, 2, dtype=F32) / ROPE))
    ang = positions.astype(F32)[:, None] * inv_freq
    return jnp.tile(jnp.cos(ang), (1, 4)), jnp.tile(jnp.sin(ang), (1, 4))


def _adamw(w, g, m, v, name):
    bc1 = 1.0 - ADAM_B1 ** ADAM_STEP
    bc2 = 1.0 - ADAM_B2 ** ADAM_STEP

    def body(w_ref, g_ref, m_ref, v_ref, d_ref, mo_ref, vo_ref):
        g = g_ref[...]
        m = ADAM_B1 * m_ref[...] + (1.0 - ADAM_B1) * g
        v = ADAM_B2 * v_ref[...] + (1.0 - ADAM_B2) * (g * g)
        mo_ref[...] = m
        vo_ref[...] = v
        d_ref[...] = -ADAM_LR * ((m / bc1) / (jnp.sqrt(v / bc2) + ADAM_EPS) + ADAM_WD * w_ref[...])

    return _rows(body, [(w, "row"), (g, "row"), (m, "row"), (v, "row")], [(w.shape, F32, "row")] * 3, name)


ANY = pl.BlockSpec(memory_space=pl.ANY)
COMM_AXES = ("x", "y", "c")


def _place():
    x, y, c = (lax.axis_index(n) for n in COMM_AXES)
    chips = [(1 - x, y), (x, 1 - y), (1 - x, 1 - y)]
    return x, y, c, 2 * x + y, (x, y, 1 - c), chips


def _remote(src, dst, send_sems, recv_sems, j, to):
    return pltpu.make_async_remote_copy(src_ref=src, dst_ref=dst, send_sem=send_sems.at[j], recv_sem=recv_sems.at[j],
                                        device_id=to, device_id_type=MESH)


def _all_gather(shard, name):
    def body(src, out, send_sems, recv_sems, local_sem):
        x, y, c, me, sibling, chips = _place()
        mine = pltpu.make_async_copy(src, out.at[me], local_sem)
        mine.start()
        sent = []
        for j, (cx, cy) in enumerate(chips):
            sent.append(_remote(src.at[c], out.at[me, c], send_sems, recv_sems, j, (cx, cy, c)))
            sent[-1].start()
        for j, (cx, cy) in enumerate(chips):
            k = 2 * cx + cy
            _remote(src.at[c], out.at[k, c], send_sems, recv_sems, j, (cx, cy, c)).wait_recv()
            sent.append(_remote(out.at[k, c], out.at[k, c], send_sems, recv_sems, 3 + j, sibling))
            sent[-1].start()
        for j, (cx, cy) in enumerate(chips):
            k = 2 * cx + cy
            _remote(out.at[k, 1 - c], out.at[k, 1 - c], send_sems, recv_sems, 3 + j, sibling).wait_recv()
        for cp in sent:
            cp.wait_send()
        mine.wait()

    return pl.pallas_call(
        body, in_specs=[ANY], out_specs=ANY, out_shape=jax.ShapeDtypeStruct((N_CHIP,) + shard.shape, shard.dtype),
        scratch_shapes=[pltpu.SemaphoreType.DMA((6,)), pltpu.SemaphoreType.DMA((6,)), pltpu.SemaphoreType.DMA(())],
        name=name,
    )(shard)


def _rs_swap(g, name):
    def body(g_ref, out, send_sems, recv_sems):
        x, y, c, me, sibling, chips = _place()
        cp = _remote(g_ref.at[1 - c], out, send_sems, recv_sems, 0, sibling)
        cp.start()
        cp.wait()

    return pl.pallas_call(
        body, in_specs=[ANY], out_specs=ANY, out_shape=jax.ShapeDtypeStruct(g.shape[1:], g.dtype),
        scratch_shapes=[pltpu.SemaphoreType.DMA((1,)), pltpu.SemaphoreType.DMA((1,))], name=name,
    )(g)


def _rs_pair_sum(g, got, name):
    _, n, R, W = g.shape
    tr = _tile(R, 512, 16)
    c = lax.axis_index("c").astype(jnp.int32).reshape((1,))

    def body(c_ref, a_ref, b_ref, o_ref):
        o_ref[...] = (a_ref[...].astype(F32) + b_ref[...].astype(F32)).astype(BF16)

    return pl.pallas_call(
        body,
        grid_spec=pltpu.PrefetchScalarGridSpec(
            num_scalar_prefetch=1, grid=(n, R // tr),
            in_specs=[pl.BlockSpec((None, None, tr, W), lambda k, i, c_ref: (c_ref[0], k, i, 0)),
                      pl.BlockSpec((None, tr, W), lambda k, i, c_ref: (k, i, 0))],
            out_specs=pl.BlockSpec((None, tr, W), lambda k, i, c_ref: (k, i, 0)),
        ),
        out_shape=jax.ShapeDtypeStruct((n, R, W), BF16),
        compiler_params=_params(("parallel", "parallel")), name=name,
    )(c, g, got)


def _rs_exchange(a, name):
    def body(a_ref, z_ref, send_sems, recv_sems, local_sem):
        x, y, c, me, sibling, chips = _place()
        loc = pltpu.make_async_copy(a_ref.at[me], z_ref.at[me], local_sem)
        loc.start()
        sent = []
        for j, (cx, cy) in enumerate(chips):
            sent.append(_remote(a_ref.at[2 * cx + cy], z_ref.at[me], send_sems, recv_sems, j, (cx, cy, c)))
            sent[-1].start()
        for j, (cx, cy) in enumerate(chips):
            k = 2 * cx + cy
            _remote(a_ref.at[k], z_ref.at[k], send_sems, recv_sems, j, (cx, cy, c)).wait_recv()
        for cp in sent:
            cp.wait_send()
        loc.wait()

    return pl.pallas_call(
        body, in_specs=[ANY], out_specs=ANY, out_shape=jax.ShapeDtypeStruct(a.shape, a.dtype),
        scratch_shapes=[pltpu.SemaphoreType.DMA((3,)), pltpu.SemaphoreType.DMA((3,)), pltpu.SemaphoreType.DMA(())],
        name=name,
    )(a)


def _rs_chip_sum(z, name):
    n, R, W = z.shape
    tr = _tile(R, 512, 16)

    def body(z_ref, o_ref):
        acc = z_ref[0].astype(F32)
        for k in range(1, n):
            acc = acc + z_ref[k].astype(F32)
        o_ref[...] = acc

    return pl.pallas_call(
        body, grid=(R // tr,), in_specs=[pl.BlockSpec((n, tr, W), lambda i: (0, i, 0))],
        out_specs=pl.BlockSpec((tr, W), lambda i: (i, 0)), out_shape=jax.ShapeDtypeStruct((R, W), F32),
        compiler_params=_params(("parallel",)), name=name,
    )(z)


def _rs_share(r, name):
    def body(r_ref, out, send_sems, recv_sems, local_sem):
        x, y, c, me, sibling, chips = _place()
        loc = pltpu.make_async_copy(r_ref, out.at[c], local_sem)
        loc.start()
        cp = _remote(r_ref, out.at[c], send_sems, recv_sems, 0, sibling)
        cp.start()
        _remote(r_ref, out.at[1 - c], send_sems, recv_sems, 0, sibling).wait_recv()
        cp.wait_send()
        loc.wait()

    return pl.pallas_call(
        body, in_specs=[ANY], out_specs=ANY, out_shape=jax.ShapeDtypeStruct((2,) + r.shape, r.dtype),
        scratch_shapes=[pltpu.SemaphoreType.DMA((1,)), pltpu.SemaphoreType.DMA((1,)), pltpu.SemaphoreType.DMA(())],
        name=name,
    )(r)


def _all_reduce_small(s, name):
    flips = [(dx, dy, dc) for dx in (0, 1) for dy in (0, 1) for dc in (0, 1) if (dx, dy, dc) != (0, 0, 0)]

    def body(s_ref, o_ref, buf, send_sems, recv_sems):
        x, y, c = (lax.axis_index(n) for n in COMM_AXES)
        me = 4 * x + 2 * y + c
        buf[me] = s_ref[...]
        peers = [((1 - x) if dx else x, (1 - y) if dy else y, (1 - c) if dc else c) for dx, dy, dc in flips]
        sent = [_remote(s_ref, buf.at[me], send_sems, recv_sems, j, p) for j, p in enumerate(peers)]
        for cp in sent:
            cp.start()
        for j, (px, py, pc) in enumerate(peers):
            _remote(s_ref, buf.at[4 * px + 2 * py + pc], send_sems, recv_sems, j, (px, py, pc)).wait_recv()
        for cp in sent:
            cp.wait_send()
        acc = buf[0]
        for d in range(1, N_DEV):
            acc = acc + buf[d]
        o_ref[...] = acc

    vmem = pl.BlockSpec(memory_space=pltpu.VMEM)
    return pl.pallas_call(
        body, in_specs=[vmem], out_specs=vmem, out_shape=jax.ShapeDtypeStruct(s.shape, F32),
        scratch_shapes=[pltpu.VMEM((N_DEV,) + s.shape, F32), pltpu.SemaphoreType.DMA((7,)), pltpu.SemaphoreType.DMA((7,))],
        name=name,
    )(s)


BIG = {
    "ffn1_w_gate": 1, "ffn1_w_up": 1, "ffn1_w_down": 0, "w_in": 1, "mla_w_q_up": 1, "mla_w_kv_up": 1,
    "w_branch_a": 0, "w_branch_b": 0, "w_out": 0, "xa_w_q": 0, "xa_w_k": 0, "xa_w_v": 0, "xa_w_o": 1,
    "ffn2_w_gate": 1, "ffn2_w_up": 1, "ffn2_w_down": 0,
}
WEIGHTS = [
    "hgrn_lb_logits", "ffn1_pre_g", "ffn1_w_gate", "ffn1_w_up", "ffn1_w_down", "ffn1_post_g", "mix_pre_g", "w_in",
    "hg_norm_g", "mla_q_norm_g", "mla_w_q_up", "mla_kv_norm_g", "mla_w_kv_up", "w_branch_a", "w_branch_b", "w_out",
    "mix_post_g", "xa_pre_g", "xa_mem_g", "xa_w_q", "xa_w_k", "xa_w_v", "xa_w_o", "xa_post_g", "ffn2_pre_g",
    "ffn2_w_gate", "ffn2_w_up", "ffn2_w_down", "ffn2_post_g",
]
SMALL = [n for n in WEIGHTS if n not in BIG]
PACK_ALIGN = 2 * PACK_W * 16


def _pack_flat(parts):
    flat = jnp.concatenate([p.reshape(-1) for p in parts])
    pad = -flat.shape[0] % PACK_ALIGN
    return jnp.pad(flat, (0, pad)) if pad else flat


def _unpack_flat(flat, shapes):
    out, o = [], 0
    for s in shapes:
        n = s[0] * s[1]
        out.append(flat[o:o + n].reshape(s))
        o += n
    return out


def _pack_small(vals, width):
    rows = [jnp.pad(v, ((0, 0), (0, width - v.shape[1]))) for v in vals]
    s = jnp.concatenate(rows, axis=0)
    return jnp.pad(s, ((0, -s.shape[0] % 8), (0, 0)))


def _unpack_small(s, shapes):
    out, o = [], 0
    for r, w in shapes:
        out.append(s[o:o + r, :w])
        o += r
    return out


def kernel(x, mem, positions, hgrn_lb_logits, ffn1_pre_g, ffn1_w_gate, ffn1_w_up, ffn1_w_down, ffn1_post_g, mix_pre_g, w_in, hg_norm_g, mla_q_norm_g, mla_w_q_up, mla_kv_norm_g, mla_w_kv_up, w_branch_a, w_branch_b, w_out, mix_post_g, xa_pre_g, xa_mem_g, xa_w_q, xa_w_k, xa_w_v, xa_w_o, xa_post_g, ffn2_pre_g, ffn2_w_gate, ffn2_w_up, ffn2_w_down, ffn2_post_g, loss_target, m_hgrn_lb_logits, m_ffn1_pre_g, m_ffn1_w_gate, m_ffn1_w_up, m_ffn1_w_down, m_ffn1_post_g, m_mix_pre_g, m_w_in, m_hg_norm_g, m_mla_q_norm_g, m_mla_w_q_up, m_mla_kv_norm_g, m_mla_w_kv_up, m_w_branch_a, m_w_branch_b, m_w_out, m_mix_post_g, m_xa_pre_g, m_xa_mem_g, m_xa_w_q, m_xa_w_k, m_xa_w_v, m_xa_w_o, m_xa_post_g, m_ffn2_pre_g, m_ffn2_w_gate, m_ffn2_w_up, m_ffn2_w_down, m_ffn2_post_g, v_hgrn_lb_logits, v_ffn1_pre_g, v_ffn1_w_gate, v_ffn1_w_up, v_ffn1_w_down, v_ffn1_post_g, v_mix_pre_g, v_w_in, v_hg_norm_g, v_mla_q_norm_g, v_mla_w_q_up, v_mla_kv_norm_g, v_mla_w_kv_up, v_w_branch_a, v_w_branch_b, v_w_out, v_mix_post_g, v_xa_pre_g, v_xa_mem_g, v_xa_w_q, v_xa_w_k, v_xa_w_v, v_xa_w_o, v_xa_post_g, v_ffn2_pre_g, v_ffn2_w_gate, v_ffn2_w_up, v_ffn2_w_down, v_ffn2_post_g):
    a = dict(locals())
    big = list(BIG)
    shard_shapes = [a[n].shape[1:] for n in big]
    n_flat = sum(s[0] * s[1] for s in shard_shapes)

    packed = _pack_flat([a[n][0].astype(BF16) for n in big]).reshape(2, -1, PACK_W)
    gathered = _all_gather(packed, "weights_all_gather").reshape(N_CHIP, -1)
    per_chip = [_unpack_flat(gathered[k], shard_shapes) for k in range(N_CHIP)]
    full = {n: jnp.concatenate([per_chip[k][i] for k in range(N_CHIP)], axis=BIG[n]) for i, n in enumerate(big)}

    params = _layer_params(full, {n: a[n] for n in SMALL})
    cos, sin = _rope_tables(positions[0])
    loss_part, grad_x, g = _local_step(x[0], mem[0], cos, sin, loss_target[0], params)
    grads = _natural_grads(g)

    slots = []
    for k in range(N_CHIP):
        parts = []
        for n, s in zip(big, shard_shapes):
            w = s[BIG[n]]
            parts.append(lax.slice_in_dim(grads[n], k * w, (k + 1) * w, axis=BIG[n]))
        slots.append(_pack_flat(parts).reshape(2, -1, PACK_W))
    gp = jnp.stack(slots, axis=1)
    got = _rs_swap(gp, "grads_sibling_swap")
    pair = _rs_pair_sum(gp, got, "grads_pair_sum")
    mine = _rs_exchange(pair, "grads_chip_exchange")
    half = _rs_chip_sum(mine, "grads_chip_sum")
    both = _rs_share(half, "grads_sibling_share")
    g_big = dict(zip(big, _unpack_flat(both.reshape(-1), shard_shapes)))

    small_shapes = [a[n].shape for n in SMALL]
    width = max(s[1] for s in small_shapes)
    g_small = _all_reduce_small(_pack_small([grads[n] for n in SMALL], width), "small_grads_all_reduce")

    out_g, out_d, out_m, out_v = {}, {}, {}, {}
    for n in big:
        shp = a[n].shape
        d, m, v = _adamw(a[n][0], g_big[n], a["m_" + n][0], a["v_" + n][0], f"adamw_{n}")
        out_g[n], out_d[n], out_m[n], out_v[n] = (t.reshape(shp) for t in (g_big[n], d, m, v))
    sw, sm, sv = (_pack_small([a[p + n] for n in SMALL], width) for p in ("", "m_", "v_"))
    d, m, v = _adamw(sw, g_small, sm, sv, "adamw_small")
    for t, dst in ((g_small, out_g), (d, out_d), (m, out_m), (v, out_v)):
        dst.update(zip(SMALL, _unpack_small(t, small_shapes)))

    loss = lax.psum(loss_part, COMM_AXES)
    return (loss, grad_x[None], *[out_g[n] for n in WEIGHTS], *[out_d[n] for n in WEIGHTS],
            *[out_m[n] for n in WEIGHTS], *[out_v[n] for n in WEIGHTS])
```

```python
import functools

import jax
import jax.numpy as jnp
from jax import lax
from jax.experimental import pallas as pl
from jax.experimental.pallas import tpu as pltpu

F32 = jnp.float32
BF16 = jnp.bfloat16
EPS = 1e-6
HEAD = 128
ROPE = 64
CHUNK = 64
SUB = 16
HG_BLOCK = 256
ROPE_THETA = 10000.0
FFN_RESIDUAL_WEIGHT = 0.5
ADAM_LR, ADAM_B1, ADAM_B2, ADAM_EPS, ADAM_WD, ADAM_STEP = 0.001, 0.9, 0.999, 1e-08, 0.01, 10
VMEM_LIMIT = 56 * 2**20
ROW_BUDGET = 20 * 2**20
NEG = -1e30
MESH = pl.DeviceIdType.MESH
N_CHIP = 4
N_DEV = 8


def _params(sem):
    return pltpu.CompilerParams(dimension_semantics=sem, vmem_limit_bytes=VMEM_LIMIT)


def _tile(n, cap, mult):
    if n <= cap:
        return n
    t = (cap // mult) * mult
    while t >= mult:
        if n % t == 0:
            return t
        t -= mult
    raise ValueError(f"no tile for {n} under {cap}")


def _sigmoid(x):
    return 1.0 / (1.0 + jnp.exp(-x))


def _dot(a, b, dims):
    return lax.dot_general(a, b, (dims, ((), ())), preferred_element_type=F32)


NN = ((1,), (0,))
NT = ((1,), (1,))
TN = ((0,), (0,))


def _mm(a, b, mode, out_dtype, name, add=None, out_slots=0, tm_cap=1024, tn_cap=512, tk_cap=2816):
    slot_cap = 1408
    b_slots = b.shape[0] if b.ndim == 3 else 0
    bs = (b.shape[1], b_slots * b.shape[2]) if b_slots else b.shape
    if mode == "nn":
        (M, K), (K2, N) = a.shape, bs
    elif mode == "nt":
        (M, K), (N, K2) = a.shape, bs
    else:
        (K, M), (K2, N) = a.shape, bs
    assert K == K2, (a.shape, b.shape, mode)
    tm = _tile(M, tm_cap, 128)
    tn = _tile(N // (b_slots or out_slots), slot_cap, 128) if (out_slots or (b_slots and mode == "nn")) else _tile(N, tn_cap, 128)
    tk = _tile(K // b_slots, slot_cap, 128) if (b_slots and mode == "nt") else _tile(K, tk_cap, 128)
    nk = K // tk
    a_spec = pl.BlockSpec((tk, tm), lambda i, j, k: (k, i)) if mode == "tn" else pl.BlockSpec((tm, tk), lambda i, j, k: (i, k))
    if b_slots and mode == "nn":
        per = b.shape[2] // tn
        b_spec = pl.BlockSpec((None, tk, tn), lambda i, j, k: (j // per, k, j % per))
    elif b_slots:
        per = b.shape[2] // tk
        b_spec = pl.BlockSpec((None, tn, tk), lambda i, j, k: (k // per, j, k % per))
    else:
        b_spec = pl.BlockSpec((tn, tk), lambda i, j, k: (j, k)) if mode == "nt" else pl.BlockSpec((tk, tn), lambda i, j, k: (k, j))
    if out_slots:
        per_o = N // out_slots // tn
        o_spec = pl.BlockSpec((None, tm, tn), lambda i, j, k: (j // per_o, i, j % per_o))
        o_shape = (out_slots, M, N // out_slots)
    else:
        o_spec = pl.BlockSpec((tm, tn), lambda i, j, k: (i, j))
        o_shape = (M, N)
    dims = {"nn": NN, "nt": NT, "tn": TN}[mode]
    has_add = add is not None

    def body(*refs):
        a_ref, b_ref = refs[0], refs[1]
        add_ref = refs[2] if has_add else None
        o_ref = refs[3] if has_add else refs[2]
        p = _dot(a_ref[...].astype(BF16), b_ref[...].astype(BF16), dims)

        def finish(val):
            if has_add:
                val = val + add_ref[...]
            o_ref[...] = val.astype(out_dtype)

        if nk == 1:
            finish(p)
        else:
            acc_ref = refs[-1]
            k = pl.program_id(2)

            @pl.when(k == 0)
            def _():
                acc_ref[...] = p

            @pl.when(k > 0)
            def _():
                acc_ref[...] += p

            @pl.when(k == nk - 1)
            def _():
                finish(acc_ref[...])

    assert not (has_add and out_slots)
    in_specs = [a_spec, b_spec] + ([o_spec] if has_add else [])
    args = (a, b) + ((add,) if has_add else ())
    return pl.pallas_call(
        body,
        grid=(M // tm, N // tn, nk),
        in_specs=in_specs,
        out_specs=o_spec,
        out_shape=jax.ShapeDtypeStruct(o_shape, out_dtype),
        scratch_shapes=[pltpu.VMEM((tm, tn), F32)] if nk > 1 else [],
        compiler_params=_params(("parallel", "parallel", "arbitrary")),
        name=name,
    )(*args)


def _rows(body, ins, outs, name):
    T = next(a.shape[0] for a, k in ins if k == "row")
    per_row = sum(a.shape[1] * a.dtype.itemsize for a, k in ins if k == "row")
    per_row += sum(s[1] * jnp.dtype(d).itemsize for s, d, k in outs if k == "row")
    tr = next(t for t in (512, 256, 128, 64, 32, 16, 8) if T % t == 0 and 2 * t * per_row <= ROW_BUDGET)
    in_specs = [
        pl.BlockSpec((tr, a.shape[1]), lambda i: (i, 0)) if k == "row" else pl.BlockSpec(a.shape, lambda i: (0, 0))
        for a, k in ins
    ]
    out_specs = [
        pl.BlockSpec((tr, s[1]), lambda i: (i, 0)) if k == "row" else pl.BlockSpec(s, lambda i: (0, 0))
        for s, d, k in outs
    ]
    has_acc = any(k == "acc" for _, _, k in outs)
    res = pl.pallas_call(
        body,
        grid=(T // tr,),
        in_specs=in_specs,
        out_specs=out_specs,
        out_shape=[jax.ShapeDtypeStruct(s, d) for s, d, k in outs],
        compiler_params=_params(("arbitrary",) if has_acc else ("parallel",)),
        name=name,
    )(*[a for a, _ in ins])
    return res


def _rstd(x):
    return lax.rsqrt(jnp.mean(x * x, axis=-1, keepdims=True) + EPS)


def _norm_fwd(x, g, name):
    def body(x_ref, g_ref, o_ref):
        x = x_ref[...]
        o_ref[...] = (x * _rstd(x) * g_ref[...]).astype(BF16)

    return _rows(body, [(x, "row"), (g, "vec")], [(x.shape, BF16, "row")], name)[0]


def _postnorm_fwd(x, y, g, weight, name):
    def body(x_ref, y_ref, g_ref, o_ref):
        y = y_ref[...]
        o_ref[...] = x_ref[...] + weight * (y * _rstd(y) * g_ref[...])

    return _rows(body, [(x, "row"), (y, "row"), (g, "vec")], [(x.shape, F32, "row")], name)[0]


def _norm_bwd(x, g, dy, weight, out_dtype, name, res=None):
    has_res = res is not None

    def body(*refs):
        x_ref, g_ref, dy_ref = refs[:3]
        res_ref = refs[3] if has_res else None
        dx_ref, dg_ref = refs[-2], refs[-1]

        @pl.when(pl.program_id(0) == 0)
        def _():
            dg_ref[...] = jnp.zeros_like(dg_ref)

        x = x_ref[...]
        dn = dy_ref[...].astype(F32) * weight
        r = _rstd(x)
        xhat = x * r
        dg_ref[...] += jnp.sum(dn * xhat, axis=0, keepdims=True)
        dxh = dn * g_ref[...]
        dx = r * (dxh - xhat * jnp.mean(dxh * xhat, axis=-1, keepdims=True))
        if has_res:
            dx = dx + res_ref[...]
        dx_ref[...] = dx.astype(out_dtype)

    ins = [(x, "row"), (g, "vec"), (dy, "row")] + ([(res, "row")] if has_res else [])
    return _rows(body, ins, [(x.shape, out_dtype, "row"), (g.shape, F32, "acc")], name)


def _swiglu_fwd(a, b, name):
    def body(a_ref, b_ref, o_ref):
        a = a_ref[...]
        o_ref[...] = (a * _sigmoid(a) * b_ref[...]).astype(BF16)

    return _rows(body, [(a, "row"), (b, "row")], [(a.shape, BF16, "row")], name)[0]


def _swiglu_bwd(a, b, ds, name):
    def body(a_ref, b_ref, ds_ref, da_ref, db_ref):
        a, ds = a_ref[...], ds_ref[...]
        sg = _sigmoid(a)
        da_ref[...] = (ds * b_ref[...] * (sg * (1.0 + a * (1.0 - sg)))).astype(BF16)
        db_ref[...] = (ds * (a * sg)).astype(BF16)

    return _rows(body, [(a, "row"), (b, "row"), (ds, "row")], [(a.shape, BF16, "row"), (a.shape, BF16, "row")], name)


def _merge_fwd(ga, gb, ya, yb, name):
    def body(ga_ref, gb_ref, ya_ref, yb_ref, o_ref):
        o_ref[...] = (_sigmoid(ga_ref[...]) * ya_ref[...] + _sigmoid(gb_ref[...]) * yb_ref[...]).astype(BF16)

    return _rows(body, [(ga, "row"), (gb, "row"), (ya, "row"), (yb, "row")], [(ga.shape, BF16, "row")], name)[0]


def _merge_bwd(ga, gb, ya, yb, dy, name):
    def body(ga_ref, gb_ref, ya_ref, yb_ref, dy_ref, dya_ref, dyb_ref, dga_ref, dgb_ref):
        dy = dy_ref[...]
        sa, sb = _sigmoid(ga_ref[...]), _sigmoid(gb_ref[...])
        dya_ref[...] = (dy * sa).astype(BF16)
        dyb_ref[...] = (dy * sb).astype(BF16)
        dga_ref[...] = (dy * ya_ref[...] * (sa * (1.0 - sa))).astype(BF16)
        dgb_ref[...] = (dy * yb_ref[...] * (sb * (1.0 - sb))).astype(BF16)

    ins = [(ga, "row"), (gb, "row"), (ya, "row"), (yb, "row"), (dy, "row")]
    return _rows(body, ins, [(ga.shape, BF16, "row")] * 4, name)


def _loss_head(y, target, name):
    D = y.shape[1]

    def body(y_ref, t_ref, dy_ref, acc_ref):
        @pl.when(pl.program_id(0) == 0)
        def _():
            acc_ref[...] = jnp.zeros_like(acc_ref)

        err = y_ref[...] - t_ref[...]
        dy_ref[...] = err * (1.0 / D)
        acc_ref[...] += jnp.sum(err * err, axis=0, keepdims=True)

    return _rows(body, [(y, "row"), (target, "row")], [(y.shape, F32, "row"), ((1, D), F32, "acc")], name)


def _rot(x):
    lane = lax.broadcasted_iota(jnp.int32, x.shape, 1)
    return jnp.where((lane % ROPE) < ROPE // 2, -pltpu.roll(x, 128 - ROPE // 2, 1), pltpu.roll(x, ROPE // 2, 1))


def _rope_q_fwd(qpe, cos, sin, name):
    T, W = qpe.shape
    tr = min(T, 512)
    blk = pl.BlockSpec((tr, 128), lambda i, j: (i, j))
    tab = pl.BlockSpec((tr, 128), lambda i, j: (i, 0))

    def body(x_ref, c_ref, s_ref, o_ref):
        x = x_ref[...]
        o_ref[...] = (x * c_ref[...] + _rot(x) * s_ref[...]).astype(BF16)

    return pl.pallas_call(
        body, grid=(T // tr, W // 128), in_specs=[blk, tab, tab], out_specs=blk,
        out_shape=jax.ShapeDtypeStruct((T, W), BF16), compiler_params=_params(("parallel", "parallel")), name=name,
    )(qpe, cos, sin)


def _rope_q_bwd(dq_heads, cos, sin, name):
    T, W = dq_heads.shape
    tr = min(T, 512)
    even = pl.BlockSpec((tr, 128), lambda i, j: (i, 2 * j))
    odd = pl.BlockSpec((tr, 128), lambda i, j: (i, 2 * j + 1))
    tab = pl.BlockSpec((tr, 128), lambda i, j: (i, 0))

    def body(a_ref, b_ref, c_ref, s_ref, o_ref):
        d = a_ref[...] + b_ref[...]
        o_ref[...] = (d * c_ref[...] - _rot(d * s_ref[...])).astype(BF16)

    return pl.pallas_call(
        body, grid=(T // tr, W // 256), in_specs=[even, odd, tab, tab],
        out_specs=pl.BlockSpec((tr, 128), lambda i, j: (i, j)),
        out_shape=jax.ShapeDtypeStruct((T, W // 2), BF16), compiler_params=_params(("parallel", "parallel")), name=name,
    )(dq_heads, dq_heads, cos, sin)


def _rope_k_fwd(kpe, cos, sin, name):
    def body(x_ref, c_ref, s_ref, o_ref):
        x = x_ref[...]
        y = x * c_ref[...] + _rot(x) * s_ref[...]
        o_ref[...] = (y + pltpu.roll(y, ROPE, 1)).astype(BF16)

    return _rows(body, [(kpe, "row"), (cos, "row"), (sin, "row")], [(kpe.shape, BF16, "row")], name)[0]


def _rope_k_bwd(dk_heads, cos, sin, name):
    T, W = dk_heads.shape

    def body(d_ref, c_ref, s_ref, o_ref):
        d = d_ref[:, 0:128]
        for h in range(1, W // 128):
            d = d + d_ref[:, h * 128:(h + 1) * 128]
        d = d + pltpu.roll(d, ROPE, 1)
        dx = d * c_ref[...] - _rot(d * s_ref[...])
        lane = lax.broadcasted_iota(jnp.int32, dx.shape, 1)
        o_ref[...] = jnp.where(lane < ROPE, dx, 0.0).astype(BF16)

    return _rows(body, [(dk_heads, "row"), (cos, "row"), (sin, "row")], [((T, 128), BF16, "row")], name)[0]


def _attn_probs(q, k, qpe, kpe, scale, causal, q0):
    s = _dot(q, k, NT)
    if qpe is not None:
        s = s + _dot(qpe, kpe, NT)
    s = s * scale
    if causal:
        row = q0 + lax.broadcasted_iota(jnp.int32, s.shape, 0)
        col = lax.broadcasted_iota(jnp.int32, s.shape, 1)
        s = jnp.where((col // CHUNK) <= (row // CHUNK), s, NEG)
    p = jnp.exp(s - jnp.max(s, axis=-1, keepdims=True))
    return p / jnp.sum(p, axis=-1, keepdims=True)


def _pe_mask(x, h):
    lane = lax.broadcasted_iota(jnp.int32, x.shape, 1)
    return jnp.where((lane // ROPE) == (h % 2), x, jnp.zeros_like(x))


def _attn_fwd(q, k, v, scale, name, qpe=None, kpe=None, causal=False):
    T, W = q.shape
    Tk = k.shape[0]
    H = W // HEAD
    tq = min(T, 256)
    has_pe = qpe is not None
    qs = pl.BlockSpec((tq, HEAD), lambda h, i: (i, h))
    ks = pl.BlockSpec((Tk, HEAD), lambda h, i: (0, h))
    in_specs, args = [qs, ks, ks], [q, k, v]
    if has_pe:
        in_specs += [pl.BlockSpec((tq, HEAD), lambda h, i: (i, h // 2)), pl.BlockSpec((Tk, HEAD), lambda h, i: (0, 0))]
        args += [qpe, kpe]

    def body(*refs):
        q_ref, k_ref, v_ref = refs[:3]
        o_ref = refs[-1]
        h, i = pl.program_id(0), pl.program_id(1)
        qp = _pe_mask(refs[3][...], h) if has_pe else None
        kp = refs[4][...] if has_pe else None
        p = _attn_probs(q_ref[...], k_ref[...], qp, kp, scale, causal, i * tq)
        o_ref[...] = _dot(p.astype(BF16), v_ref[...], NN).astype(BF16)

    return pl.pallas_call(
        body, grid=(H, T // tq), in_specs=in_specs, out_specs=qs, out_shape=jax.ShapeDtypeStruct((T, W), BF16),
        compiler_params=_params(("parallel", "parallel")), name=name,
    )(*args)


def _attn_bwd(q, k, v, do, scale, name, qpe=None, kpe=None, causal=False):
    T, W = q.shape
    Tk = k.shape[0]
    H = W // HEAD
    tq = min(T, 256)
    nq = T // tq
    has_pe = qpe is not None
    qs = pl.BlockSpec((tq, HEAD), lambda h, i: (i, h))
    ks = pl.BlockSpec((Tk, HEAD), lambda h, i: (0, h))
    in_specs, args = [qs, ks, ks, qs], [q, k, v, do]
    out_specs = [qs, ks, ks]
    out_shape = [jax.ShapeDtypeStruct((T, W), BF16), jax.ShapeDtypeStruct((Tk, W), BF16), jax.ShapeDtypeStruct((Tk, W), BF16)]
    scratch = [pltpu.VMEM((Tk, HEAD), F32), pltpu.VMEM((Tk, HEAD), F32)]
    if has_pe:
        in_specs += [pl.BlockSpec((tq, HEAD), lambda h, i: (i, h // 2)), pl.BlockSpec((Tk, HEAD), lambda h, i: (0, 0))]
        args += [qpe, kpe]
        out_specs += [qs, ks]
        out_shape += [jax.ShapeDtypeStruct((T, W), F32), jax.ShapeDtypeStruct((Tk, W), F32)]
        scratch += [pltpu.VMEM((Tk, HEAD), F32)]
    n_in = len(in_specs)

    def body(*refs):
        q_ref, k_ref, v_ref, do_ref = refs[:4]
        outs = refs[n_in:n_in + len(out_specs)]
        accs = refs[n_in + len(out_specs):]
        dq_ref, dk_ref, dv_ref = outs[:3]
        h, i = pl.program_id(0), pl.program_id(1)
        qp = _pe_mask(refs[4][...], h) if has_pe else None
        kp = refs[5][...] if has_pe else None
        qv, kv, vv, dov = q_ref[...], k_ref[...], v_ref[...], do_ref[...]
        p = _attn_probs(qv, kv, qp, kp, scale, causal, i * tq)
        dp = _dot(dov, vv, NT)
        ds = (p * (dp - jnp.sum(p * dp, axis=-1, keepdims=True)) * scale).astype(BF16)
        dq_ref[...] = _dot(ds, kv, NN).astype(BF16)
        dk_part = _dot(ds, qv, TN)
        dv_part = _dot(p.astype(BF16), dov, TN)
        if has_pe:
            outs[3][...] = _pe_mask(_dot(ds, kp, NN), h)
            dkp_part = _dot(ds, qp, TN)

        @pl.when(i == 0)
        def _():
            accs[0][...] = dk_part
            accs[1][...] = dv_part
            if has_pe:
                accs[2][...] = dkp_part

        @pl.when(i > 0)
        def _():
            accs[0][...] += dk_part
            accs[1][...] += dv_part
            if has_pe:
                accs[2][...] += dkp_part

        @pl.when(i == nq - 1)
        def _():
            dk_ref[...] = accs[0][...].astype(BF16)
            dv_ref[...] = accs[1][...].astype(BF16)
            if has_pe:
                outs[4][...] = accs[2][...]

    return pl.pallas_call(
        body, grid=(H, nq), in_specs=in_specs, out_specs=out_specs, out_shape=out_shape, scratch_shapes=scratch,
        compiler_params=_params(("parallel", "arbitrary")), name=name,
    )(*args)


def _split3(x):
    hi = x.astype(BF16)
    r1 = x - hi.astype(F32)
    mid = r1.astype(BF16)
    lo = (r1 - mid.astype(F32)).astype(BF16)
    return hi, mid, lo


def _tri_dot(tri, x):
    hi, mid, lo = _split3(x)
    return _dot(tri, hi, NN) + _dot(tri, mid, NN) + _dot(tri, lo, NN)


def _hg_gates(u, lb):
    q, fr, v = u[:, 0:HEAD], u[:, HEAD:2 * HEAD], u[:, 2 * HEAD:3 * HEAD]
    sg = 1.0 / (1.0 + jnp.exp(-fr))
    sgm = 1.0 / (1.0 + jnp.exp(fr))
    f = lb + (1.0 - lb) * sg
    kin = (1.0 - lb) * sgm
    sq = _sigmoid(q)
    return q, v, sg, sgm, f, kin, sq, q * sq


def _hg_block_mats(blk):
    t = lax.broadcasted_iota(jnp.int32, (blk, blk), 0)
    s = lax.broadcasted_iota(jnp.int32, (blk, blk), 1)
    same = (t // SUB) == (s // SUB)
    one = lambda m: jnp.where(m, 1.0, 0.0).astype(BF16)
    return one(same & (s <= t)), one(same), one(same & (s >= t))


def _hg_stage(pairs, blk):
    for sc, val in pairs:
        sc[0:SUB, :] = jnp.zeros((SUB, HEAD), F32)
        sc[SUB:SUB + blk, :] = val


def _hg_scan_fwd(u, lb, name):
    T, W = u.shape
    H = W // (4 * HEAD)
    blk = min(T, HG_BLOCK)
    nb, nsb = T // blk, blk // SUB

    def body(u_ref, lb_ref, o_ref, st_ref, state, k_sc, b_sc, v_sc):
        @pl.when(pl.program_id(1) == 0)
        def _():
            state[...] = jnp.zeros_like(state)

        q, v, sg, sgm, f, kin, sq, qin = _hg_gates(u_ref[...], lb_ref[...])
        tri, ones, _ = _hg_block_mats(blk)
        logf = jnp.log(f)
        brel = _tri_dot(tri, logf)
        btot = _tri_dot(ones, logf)
        _hg_stage(((k_sc, kin), (b_sc, brel), (v_sc, v)), blk)
        sub_row = lax.broadcasted_iota(jnp.int32, (blk, HEAD), 0) % SUB
        o = jnp.zeros((blk, HEAD), F32)
        for d in range(SUB):
            win = slice(SUB - d, SUB - d + blk)
            e = jnp.exp(jnp.where(sub_row >= d, brel - b_sc[win, :], NEG))
            o = o + jnp.sum(qin * e * k_sc[win, :], axis=-1, keepdims=True) * v_sc[win, :]
        ab = (qin * jnp.exp(brel)).astype(BF16)
        kdb = (kin * jnp.exp(btot - brel)).astype(BF16)
        vb = v.astype(BF16)
        ebt = jnp.exp(btot)
        st = state[...]
        st_ref[...] = st
        for i in range(nsb):
            sl = slice(i * SUB, (i + 1) * SUB)
            o_ref[sl, :] = o[sl] + _dot(ab[sl], st.astype(BF16), NT)
            st = ebt[i * SUB:i * SUB + 1, :] * st + _dot(vb[sl], kdb[sl], TN)
        state[...] = st

    return pl.pallas_call(
        body, grid=(H, nb),
        in_specs=[pl.BlockSpec((blk, 4 * HEAD), lambda h, c: (c, h)), pl.BlockSpec((1, HEAD), lambda h, c: (0, h))],
        out_specs=[pl.BlockSpec((blk, HEAD), lambda h, c: (c, h)), pl.BlockSpec((None, None, HEAD, HEAD), lambda h, c: (h, c, 0, 0))],
        out_shape=[jax.ShapeDtypeStruct((T, H * HEAD), F32), jax.ShapeDtypeStruct((H, nb, HEAD, HEAD), F32)],
        scratch_shapes=[pltpu.VMEM((HEAD, HEAD), F32)] + [pltpu.VMEM((SUB + blk, HEAD), F32)] * 3,
        compiler_params=_params(("parallel", "arbitrary")), name=name,
    )(u, lb)


def _hg_scan_bwd(u, lb, do, dog, states, name):
    T, W = u.shape
    H = W // (4 * HEAD)
    blk = min(T, HG_BLOCK)
    NC, nsb = T // blk, blk // SUB

    def body(u_ref, lb_ref, do_ref, dog_ref, st_ref, du_ref, dlb_ref, dstate, s_all, k_sc, b_sc, v_sc, dk_sc, dbn_sc,
             dv_sc, da_sc, dkd_sc, dvs_sc, dbt_sc):
        @pl.when(pl.program_id(1) == 0)
        def _():
            dstate[...] = jnp.zeros_like(dstate)
            dlb_ref[...] = jnp.zeros_like(dlb_ref)

        lb = lb_ref[...]
        q, v, sg, sgm, f, kin, sq, qin = _hg_gates(u_ref[...], lb)
        tri, ones, tri_t = _hg_block_mats(blk)
        logf = jnp.log(f)
        brel = _tri_dot(tri, logf)
        btot = _tri_dot(ones, logf)
        eb, ekd, ebt = jnp.exp(brel), jnp.exp(btot - brel), jnp.exp(btot)
        a, kd = qin * eb, kin * ekd
        ab, kdb, vb = a.astype(BF16), kd.astype(BF16), v.astype(BF16)
        do = do_ref[...]
        dob = do.astype(BF16)
        st = st_ref[...]
        for i in range(nsb):
            sl = slice(i * SUB, (i + 1) * SUB)
            s_all[i] = st
            st = ebt[i * SUB:i * SUB + 1, :] * st + _dot(vb[sl], kdb[sl], TN)
        ds = dstate[...]
        for i in reversed(range(nsb)):
            sl = slice(i * SUB, (i + 1) * SUB)
            st_i = s_all[i]
            dsb = ds.astype(BF16)
            e_i = ebt[i * SUB:i * SUB + 1, :]
            da_sc[sl, :] = _dot(dob[sl], st_i.astype(BF16), NN)
            dvs_sc[sl, :] = _dot(kdb[sl], dsb, NT)
            dkd_sc[sl, :] = _dot(vb[sl], dsb, NN)
            dbt_sc[sl, :] = jnp.broadcast_to(jnp.sum(ds * st_i, axis=0, keepdims=True) * e_i, (SUB, HEAD))
            ds = e_i * ds + _dot(dob[sl], ab[sl], TN)
        dstate[...] = ds
        da, dkd = da_sc[...], dkd_sc[...]
        t1 = dkd * kd
        dqin = da * eb
        dbrel = da * a - t1
        dkin = dkd * ekd
        dbtot = dbt_sc[...] + _tri_dot(ones, t1)
        _hg_stage(((k_sc, kin), (b_sc, brel), (v_sc, v)), blk)
        for sc in (dk_sc, dbn_sc, dv_sc):
            sc[...] = jnp.zeros_like(sc)
        sub_row = lax.broadcasted_iota(jnp.int32, (blk, HEAD), 0) % SUB
        for d in range(SUB):
            win = slice(SUB - d, SUB - d + blk)
            ks = k_sc[win, :]
            e = jnp.exp(jnp.where(sub_row >= d, brel - b_sc[win, :], NEG))
            qe = qin * e
            col = jnp.sum(qe * ks, axis=-1, keepdims=True)
            dcol = jnp.sum(do * v_sc[win, :], axis=-1, keepdims=True)
            dqe = dcol * qe
            g = dqe * ks
            dqin = dqin + dcol * (e * ks)
            dbrel = dbrel + g
            dk_sc[win, :] += dqe
            dbn_sc[win, :] += g
            dv_sc[win, :] += col * do
        dkin = dkin + dk_sc[SUB:SUB + blk, :]
        dbrel = dbrel - dbn_sc[SUB:SUB + blk, :]
        dv = dvs_sc[...] + dv_sc[SUB:SUB + blk, :]
        dlogf = _tri_dot(tri_t, dbrel) + dbtot
        diff = dlogf / f - dkin
        dlb_ref[...] += jnp.sum(sgm * diff, axis=0, keepdims=True)
        du_ref[:, 0:HEAD] = (dqin * (sq * (1.0 + q * (1.0 - sq)))).astype(BF16)
        du_ref[:, HEAD:2 * HEAD] = ((1.0 - lb) * sg * sgm * diff).astype(BF16)
        du_ref[:, 2 * HEAD:3 * HEAD] = dv.astype(BF16)
        du_ref[:, 3 * HEAD:4 * HEAD] = dog_ref[...]

    rev = lambda h, c: (NC - 1 - c, h)
    return pl.pallas_call(
        body, grid=(H, NC),
        in_specs=[
            pl.BlockSpec((blk, 4 * HEAD), rev), pl.BlockSpec((1, HEAD), lambda h, c: (0, h)),
            pl.BlockSpec((blk, HEAD), rev), pl.BlockSpec((blk, HEAD), rev),
            pl.BlockSpec((None, None, HEAD, HEAD), lambda h, c: (h, NC - 1 - c, 0, 0)),
        ],
        out_specs=[pl.BlockSpec((blk, 4 * HEAD), rev), pl.BlockSpec((1, HEAD), lambda h, c: (0, h))],
        out_shape=[jax.ShapeDtypeStruct((T, W), BF16), jax.ShapeDtypeStruct((1, H * HEAD), F32)],
        scratch_shapes=[pltpu.VMEM((HEAD, HEAD), F32), pltpu.VMEM((nsb, HEAD, HEAD), F32)]
        + [pltpu.VMEM((SUB + blk, HEAD), F32)] * 6 + [pltpu.VMEM((blk, HEAD), F32)] * 4,
        compiler_params=_params(("parallel", "arbitrary")), name=name,
    )(u, lb, do, dog, states)


def _hg_tail_fwd(o_raw, u, g, name):
    T, D = o_raw.shape
    H = D // HEAD
    tr = min(T, 512)
    blk = pl.BlockSpec((tr, HEAD), lambda h, i: (i, h))

    def body(o_ref, og_ref, g_ref, out_ref):
        o, og = o_ref[...], og_ref[...]
        out_ref[...] = (o * _rstd(o) * g_ref[...] * (og * _sigmoid(og))).astype(BF16)

    return pl.pallas_call(
        body, grid=(H, T // tr),
        in_specs=[blk, pl.BlockSpec((tr, HEAD), lambda h, i: (i, 4 * h + 3)), pl.BlockSpec((1, HEAD), lambda h, i: (0, h))],
        out_specs=blk, out_shape=jax.ShapeDtypeStruct((T, D), BF16),
        compiler_params=_params(("parallel", "parallel")), name=name,
    )(o_raw, u, g)


def _hg_tail_bwd(o_raw, u, g, doa, name):
    T, D = o_raw.shape
    H = D // HEAD
    tr = min(T, 512)
    blk = pl.BlockSpec((tr, HEAD), lambda h, i: (i, h))
    vec = pl.BlockSpec((1, HEAD), lambda h, i: (0, h))

    def body(o_ref, og_ref, g_ref, doa_ref, do_ref, dog_ref, dg_ref):
        @pl.when(pl.program_id(1) == 0)
        def _():
            dg_ref[...] = jnp.zeros_like(dg_ref)

        o, og, doa, g = o_ref[...], og_ref[...], doa_ref[...], g_ref[...]
        sg = _sigmoid(og)
        r = _rstd(o)
        xhat = o * r
        dog_ref[...] = (doa * (xhat * g) * (sg * (1.0 + og * (1.0 - sg)))).astype(BF16)
        dn = doa * (og * sg)
        dg_ref[...] += jnp.sum(dn * xhat, axis=0, keepdims=True)
        dxh = dn * g
        do_ref[...] = r * (dxh - xhat * jnp.mean(dxh * xhat, axis=-1, keepdims=True))

    return pl.pallas_call(
        body, grid=(H, T // tr),
        in_specs=[blk, pl.BlockSpec((tr, HEAD), lambda h, i: (i, 4 * h + 3)), vec, blk],
        out_specs=[blk, blk, vec],
        out_shape=[jax.ShapeDtypeStruct((T, D), F32), jax.ShapeDtypeStruct((T, D), BF16), jax.ShapeDtypeStruct((1, D), F32)],
        compiler_params=_params(("parallel", "arbitrary")), name=name,
    )(o_raw, u, g, doa)


def _lb_fwd(logits, name):
    def body(l_ref, o_ref):
        l0, l1 = l_ref[0:1, :], l_ref[1:2, :]
        m = jnp.maximum(l0, l1)
        e0, e1 = jnp.exp(l0 - m), jnp.exp(l1 - m)
        o_ref[...] = e0 / (e0 + e1)

    D = logits.shape[1]
    return pl.pallas_call(body, out_shape=jax.ShapeDtypeStruct((1, D), F32), name=name)(logits)


def _lb_bwd(lb, dlb, name):
    def body(lb_ref, d_ref, o_ref):
        lb = lb_ref[...]
        d0 = d_ref[...] * lb * (1.0 - lb)
        o_ref[0:1, :] = d0
        o_ref[1:2, :] = -d0

    D = lb.shape[1]
    return pl.pallas_call(body, out_shape=jax.ShapeDtypeStruct((2, D), F32), name=name)(lb, dlb)


def _slots(w):
    return w.shape[0] if w.ndim == 3 else 0


def _ffn_fwd(x, p, tag):
    hb = _norm_fwd(x, p["pre_g"], f"{tag}_pre_norm")
    a = _mm(hb, p["w_gate"], "nn", F32, f"{tag}_gate")
    b = _mm(hb, p["w_up"], "nn", F32, f"{tag}_up")
    sb = _swiglu_fwd(a, b, f"{tag}_swiglu")
    y = _mm(sb, p["w_down"], "nn", F32, f"{tag}_down")
    xo = _postnorm_fwd(x, y, p["post_g"], FFN_RESIDUAL_WEIGHT, f"{tag}_post_norm")
    return xo, (x, hb, a, b, sb, y)


def _ffn_bwd(dxo, p, saved, tag):
    x, hb, a, b, sb, y = saved
    dyb, dpost = _norm_bwd(y, p["post_g"], dxo, FFN_RESIDUAL_WEIGHT, BF16, f"{tag}_post_norm_bwd")
    dw_down = _mm(sb, dyb, "tn", BF16, f"{tag}_down_dw")
    ds = _mm(dyb, p["w_down"], "nt", F32, f"{tag}_down_dx")
    dab, dbb = _swiglu_bwd(a, b, ds, f"{tag}_swiglu_bwd")
    dw_gate = _mm(hb, dab, "tn", BF16, f"{tag}_gate_dw", out_slots=_slots(p["w_gate"]))
    dw_up = _mm(hb, dbb, "tn", BF16, f"{tag}_up_dw", out_slots=_slots(p["w_up"]))
    dh = _mm(dab, p["w_gate"], "nt", F32, f"{tag}_gate_dx")
    dh = _mm(dbb, p["w_up"], "nt", F32, f"{tag}_up_dx", add=dh)
    dx, dpre = _norm_bwd(x, p["pre_g"], dh, 1.0, F32, f"{tag}_pre_norm_bwd", res=dxo)
    return dx, {"pre_g": dpre, "w_gate": dw_gate, "w_up": dw_up, "w_down": dw_down, "post_g": dpost}


def _mixer_fwd(x, cos, sin, p):
    scale = (HEAD + ROPE) ** -0.5
    hb = _norm_fwd(x, p["pre_g"], "mix_pre_norm")
    u = _mm(hb, p["w_hg"], "nn", F32, "mix_in_hg")
    cq = _mm(hb, p["w_cq"], "nn", F32, "mix_in_cq")
    ckv = _mm(hb, p["w_ckv"], "nn", F32, "mix_in_ckv")
    kpe = _mm(hb, p["w_kpe"], "nn", F32, "mix_in_kpe")
    ga = _mm(hb, p["w_ga"], "nn", F32, "mix_in_ga")
    gb = _mm(hb, p["w_gb"], "nn", F32, "mix_in_gb")
    lb = _lb_fwd(p["lb_logits"], "hg_lb")
    o_raw, states = _hg_scan_fwd(u, lb, "hg_scan")
    oa = _hg_tail_fwd(o_raw, u, p["hg_norm_g"], "hg_tail")
    ya = _mm(oa, p["w_branch_a"], "nn", F32, "mix_branch_a")
    cqn = _norm_fwd(cq, p["q_norm_g"], "mla_q_norm")
    qn = _mm(cqn, p["w_qn"], "nn", BF16, "mla_q_up_nope")
    qpe = _rope_q_fwd(_mm(cqn, p["w_qpe"], "nn", F32, "mla_q_up_pe"), cos, sin, "mla_rope_q")
    ckvn = _norm_fwd(ckv, p["kv_norm_g"], "mla_kv_norm")
    kn = _mm(ckvn, p["w_kn"], "nn", BF16, "mla_k_up")
    vv = _mm(ckvn, p["w_vv"], "nn", BF16, "mla_v_up")
    kpe2 = _rope_k_fwd(kpe, cos, sin, "mla_rope_k")
    ob = _attn_fwd(qn, kn, vv, scale, "mla_attn", qpe=qpe, kpe=kpe2, causal=True)
    yb = _mm(ob, p["w_branch_b"], "nn", F32, "mix_branch_b")
    ym = _merge_fwd(ga, gb, ya, yb, "mix_merge")
    z = _mm(ym, p["w_out"], "nn", F32, "mix_out")
    xo = _postnorm_fwd(x, z, p["post_g"], 1.0, "mix_post_norm")
    saved = (x, hb, u, cq, ckv, ga, gb, lb, o_raw, states, oa, ya, cqn, qn, qpe, ckvn, kn, vv, kpe2, ob, yb, ym, z)
    return xo, saved


def _mixer_bwd(dxo, cos, sin, p, saved):
    x, hb, u, cq, ckv, ga, gb, lb, o_raw, states, oa, ya, cqn, qn, qpe, ckvn, kn, vv, kpe2, ob, yb, ym, z = saved
    scale = (HEAD + ROPE) ** -0.5
    g = {}
    dzb, g["post_g"] = _norm_bwd(z, p["post_g"], dxo, 1.0, BF16, "mix_post_norm_bwd")
    g["w_out"] = _mm(ym, dzb, "tn", BF16, "mix_out_dw")
    dym = _mm(dzb, p["w_out"], "nt", F32, "mix_out_dx")
    dya, dyb, dga, dgb = _merge_bwd(ga, gb, ya, yb, dym, "mix_merge_bwd")
    g["w_branch_a"] = _mm(oa, dya, "tn", BF16, "mix_branch_a_dw")
    doa = _mm(dya, p["w_branch_a"], "nt", F32, "mix_branch_a_dx")
    do_raw, dog, g["hg_norm_g"] = _hg_tail_bwd(o_raw, u, p["hg_norm_g"], doa, "hg_tail_bwd")
    du, dlb = _hg_scan_bwd(u, lb, do_raw, dog, states, "hg_scan_bwd")
    g["lb_logits"] = _lb_bwd(lb, dlb, "hg_lb_bwd")
    g["w_branch_b"] = _mm(ob, dyb, "tn", BF16, "mix_branch_b_dw")
    dob = _mm(dyb, p["w_branch_b"], "nt", BF16, "mix_branch_b_dx")
    dqn, dkn, dvv, dqpe_h, dkpe_h = _attn_bwd(qn, kn, vv, dob, scale, "mla_attn_bwd", qpe=qpe, kpe=kpe2, causal=True)
    dqpe = _rope_q_bwd(dqpe_h, cos, sin, "mla_rope_q_bwd")
    dkpe = _rope_k_bwd(dkpe_h, cos, sin, "mla_rope_k_bwd")
    g["w_qn"] = _mm(cqn, dqn, "tn", BF16, "mla_q_up_nope_dw")
    g["w_qpe"] = _mm(cqn, dqpe, "tn", BF16, "mla_q_up_pe_dw")
    dcqn = _mm(dqn, p["w_qn"], "nt", F32, "mla_q_up_nope_dx")
    dcqn = _mm(dqpe, p["w_qpe"], "nt", F32, "mla_q_up_pe_dx", add=dcqn)
    dcq, g["q_norm_g"] = _norm_bwd(cq, p["q_norm_g"], dcqn, 1.0, BF16, "mla_q_norm_bwd")
    g["w_kn"] = _mm(ckvn, dkn, "tn", BF16, "mla_k_up_dw")
    g["w_vv"] = _mm(ckvn, dvv, "tn", BF16, "mla_v_up_dw")
    dckvn = _mm(dkn, p["w_kn"], "nt", F32, "mla_k_up_dx")
    dckvn = _mm(dvv, p["w_vv"], "nt", F32, "mla_v_up_dx", add=dckvn)
    dckv, g["kv_norm_g"] = _norm_bwd(ckv, p["kv_norm_g"], dckvn, 1.0, BF16, "mla_kv_norm_bwd")
    dh = None
    for key, d in (("w_hg", du), ("w_cq", dcq), ("w_ckv", dckv), ("w_kpe", dkpe), ("w_ga", dga), ("w_gb", dgb)):
        g[key] = _mm(hb, d, "tn", BF16, f"mix_in_{key}_dw")
        dh = _mm(d, p[key], "nt", F32, f"mix_in_{key}_dx", add=dh)
    dx, g["pre_g"] = _norm_bwd(x, p["pre_g"], dh, 1.0, F32, "mix_pre_norm_bwd", res=dxo)
    return dx, g


def _xa_fwd(x, mem, p):
    scale = HEAD ** -0.5
    hb = _norm_fwd(x, p["pre_g"], "xa_pre_norm")
    mb = _norm_fwd(mem, p["mem_g"], "xa_mem_norm")
    q = _mm(hb, p["w_q"], "nn", BF16, "xa_q")
    k = _mm(mb, p["w_k"], "nn", BF16, "xa_k")
    v = _mm(mb, p["w_v"], "nn", BF16, "xa_v")
    o = _attn_fwd(q, k, v, scale, "xa_attn")
    z = _mm(o, p["w_o"], "nn", F32, "xa_o")
    xo = _postnorm_fwd(x, z, p["post_g"], 1.0, "xa_post_norm")
    return xo, (x, mem, hb, mb, q, k, v, o, z)


def _xa_bwd(dxo, p, saved):
    x, mem, hb, mb, q, k, v, o, z = saved
    scale = HEAD ** -0.5
    g = {}
    dzb, g["post_g"] = _norm_bwd(z, p["post_g"], dxo, 1.0, BF16, "xa_post_norm_bwd")
    g["w_o"] = _mm(o, dzb, "tn", BF16, "xa_o_dw", out_slots=_slots(p["w_o"]))
    do = _mm(dzb, p["w_o"], "nt", BF16, "xa_o_dx")
    dq, dk, dv = _attn_bwd(q, k, v, do, scale, "xa_attn_bwd")
    g["w_q"] = _mm(hb, dq, "tn", BF16, "xa_q_dw")
    g["w_k"] = _mm(mb, dk, "tn", BF16, "xa_k_dw")
    g["w_v"] = _mm(mb, dv, "tn", BF16, "xa_v_dw")
    dh = _mm(dq, p["w_q"], "nt", F32, "xa_q_dx")
    dm = _mm(dk, p["w_k"], "nt", F32, "xa_k_dx")
    dm = _mm(dv, p["w_v"], "nt", F32, "xa_v_dx", add=dm)
    _, g["mem_g"] = _norm_bwd(mem, p["mem_g"], dm, 1.0, BF16, "xa_mem_norm_bwd")
    dx, g["pre_g"] = _norm_bwd(x, p["pre_g"], dh, 1.0, F32, "xa_pre_norm_bwd", res=dxo)
    return dx, g


def _local_step(x, mem, cos, sin, target, params):
    x1, s1 = _ffn_fwd(x, params["ffn1"], "ffn1")
    x2, s2 = _mixer_fwd(x1, cos, sin, params["mix"])
    x3, s3 = _xa_fwd(x2, mem, params["xa"])
    x4, s4 = _ffn_fwd(x3, params["ffn2"], "ffn2")
    dy, sq_err = _loss_head(x4, target, "loss_head")
    loss = 0.5 / x.shape[1] * jnp.sum(sq_err)
    dx, g4 = _ffn_bwd(dy, params["ffn2"], s4, "ffn2")
    dx, g3 = _xa_bwd(dx, params["xa"], s3)
    dx, g2 = _mixer_bwd(dx, cos, sin, params["mix"], s2)
    dx, g1 = _ffn_bwd(dx, params["ffn1"], s1, "ffn1")
    return loss, dx, {"ffn1": g1, "mix": g2, "xa": g3, "ffn2": g4}


def _split_w_in(w_in):
    D = w_in.shape[0]
    H = D // HEAD
    lora = (w_in.shape[1] - 6 * D - ROPE) // 2
    o = 4 * D
    w_hg = w_in[:, :o].reshape(D, 4, H, HEAD).transpose(0, 2, 1, 3).reshape(D, 4 * D)
    w_cq, w_ckv = w_in[:, o:o + lora], w_in[:, o + lora:o + 2 * lora]
    o += 2 * lora
    w_kpe = jnp.pad(w_in[:, o:o + ROPE], ((0, 0), (0, HEAD - ROPE)))
    o += ROPE
    return {"w_hg": w_hg, "w_cq": w_cq, "w_ckv": w_ckv, "w_kpe": w_kpe, "w_ga": w_in[:, o:o + D], "w_gb": w_in[:, o + D:o + 2 * D]}


def _merge_w_in(g):
    D = g["w_ga"].shape[0]
    H = D // HEAD
    hg = g["w_hg"].reshape(D, H, 4, HEAD).transpose(0, 2, 1, 3).reshape(D, 4 * D)
    return jnp.concatenate([hg, g["w_cq"], g["w_ckv"], g["w_kpe"][:, :ROPE], g["w_ga"], g["w_gb"]], axis=1)


def _split_heads(w, rest):
    K, N = w.shape
    w3 = w.reshape(K, N // (HEAD + rest), HEAD + rest)
    return w3[:, :, :HEAD].reshape(K, -1), w3[:, :, HEAD:].reshape(K, -1)


def _merge_heads(a, b, rest):
    K = a.shape[0]
    H = a.shape[1] // HEAD
    return jnp.concatenate([a.reshape(K, H, HEAD), b.reshape(K, H, rest)], axis=2).reshape(K, H * (HEAD + rest))


def _layer_params(w, small):
    ffn = lambda t: {"pre_g": small[f"{t}_pre_g"], "w_gate": w[f"{t}_w_gate"], "w_up": w[f"{t}_w_up"],
                     "w_down": w[f"{t}_w_down"], "post_g": small[f"{t}_post_g"]}
    mix = _split_w_in(w["w_in"])
    mix["w_qn"], mix["w_qpe"] = _split_heads(w["mla_w_q_up"], ROPE)
    mix["w_kn"], mix["w_vv"] = _split_heads(w["mla_w_kv_up"], HEAD)
    mix.update(w_branch_a=w["w_branch_a"], w_branch_b=w["w_branch_b"], w_out=w["w_out"], pre_g=small["mix_pre_g"],
               post_g=small["mix_post_g"], hg_norm_g=small["hg_norm_g"], q_norm_g=small["mla_q_norm_g"],
               kv_norm_g=small["mla_kv_norm_g"], lb_logits=small["hgrn_lb_logits"])
    xa = {"pre_g": small["xa_pre_g"], "mem_g": small["xa_mem_g"], "post_g": small["xa_post_g"],
          "w_q": w["xa_w_q"], "w_k": w["xa_w_k"], "w_v": w["xa_w_v"], "w_o": w["xa_w_o"]}
    return {"ffn1": ffn("ffn1"), "mix": mix, "xa": xa, "ffn2": ffn("ffn2")}


def _natural_grads(g):
    out = {}
    for t in ("ffn1", "ffn2"):
        for k in ("pre_g", "w_gate", "w_up", "w_down", "post_g"):
            out[f"{t}_{k}"] = g[t][k]
    m = g["mix"]
    out["w_in"] = _merge_w_in(m)
    out["mla_w_q_up"] = _merge_heads(m["w_qn"], m["w_qpe"], ROPE)
    out["mla_w_kv_up"] = _merge_heads(m["w_kn"], m["w_vv"], HEAD)
    out.update(w_branch_a=m["w_branch_a"], w_branch_b=m["w_branch_b"], w_out=m["w_out"], mix_pre_g=m["pre_g"],
               mix_post_g=m["post_g"], hg_norm_g=m["hg_norm_g"], mla_q_norm_g=m["q_norm_g"],
               mla_kv_norm_g=m["kv_norm_g"], hgrn_lb_logits=m["lb_logits"])
    x = g["xa"]
    out.update(xa_pre_g=x["pre_g"], xa_mem_g=x["mem_g"], xa_post_g=x["post_g"], xa_w_q=x["w_q"], xa_w_k=x["w_k"],
               xa_w_v=x["w_v"], xa_w_o=x["w_o"])
    return out


def _rope_tables(positions):
    inv_freq = 1.0 / (ROPE_THETA ** (jnp.arange(0, ROPE, 2, dtype=F32) / ROPE))
    ang = positions.astype(F32)[:, None] * inv_freq
    return jnp.tile(jnp.cos(ang), (1, 4)), jnp.tile(jnp.sin(ang), (1, 4))


def _adamw(w, g, m, v, name):
    bc1 = 1.0 - ADAM_B1 ** ADAM_STEP
    bc2 = 1.0 - ADAM_B2 ** ADAM_STEP

    def body(w_ref, g_ref, m_ref, v_ref, d_ref, mo_ref, vo_ref):
        g = g_ref[...]
        m = ADAM_B1 * m_ref[...] + (1.0 - ADAM_B1) * g
        v = ADAM_B2 * v_ref[...] + (1.0 - ADAM_B2) * (g * g)
        mo_ref[...] = m
        vo_ref[...] = v
        d_ref[...] = -ADAM_LR * ((m / bc1) / (jnp.sqrt(v / bc2) + ADAM_EPS) + ADAM_WD * w_ref[...])

    return _rows(body, [(w, "row"), (g, "row"), (m, "row"), (v, "row")], [(w.shape, F32, "row")] * 3, name)


ANY = pl.BlockSpec(memory_space=pl.ANY)
COMM_AXES = ("x", "y", "c")


def _place():
    x, y, c = (lax.axis_index(n) for n in COMM_AXES)
    chips = [(1 - x, y), (x, 1 - y), (1 - x, 1 - y)]
    return x, y, c, 2 * x + y, (x, y, 1 - c), chips


def _remote(src, dst, send_sems, recv_sems, j, to):
    return pltpu.make_async_remote_copy(src_ref=src, dst_ref=dst, send_sem=send_sems.at[j], recv_sem=recv_sems.at[j],
                                        device_id=to, device_id_type=MESH)


def _dma_sems(n):
    return [pltpu.SemaphoreType.DMA((n,)), pltpu.SemaphoreType.DMA((n,))]


def _all_gather(shards, name):
    n = len(shards)

    def body(*refs):
        srcs, outs, send_sems, recv_sems = refs[:n], refs[n:2 * n], refs[2 * n], refs[2 * n + 1]
        x, y, c, me, sibling, chips = _place()
        sent = []

        def start(cp):
            cp.start()
            sent.append(cp)

        def rows(w, h):
            hr = srcs[w].shape[0] // 2
            return pl.ds(h * hr, hr)

        for w in range(n):
            start(_remote(srcs[w], outs[w].at[me], send_sems, recv_sems, 7 * w + 6, sibling))
            for j, (cx, cy) in enumerate(chips):
                start(_remote(srcs[w].at[rows(w, c)], outs[w].at[me, rows(w, c)], send_sems, recv_sems, 7 * w + j, (cx, cy, c)))
        for w in range(n):
            for j, (cx, cy) in enumerate(chips):
                blk = outs[w].at[2 * cx + cy, rows(w, c)]
                _remote(srcs[w].at[rows(w, c)], blk, send_sems, recv_sems, 7 * w + j, (cx, cy, c)).wait_recv()
                start(_remote(blk, blk, send_sems, recv_sems, 7 * w + 3 + j, sibling))
        for w in range(n):
            for j, (cx, cy) in enumerate(chips):
                blk = outs[w].at[2 * cx + cy, rows(w, 1 - c)]
                _remote(blk, blk, send_sems, recv_sems, 7 * w + 3 + j, sibling).wait_recv()
            _remote(srcs[w], outs[w].at[me], send_sems, recv_sems, 7 * w + 6, sibling).wait_recv()
        for cp in sent:
            cp.wait_send()

    return pl.pallas_call(
        body, in_specs=[ANY] * n, out_specs=[ANY] * n,
        out_shape=[jax.ShapeDtypeStruct((N_CHIP,) + s.shape, s.dtype) for s in shards],
        scratch_shapes=_dma_sems(7 * n), name=name,
    )(*shards)


def _rs_swap(grads, name):
    n = len(grads)

    def body(*refs):
        gs, outs, send_sems, recv_sems = refs[:n], refs[n:2 * n], refs[2 * n], refs[2 * n + 1]
        x, y, c, me, sibling, chips = _place()
        cps = []
        for w in range(n):
            hr = gs[w].shape[1] // 2
            cps.append(_remote(gs[w].at[:, pl.ds((1 - c) * hr, hr)], outs[w], send_sems, recv_sems, w, sibling))
            cps[-1].start()
        for cp in cps:
            cp.wait()

    return pl.pallas_call(
        body, in_specs=[ANY] * n, out_specs=[ANY] * n,
        out_shape=[jax.ShapeDtypeStruct((g.shape[0], g.shape[1] // 2, g.shape[2]), g.dtype) for g in grads],
        scratch_shapes=_dma_sems(n), name=name,
    )(*grads)


def _sum_rows(hr, row_bytes):
    return _tile(hr, max(16, ROW_BUDGET // (2 * row_bytes) // 16 * 16), 16)


def _rs_pair_sum(g, got, c, name):
    S, r, cw = g.shape
    hr = r // 2
    tr = _sum_rows(hr, 3 * cw * 2)
    nrb = hr // tr

    def body(c_ref, a_ref, b_ref, o_ref):
        o_ref[...] = (a_ref[...].astype(F32) + b_ref[...].astype(F32)).astype(BF16)

    return pl.pallas_call(
        body,
        grid_spec=pltpu.PrefetchScalarGridSpec(
            num_scalar_prefetch=1, grid=(S, nrb),
            in_specs=[pl.BlockSpec((None, tr, cw), lambda k, i, c_ref: (k, c_ref[0] * nrb + i, 0)),
                      pl.BlockSpec((None, tr, cw), lambda k, i, c_ref: (k, i, 0))],
            out_specs=pl.BlockSpec((None, tr, cw), lambda k, i, c_ref: (k, i, 0)),
        ),
        out_shape=jax.ShapeDtypeStruct((S, hr, cw), BF16),
        compiler_params=_params(("parallel", "parallel")), name=name,
    )(c, g, got)


def _rs_exchange(pairs, name):
    n = len(pairs)

    def body(*refs):
        srcs, outs, send_sems, recv_sems = refs[:n], refs[n:2 * n], refs[2 * n], refs[2 * n + 1]
        x, y, c, me, sibling, chips = _place()
        cps = []
        for w in range(n):
            for j, (cx, cy) in enumerate(chips):
                cps.append(_remote(srcs[w].at[2 * cx + cy], outs[w].at[j], send_sems, recv_sems, 3 * w + j, (cx, cy, c)))
                cps[-1].start()
        for cp in cps:
            cp.wait()

    return pl.pallas_call(
        body, in_specs=[ANY] * n, out_specs=[ANY] * n,
        out_shape=[jax.ShapeDtypeStruct((3,) + p.shape[1:], p.dtype) for p in pairs],
        scratch_shapes=_dma_sems(3 * n), name=name,
    )(*pairs)


def _rs_chip_sum(pair, got, place, name):
    S, hr, cw = pair.shape
    tr = _sum_rows(hr, cw * (4 * 2 + 4))

    def body(p_ref, a_ref, z_ref, o_ref):
        o_ref[...] = a_ref[...].astype(F32) + z_ref[0].astype(F32) + z_ref[1].astype(F32) + z_ref[2].astype(F32)

    return pl.pallas_call(
        body,
        grid_spec=pltpu.PrefetchScalarGridSpec(
            num_scalar_prefetch=1, grid=(hr // tr,),
            in_specs=[pl.BlockSpec((None, tr, cw), lambda i, p_ref: (p_ref[0], i, 0)),
                      pl.BlockSpec((3, tr, cw), lambda i, p_ref: (0, i, 0))],
            out_specs=pl.BlockSpec((None, tr, cw), lambda i, p_ref: (p_ref[1], i, 0)),
        ),
        out_shape=jax.ShapeDtypeStruct((2, hr, cw), F32),
        compiler_params=_params(("parallel",)), name=name,
    )(place, pair, got)


def _rs_share(halves, name):
    n = len(halves)

    def body(*refs):
        outs, send_sems, recv_sems = refs[n:2 * n], refs[2 * n], refs[2 * n + 1]
        x, y, c, me, sibling, chips = _place()
        cps = []
        for w in range(n):
            cps.append(_remote(outs[w].at[c], outs[w].at[c], send_sems, recv_sems, w, sibling))
            cps[-1].start()
        for w in range(n):
            _remote(outs[w].at[1 - c], outs[w].at[1 - c], send_sems, recv_sems, w, sibling).wait_recv()
        for cp in cps:
            cp.wait_send()

    return pl.pallas_call(
        body, in_specs=[ANY] * n, out_specs=[ANY] * n,
        out_shape=[jax.ShapeDtypeStruct(h.shape, h.dtype) for h in halves],
        input_output_aliases={i: i for i in range(n)},
        scratch_shapes=_dma_sems(n), name=name,
    )(*halves)


def _all_reduce_small(s, name):
    flips = [(dx, dy, dc) for dx in (0, 1) for dy in (0, 1) for dc in (0, 1) if (dx, dy, dc) != (0, 0, 0)]

    def body(s_ref, o_ref, buf, send_sems, recv_sems):
        x, y, c = (lax.axis_index(n) for n in COMM_AXES)
        me = 4 * x + 2 * y + c
        buf[me] = s_ref[...]
        peers = [((1 - x) if dx else x, (1 - y) if dy else y, (1 - c) if dc else c) for dx, dy, dc in flips]
        sent = [_remote(s_ref, buf.at[me], send_sems, recv_sems, j, p) for j, p in enumerate(peers)]
        for cp in sent:
            cp.start()
        for j, (px, py, pc) in enumerate(peers):
            _remote(s_ref, buf.at[4 * px + 2 * py + pc], send_sems, recv_sems, j, (px, py, pc)).wait_recv()
        for cp in sent:
            cp.wait_send()
        acc = buf[0]
        for d in range(1, N_DEV):
            acc = acc + buf[d]
        o_ref[...] = acc

    vmem = pl.BlockSpec(memory_space=pltpu.VMEM)
    return pl.pallas_call(
        body, in_specs=[vmem], out_specs=vmem, out_shape=jax.ShapeDtypeStruct(s.shape, F32),
        scratch_shapes=[pltpu.VMEM((N_DEV,) + s.shape, F32), pltpu.SemaphoreType.DMA((7,)), pltpu.SemaphoreType.DMA((7,))],
        name=name,
    )(s)


BIG = {
    "ffn1_w_gate": 1, "ffn1_w_up": 1, "ffn1_w_down": 0, "w_in": 1, "mla_w_q_up": 1, "mla_w_kv_up": 1,
    "w_branch_a": 0, "w_branch_b": 0, "w_out": 0, "xa_w_q": 0, "xa_w_k": 0, "xa_w_v": 0, "xa_w_o": 1,
    "ffn2_w_gate": 1, "ffn2_w_up": 1, "ffn2_w_down": 0,
}
WEIGHTS = [
    "hgrn_lb_logits", "ffn1_pre_g", "ffn1_w_gate", "ffn1_w_up", "ffn1_w_down", "ffn1_post_g", "mix_pre_g", "w_in",
    "hg_norm_g", "mla_q_norm_g", "mla_w_q_up", "mla_kv_norm_g", "mla_w_kv_up", "w_branch_a", "w_branch_b", "w_out",
    "mix_post_g", "xa_pre_g", "xa_mem_g", "xa_w_q", "xa_w_k", "xa_w_v", "xa_w_o", "xa_post_g", "ffn2_pre_g",
    "ffn2_w_gate", "ffn2_w_up", "ffn2_w_down", "ffn2_post_g",
]
SMALL = [n for n in WEIGHTS if n not in BIG]
SLOTTED = ("ffn1_w_gate", "ffn1_w_up", "ffn2_w_gate", "ffn2_w_up", "xa_w_o")


def _from_slots(name, g):
    S, r, cw = g.shape
    if BIG[name] == 0:
        return g.reshape(S * r, cw)
    return g if name in SLOTTED else g.transpose(1, 0, 2).reshape(r, S * cw)


def _to_slots(name, g):
    if g.ndim == 3:
        return g
    if BIG[name] == 0:
        return g.reshape(N_CHIP, g.shape[0] // N_CHIP, g.shape[1])
    return g.reshape(g.shape[0], N_CHIP, g.shape[1] // N_CHIP).transpose(1, 0, 2)


def _pack_small(vals, width):
    rows = [jnp.pad(v, ((0, 0), (0, width - v.shape[1]))) for v in vals]
    s = jnp.concatenate(rows, axis=0)
    return jnp.pad(s, ((0, -s.shape[0] % 8), (0, 0)))


def _unpack_small(s, shapes):
    out, o = [], 0
    for r, w in shapes:
        out.append(s[o:o + r, :w])
        o += r
    return out


def kernel(x, mem, positions, hgrn_lb_logits, ffn1_pre_g, ffn1_w_gate, ffn1_w_up, ffn1_w_down, ffn1_post_g, mix_pre_g, w_in, hg_norm_g, mla_q_norm_g, mla_w_q_up, mla_kv_norm_g, mla_w_kv_up, w_branch_a, w_branch_b, w_out, mix_post_g, xa_pre_g, xa_mem_g, xa_w_q, xa_w_k, xa_w_v, xa_w_o, xa_post_g, ffn2_pre_g, ffn2_w_gate, ffn2_w_up, ffn2_w_down, ffn2_post_g, loss_target, m_hgrn_lb_logits, m_ffn1_pre_g, m_ffn1_w_gate, m_ffn1_w_up, m_ffn1_w_down, m_ffn1_post_g, m_mix_pre_g, m_w_in, m_hg_norm_g, m_mla_q_norm_g, m_mla_w_q_up, m_mla_kv_norm_g, m_mla_w_kv_up, m_w_branch_a, m_w_branch_b, m_w_out, m_mix_post_g, m_xa_pre_g, m_xa_mem_g, m_xa_w_q, m_xa_w_k, m_xa_w_v, m_xa_w_o, m_xa_post_g, m_ffn2_pre_g, m_ffn2_w_gate, m_ffn2_w_up, m_ffn2_w_down, m_ffn2_post_g, v_hgrn_lb_logits, v_ffn1_pre_g, v_ffn1_w_gate, v_ffn1_w_up, v_ffn1_w_down, v_ffn1_post_g, v_mix_pre_g, v_w_in, v_hg_norm_g, v_mla_q_norm_g, v_mla_w_q_up, v_mla_kv_norm_g, v_mla_w_kv_up, v_w_branch_a, v_w_branch_b, v_w_out, v_mix_post_g, v_xa_pre_g, v_xa_mem_g, v_xa_w_q, v_xa_w_k, v_xa_w_v, v_xa_w_o, v_xa_post_g, v_ffn2_pre_g, v_ffn2_w_gate, v_ffn2_w_up, v_ffn2_w_down, v_ffn2_post_g):
    a = dict(locals())
    big = list(BIG)

    gathered = _all_gather([a[n][0].astype(BF16) for n in big], "weights_all_gather")
    full = {n: _from_slots(n, g) for n, g in zip(big, gathered)}

    params = _layer_params(full, {n: a[n] for n in SMALL})
    cos, sin = _rope_tables(positions[0])
    loss_part, grad_x, g = _local_step(x[0], mem[0], cos, sin, loss_target[0], params)
    grads = _natural_grads(g)

    core = lax.axis_index("c").astype(jnp.int32)
    chip = (2 * lax.axis_index("x") + lax.axis_index("y")).astype(jnp.int32)
    slotted = [_to_slots(n, grads[n]) for n in big]
    got = _rs_swap(slotted, "grads_sibling_swap")
    pairs = [_rs_pair_sum(s, t, core.reshape(1), f"grads_pair_sum_{n}") for n, s, t in zip(big, slotted, got)]
    others = _rs_exchange(pairs, "grads_chip_exchange")
    place = jnp.stack([chip, core])
    halves = [_rs_chip_sum(p, o, place, f"grads_chip_sum_{n}") for n, p, o in zip(big, pairs, others)]
    both = _rs_share(halves, "grads_sibling_share")
    g_big = {n: b.reshape(a[n].shape[1:]) for n, b in zip(big, both)}

    small_shapes = [a[n].shape for n in SMALL]
    width = max(s[1] for s in small_shapes)
    g_small = _all_reduce_small(_pack_small([grads[n] for n in SMALL], width), "small_grads_all_reduce")

    out_g, out_d, out_m, out_v = {}, {}, {}, {}
    for n in big:
        shp = a[n].shape
        d, m, v = _adamw(a[n][0], g_big[n], a["m_" + n][0], a["v_" + n][0], f"adamw_{n}")
        out_g[n], out_d[n], out_m[n], out_v[n] = (t.reshape(shp) for t in (g_big[n], d, m, v))
    sw, sm, sv = (_pack_small([a[p + n] for n in SMALL], width) for p in ("", "m_", "v_"))
    d, m, v = _adamw(sw, g_small, sm, sv, "adamw_small")
    for t, dst in ((g_small, out_g), (d, out_d), (m, out_m), (v, out_v)):
        dst.update(zip(SMALL, _unpack_small(t, small_shapes)))

    loss = lax.psum(loss_part, COMM_AXES)
    return (loss, grad_x[None], *[out_g[n] for n in WEIGHTS], *[out_d[n] for n in WEIGHTS],
            *[out_m[n] for n in WEIGHTS], *[out_v[n] for n in WEIGHTS])
```

```python
import functools

import jax
import jax.numpy as jnp
from jax import lax
from jax.experimental import pallas as pl
from jax.experimental.pallas import tpu as pltpu

F32 = jnp.float32
BF16 = jnp.bfloat16
EPS = 1e-6
HEAD = 128
ROPE = 64
CHUNK = 64
SUB = 16
HG_BLOCK = 256
ROPE_THETA = 10000.0
FFN_RESIDUAL_WEIGHT = 0.5
ADAM_LR, ADAM_B1, ADAM_B2, ADAM_EPS, ADAM_WD, ADAM_STEP = 0.001, 0.9, 0.999, 1e-08, 0.01, 10
VMEM_LIMIT = 56 * 2**20
ROW_BUDGET = 20 * 2**20
NEG = -1e30
MESH = pl.DeviceIdType.MESH
N_CHIP = 4
N_DEV = 8


def _params(sem):
    return pltpu.CompilerParams(dimension_semantics=sem, vmem_limit_bytes=VMEM_LIMIT)


def _tile(n, cap, mult):
    if n <= cap:
        return n
    t = (cap // mult) * mult
    while t >= mult:
        if n % t == 0:
            return t
        t -= mult
    raise ValueError(f"no tile for {n} under {cap}")


def _split_rows(n, fractions):
    if n % 16:
        return [(0, n)] + [(n, 0)] * (len(fractions) - 1)
    units, total, acc, cuts = n // 16, sum(fractions), 0, [0]
    for f in fractions[:-1]:
        acc += f
        cuts.append(round(units * acc / total))
    cuts.append(units)
    return [(16 * lo, 16 * (hi - lo)) for lo, hi in zip(cuts, cuts[1:])]


class _Pipe:
    def __init__(self, kind, srcs, lands, fractions):
        self.kind, self.srcs, self.lands = kind, list(srcs), list(lands)
        per_w = [_split_rows(s.shape[0] // 2 if kind == "gather" else s.shape[1], fractions) for s in srcs]
        self.parts = [[pw[i] for pw in per_w] for i in range(len(fractions))]
        self.taken = 0
        self.sems = (6 if kind == "gather" else 3) * len(srcs)

    def take(self):
        self.taken += 1
        return self.parts[self.taken - 1]

    def rest(self):
        left = self.parts[self.taken:]
        self.taken = len(self.parts)
        return [(left[0][w][0], sum(p[w][1] for p in left)) for w in range(len(self.srcs))] if left else None


def _pipe_copies(kind, rows, lands, srcs, send_sems, recv_sems):
    x, y, c, me, sibling, chips = _place()
    out = []
    for w, (r0, nr) in enumerate(rows):
        for j, (cx, cy) in enumerate(chips if nr else []):
            k, to = 2 * cx + cy, (cx, cy, c)
            if kind == "gather":
                rs = pl.ds(c * (srcs[w].shape[0] // 2) + r0, nr)
                src, dst, got, j0 = srcs[w].at[rs], lands[w].at[me, rs], lands[w].at[k, rs], 6 * w + j
            else:
                rs = pl.ds(r0, nr)
                src, dst, got, j0 = srcs[w].at[k, rs], lands[w].at[j, rs], lands[w].at[j, rs], 3 * w + j
            out.append((_remote(src, dst, send_sems, recv_sems, j0, to), _remote(src, got, send_sems, recv_sems, j0, to), w, j, k))
    return out


def _pipe_start(kind, rows, lands, srcs, send_sems, recv_sems):
    for send, _, _, _, _ in _pipe_copies(kind, rows, lands, srcs, send_sems, recv_sems):
        send.start()


def _pipe_finish(kind, rows, lands, srcs, send_sems, recv_sems):
    x, y, c, me, sibling, chips = _place()
    copies = _pipe_copies(kind, rows, lands, srcs, send_sems, recv_sems)
    passed = []
    for _, arrival, _, _, _ in copies:
        arrival.wait_recv()
    if kind == "gather":
        for w, (r0, nr) in enumerate(rows):
            hr = srcs[w].shape[0] // 2
            for j, (cx, cy) in enumerate(chips if nr else []):
                blk = lands[w].at[2 * cx + cy, pl.ds(c * hr + r0, nr)]
                passed.append(_remote(blk, blk, send_sems, recv_sems, 6 * w + 3 + j, sibling))
                passed[-1].start()
        for w, (r0, nr) in enumerate(rows):
            hr = srcs[w].shape[0] // 2
            for j, (cx, cy) in enumerate(chips if nr else []):
                blk = lands[w].at[2 * cx + cy, pl.ds((1 - c) * hr + r0, nr)]
                _remote(blk, blk, send_sems, recv_sems, 6 * w + 3 + j, sibling).wait_recv()
    for send, _, _, _, _ in copies:
        send.wait_send()
    for cp in passed:
        cp.wait_send()


def _carry_call(body, *, grid, in_specs, out_specs, out_shape, scratch_shapes, sem, name, args, pipe=None):
    single = not isinstance(out_shape, (list, tuple))
    if single:
        out_specs, out_shape = [out_specs], [out_shape]
    if pipe is None or pipe.taken >= len(pipe.parts):
        res = pl.pallas_call(body, grid=grid, in_specs=in_specs, out_specs=out_specs, out_shape=out_shape,
                             scratch_shapes=scratch_shapes, compiler_params=_params(sem), name=name)(*args)
        return res[0] if single else res
    rows, kind = pipe.take(), pipe.kind
    n_in, n_out, n_l, n_s, n_scr = len(args), len(out_shape), len(pipe.lands), len(pipe.srcs), len(scratch_shapes)

    def wrapped(*refs):
        ins, srcs = refs[:n_in], refs[n_in + n_l:n_in + n_l + n_s]
        o0 = n_in + n_l + n_s
        outs, lands = refs[o0:o0 + n_out], refs[o0 + n_out:o0 + n_out + n_l]
        scr = refs[o0 + n_out + n_l:o0 + n_out + n_l + n_scr]
        send_sems, recv_sems = refs[-2], refs[-1]
        ids = [pl.program_id(ax) for ax in range(len(grid))]
        first = functools.reduce(jnp.logical_and, [i == 0 for i in ids])
        last = functools.reduce(jnp.logical_and, [i == g - 1 for i, g in zip(ids, grid)])

        @pl.when(first)
        def _():
            _pipe_start(kind, rows, lands, srcs, send_sems, recv_sems)

        body(*ins, *outs, *scr)

        @pl.when(last)
        def _():
            _pipe_finish(kind, rows, lands, srcs, send_sems, recv_sems)

    res = pl.pallas_call(
        wrapped, grid=grid, in_specs=list(in_specs) + [ANY] * (n_l + n_s), out_specs=list(out_specs) + [ANY] * n_l,
        out_shape=list(out_shape) + [jax.ShapeDtypeStruct(l.shape, l.dtype) for l in pipe.lands],
        input_output_aliases={n_in + i: n_out + i for i in range(n_l)},
        scratch_shapes=list(scratch_shapes) + _dma_sems(pipe.sems),
        compiler_params=_params(("arbitrary",) * len(grid)), name=name,
    )(*args, *pipe.lands, *pipe.srcs)
    pipe.lands = list(res[n_out:])
    return res[0] if single else list(res[:n_out])


def _pipe_flush(pipe, name):
    rows = pipe.rest()
    if rows is None:
        return
    n_l, n_s, kind = len(pipe.lands), len(pipe.srcs), pipe.kind

    def body(*refs):
        srcs, lands = refs[n_l:n_l + n_s], refs[n_l + n_s:2 * n_l + n_s]
        _pipe_start(kind, rows, lands, srcs, refs[-2], refs[-1])
        _pipe_finish(kind, rows, lands, srcs, refs[-2], refs[-1])

    pipe.lands = list(pl.pallas_call(
        body, in_specs=[ANY] * (n_l + n_s), out_specs=[ANY] * n_l,
        out_shape=[jax.ShapeDtypeStruct(l.shape, l.dtype) for l in pipe.lands],
        input_output_aliases={i: i for i in range(n_l)}, scratch_shapes=_dma_sems(pipe.sems), name=name,
    )(*pipe.lands, *pipe.srcs))


def _sigmoid(x):
    return 1.0 / (1.0 + jnp.exp(-x))


def _dot(a, b, dims):
    return lax.dot_general(a, b, (dims, ((), ())), preferred_element_type=F32)


NN = ((1,), (0,))
NT = ((1,), (1,))
TN = ((0,), (0,))


def _mm(a, b, mode, out_dtype, name, add=None, out_slots=0, pipe=None, tm_cap=1024, tn_cap=512, tk_cap=2816):
    slot_cap = 1408
    b_slots = b.shape[0] if b.ndim == 3 else 0
    bs = (b.shape[1], b_slots * b.shape[2]) if b_slots else b.shape
    if mode == "nn":
        (M, K), (K2, N) = a.shape, bs
    elif mode == "nt":
        (M, K), (N, K2) = a.shape, bs
    else:
        (K, M), (K2, N) = a.shape, bs
    assert K == K2, (a.shape, b.shape, mode)
    tm = _tile(M, tm_cap, 128)
    tn = _tile(N // (b_slots or out_slots), slot_cap, 128) if (out_slots or (b_slots and mode == "nn")) else _tile(N, tn_cap, 128)
    tk = _tile(K // b_slots, slot_cap, 128) if (b_slots and mode == "nt") else _tile(K, tk_cap, 128)
    nk = K // tk
    a_spec = pl.BlockSpec((tk, tm), lambda i, j, k: (k, i)) if mode == "tn" else pl.BlockSpec((tm, tk), lambda i, j, k: (i, k))
    if b_slots and mode == "nn":
        per = b.shape[2] // tn
        b_spec = pl.BlockSpec((None, tk, tn), lambda i, j, k: (j // per, k, j % per))
    elif b_slots:
        per = b.shape[2] // tk
        b_spec = pl.BlockSpec((None, tn, tk), lambda i, j, k: (k // per, j, k % per))
    else:
        b_spec = pl.BlockSpec((tn, tk), lambda i, j, k: (j, k)) if mode == "nt" else pl.BlockSpec((tk, tn), lambda i, j, k: (k, j))
    if out_slots:
        per_o = N // out_slots // tn
        o_spec = pl.BlockSpec((None, tm, tn), lambda i, j, k: (j // per_o, i, j % per_o))
        o_shape = (out_slots, M, N // out_slots)
    else:
        o_spec = pl.BlockSpec((tm, tn), lambda i, j, k: (i, j))
        o_shape = (M, N)
    dims = {"nn": NN, "nt": NT, "tn": TN}[mode]
    has_add = add is not None

    def body(*refs):
        a_ref, b_ref = refs[0], refs[1]
        add_ref = refs[2] if has_add else None
        o_ref = refs[3] if has_add else refs[2]
        p = _dot(a_ref[...].astype(BF16), b_ref[...].astype(BF16), dims)

        def finish(val):
            if has_add:
                val = val + add_ref[...]
            o_ref[...] = val.astype(out_dtype)

        if nk == 1:
            finish(p)
        else:
            acc_ref = refs[-1]
            k = pl.program_id(2)

            @pl.when(k == 0)
            def _():
                acc_ref[...] = p

            @pl.when(k > 0)
            def _():
                acc_ref[...] += p

            @pl.when(k == nk - 1)
            def _():
                finish(acc_ref[...])

    assert not (has_add and out_slots)
    in_specs = [a_spec, b_spec] + ([o_spec] if has_add else [])
    args = (a, b) + ((add,) if has_add else ())
    return _carry_call(
        body, grid=(M // tm, N // tn, nk), in_specs=in_specs, out_specs=o_spec,
        out_shape=jax.ShapeDtypeStruct(o_shape, out_dtype),
        scratch_shapes=[pltpu.VMEM((tm, tn), F32)] if nk > 1 else [],
        sem=("parallel", "parallel", "arbitrary"), name=name, args=args, pipe=pipe,
    )


def _rows(body, ins, outs, name, pipe=None):
    T = next(a.shape[0] for a, k in ins if k == "row")
    per_row = sum(a.shape[1] * a.dtype.itemsize for a, k in ins if k == "row")
    per_row += sum(s[1] * jnp.dtype(d).itemsize for s, d, k in outs if k == "row")
    tr = next(t for t in (512, 256, 128, 64, 32, 16, 8) if T % t == 0 and 2 * t * per_row <= ROW_BUDGET)
    in_specs = [
        pl.BlockSpec((tr, a.shape[1]), lambda i: (i, 0)) if k == "row" else pl.BlockSpec(a.shape, lambda i: (0, 0))
        for a, k in ins
    ]
    out_specs = [
        pl.BlockSpec((tr, s[1]), lambda i: (i, 0)) if k == "row" else pl.BlockSpec(s, lambda i: (0, 0))
        for s, d, k in outs
    ]
    has_acc = any(k == "acc" for _, _, k in outs)
    return _carry_call(
        body, grid=(T // tr,), in_specs=in_specs, out_specs=out_specs,
        out_shape=[jax.ShapeDtypeStruct(s, d) for s, d, k in outs], scratch_shapes=[],
        sem=("arbitrary",) if has_acc else ("parallel",), name=name, args=[a for a, _ in ins], pipe=pipe,
    )


def _rstd(x):
    return lax.rsqrt(jnp.mean(x * x, axis=-1, keepdims=True) + EPS)


def _norm_fwd(x, g, name):
    def body(x_ref, g_ref, o_ref):
        x = x_ref[...]
        o_ref[...] = (x * _rstd(x) * g_ref[...]).astype(BF16)

    return _rows(body, [(x, "row"), (g, "vec")], [(x.shape, BF16, "row")], name)[0]


def _postnorm_fwd(x, y, g, weight, name):
    def body(x_ref, y_ref, g_ref, o_ref):
        y = y_ref[...]
        o_ref[...] = x_ref[...] + weight * (y * _rstd(y) * g_ref[...])

    return _rows(body, [(x, "row"), (y, "row"), (g, "vec")], [(x.shape, F32, "row")], name)[0]


def _norm_bwd(x, g, dy, weight, out_dtype, name, res=None):
    has_res = res is not None

    def body(*refs):
        x_ref, g_ref, dy_ref = refs[:3]
        res_ref = refs[3] if has_res else None
        dx_ref, dg_ref = refs[-2], refs[-1]

        @pl.when(pl.program_id(0) == 0)
        def _():
            dg_ref[...] = jnp.zeros_like(dg_ref)

        x = x_ref[...]
        dn = dy_ref[...].astype(F32) * weight
        r = _rstd(x)
        xhat = x * r
        dg_ref[...] += jnp.sum(dn * xhat, axis=0, keepdims=True)
        dxh = dn * g_ref[...]
        dx = r * (dxh - xhat * jnp.mean(dxh * xhat, axis=-1, keepdims=True))
        if has_res:
            dx = dx + res_ref[...]
        dx_ref[...] = dx.astype(out_dtype)

    ins = [(x, "row"), (g, "vec"), (dy, "row")] + ([(res, "row")] if has_res else [])
    return _rows(body, ins, [(x.shape, out_dtype, "row"), (g.shape, F32, "acc")], name)


def _swiglu_fwd(a, b, name):
    def body(a_ref, b_ref, o_ref):
        a = a_ref[...]
        o_ref[...] = (a * _sigmoid(a) * b_ref[...]).astype(BF16)

    return _rows(body, [(a, "row"), (b, "row")], [(a.shape, BF16, "row")], name)[0]


def _swiglu_bwd(a, b, ds, name):
    def body(a_ref, b_ref, ds_ref, da_ref, db_ref):
        a, ds = a_ref[...], ds_ref[...]
        sg = _sigmoid(a)
        da_ref[...] = (ds * b_ref[...] * (sg * (1.0 + a * (1.0 - sg)))).astype(BF16)
        db_ref[...] = (ds * (a * sg)).astype(BF16)

    return _rows(body, [(a, "row"), (b, "row"), (ds, "row")], [(a.shape, BF16, "row"), (a.shape, BF16, "row")], name)


def _merge_fwd(ga, gb, ya, yb, name):
    def body(ga_ref, gb_ref, ya_ref, yb_ref, o_ref):
        o_ref[...] = (_sigmoid(ga_ref[...]) * ya_ref[...] + _sigmoid(gb_ref[...]) * yb_ref[...]).astype(BF16)

    return _rows(body, [(ga, "row"), (gb, "row"), (ya, "row"), (yb, "row")], [(ga.shape, BF16, "row")], name)[0]


def _merge_bwd(ga, gb, ya, yb, dy, name):
    def body(ga_ref, gb_ref, ya_ref, yb_ref, dy_ref, dya_ref, dyb_ref, dga_ref, dgb_ref):
        dy = dy_ref[...]
        sa, sb = _sigmoid(ga_ref[...]), _sigmoid(gb_ref[...])
        dya_ref[...] = (dy * sa).astype(BF16)
        dyb_ref[...] = (dy * sb).astype(BF16)
        dga_ref[...] = (dy * ya_ref[...] * (sa * (1.0 - sa))).astype(BF16)
        dgb_ref[...] = (dy * yb_ref[...] * (sb * (1.0 - sb))).astype(BF16)

    ins = [(ga, "row"), (gb, "row"), (ya, "row"), (yb, "row"), (dy, "row")]
    return _rows(body, ins, [(ga.shape, BF16, "row")] * 4, name)


def _loss_head(y, target, name):
    D = y.shape[1]

    def body(y_ref, t_ref, dy_ref, acc_ref):
        @pl.when(pl.program_id(0) == 0)
        def _():
            acc_ref[...] = jnp.zeros_like(acc_ref)

        err = y_ref[...] - t_ref[...]
        dy_ref[...] = err * (1.0 / D)
        acc_ref[...] += jnp.sum(err * err, axis=0, keepdims=True)

    return _rows(body, [(y, "row"), (target, "row")], [(y.shape, F32, "row"), ((1, D), F32, "acc")], name)


def _rot(x):
    lane = lax.broadcasted_iota(jnp.int32, x.shape, 1)
    return jnp.where((lane % ROPE) < ROPE // 2, -pltpu.roll(x, 128 - ROPE // 2, 1), pltpu.roll(x, ROPE // 2, 1))


def _rope_q_fwd(qpe, cos, sin, name):
    T, W = qpe.shape
    tr = min(T, 512)
    blk = pl.BlockSpec((tr, 128), lambda i, j: (i, j))
    tab = pl.BlockSpec((tr, 128), lambda i, j: (i, 0))

    def body(x_ref, c_ref, s_ref, o_ref):
        x = x_ref[...]
        o_ref[...] = (x * c_ref[...] + _rot(x) * s_ref[...]).astype(BF16)

    return pl.pallas_call(
        body, grid=(T // tr, W // 128), in_specs=[blk, tab, tab], out_specs=blk,
        out_shape=jax.ShapeDtypeStruct((T, W), BF16), compiler_params=_params(("parallel", "parallel")), name=name,
    )(qpe, cos, sin)


def _rope_q_bwd(dq_heads, cos, sin, name):
    T, W = dq_heads.shape
    tr = min(T, 512)
    even = pl.BlockSpec((tr, 128), lambda i, j: (i, 2 * j))
    odd = pl.BlockSpec((tr, 128), lambda i, j: (i, 2 * j + 1))
    tab = pl.BlockSpec((tr, 128), lambda i, j: (i, 0))

    def body(a_ref, b_ref, c_ref, s_ref, o_ref):
        d = a_ref[...] + b_ref[...]
        o_ref[...] = (d * c_ref[...] - _rot(d * s_ref[...])).astype(BF16)

    return pl.pallas_call(
        body, grid=(T // tr, W // 256), in_specs=[even, odd, tab, tab],
        out_specs=pl.BlockSpec((tr, 128), lambda i, j: (i, j)),
        out_shape=jax.ShapeDtypeStruct((T, W // 2), BF16), compiler_params=_params(("parallel", "parallel")), name=name,
    )(dq_heads, dq_heads, cos, sin)


def _rope_k_fwd(kpe, cos, sin, name):
    def body(x_ref, c_ref, s_ref, o_ref):
        x = x_ref[...]
        y = x * c_ref[...] + _rot(x) * s_ref[...]
        o_ref[...] = (y + pltpu.roll(y, ROPE, 1)).astype(BF16)

    return _rows(body, [(kpe, "row"), (cos, "row"), (sin, "row")], [(kpe.shape, BF16, "row")], name)[0]


def _rope_k_bwd(dk_heads, cos, sin, name):
    T, W = dk_heads.shape

    def body(d_ref, c_ref, s_ref, o_ref):
        d = d_ref[:, 0:128]
        for h in range(1, W // 128):
            d = d + d_ref[:, h * 128:(h + 1) * 128]
        d = d + pltpu.roll(d, ROPE, 1)
        dx = d * c_ref[...] - _rot(d * s_ref[...])
        lane = lax.broadcasted_iota(jnp.int32, dx.shape, 1)
        o_ref[...] = jnp.where(lane < ROPE, dx, 0.0).astype(BF16)

    return _rows(body, [(dk_heads, "row"), (cos, "row"), (sin, "row")], [((T, 128), BF16, "row")], name)[0]


def _attn_probs(q, k, qpe, kpe, scale, causal, q0):
    s = _dot(q, k, NT)
    if qpe is not None:
        s = s + _dot(qpe, kpe, NT)
    s = s * scale
    if causal:
        row = q0 + lax.broadcasted_iota(jnp.int32, s.shape, 0)
        col = lax.broadcasted_iota(jnp.int32, s.shape, 1)
        s = jnp.where((col // CHUNK) <= (row // CHUNK), s, NEG)
    p = jnp.exp(s - jnp.max(s, axis=-1, keepdims=True))
    return p / jnp.sum(p, axis=-1, keepdims=True)


def _pe_mask(x, h):
    lane = lax.broadcasted_iota(jnp.int32, x.shape, 1)
    return jnp.where((lane // ROPE) == (h % 2), x, jnp.zeros_like(x))


def _attn_fwd(q, k, v, scale, name, qpe=None, kpe=None, causal=False, pipe=None):
    T, W = q.shape
    Tk = k.shape[0]
    H = W // HEAD
    tq = min(T, 256)
    nq = T // tq
    has_pe = qpe is not None
    qs = pl.BlockSpec((tq, HEAD), lambda h, i: (i, h))
    ks = pl.BlockSpec((Tk, HEAD), lambda h, i: (0, h))
    in_specs, args = [qs, ks, ks], [q, k, v]
    if has_pe:
        in_specs += [pl.BlockSpec((tq, HEAD), lambda h, i: (i, h // 2)), pl.BlockSpec((Tk, HEAD), lambda h, i: (0, 0))]
        args += [qpe, kpe]

    def body(*refs):
        q_ref, k_ref, v_ref = refs[:3]
        o_ref = refs[-1]
        h, i = pl.program_id(0), pl.program_id(1)

        def compute(klen):
            qp = _pe_mask(refs[3][...], h) if has_pe else None
            kp = refs[4][0:klen, :] if has_pe else None
            p = _attn_probs(q_ref[...], k_ref[0:klen, :], qp, kp, scale, causal, i * tq)
            o_ref[...] = _dot(p.astype(BF16), v_ref[0:klen, :], NN).astype(BF16)

        if causal:
            for qi in range(nq):
                pl.when(i == qi)(functools.partial(compute, (qi + 1) * tq))
        else:
            compute(Tk)

    return _carry_call(
        body, grid=(H, nq), in_specs=in_specs, out_specs=qs, out_shape=jax.ShapeDtypeStruct((T, W), BF16),
        scratch_shapes=[], sem=("parallel", "parallel"), name=name, args=args, pipe=pipe,
    )


def _attn_bwd(q, k, v, do, scale, name, qpe=None, kpe=None, causal=False, pipe=None):
    T, W = q.shape
    Tk = k.shape[0]
    H = W // HEAD
    tq = min(T, 256)
    nq = T // tq
    has_pe = qpe is not None
    qs = pl.BlockSpec((tq, HEAD), lambda h, i: (i, h))
    ks = pl.BlockSpec((Tk, HEAD), lambda h, i: (0, h))
    in_specs, args = [qs, ks, ks, qs], [q, k, v, do]
    out_specs = [qs, ks, ks]
    out_shape = [jax.ShapeDtypeStruct((T, W), BF16), jax.ShapeDtypeStruct((Tk, W), BF16), jax.ShapeDtypeStruct((Tk, W), BF16)]
    scratch = [pltpu.VMEM((Tk, HEAD), F32), pltpu.VMEM((Tk, HEAD), F32)]
    if has_pe:
        in_specs += [pl.BlockSpec((tq, HEAD), lambda h, i: (i, h // 2)), pl.BlockSpec((Tk, HEAD), lambda h, i: (0, 0))]
        args += [qpe, kpe]
        out_specs += [qs, ks]
        out_shape += [jax.ShapeDtypeStruct((T, W), F32), jax.ShapeDtypeStruct((Tk, W), F32)]
        scratch += [pltpu.VMEM((Tk, HEAD), F32)]
    n_in = len(in_specs)

    def body(*refs):
        q_ref, k_ref, v_ref, do_ref = refs[:4]
        outs = refs[n_in:n_in + len(out_specs)]
        accs = refs[n_in + len(out_specs):]
        dq_ref, dk_ref, dv_ref = outs[:3]
        h, i = pl.program_id(0), pl.program_id(1)

        @pl.when(i == 0)
        def _():
            for acc in accs:
                acc[...] = jnp.zeros_like(acc)

        def compute(klen):
            qp = _pe_mask(refs[4][...], h) if has_pe else None
            kp = refs[5][0:klen, :] if has_pe else None
            qv, kv, vv, dov = q_ref[...], k_ref[0:klen, :], v_ref[0:klen, :], do_ref[...]
            p = _attn_probs(qv, kv, qp, kp, scale, causal, i * tq)
            dp = _dot(dov, vv, NT)
            ds = (p * (dp - jnp.sum(p * dp, axis=-1, keepdims=True)) * scale).astype(BF16)
            dq_ref[...] = _dot(ds, kv, NN).astype(BF16)
            accs[0][0:klen, :] += _dot(ds, qv, TN)
            accs[1][0:klen, :] += _dot(p.astype(BF16), dov, TN)
            if has_pe:
                outs[3][...] = _pe_mask(_dot(ds, kp, NN), h)
                accs[2][0:klen, :] += _dot(ds, qp, TN)

        if causal:
            for qi in range(nq):
                pl.when(i == qi)(functools.partial(compute, (qi + 1) * tq))
        else:
            compute(Tk)

        @pl.when(i == nq - 1)
        def _():
            dk_ref[...] = accs[0][...].astype(BF16)
            dv_ref[...] = accs[1][...].astype(BF16)
            if has_pe:
                outs[4][...] = accs[2][...]

    return _carry_call(
        body, grid=(H, nq), in_specs=in_specs, out_specs=out_specs, out_shape=out_shape, scratch_shapes=scratch,
        sem=("parallel", "arbitrary"), name=name, args=args, pipe=pipe,
    )


def _split3(x):
    hi = x.astype(BF16)
    r1 = x - hi.astype(F32)
    mid = r1.astype(BF16)
    lo = (r1 - mid.astype(F32)).astype(BF16)
    return hi, mid, lo


def _tri_dot(tri, x):
    hi, mid, lo = _split3(x)
    return _dot(tri, hi, NN) + _dot(tri, mid, NN) + _dot(tri, lo, NN)


def _hg_gates(u, lb):
    q, fr, v = u[:, 0:HEAD], u[:, HEAD:2 * HEAD], u[:, 2 * HEAD:3 * HEAD]
    sg = 1.0 / (1.0 + jnp.exp(-fr))
    sgm = 1.0 / (1.0 + jnp.exp(fr))
    f = lb + (1.0 - lb) * sg
    kin = (1.0 - lb) * sgm
    sq = _sigmoid(q)
    return q, v, sg, sgm, f, kin, sq, q * sq


def _hg_block_mats(blk):
    t = lax.broadcasted_iota(jnp.int32, (blk, blk), 0)
    s = lax.broadcasted_iota(jnp.int32, (blk, blk), 1)
    same = (t // SUB) == (s // SUB)
    one = lambda m: jnp.where(m, 1.0, 0.0).astype(BF16)
    return one(same & (s <= t)), one(same), one(same & (s >= t))


def _hg_stage(pairs, blk):
    for sc, val in pairs:
        sc[0:SUB, :] = jnp.zeros((SUB, HEAD), F32)
        sc[SUB:SUB + blk, :] = val


def _hg_scan_fwd(u, lb, name, pipe=None):
    T, W = u.shape
    H = W // (4 * HEAD)
    blk = min(T, HG_BLOCK)
    nb, nsb = T // blk, blk // SUB

    def body(u_ref, lb_ref, o_ref, st_ref, state, k_sc, b_sc, v_sc):
        @pl.when(pl.program_id(1) == 0)
        def _():
            state[...] = jnp.zeros_like(state)

        q, v, sg, sgm, f, kin, sq, qin = _hg_gates(u_ref[...], lb_ref[...])
        tri, ones, _ = _hg_block_mats(blk)
        logf = jnp.log(f)
        brel = _tri_dot(tri, logf)
        btot = _tri_dot(ones, logf)
        _hg_stage(((k_sc, kin), (b_sc, brel), (v_sc, v)), blk)
        sub_row = lax.broadcasted_iota(jnp.int32, (blk, HEAD), 0) % SUB
        o = jnp.zeros((blk, HEAD), F32)
        for d in range(SUB):
            win = slice(SUB - d, SUB - d + blk)
            e = jnp.exp(jnp.where(sub_row >= d, brel - b_sc[win, :], NEG))
            o = o + jnp.sum(qin * e * k_sc[win, :], axis=-1, keepdims=True) * v_sc[win, :]
        ab = (qin * jnp.exp(brel)).astype(BF16)
        kdb = (kin * jnp.exp(btot - brel)).astype(BF16)
        vb = v.astype(BF16)
        ebt = jnp.exp(btot)
        st = state[...]
        st_ref[...] = st
        for i in range(nsb):
            sl = slice(i * SUB, (i + 1) * SUB)
            o_ref[sl, :] = o[sl] + _dot(ab[sl], st.astype(BF16), NT)
            st = ebt[i * SUB:i * SUB + 1, :] * st + _dot(vb[sl], kdb[sl], TN)
        state[...] = st

    return _carry_call(
        body, grid=(H, nb),
        in_specs=[pl.BlockSpec((blk, 4 * HEAD), lambda h, c: (c, h)), pl.BlockSpec((1, HEAD), lambda h, c: (0, h))],
        out_specs=[pl.BlockSpec((blk, HEAD), lambda h, c: (c, h)), pl.BlockSpec((None, None, HEAD, HEAD), lambda h, c: (h, c, 0, 0))],
        out_shape=[jax.ShapeDtypeStruct((T, H * HEAD), F32), jax.ShapeDtypeStruct((H, nb, HEAD, HEAD), F32)],
        scratch_shapes=[pltpu.VMEM((HEAD, HEAD), F32)] + [pltpu.VMEM((SUB + blk, HEAD), F32)] * 3,
        sem=("parallel", "arbitrary"), name=name, args=(u, lb), pipe=pipe,
    )


def _hg_scan_bwd(u, lb, do, dog, states, name, pipe=None):
    T, W = u.shape
    H = W // (4 * HEAD)
    blk = min(T, HG_BLOCK)
    NC, nsb = T // blk, blk // SUB

    def body(u_ref, lb_ref, do_ref, dog_ref, st_ref, du_ref, dlb_ref, dstate, s_all, k_sc, b_sc, v_sc, dk_sc, dbn_sc,
             dv_sc, da_sc, dkd_sc, dvs_sc, dbt_sc):
        @pl.when(pl.program_id(1) == 0)
        def _():
            dstate[...] = jnp.zeros_like(dstate)
            dlb_ref[...] = jnp.zeros_like(dlb_ref)

        lb = lb_ref[...]
        q, v, sg, sgm, f, kin, sq, qin = _hg_gates(u_ref[...], lb)
        tri, ones, tri_t = _hg_block_mats(blk)
        logf = jnp.log(f)
        brel = _tri_dot(tri, logf)
        btot = _tri_dot(ones, logf)
        eb, ekd, ebt = jnp.exp(brel), jnp.exp(btot - brel), jnp.exp(btot)
        a, kd = qin * eb, kin * ekd
        ab, kdb, vb = a.astype(BF16), kd.astype(BF16), v.astype(BF16)
        do = do_ref[...]
        dob = do.astype(BF16)
        st = st_ref[...]
        for i in range(nsb):
            sl = slice(i * SUB, (i + 1) * SUB)
            s_all[i] = st
            st = ebt[i * SUB:i * SUB + 1, :] * st + _dot(vb[sl], kdb[sl], TN)
        ds = dstate[...]
        for i in reversed(range(nsb)):
            sl = slice(i * SUB, (i + 1) * SUB)
            st_i = s_all[i]
            dsb = ds.astype(BF16)
            e_i = ebt[i * SUB:i * SUB + 1, :]
            da_sc[sl, :] = _dot(dob[sl], st_i.astype(BF16), NN)
            dvs_sc[sl, :] = _dot(kdb[sl], dsb, NT)
            dkd_sc[sl, :] = _dot(vb[sl], dsb, NN)
            dbt_sc[sl, :] = jnp.broadcast_to(jnp.sum(ds * st_i, axis=0, keepdims=True) * e_i, (SUB, HEAD))
            ds = e_i * ds + _dot(dob[sl], ab[sl], TN)
        dstate[...] = ds
        da, dkd = da_sc[...], dkd_sc[...]
        t1 = dkd * kd
        dqin = da * eb
        dbrel = da * a - t1
        dkin = dkd * ekd
        dbtot = dbt_sc[...] + _tri_dot(ones, t1)
        _hg_stage(((k_sc, kin), (b_sc, brel), (v_sc, v)), blk)
        for sc in (dk_sc, dbn_sc, dv_sc):
            sc[...] = jnp.zeros_like(sc)
        sub_row = lax.broadcasted_iota(jnp.int32, (blk, HEAD), 0) % SUB
        for d in range(SUB):
            win = slice(SUB - d, SUB - d + blk)
            ks = k_sc[win, :]
            e = jnp.exp(jnp.where(sub_row >= d, brel - b_sc[win, :], NEG))
            qe = qin * e
            col = jnp.sum(qe * ks, axis=-1, keepdims=True)
            dcol = jnp.sum(do * v_sc[win, :], axis=-1, keepdims=True)
            dqe = dcol * qe
            g = dqe * ks
            dqin = dqin + dcol * (e * ks)
            dbrel = dbrel + g
            dk_sc[win, :] += dqe
            dbn_sc[win, :] += g
            dv_sc[win, :] += col * do
        dkin = dkin + dk_sc[SUB:SUB + blk, :]
        dbrel = dbrel - dbn_sc[SUB:SUB + blk, :]
        dv = dvs_sc[...] + dv_sc[SUB:SUB + blk, :]
        dlogf = _tri_dot(tri_t, dbrel) + dbtot
        diff = dlogf / f - dkin
        dlb_ref[...] += jnp.sum(sgm * diff, axis=0, keepdims=True)
        du_ref[:, 0:HEAD] = (dqin * (sq * (1.0 + q * (1.0 - sq)))).astype(BF16)
        du_ref[:, HEAD:2 * HEAD] = ((1.0 - lb) * sg * sgm * diff).astype(BF16)
        du_ref[:, 2 * HEAD:3 * HEAD] = dv.astype(BF16)
        du_ref[:, 3 * HEAD:4 * HEAD] = dog_ref[...]

    rev = lambda h, c: (NC - 1 - c, h)
    return _carry_call(
        body, grid=(H, NC),
        in_specs=[
            pl.BlockSpec((blk, 4 * HEAD), rev), pl.BlockSpec((1, HEAD), lambda h, c: (0, h)),
            pl.BlockSpec((blk, HEAD), rev), pl.BlockSpec((blk, HEAD), rev),
            pl.BlockSpec((None, None, HEAD, HEAD), lambda h, c: (h, NC - 1 - c, 0, 0)),
        ],
        out_specs=[pl.BlockSpec((blk, 4 * HEAD), rev), pl.BlockSpec((1, HEAD), lambda h, c: (0, h))],
        out_shape=[jax.ShapeDtypeStruct((T, W), BF16), jax.ShapeDtypeStruct((1, H * HEAD), F32)],
        scratch_shapes=[pltpu.VMEM((HEAD, HEAD), F32), pltpu.VMEM((nsb, HEAD, HEAD), F32)]
        + [pltpu.VMEM((SUB + blk, HEAD), F32)] * 6 + [pltpu.VMEM((blk, HEAD), F32)] * 4,
        sem=("parallel", "arbitrary"), name=name, args=(u, lb, do, dog, states), pipe=pipe,
    )


def _hg_tail_fwd(o_raw, u, g, name):
    T, D = o_raw.shape
    H = D // HEAD
    tr = min(T, 512)
    blk = pl.BlockSpec((tr, HEAD), lambda h, i: (i, h))

    def body(o_ref, og_ref, g_ref, out_ref):
        o, og = o_ref[...], og_ref[...]
        out_ref[...] = (o * _rstd(o) * g_ref[...] * (og * _sigmoid(og))).astype(BF16)

    return pl.pallas_call(
        body, grid=(H, T // tr),
        in_specs=[blk, pl.BlockSpec((tr, HEAD), lambda h, i: (i, 4 * h + 3)), pl.BlockSpec((1, HEAD), lambda h, i: (0, h))],
        out_specs=blk, out_shape=jax.ShapeDtypeStruct((T, D), BF16),
        compiler_params=_params(("parallel", "parallel")), name=name,
    )(o_raw, u, g)


def _hg_tail_bwd(o_raw, u, g, doa, name):
    T, D = o_raw.shape
    H = D // HEAD
    tr = min(T, 512)
    blk = pl.BlockSpec((tr, HEAD), lambda h, i: (i, h))
    vec = pl.BlockSpec((1, HEAD), lambda h, i: (0, h))

    def body(o_ref, og_ref, g_ref, doa_ref, do_ref, dog_ref, dg_ref):
        @pl.when(pl.program_id(1) == 0)
        def _():
            dg_ref[...] = jnp.zeros_like(dg_ref)

        o, og, doa, g = o_ref[...], og_ref[...], doa_ref[...], g_ref[...]
        sg = _sigmoid(og)
        r = _rstd(o)
        xhat = o * r
        dog_ref[...] = (doa * (xhat * g) * (sg * (1.0 + og * (1.0 - sg)))).astype(BF16)
        dn = doa * (og * sg)
        dg_ref[...] += jnp.sum(dn * xhat, axis=0, keepdims=True)
        dxh = dn * g
        do_ref[...] = r * (dxh - xhat * jnp.mean(dxh * xhat, axis=-1, keepdims=True))

    return pl.pallas_call(
        body, grid=(H, T // tr),
        in_specs=[blk, pl.BlockSpec((tr, HEAD), lambda h, i: (i, 4 * h + 3)), vec, blk],
        out_specs=[blk, blk, vec],
        out_shape=[jax.ShapeDtypeStruct((T, D), F32), jax.ShapeDtypeStruct((T, D), BF16), jax.ShapeDtypeStruct((1, D), F32)],
        compiler_params=_params(("parallel", "arbitrary")), name=name,
    )(o_raw, u, g, doa)


def _lb_fwd(logits, name):
    def body(l_ref, o_ref):
        l0, l1 = l_ref[0:1, :], l_ref[1:2, :]
        m = jnp.maximum(l0, l1)
        e0, e1 = jnp.exp(l0 - m), jnp.exp(l1 - m)
        o_ref[...] = e0 / (e0 + e1)

    D = logits.shape[1]
    return pl.pallas_call(body, out_shape=jax.ShapeDtypeStruct((1, D), F32), name=name)(logits)


def _lb_bwd(lb, dlb, name):
    def body(lb_ref, d_ref, o_ref):
        lb = lb_ref[...]
        d0 = d_ref[...] * lb * (1.0 - lb)
        o_ref[0:1, :] = d0
        o_ref[1:2, :] = -d0

    D = lb.shape[1]
    return pl.pallas_call(body, out_shape=jax.ShapeDtypeStruct((2, D), F32), name=name)(lb, dlb)


def _slots(w):
    return w.shape[0] if w.ndim == 3 else 0


def _ffn_fwd(x, p, tag, carry):
    mm = lambda a, b, mode, dt, name, **kw: _mm(a, b, mode, dt, name, pipe=carry.get(name), **kw)
    hb = _norm_fwd(x, p["pre_g"], f"{tag}_pre_norm")
    a = mm(hb, p["w_gate"], "nn", F32, f"{tag}_gate")
    b = mm(hb, p["w_up"], "nn", F32, f"{tag}_up")
    sb = _swiglu_fwd(a, b, f"{tag}_swiglu")
    y = mm(sb, p["w_down"], "nn", F32, f"{tag}_down")
    xo = _postnorm_fwd(x, y, p["post_g"], FFN_RESIDUAL_WEIGHT, f"{tag}_post_norm")
    return xo, (x, hb, a, b, sb, y)


def _ffn_bwd(dxo, p, saved, tag, carry):
    mm = lambda a, b, mode, dt, name, **kw: _mm(a, b, mode, dt, name, pipe=carry.get(name), **kw)
    x, hb, a, b, sb, y = saved
    dyb, dpost = _norm_bwd(y, p["post_g"], dxo, FFN_RESIDUAL_WEIGHT, BF16, f"{tag}_post_norm_bwd")
    dw_down = mm(sb, dyb, "tn", BF16, f"{tag}_down_dw")
    ds = mm(dyb, p["w_down"], "nt", F32, f"{tag}_down_dx")
    dab, dbb = _swiglu_bwd(a, b, ds, f"{tag}_swiglu_bwd")
    dw_gate = mm(hb, dab, "tn", BF16, f"{tag}_gate_dw", out_slots=_slots(p["w_gate"]))
    dw_up = mm(hb, dbb, "tn", BF16, f"{tag}_up_dw", out_slots=_slots(p["w_up"]))
    dh = mm(dab, p["w_gate"], "nt", F32, f"{tag}_gate_dx")
    dh = mm(dbb, p["w_up"], "nt", F32, f"{tag}_up_dx", add=dh)
    dx, dpre = _norm_bwd(x, p["pre_g"], dh, 1.0, F32, f"{tag}_pre_norm_bwd", res=dxo)
    return dx, {"pre_g": dpre, "w_gate": dw_gate, "w_up": dw_up, "w_down": dw_down, "post_g": dpost}


def _mixer_fwd(x, cos, sin, p, carry):
    scale = (HEAD + ROPE) ** -0.5
    hb = _norm_fwd(x, p["pre_g"], "mix_pre_norm")
    u = _mm(hb, p["w_hg"], "nn", F32, "mix_in_hg", pipe=carry.get("mix_in_hg"))
    cq = _mm(hb, p["w_cq"], "nn", F32, "mix_in_cq")
    ckv = _mm(hb, p["w_ckv"], "nn", F32, "mix_in_ckv")
    kpe = _mm(hb, p["w_kpe"], "nn", F32, "mix_in_kpe")
    ga = _mm(hb, p["w_ga"], "nn", F32, "mix_in_ga")
    gb = _mm(hb, p["w_gb"], "nn", F32, "mix_in_gb")
    lb = _lb_fwd(p["lb_logits"], "hg_lb")
    o_raw, states = _hg_scan_fwd(u, lb, "hg_scan", pipe=carry.get("hg_scan"))
    oa = _hg_tail_fwd(o_raw, u, p["hg_norm_g"], "hg_tail")
    ya = _mm(oa, p["w_branch_a"], "nn", F32, "mix_branch_a")
    cqn = _norm_fwd(cq, p["q_norm_g"], "mla_q_norm")
    qn = _mm(cqn, p["w_qn"], "nn", BF16, "mla_q_up_nope")
    qpe = _rope_q_fwd(_mm(cqn, p["w_qpe"], "nn", F32, "mla_q_up_pe"), cos, sin, "mla_rope_q")
    ckvn = _norm_fwd(ckv, p["kv_norm_g"], "mla_kv_norm")
    kn = _mm(ckvn, p["w_kn"], "nn", BF16, "mla_k_up")
    vv = _mm(ckvn, p["w_vv"], "nn", BF16, "mla_v_up")
    kpe2 = _rope_k_fwd(kpe, cos, sin, "mla_rope_k")
    ob = _attn_fwd(qn, kn, vv, scale, "mla_attn", qpe=qpe, kpe=kpe2, causal=True, pipe=carry.get("mla_attn"))
    yb = _mm(ob, p["w_branch_b"], "nn", F32, "mix_branch_b")
    ym = _merge_fwd(ga, gb, ya, yb, "mix_merge")
    z = _mm(ym, p["w_out"], "nn", F32, "mix_out")
    xo = _postnorm_fwd(x, z, p["post_g"], 1.0, "mix_post_norm")
    saved = (x, hb, u, cq, ckv, ga, gb, lb, o_raw, states, oa, ya, cqn, qn, qpe, ckvn, kn, vv, kpe2, ob, yb, ym, z)
    return xo, saved


def _mixer_bwd(dxo, cos, sin, p, saved, carry):
    x, hb, u, cq, ckv, ga, gb, lb, o_raw, states, oa, ya, cqn, qn, qpe, ckvn, kn, vv, kpe2, ob, yb, ym, z = saved
    scale = (HEAD + ROPE) ** -0.5
    g = {}
    dzb, g["post_g"] = _norm_bwd(z, p["post_g"], dxo, 1.0, BF16, "mix_post_norm_bwd")
    g["w_out"] = _mm(ym, dzb, "tn", BF16, "mix_out_dw")
    dym = _mm(dzb, p["w_out"], "nt", F32, "mix_out_dx")
    dya, dyb, dga, dgb = _merge_bwd(ga, gb, ya, yb, dym, "mix_merge_bwd")
    g["w_branch_a"] = _mm(oa, dya, "tn", BF16, "mix_branch_a_dw")
    doa = _mm(dya, p["w_branch_a"], "nt", F32, "mix_branch_a_dx")
    do_raw, dog, g["hg_norm_g"] = _hg_tail_bwd(o_raw, u, p["hg_norm_g"], doa, "hg_tail_bwd")
    du, dlb = _hg_scan_bwd(u, lb, do_raw, dog, states, "hg_scan_bwd", pipe=carry.get("hg_scan_bwd"))
    g["lb_logits"] = _lb_bwd(lb, dlb, "hg_lb_bwd")
    g["w_branch_b"] = _mm(ob, dyb, "tn", BF16, "mix_branch_b_dw")
    dob = _mm(dyb, p["w_branch_b"], "nt", BF16, "mix_branch_b_dx")
    dqn, dkn, dvv, dqpe_h, dkpe_h = _attn_bwd(qn, kn, vv, dob, scale, "mla_attn_bwd", qpe=qpe, kpe=kpe2, causal=True,
                                              pipe=carry.get("mla_attn_bwd"))
    dqpe = _rope_q_bwd(dqpe_h, cos, sin, "mla_rope_q_bwd")
    dkpe = _rope_k_bwd(dkpe_h, cos, sin, "mla_rope_k_bwd")
    g["w_qn"] = _mm(cqn, dqn, "tn", BF16, "mla_q_up_nope_dw")
    g["w_qpe"] = _mm(cqn, dqpe, "tn", BF16, "mla_q_up_pe_dw")
    dcqn = _mm(dqn, p["w_qn"], "nt", F32, "mla_q_up_nope_dx")
    dcqn = _mm(dqpe, p["w_qpe"], "nt", F32, "mla_q_up_pe_dx", add=dcqn)
    dcq, g["q_norm_g"] = _norm_bwd(cq, p["q_norm_g"], dcqn, 1.0, BF16, "mla_q_norm_bwd")
    g["w_kn"] = _mm(ckvn, dkn, "tn", BF16, "mla_k_up_dw")
    g["w_vv"] = _mm(ckvn, dvv, "tn", BF16, "mla_v_up_dw")
    dckvn = _mm(dkn, p["w_kn"], "nt", F32, "mla_k_up_dx")
    dckvn = _mm(dvv, p["w_vv"], "nt", F32, "mla_v_up_dx", add=dckvn)
    dckv, g["kv_norm_g"] = _norm_bwd(ckv, p["kv_norm_g"], dckvn, 1.0, BF16, "mla_kv_norm_bwd")
    dh = None
    for key, d in (("w_hg", du), ("w_cq", dcq), ("w_ckv", dckv), ("w_kpe", dkpe), ("w_ga", dga), ("w_gb", dgb)):
        g[key] = _mm(hb, d, "tn", BF16, f"mix_in_{key}_dw")
        dh = _mm(d, p[key], "nt", F32, f"mix_in_{key}_dx", add=dh)
    dx, g["pre_g"] = _norm_bwd(x, p["pre_g"], dh, 1.0, F32, "mix_pre_norm_bwd", res=dxo)
    return dx, g


def _xa_fwd(x, mem, p):
    scale = HEAD ** -0.5
    hb = _norm_fwd(x, p["pre_g"], "xa_pre_norm")
    mb = _norm_fwd(mem, p["mem_g"], "xa_mem_norm")
    q = _mm(hb, p["w_q"], "nn", BF16, "xa_q")
    k = _mm(mb, p["w_k"], "nn", BF16, "xa_k")
    v = _mm(mb, p["w_v"], "nn", BF16, "xa_v")
    o = _attn_fwd(q, k, v, scale, "xa_attn")
    z = _mm(o, p["w_o"], "nn", F32, "xa_o")
    xo = _postnorm_fwd(x, z, p["post_g"], 1.0, "xa_post_norm")
    return xo, (x, mem, hb, mb, q, k, v, o, z)


def _xa_bwd(dxo, p, saved):
    x, mem, hb, mb, q, k, v, o, z = saved
    scale = HEAD ** -0.5
    g = {}
    dzb, g["post_g"] = _norm_bwd(z, p["post_g"], dxo, 1.0, BF16, "xa_post_norm_bwd")
    g["w_o"] = _mm(o, dzb, "tn", BF16, "xa_o_dw", out_slots=_slots(p["w_o"]))
    do = _mm(dzb, p["w_o"], "nt", BF16, "xa_o_dx")
    dq, dk, dv = _attn_bwd(q, k, v, do, scale, "xa_attn_bwd")
    g["w_q"] = _mm(hb, dq, "tn", BF16, "xa_q_dw")
    g["w_k"] = _mm(mb, dk, "tn", BF16, "xa_k_dw")
    g["w_v"] = _mm(mb, dv, "tn", BF16, "xa_v_dw")
    dh = _mm(dq, p["w_q"], "nt", F32, "xa_q_dx")
    dm = _mm(dk, p["w_k"], "nt", F32, "xa_k_dx")
    dm = _mm(dv, p["w_v"], "nt", F32, "xa_v_dx", add=dm)
    _, g["mem_g"] = _norm_bwd(mem, p["mem_g"], dm, 1.0, BF16, "xa_mem_norm_bwd")
    dx, g["pre_g"] = _norm_bwd(x, p["pre_g"], dh, 1.0, F32, "xa_pre_norm_bwd", res=dxo)
    return dx, g


def _local_step(x, mem, cos, sin, target, params_of, carry, on_grads):
    p1 = params_of("ffn1")
    x1, s1 = _ffn_fwd(x, p1, "ffn1", carry)
    p2 = params_of("mix")
    x2, s2 = _mixer_fwd(x1, cos, sin, p2, carry)
    p3 = params_of("xa")
    x3, s3 = _xa_fwd(x2, mem, p3)
    p4 = params_of("ffn2")
    x4, s4 = _ffn_fwd(x3, p4, "ffn2", carry)
    dy, sq_err = _loss_head(x4, target, "loss_head")
    loss = 0.5 / x.shape[1] * jnp.sum(sq_err)
    dx, g4 = _ffn_bwd(dy, p4, s4, "ffn2", carry)
    on_grads("ffn2", g4)
    dx, g3 = _xa_bwd(dx, p3, s3)
    on_grads("xa", g3)
    dx, g2 = _mixer_bwd(dx, cos, sin, p2, s2, carry)
    on_grads("mix", g2)
    dx, g1 = _ffn_bwd(dx, p1, s1, "ffn1", carry)
    on_grads("ffn1", g1)
    return loss, dx


def _split_w_in(w_in):
    D = w_in.shape[0]
    H = D // HEAD
    lora = (w_in.shape[1] - 6 * D - ROPE) // 2
    o = 4 * D
    w_hg = w_in[:, :o].reshape(D, 4, H, HEAD).transpose(0, 2, 1, 3).reshape(D, 4 * D)
    w_cq, w_ckv = w_in[:, o:o + lora], w_in[:, o + lora:o + 2 * lora]
    o += 2 * lora
    w_kpe = jnp.pad(w_in[:, o:o + ROPE], ((0, 0), (0, HEAD - ROPE)))
    o += ROPE
    return {"w_hg": w_hg, "w_cq": w_cq, "w_ckv": w_ckv, "w_kpe": w_kpe, "w_ga": w_in[:, o:o + D], "w_gb": w_in[:, o + D:o + 2 * D]}


def _merge_w_in(g):
    D = g["w_ga"].shape[0]
    H = D // HEAD
    hg = g["w_hg"].reshape(D, H, 4, HEAD).transpose(0, 2, 1, 3).reshape(D, 4 * D)
    return jnp.concatenate([hg, g["w_cq"], g["w_ckv"], g["w_kpe"][:, :ROPE], g["w_ga"], g["w_gb"]], axis=1)


def _split_heads(w, rest):
    K, N = w.shape
    w3 = w.reshape(K, N // (HEAD + rest), HEAD + rest)
    return w3[:, :, :HEAD].reshape(K, -1), w3[:, :, HEAD:].reshape(K, -1)


def _merge_heads(a, b, rest):
    K = a.shape[0]
    H = a.shape[1] // HEAD
    return jnp.concatenate([a.reshape(K, H, HEAD), b.reshape(K, H, rest)], axis=2).reshape(K, H * (HEAD + rest))


BLOCK_WEIGHTS = {
    "ffn1": ("ffn1_w_gate", "ffn1_w_up", "ffn1_w_down"),
    "mix": ("w_in", "mla_w_q_up", "mla_w_kv_up", "w_branch_a", "w_branch_b", "w_out"),
    "xa": ("xa_w_q", "xa_w_k", "xa_w_v", "xa_w_o"),
    "ffn2": ("ffn2_w_gate", "ffn2_w_up", "ffn2_w_down"),
}


def _block_params(block, w, small):
    if block in ("ffn1", "ffn2"):
        return {"pre_g": small[f"{block}_pre_g"], "w_gate": w[f"{block}_w_gate"], "w_up": w[f"{block}_w_up"],
                "w_down": w[f"{block}_w_down"], "post_g": small[f"{block}_post_g"]}
    if block == "xa":
        return {"pre_g": small["xa_pre_g"], "mem_g": small["xa_mem_g"], "post_g": small["xa_post_g"],
                "w_q": w["xa_w_q"], "w_k": w["xa_w_k"], "w_v": w["xa_w_v"], "w_o": w["xa_w_o"]}
    mix = _split_w_in(w["w_in"])
    mix["w_qn"], mix["w_qpe"] = _split_heads(w["mla_w_q_up"], ROPE)
    mix["w_kn"], mix["w_vv"] = _split_heads(w["mla_w_kv_up"], HEAD)
    mix.update(w_branch_a=w["w_branch_a"], w_branch_b=w["w_branch_b"], w_out=w["w_out"], pre_g=small["mix_pre_g"],
               post_g=small["mix_post_g"], hg_norm_g=small["hg_norm_g"], q_norm_g=small["mla_q_norm_g"],
               kv_norm_g=small["mla_kv_norm_g"], lb_logits=small["hgrn_lb_logits"])
    return mix


def _block_grads(block, g):
    if block in ("ffn1", "ffn2"):
        return {f"{block}_{k}": g[k] for k in ("pre_g", "w_gate", "w_up", "w_down", "post_g")}
    if block == "xa":
        return {f"xa_{k}": g[k] for k in ("pre_g", "mem_g", "post_g", "w_q", "w_k", "w_v", "w_o")}
    return dict(w_in=_merge_w_in(g), mla_w_q_up=_merge_heads(g["w_qn"], g["w_qpe"], ROPE),
                mla_w_kv_up=_merge_heads(g["w_kn"], g["w_vv"], HEAD), w_branch_a=g["w_branch_a"],
                w_branch_b=g["w_branch_b"], w_out=g["w_out"], mix_pre_g=g["pre_g"], mix_post_g=g["post_g"],
                hg_norm_g=g["hg_norm_g"], mla_q_norm_g=g["q_norm_g"], mla_kv_norm_g=g["kv_norm_g"],
                hgrn_lb_logits=g["lb_logits"])


def _rope_tables(positions):
    inv_freq = 1.0 / (ROPE_THETA ** (jnp.arange(0, ROPE, 2, dtype=F32) / ROPE))
    ang = positions.astype(F32)[:, None] * inv_freq
    return jnp.tile(jnp.cos(ang), (1, 4)), jnp.tile(jnp.sin(ang), (1, 4))


def _adamw(w, g, m, v, name, pipe=None):
    bc1 = 1.0 - ADAM_B1 ** ADAM_STEP
    bc2 = 1.0 - ADAM_B2 ** ADAM_STEP

    def body(w_ref, g_ref, m_ref, v_ref, go_ref, d_ref, mo_ref, vo_ref):
        g = g_ref[...]
        m = ADAM_B1 * m_ref[...] + (1.0 - ADAM_B1) * g
        v = ADAM_B2 * v_ref[...] + (1.0 - ADAM_B2) * (g * g)
        go_ref[...] = g
        mo_ref[...] = m
        vo_ref[...] = v
        d_ref[...] = -ADAM_LR * ((m / bc1) / (jnp.sqrt(v / bc2) + ADAM_EPS) + ADAM_WD * w_ref[...])

    return _rows(body, [(w, "row"), (g, "row"), (m, "row"), (v, "row")], [(w.shape, F32, "row")] * 4, name, pipe=pipe)


ANY = pl.BlockSpec(memory_space=pl.ANY)
COMM_AXES = ("x", "y", "c")


def _place():
    x, y, c = (lax.axis_index(n) for n in COMM_AXES)
    chips = [(1 - x, y), (x, 1 - y), (1 - x, 1 - y)]
    return x, y, c, 2 * x + y, (x, y, 1 - c), chips


def _remote(src, dst, send_sems, recv_sems, j, to):
    return pltpu.make_async_remote_copy(src_ref=src, dst_ref=dst, send_sem=send_sems.at[j], recv_sem=recv_sems.at[j],
                                        device_id=to, device_id_type=MESH)


def _dma_sems(n):
    return [pltpu.SemaphoreType.DMA((n,)), pltpu.SemaphoreType.DMA((n,))]


def _all_gather(shards, whole, name):
    n = len(shards)

    def body(*refs):
        srcs, outs, send_sems, recv_sems = refs[:n], refs[n:2 * n], refs[2 * n], refs[2 * n + 1]
        x, y, c, me, sibling, chips = _place()
        sent = []

        def start(cp):
            cp.start()
            sent.append(cp)

        def rows(w, h):
            hr = srcs[w].shape[0] // 2
            return pl.ds(h * hr, hr)

        gathered = [w for w in range(n) if whole[w]]
        for w in gathered:
            for j, (cx, cy) in enumerate(chips):
                start(_remote(srcs[w].at[rows(w, c)], outs[w].at[me, rows(w, c)], send_sems, recv_sems, 7 * w + j, (cx, cy, c)))
        for w in range(n):
            start(_remote(srcs[w], outs[w].at[me], send_sems, recv_sems, 7 * w + 6, sibling))
        for w in gathered:
            for j, (cx, cy) in enumerate(chips):
                blk = outs[w].at[2 * cx + cy, rows(w, c)]
                _remote(srcs[w].at[rows(w, c)], blk, send_sems, recv_sems, 7 * w + j, (cx, cy, c)).wait_recv()
                start(_remote(blk, blk, send_sems, recv_sems, 7 * w + 3 + j, sibling))
        for w in gathered:
            for j, (cx, cy) in enumerate(chips):
                blk = outs[w].at[2 * cx + cy, rows(w, 1 - c)]
                _remote(blk, blk, send_sems, recv_sems, 7 * w + 3 + j, sibling).wait_recv()
        for w in range(n):
            _remote(srcs[w], outs[w].at[me], send_sems, recv_sems, 7 * w + 6, sibling).wait_recv()
        for cp in sent:
            cp.wait_send()

    return pl.pallas_call(
        body, in_specs=[ANY] * n, out_specs=[ANY] * n,
        out_shape=[jax.ShapeDtypeStruct((N_CHIP,) + s.shape, s.dtype) for s in shards],
        scratch_shapes=_dma_sems(7 * n), name=name,
    )(*shards)


def _rs_swap(grads, name):
    n = len(grads)

    def body(*refs):
        gs, outs, send_sems, recv_sems = refs[:n], refs[n:2 * n], refs[2 * n], refs[2 * n + 1]
        x, y, c, me, sibling, chips = _place()
        cps = []
        for w in range(n):
            hr = gs[w].shape[1] // 2
            cps.append(_remote(gs[w].at[:, pl.ds((1 - c) * hr, hr)], outs[w], send_sems, recv_sems, w, sibling))
            cps[-1].start()
        for cp in cps:
            cp.wait()

    return pl.pallas_call(
        body, in_specs=[ANY] * n, out_specs=[ANY] * n,
        out_shape=[jax.ShapeDtypeStruct((g.shape[0], g.shape[1] // 2, g.shape[2]), g.dtype) for g in grads],
        scratch_shapes=_dma_sems(n), name=name,
    )(*grads)


def _sum_rows(hr, row_bytes):
    return _tile(hr, max(16, ROW_BUDGET // (2 * row_bytes) // 16 * 16), 16)


def _rs_pair_sum(g, got, c, name):
    S, r, cw = g.shape
    hr = r // 2
    tr = _sum_rows(hr, 3 * cw * 2)
    nrb = hr // tr

    def body(c_ref, a_ref, b_ref, o_ref):
        o_ref[...] = (a_ref[...].astype(F32) + b_ref[...].astype(F32)).astype(BF16)

    return pl.pallas_call(
        body,
        grid_spec=pltpu.PrefetchScalarGridSpec(
            num_scalar_prefetch=1, grid=(S, nrb),
            in_specs=[pl.BlockSpec((None, tr, cw), lambda k, i, c_ref: (k, c_ref[0] * nrb + i, 0)),
                      pl.BlockSpec((None, tr, cw), lambda k, i, c_ref: (k, i, 0))],
            out_specs=pl.BlockSpec((None, tr, cw), lambda k, i, c_ref: (k, i, 0)),
        ),
        out_shape=jax.ShapeDtypeStruct((S, hr, cw), BF16),
        compiler_params=_params(("parallel", "parallel")), name=name,
    )(c, g, got)


def _rs_chip_sum(pair, got, place, name):
    S, hr, cw = pair.shape
    tr = _sum_rows(hr, cw * (4 * 2 + 4))

    def body(p_ref, a_ref, z_ref, o_ref):
        o_ref[...] = a_ref[...].astype(F32) + z_ref[0].astype(F32) + z_ref[1].astype(F32) + z_ref[2].astype(F32)

    return pl.pallas_call(
        body,
        grid_spec=pltpu.PrefetchScalarGridSpec(
            num_scalar_prefetch=1, grid=(hr // tr,),
            in_specs=[pl.BlockSpec((None, tr, cw), lambda i, p_ref: (p_ref[0], i, 0)),
                      pl.BlockSpec((3, tr, cw), lambda i, p_ref: (0, i, 0))],
            out_specs=pl.BlockSpec((None, tr, cw), lambda i, p_ref: (p_ref[1], i, 0)),
        ),
        out_shape=jax.ShapeDtypeStruct((2, hr, cw), F32),
        compiler_params=_params(("parallel",)), name=name,
    )(place, pair, got)


def _rs_share(halves, name):
    n = len(halves)

    def body(*refs):
        outs, send_sems, recv_sems = refs[n:2 * n], refs[2 * n], refs[2 * n + 1]
        x, y, c, me, sibling, chips = _place()
        cps = []
        for w in range(n):
            cps.append(_remote(outs[w].at[c], outs[w].at[c], send_sems, recv_sems, w, sibling))
            cps[-1].start()
        for w in range(n):
            _remote(outs[w].at[1 - c], outs[w].at[1 - c], send_sems, recv_sems, w, sibling).wait_recv()
        for cp in cps:
            cp.wait_send()

    return pl.pallas_call(
        body, in_specs=[ANY] * n, out_specs=[ANY] * n,
        out_shape=[jax.ShapeDtypeStruct(h.shape, h.dtype) for h in halves],
        input_output_aliases={i: i for i in range(n)},
        scratch_shapes=_dma_sems(n), name=name,
    )(*halves)


def _rs_begin(names, grads, core, fractions, tag):
    slotted = [_to_slots(n, grads[n]) for n in names]
    got = _rs_swap(slotted, f"grads_sibling_swap_{tag}")
    pairs = [_rs_pair_sum(s, t, core.reshape(1), f"grads_pair_sum_{n}") for n, s, t in zip(names, slotted, got)]
    lands = [lax.empty((N_CHIP - 1,) + p.shape[1:], p.dtype) for p in pairs]
    return pairs, _Pipe("reduce", pairs, lands, fractions)


def _rs_end(names, pairs, pipe, place, shapes, tag):
    _pipe_flush(pipe, f"grads_chip_exchange_rest_{tag}")
    halves = [_rs_chip_sum(p, o, place, f"grads_chip_sum_{n}") for n, p, o in zip(names, pairs, pipe.lands)]
    both = _rs_share(halves, f"grads_sibling_share_{tag}")
    return {n: b.reshape(s) for n, b, s in zip(names, both, shapes)}


def _all_reduce_small(s, name):
    flips = [(dx, dy, dc) for dx in (0, 1) for dy in (0, 1) for dc in (0, 1) if (dx, dy, dc) != (0, 0, 0)]

    def body(s_ref, o_ref, buf, send_sems, recv_sems):
        x, y, c = (lax.axis_index(n) for n in COMM_AXES)
        me = 4 * x + 2 * y + c
        buf[me] = s_ref[...]
        peers = [((1 - x) if dx else x, (1 - y) if dy else y, (1 - c) if dc else c) for dx, dy, dc in flips]
        sent = [_remote(s_ref, buf.at[me], send_sems, recv_sems, j, p) for j, p in enumerate(peers)]
        for cp in sent:
            cp.start()
        for j, (px, py, pc) in enumerate(peers):
            _remote(s_ref, buf.at[4 * px + 2 * py + pc], send_sems, recv_sems, j, (px, py, pc)).wait_recv()
        for cp in sent:
            cp.wait_send()
        acc = buf[0]
        for d in range(1, N_DEV):
            acc = acc + buf[d]
        o_ref[...] = acc

    vmem = pl.BlockSpec(memory_space=pltpu.VMEM)
    return pl.pallas_call(
        body, in_specs=[vmem], out_specs=vmem, out_shape=jax.ShapeDtypeStruct(s.shape, F32),
        scratch_shapes=[pltpu.VMEM((N_DEV,) + s.shape, F32), pltpu.SemaphoreType.DMA((7,)), pltpu.SemaphoreType.DMA((7,))],
        name=name,
    )(s)


BIG = {
    "ffn1_w_gate": 1, "ffn1_w_up": 1, "ffn1_w_down": 0, "w_in": 1, "mla_w_q_up": 1, "mla_w_kv_up": 1,
    "w_branch_a": 0, "w_branch_b": 0, "w_out": 0, "xa_w_q": 0, "xa_w_k": 0, "xa_w_v": 0, "xa_w_o": 1,
    "ffn2_w_gate": 1, "ffn2_w_up": 1, "ffn2_w_down": 0,
}
WEIGHTS = [
    "hgrn_lb_logits", "ffn1_pre_g", "ffn1_w_gate", "ffn1_w_up", "ffn1_w_down", "ffn1_post_g", "mix_pre_g", "w_in",
    "hg_norm_g", "mla_q_norm_g", "mla_w_q_up", "mla_kv_norm_g", "mla_w_kv_up", "w_branch_a", "w_branch_b", "w_out",
    "mix_post_g", "xa_pre_g", "xa_mem_g", "xa_w_q", "xa_w_k", "xa_w_v", "xa_w_o", "xa_post_g", "ffn2_pre_g",
    "ffn2_w_gate", "ffn2_w_up", "ffn2_w_down", "ffn2_post_g",
]
SMALL = [n for n in WEIGHTS if n not in BIG]
SLOTTED = ("ffn1_w_gate", "ffn1_w_up", "ffn2_w_gate", "ffn2_w_up", "xa_w_o")
ADAMW_CARRIERS = ("w_in", "ffn2_w_gate", "ffn2_w_up", "ffn2_w_down")


def _from_slots(name, g):
    S, r, cw = g.shape
    if BIG[name] == 0:
        return g.reshape(S * r, cw)
    return g if name in SLOTTED else g.transpose(1, 0, 2).reshape(r, S * cw)


def _to_slots(name, g):
    if g.ndim == 3:
        return g
    if BIG[name] == 0:
        return g.reshape(N_CHIP, g.shape[0] // N_CHIP, g.shape[1])
    return g.reshape(g.shape[0], N_CHIP, g.shape[1] // N_CHIP).transpose(1, 0, 2)


def _pack_small(vals, width):
    rows = [jnp.pad(v, ((0, 0), (0, width - v.shape[1]))) for v in vals]
    s = jnp.concatenate(rows, axis=0)
    return jnp.pad(s, ((0, -s.shape[0] % 8), (0, 0)))


def _unpack_small(s, shapes):
    out, o = [], 0
    for r, w in shapes:
        out.append(s[o:o + r, :w])
        o += r
    return out


def kernel(x, mem, positions, hgrn_lb_logits, ffn1_pre_g, ffn1_w_gate, ffn1_w_up, ffn1_w_down, ffn1_post_g, mix_pre_g, w_in, hg_norm_g, mla_q_norm_g, mla_w_q_up, mla_kv_norm_g, mla_w_kv_up, w_branch_a, w_branch_b, w_out, mix_post_g, xa_pre_g, xa_mem_g, xa_w_q, xa_w_k, xa_w_v, xa_w_o, xa_post_g, ffn2_pre_g, ffn2_w_gate, ffn2_w_up, ffn2_w_down, ffn2_post_g, loss_target, m_hgrn_lb_logits, m_ffn1_pre_g, m_ffn1_w_gate, m_ffn1_w_up, m_ffn1_w_down, m_ffn1_post_g, m_mix_pre_g, m_w_in, m_hg_norm_g, m_mla_q_norm_g, m_mla_w_q_up, m_mla_kv_norm_g, m_mla_w_kv_up, m_w_branch_a, m_w_branch_b, m_w_out, m_mix_post_g, m_xa_pre_g, m_xa_mem_g, m_xa_w_q, m_xa_w_k, m_xa_w_v, m_xa_w_o, m_xa_post_g, m_ffn2_pre_g, m_ffn2_w_gate, m_ffn2_w_up, m_ffn2_w_down, m_ffn2_post_g, v_hgrn_lb_logits, v_ffn1_pre_g, v_ffn1_w_gate, v_ffn1_w_up, v_ffn1_w_down, v_ffn1_post_g, v_mix_pre_g, v_w_in, v_hg_norm_g, v_mla_q_norm_g, v_mla_w_q_up, v_mla_kv_norm_g, v_mla_w_kv_up, v_w_branch_a, v_w_branch_b, v_w_out, v_mix_post_g, v_xa_pre_g, v_xa_mem_g, v_xa_w_q, v_xa_w_k, v_xa_w_v, v_xa_w_o, v_xa_post_g, v_ffn2_pre_g, v_ffn2_w_gate, v_ffn2_w_up, v_ffn2_w_down, v_ffn2_post_g):
    a = dict(locals())
    big = list(BIG)
    small = {n: a[n] for n in SMALL}
    core = lax.axis_index("c").astype(jnp.int32)
    place = jnp.stack([(2 * lax.axis_index("x") + lax.axis_index("y")).astype(jnp.int32), core])

    shards = {n: a[n][0].astype(BF16) for n in big}
    whole = [n in BLOCK_WEIGHTS["ffn1"] for n in big]
    lands = dict(zip(big, _all_gather([shards[n] for n in big], whole, "weights_all_gather")))
    late = BLOCK_WEIGHTS["xa"] + BLOCK_WEIGHTS["ffn2"]
    gather = lambda names, fr: _Pipe("gather", [shards[n] for n in names], [lands[n] for n in names], fr)
    pipe_mix, pipe_late = gather(BLOCK_WEIGHTS["mix"], (1, 1, 1)), gather(late, (2, 5, 3))
    carry = {"ffn1_gate": pipe_mix, "ffn1_up": pipe_mix, "ffn1_down": pipe_mix,
             "mix_in_hg": pipe_late, "hg_scan": pipe_late, "mla_attn": pipe_late}

    def params_of(block):
        for first, names, pipe in (("mix", BLOCK_WEIGHTS["mix"], pipe_mix), ("xa", late, pipe_late)):
            if block == first:
                _pipe_flush(pipe, f"weights_gather_rest_{block}")
                lands.update(zip(names, pipe.lands))
        return _block_params(block, {n: _from_slots(n, lands[n]) for n in BLOCK_WEIGHTS[block]}, small)

    g_small, g_big, open_groups, held = {}, {}, {}, {}

    def begin(tag, names, fractions, carriers):
        pairs, pipe = _rs_begin(names, held, core, fractions, tag)
        open_groups[tag] = (names, pairs, pipe)
        carry.update({c: pipe for c in carriers})

    def end(tag):
        names, pairs, pipe = open_groups.pop(tag)
        g_big.update(_rs_end(names, pairs, pipe, place, [a[n].shape[1:] for n in names], tag))

    def on_grads(block, g):
        for n, v in _block_grads(block, g).items():
            (held if n in BIG else g_small)[n] = v
        if block == "ffn2":
            begin("ffn2", BLOCK_WEIGHTS["ffn2"], (3, 2), ("hg_scan_bwd", "mla_attn_bwd"))
        elif block == "mix":
            end("ffn2")
            begin("mid", BLOCK_WEIGHTS["xa"] + BLOCK_WEIGHTS["mix"], (1,) * 6,
                  [f"ffn1_{k}" for k in ("down_dw", "down_dx", "gate_dw", "up_dw", "gate_dx", "up_dx")])
        elif block == "ffn1":
            end("mid")
            begin("ffn1", BLOCK_WEIGHTS["ffn1"], (2, 1, 1, 1), [f"adamw_{n}" for n in ADAMW_CARRIERS])

    cos, sin = _rope_tables(positions[0])
    loss_part, grad_x = _local_step(x[0], mem[0], cos, sin, loss_target[0], params_of, carry, on_grads)

    small_shapes = [a[n].shape for n in SMALL]
    width = max(s[1] for s in small_shapes)
    gs = _all_reduce_small(_pack_small([g_small[n] for n in SMALL], width), "small_grads_all_reduce")

    out_g, out_d, out_m, out_v = {}, {}, {}, {}

    def update(n):
        res = _adamw(a[n][0], g_big[n], a["m_" + n][0], a["v_" + n][0], f"adamw_{n}", pipe=carry.get(f"adamw_{n}"))
        out_g[n], out_d[n], out_m[n], out_v[n] = (t.reshape(a[n].shape) for t in res)

    done_first = list(ADAMW_CARRIERS) + [n for n in big if n not in ADAMW_CARRIERS and n not in BLOCK_WEIGHTS["ffn1"]]
    for n in done_first:
        update(n)
    end("ffn1")
    for n in BLOCK_WEIGHTS["ffn1"]:
        update(n)
    sw, sm, sv = (_pack_small([a[p + n] for n in SMALL], width) for p in ("", "m_", "v_"))
    for t, dst in zip(_adamw(sw, gs, sm, sv, "adamw_small"), (out_g, out_d, out_m, out_v)):
        dst.update(zip(SMALL, _unpack_small(t, small_shapes)))

    loss = lax.psum(loss_part, COMM_AXES)
    return (loss, grad_x[None], *[out_g[n] for n in WEIGHTS], *[out_d[n] for n in WEIGHTS],
            *[out_m[n] for n in WEIGHTS], *[out_v[n] for n in WEIGHTS])
```

```python
import functools

import jax
import jax.numpy as jnp
from jax import lax
from jax.experimental import pallas as pl
from jax.experimental.pallas import tpu as pltpu

F32 = jnp.float32
BF16 = jnp.bfloat16
EPS = 1e-6
HEAD = 128
ROPE = 64
CHUNK = 64
SUB = 16
HG_BLOCK = 256
ROPE_THETA = 10000.0
FFN_RESIDUAL_WEIGHT = 0.5
ADAM_LR, ADAM_B1, ADAM_B2, ADAM_EPS, ADAM_WD, ADAM_STEP = 0.001, 0.9, 0.999, 1e-08, 0.01, 10
VMEM_LIMIT = 56 * 2**20
ROW_BUDGET = 20 * 2**20
NEG = -1e30
MESH = pl.DeviceIdType.MESH
N_CHIP = 4
N_DEV = 8


def _params(sem):
    return pltpu.CompilerParams(dimension_semantics=sem, vmem_limit_bytes=VMEM_LIMIT)


def _tile(n, cap, mult):
    if n <= cap:
        return n
    t = (cap // mult) * mult
    while t >= mult:
        if n % t == 0:
            return t
        t -= mult
    raise ValueError(f"no tile for {n} under {cap}")


def _split_rows(n, fractions):
    if n % 16:
        return [(0, n)] + [(n, 0)] * (len(fractions) - 1)
    units, total, acc, cuts = n // 16, sum(fractions), 0, [0]
    for f in fractions[:-1]:
        acc += f
        cuts.append(round(units * acc / total))
    cuts.append(units)
    return [(16 * lo, 16 * (hi - lo)) for lo, hi in zip(cuts, cuts[1:])]


class _Pipe:
    def __init__(self, kind, srcs, lands, fractions):
        self.kind, self.srcs, self.lands = kind, list(srcs), list(lands)
        per_w = [_split_rows(s.shape[0] // 2 if kind == "gather" else s.shape[1], fractions) for s in srcs]
        self.parts = [[pw[i] for pw in per_w] for i in range(len(fractions))]
        self.taken = 0
        self.sems = (6 if kind == "gather" else 3) * len(srcs)

    def take(self):
        self.taken += 1
        return self.parts[self.taken - 1]

    def rest(self):
        left = self.parts[self.taken:]
        self.taken = len(self.parts)
        return [(left[0][w][0], sum(p[w][1] for p in left)) for w in range(len(self.srcs))] if left else None


def _pipe_copies(kind, rows, lands, srcs, send_sems, recv_sems):
    x, y, c, me, sibling, chips = _place()
    out = []
    for w, (r0, nr) in enumerate(rows):
        for j, (cx, cy) in enumerate(chips if nr else []):
            k, to = 2 * cx + cy, (cx, cy, c)
            if kind == "gather":
                rs = pl.ds(c * (srcs[w].shape[0] // 2) + r0, nr)
                src, dst, got, j0 = srcs[w].at[rs], lands[w].at[me, rs], lands[w].at[k, rs], 6 * w + j
            else:
                rs = pl.ds(r0, nr)
                src, dst, got, j0 = srcs[w].at[k, rs], lands[w].at[j, rs], lands[w].at[j, rs], 3 * w + j
            out.append((_remote(src, dst, send_sems, recv_sems, j0, to), _remote(src, got, send_sems, recv_sems, j0, to), w, j, k))
    return out


def _pipe_start(kind, rows, lands, srcs, send_sems, recv_sems):
    for send, _, _, _, _ in _pipe_copies(kind, rows, lands, srcs, send_sems, recv_sems):
        send.start()


def _pipe_finish(kind, rows, lands, srcs, send_sems, recv_sems):
    x, y, c, me, sibling, chips = _place()
    copies = _pipe_copies(kind, rows, lands, srcs, send_sems, recv_sems)
    passed = []
    for _, arrival, _, _, _ in copies:
        arrival.wait_recv()
    if kind == "gather":
        for w, (r0, nr) in enumerate(rows):
            hr = srcs[w].shape[0] // 2
            for j, (cx, cy) in enumerate(chips if nr else []):
                blk = lands[w].at[2 * cx + cy, pl.ds(c * hr + r0, nr)]
                passed.append(_remote(blk, blk, send_sems, recv_sems, 6 * w + 3 + j, sibling))
                passed[-1].start()
        for w, (r0, nr) in enumerate(rows):
            hr = srcs[w].shape[0] // 2
            for j, (cx, cy) in enumerate(chips if nr else []):
                blk = lands[w].at[2 * cx + cy, pl.ds((1 - c) * hr + r0, nr)]
                _remote(blk, blk, send_sems, recv_sems, 6 * w + 3 + j, sibling).wait_recv()
    for send, _, _, _, _ in copies:
        send.wait_send()
    for cp in passed:
        cp.wait_send()


def _carry_call(body, *, grid, in_specs, out_specs, out_shape, scratch_shapes, sem, name, args, pipe=None):
    single = not isinstance(out_shape, (list, tuple))
    if single:
        out_specs, out_shape = [out_specs], [out_shape]
    if pipe is None or pipe.taken >= len(pipe.parts):
        res = pl.pallas_call(body, grid=grid, in_specs=in_specs, out_specs=out_specs, out_shape=out_shape,
                             scratch_shapes=scratch_shapes, compiler_params=_params(sem), name=name)(*args)
        return res[0] if single else res
    rows, kind = pipe.take(), pipe.kind
    n_in, n_out, n_l, n_s, n_scr = len(args), len(out_shape), len(pipe.lands), len(pipe.srcs), len(scratch_shapes)

    def wrapped(*refs):
        ins, srcs = refs[:n_in], refs[n_in + n_l:n_in + n_l + n_s]
        o0 = n_in + n_l + n_s
        outs, lands = refs[o0:o0 + n_out], refs[o0 + n_out:o0 + n_out + n_l]
        scr = refs[o0 + n_out + n_l:o0 + n_out + n_l + n_scr]
        send_sems, recv_sems = refs[-2], refs[-1]
        ids = [pl.program_id(ax) for ax in range(len(grid))]
        first = functools.reduce(jnp.logical_and, [i == 0 for i in ids])
        last = functools.reduce(jnp.logical_and, [i == g - 1 for i, g in zip(ids, grid)])

        @pl.when(first)
        def _():
            _pipe_start(kind, rows, lands, srcs, send_sems, recv_sems)

        body(*ins, *outs, *scr)

        @pl.when(last)
        def _():
            _pipe_finish(kind, rows, lands, srcs, send_sems, recv_sems)

    res = pl.pallas_call(
        wrapped, grid=grid, in_specs=list(in_specs) + [ANY] * (n_l + n_s), out_specs=list(out_specs) + [ANY] * n_l,
        out_shape=list(out_shape) + [jax.ShapeDtypeStruct(l.shape, l.dtype) for l in pipe.lands],
        input_output_aliases={n_in + i: n_out + i for i in range(n_l)},
        scratch_shapes=list(scratch_shapes) + _dma_sems(pipe.sems),
        compiler_params=_params(("arbitrary",) * len(grid)), name=name,
    )(*args, *pipe.lands, *pipe.srcs)
    pipe.lands = list(res[n_out:])
    return res[0] if single else list(res[:n_out])


def _pipe_flush(pipe, name):
    rows = pipe.rest()
    if rows is None:
        return
    n_l, n_s, kind = len(pipe.lands), len(pipe.srcs), pipe.kind

    def body(*refs):
        srcs, lands = refs[n_l:n_l + n_s], refs[n_l + n_s:2 * n_l + n_s]
        _pipe_start(kind, rows, lands, srcs, refs[-2], refs[-1])
        _pipe_finish(kind, rows, lands, srcs, refs[-2], refs[-1])

    pipe.lands = list(pl.pallas_call(
        body, in_specs=[ANY] * (n_l + n_s), out_specs=[ANY] * n_l,
        out_shape=[jax.ShapeDtypeStruct(l.shape, l.dtype) for l in pipe.lands],
        input_output_aliases={i: i for i in range(n_l)}, scratch_shapes=_dma_sems(pipe.sems), name=name,
    )(*pipe.lands, *pipe.srcs))


def _sigmoid(x):
    return 1.0 / (1.0 + jnp.exp(-x))


def _dot(a, b, dims):
    return lax.dot_general(a, b, (dims, ((), ())), preferred_element_type=F32)


NN = ((1,), (0,))
NT = ((1,), (1,))
TN = ((0,), (0,))


def _mm(a, b, mode, out_dtype, name, add=None, out_slots=0, pipe=None, tm_cap=1024, tn_cap=512, tk_cap=2816):
    slot_cap = 1408
    b_slots = b.shape[0] if b.ndim == 3 else 0
    bs = (b.shape[1], b_slots * b.shape[2]) if b_slots else b.shape
    if mode == "nn":
        (M, K), (K2, N) = a.shape, bs
    elif mode == "nt":
        (M, K), (N, K2) = a.shape, bs
    else:
        (K, M), (K2, N) = a.shape, bs
    assert K == K2, (a.shape, b.shape, mode)
    tm = _tile(M, tm_cap, 128)
    tn = _tile(N // (b_slots or out_slots), slot_cap, 128) if (out_slots or (b_slots and mode == "nn")) else _tile(N, tn_cap, 128)
    tk = _tile(K // b_slots, slot_cap, 128) if (b_slots and mode == "nt") else _tile(K, tk_cap, 128)
    nk = K // tk
    a_spec = pl.BlockSpec((tk, tm), lambda i, j, k: (k, i)) if mode == "tn" else pl.BlockSpec((tm, tk), lambda i, j, k: (i, k))
    if b_slots and mode == "nn":
        per = b.shape[2] // tn
        b_spec = pl.BlockSpec((None, tk, tn), lambda i, j, k: (j // per, k, j % per))
    elif b_slots:
        per = b.shape[2] // tk
        b_spec = pl.BlockSpec((None, tn, tk), lambda i, j, k: (k // per, j, k % per))
    else:
        b_spec = pl.BlockSpec((tn, tk), lambda i, j, k: (j, k)) if mode == "nt" else pl.BlockSpec((tk, tn), lambda i, j, k: (k, j))
    if out_slots:
        per_o = N // out_slots // tn
        o_spec = pl.BlockSpec((None, tm, tn), lambda i, j, k: (j // per_o, i, j % per_o))
        o_shape = (out_slots, M, N // out_slots)
    else:
        o_spec = pl.BlockSpec((tm, tn), lambda i, j, k: (i, j))
        o_shape = (M, N)
    dims = {"nn": NN, "nt": NT, "tn": TN}[mode]
    has_add = add is not None

    def body(*refs):
        a_ref, b_ref = refs[0], refs[1]
        add_ref = refs[2] if has_add else None
        o_ref = refs[3] if has_add else refs[2]
        p = _dot(a_ref[...].astype(BF16), b_ref[...].astype(BF16), dims)

        def finish(val):
            if has_add:
                val = val + add_ref[...]
            o_ref[...] = val.astype(out_dtype)

        if nk == 1:
            finish(p)
        else:
            acc_ref = refs[-1]
            k = pl.program_id(2)

            @pl.when(k == 0)
            def _():
                acc_ref[...] = p

            @pl.when(k > 0)
            def _():
                acc_ref[...] += p

            @pl.when(k == nk - 1)
            def _():
                finish(acc_ref[...])

    assert not (has_add and out_slots)
    in_specs = [a_spec, b_spec] + ([o_spec] if has_add else [])
    args = (a, b) + ((add,) if has_add else ())
    return _carry_call(
        body, grid=(M // tm, N // tn, nk), in_specs=in_specs, out_specs=o_spec,
        out_shape=jax.ShapeDtypeStruct(o_shape, out_dtype),
        scratch_shapes=[pltpu.VMEM((tm, tn), F32)] if nk > 1 else [],
        sem=("parallel", "parallel", "arbitrary"), name=name, args=args, pipe=pipe,
    )


def _rows(body, ins, outs, name, pipe=None):
    T = next(a.shape[0] for a, k in ins if k == "row")
    per_row = sum(a.shape[1] * a.dtype.itemsize for a, k in ins if k == "row")
    per_row += sum(s[1] * jnp.dtype(d).itemsize for s, d, k in outs if k == "row")
    tr = next(t for t in (512, 256, 128, 64, 32, 16, 8) if T % t == 0 and 2 * t * per_row <= ROW_BUDGET)
    in_specs = [
        pl.BlockSpec((tr, a.shape[1]), lambda i: (i, 0)) if k == "row" else pl.BlockSpec(a.shape, lambda i: (0, 0))
        for a, k in ins
    ]
    out_specs = [
        pl.BlockSpec((tr, s[1]), lambda i: (i, 0)) if k == "row" else pl.BlockSpec(s, lambda i: (0, 0))
        for s, d, k in outs
    ]
    has_acc = any(k == "acc" for _, _, k in outs)
    return _carry_call(
        body, grid=(T // tr,), in_specs=in_specs, out_specs=out_specs,
        out_shape=[jax.ShapeDtypeStruct(s, d) for s, d, k in outs], scratch_shapes=[],
        sem=("arbitrary",) if has_acc else ("parallel",), name=name, args=[a for a, _ in ins], pipe=pipe,
    )


def _rstd(x):
    return lax.rsqrt(jnp.mean(x * x, axis=-1, keepdims=True) + EPS)


def _norm_fwd(x, g, name, pipe=None):
    def body(x_ref, g_ref, o_ref):
        x = x_ref[...]
        o_ref[...] = (x * _rstd(x) * g_ref[...]).astype(BF16)

    return _rows(body, [(x, "row"), (g, "vec")], [(x.shape, BF16, "row")], name, pipe=pipe)[0]


def _postnorm_fwd(x, y, g, weight, name, pipe=None):
    def body(x_ref, y_ref, g_ref, o_ref):
        y = y_ref[...]
        o_ref[...] = x_ref[...] + weight * (y * _rstd(y) * g_ref[...])

    return _rows(body, [(x, "row"), (y, "row"), (g, "vec")], [(x.shape, F32, "row")], name, pipe=pipe)[0]


def _norm_bwd(x, g, dy, weight, out_dtype, name, res=None):
    has_res = res is not None

    def body(*refs):
        x_ref, g_ref, dy_ref = refs[:3]
        res_ref = refs[3] if has_res else None
        dx_ref, dg_ref = refs[-2], refs[-1]

        @pl.when(pl.program_id(0) == 0)
        def _():
            dg_ref[...] = jnp.zeros_like(dg_ref)

        x = x_ref[...]
        dn = dy_ref[...].astype(F32) * weight
        r = _rstd(x)
        xhat = x * r
        dg_ref[...] += jnp.sum(dn * xhat, axis=0, keepdims=True)
        dxh = dn * g_ref[...]
        dx = r * (dxh - xhat * jnp.mean(dxh * xhat, axis=-1, keepdims=True))
        if has_res:
            dx = dx + res_ref[...]
        dx_ref[...] = dx.astype(out_dtype)

    ins = [(x, "row"), (g, "vec"), (dy, "row")] + ([(res, "row")] if has_res else [])
    return _rows(body, ins, [(x.shape, out_dtype, "row"), (g.shape, F32, "acc")], name)


def _swiglu_fwd(a, b, name, pipe=None):
    def body(a_ref, b_ref, o_ref):
        a = a_ref[...]
        o_ref[...] = (a * _sigmoid(a) * b_ref[...]).astype(BF16)

    return _rows(body, [(a, "row"), (b, "row")], [(a.shape, BF16, "row")], name, pipe=pipe)[0]


def _swiglu_bwd(a, b, ds, name):
    def body(a_ref, b_ref, ds_ref, da_ref, db_ref):
        a, ds = a_ref[...], ds_ref[...]
        sg = _sigmoid(a)
        da_ref[...] = (ds * b_ref[...] * (sg * (1.0 + a * (1.0 - sg)))).astype(BF16)
        db_ref[...] = (ds * (a * sg)).astype(BF16)

    return _rows(body, [(a, "row"), (b, "row"), (ds, "row")], [(a.shape, BF16, "row"), (a.shape, BF16, "row")], name)


def _merge_fwd(ga, gb, ya, yb, name):
    def body(ga_ref, gb_ref, ya_ref, yb_ref, o_ref):
        o_ref[...] = (_sigmoid(ga_ref[...]) * ya_ref[...] + _sigmoid(gb_ref[...]) * yb_ref[...]).astype(BF16)

    return _rows(body, [(ga, "row"), (gb, "row"), (ya, "row"), (yb, "row")], [(ga.shape, BF16, "row")], name)[0]


def _merge_bwd(ga, gb, ya, yb, dy, name):
    def body(ga_ref, gb_ref, ya_ref, yb_ref, dy_ref, dya_ref, dyb_ref, dga_ref, dgb_ref):
        dy = dy_ref[...]
        sa, sb = _sigmoid(ga_ref[...]), _sigmoid(gb_ref[...])
        dya_ref[...] = (dy * sa).astype(BF16)
        dyb_ref[...] = (dy * sb).astype(BF16)
        dga_ref[...] = (dy * ya_ref[...] * (sa * (1.0 - sa))).astype(BF16)
        dgb_ref[...] = (dy * yb_ref[...] * (sb * (1.0 - sb))).astype(BF16)

    ins = [(ga, "row"), (gb, "row"), (ya, "row"), (yb, "row"), (dy, "row")]
    return _rows(body, ins, [(ga.shape, BF16, "row")] * 4, name)


def _loss_head(y, target, name):
    D = y.shape[1]

    def body(y_ref, t_ref, dy_ref, acc_ref):
        @pl.when(pl.program_id(0) == 0)
        def _():
            acc_ref[...] = jnp.zeros_like(acc_ref)

        err = y_ref[...] - t_ref[...]
        dy_ref[...] = err * (1.0 / D)
        acc_ref[...] += jnp.sum(err * err, axis=0, keepdims=True)

    return _rows(body, [(y, "row"), (target, "row")], [(y.shape, F32, "row"), ((1, D), F32, "acc")], name)


def _rot(x):
    lane = lax.broadcasted_iota(jnp.int32, x.shape, 1)
    return jnp.where((lane % ROPE) < ROPE // 2, -pltpu.roll(x, 128 - ROPE // 2, 1), pltpu.roll(x, ROPE // 2, 1))


def _rope_q_fwd(qpe, cos, sin, name):
    T, W = qpe.shape
    tr = min(T, 512)
    blk = pl.BlockSpec((tr, 128), lambda i, j: (i, j))
    tab = pl.BlockSpec((tr, 128), lambda i, j: (i, 0))

    def body(x_ref, c_ref, s_ref, o_ref):
        x = x_ref[...]
        o_ref[...] = (x * c_ref[...] + _rot(x) * s_ref[...]).astype(BF16)

    return pl.pallas_call(
        body, grid=(T // tr, W // 128), in_specs=[blk, tab, tab], out_specs=blk,
        out_shape=jax.ShapeDtypeStruct((T, W), BF16), compiler_params=_params(("parallel", "parallel")), name=name,
    )(qpe, cos, sin)


def _rope_q_bwd(dq_heads, cos, sin, name):
    T, W = dq_heads.shape
    tr = min(T, 512)
    even = pl.BlockSpec((tr, 128), lambda i, j: (i, 2 * j))
    odd = pl.BlockSpec((tr, 128), lambda i, j: (i, 2 * j + 1))
    tab = pl.BlockSpec((tr, 128), lambda i, j: (i, 0))

    def body(a_ref, b_ref, c_ref, s_ref, o_ref):
        d = a_ref[...] + b_ref[...]
        o_ref[...] = (d * c_ref[...] - _rot(d * s_ref[...])).astype(BF16)

    return pl.pallas_call(
        body, grid=(T // tr, W // 256), in_specs=[even, odd, tab, tab],
        out_specs=pl.BlockSpec((tr, 128), lambda i, j: (i, j)),
        out_shape=jax.ShapeDtypeStruct((T, W // 2), BF16), compiler_params=_params(("parallel", "parallel")), name=name,
    )(dq_heads, dq_heads, cos, sin)


def _rope_k_fwd(kpe, cos, sin, name):
    def body(x_ref, c_ref, s_ref, o_ref):
        x = x_ref[...]
        y = x * c_ref[...] + _rot(x) * s_ref[...]
        o_ref[...] = (y + pltpu.roll(y, ROPE, 1)).astype(BF16)

    return _rows(body, [(kpe, "row"), (cos, "row"), (sin, "row")], [(kpe.shape, BF16, "row")], name)[0]


def _rope_k_bwd(dk_heads, cos, sin, name):
    T, W = dk_heads.shape

    def body(d_ref, c_ref, s_ref, o_ref):
        d = d_ref[:, 0:128]
        for h in range(1, W // 128):
            d = d + d_ref[:, h * 128:(h + 1) * 128]
        d = d + pltpu.roll(d, ROPE, 1)
        dx = d * c_ref[...] - _rot(d * s_ref[...])
        lane = lax.broadcasted_iota(jnp.int32, dx.shape, 1)
        o_ref[...] = jnp.where(lane < ROPE, dx, 0.0).astype(BF16)

    return _rows(body, [(dk_heads, "row"), (cos, "row"), (sin, "row")], [((T, 128), BF16, "row")], name)[0]


def _attn_probs(q, k, qpe, kpe, scale, causal, q0):
    s = _dot(q, k, NT)
    if qpe is not None:
        s = s + _dot(qpe, kpe, NT)
    s = s * scale
    if causal:
        row = q0 + lax.broadcasted_iota(jnp.int32, s.shape, 0)
        col = lax.broadcasted_iota(jnp.int32, s.shape, 1)
        s = jnp.where((col // CHUNK) <= (row // CHUNK), s, NEG)
    p = jnp.exp(s - jnp.max(s, axis=-1, keepdims=True))
    return p / jnp.sum(p, axis=-1, keepdims=True)


def _pe_mask(x, h):
    lane = lax.broadcasted_iota(jnp.int32, x.shape, 1)
    return jnp.where((lane // ROPE) == (h % 2), x, jnp.zeros_like(x))


def _attn_fwd(q, k, v, scale, name, qpe=None, kpe=None, causal=False, pipe=None):
    T, W = q.shape
    Tk = k.shape[0]
    H = W // HEAD
    tq = min(T, 256)
    nq = T // tq
    has_pe = qpe is not None
    qs = pl.BlockSpec((tq, HEAD), lambda h, i: (i, h))
    ks = pl.BlockSpec((Tk, HEAD), lambda h, i: (0, h))
    in_specs, args = [qs, ks, ks], [q, k, v]
    if has_pe:
        in_specs += [pl.BlockSpec((tq, HEAD), lambda h, i: (i, h // 2)), pl.BlockSpec((Tk, HEAD), lambda h, i: (0, 0))]
        args += [qpe, kpe]

    def body(*refs):
        q_ref, k_ref, v_ref = refs[:3]
        o_ref = refs[-1]
        h, i = pl.program_id(0), pl.program_id(1)

        def compute(klen):
            qp = _pe_mask(refs[3][...], h) if has_pe else None
            kp = refs[4][0:klen, :] if has_pe else None
            p = _attn_probs(q_ref[...], k_ref[0:klen, :], qp, kp, scale, causal, i * tq)
            o_ref[...] = _dot(p.astype(BF16), v_ref[0:klen, :], NN).astype(BF16)

        if causal:
            for qi in range(nq):
                pl.when(i == qi)(functools.partial(compute, (qi + 1) * tq))
        else:
            compute(Tk)

    return _carry_call(
        body, grid=(H, nq), in_specs=in_specs, out_specs=qs, out_shape=jax.ShapeDtypeStruct((T, W), BF16),
        scratch_shapes=[], sem=("parallel", "parallel"), name=name, args=args, pipe=pipe,
    )


def _attn_bwd(q, k, v, do, scale, name, qpe=None, kpe=None, causal=False, pipe=None):
    T, W = q.shape
    Tk = k.shape[0]
    H = W // HEAD
    tq = min(T, 256)
    nq = T // tq
    has_pe = qpe is not None
    qs = pl.BlockSpec((tq, HEAD), lambda h, i: (i, h))
    ks = pl.BlockSpec((Tk, HEAD), lambda h, i: (0, h))
    in_specs, args = [qs, ks, ks, qs], [q, k, v, do]
    out_specs = [qs, ks, ks]
    out_shape = [jax.ShapeDtypeStruct((T, W), BF16), jax.ShapeDtypeStruct((Tk, W), BF16), jax.ShapeDtypeStruct((Tk, W), BF16)]
    scratch = [pltpu.VMEM((Tk, HEAD), F32), pltpu.VMEM((Tk, HEAD), F32)]
    if has_pe:
        in_specs += [pl.BlockSpec((tq, HEAD), lambda h, i: (i, h // 2)), pl.BlockSpec((Tk, HEAD), lambda h, i: (0, 0))]
        args += [qpe, kpe]
        out_specs += [qs, ks]
        out_shape += [jax.ShapeDtypeStruct((T, W), F32), jax.ShapeDtypeStruct((Tk, W), F32)]
        scratch += [pltpu.VMEM((Tk, HEAD), F32)]
    n_in = len(in_specs)

    def body(*refs):
        q_ref, k_ref, v_ref, do_ref = refs[:4]
        outs = refs[n_in:n_in + len(out_specs)]
        accs = refs[n_in + len(out_specs):]
        dq_ref, dk_ref, dv_ref = outs[:3]
        h, i = pl.program_id(0), pl.program_id(1)

        @pl.when(i == 0)
        def _():
            for acc in accs:
                acc[...] = jnp.zeros_like(acc)

        def compute(klen):
            qp = _pe_mask(refs[4][...], h) if has_pe else None
            kp = refs[5][0:klen, :] if has_pe else None
            qv, kv, vv, dov = q_ref[...], k_ref[0:klen, :], v_ref[0:klen, :], do_ref[...]
            p = _attn_probs(qv, kv, qp, kp, scale, causal, i * tq)
            dp = _dot(dov, vv, NT)
            ds = (p * (dp - jnp.sum(p * dp, axis=-1, keepdims=True)) * scale).astype(BF16)
            dq_ref[...] = _dot(ds, kv, NN).astype(BF16)
            accs[0][0:klen, :] += _dot(ds, qv, TN)
            accs[1][0:klen, :] += _dot(p.astype(BF16), dov, TN)
            if has_pe:
                outs[3][...] = _pe_mask(_dot(ds, kp, NN), h)
                accs[2][0:klen, :] += _dot(ds, qp, TN)

        if causal:
            for qi in range(nq):
                pl.when(i == qi)(functools.partial(compute, (qi + 1) * tq))
        else:
            compute(Tk)

        @pl.when(i == nq - 1)
        def _():
            dk_ref[...] = accs[0][...].astype(BF16)
            dv_ref[...] = accs[1][...].astype(BF16)
            if has_pe:
                outs[4][...] = accs[2][...]

    return _carry_call(
        body, grid=(H, nq), in_specs=in_specs, out_specs=out_specs, out_shape=out_shape, scratch_shapes=scratch,
        sem=("parallel", "arbitrary"), name=name, args=args, pipe=pipe,
    )


def _split3(x):
    hi = x.astype(BF16)
    r1 = x - hi.astype(F32)
    mid = r1.astype(BF16)
    lo = (r1 - mid.astype(F32)).astype(BF16)
    return hi, mid, lo


def _tri_dot(tri, x):
    hi, mid, lo = _split3(x)
    return _dot(tri, hi, NN) + _dot(tri, mid, NN) + _dot(tri, lo, NN)


def _hg_gates(u, lb):
    q, fr, v = u[:, 0:HEAD], u[:, HEAD:2 * HEAD], u[:, 2 * HEAD:3 * HEAD]
    sg = 1.0 / (1.0 + jnp.exp(-fr))
    sgm = 1.0 / (1.0 + jnp.exp(fr))
    f = lb + (1.0 - lb) * sg
    kin = (1.0 - lb) * sgm
    sq = _sigmoid(q)
    return q, v, sg, sgm, f, kin, sq, q * sq


def _hg_block_mats(blk):
    t = lax.broadcasted_iota(jnp.int32, (blk, blk), 0)
    s = lax.broadcasted_iota(jnp.int32, (blk, blk), 1)
    same = (t // SUB) == (s // SUB)
    one = lambda m: jnp.where(m, 1.0, 0.0).astype(BF16)
    return one(same & (s <= t)), one(same), one(same & (s >= t))


def _hg_stage(pairs, blk):
    for sc, val in pairs:
        sc[0:SUB, :] = jnp.zeros((SUB, HEAD), F32)
        sc[SUB:SUB + blk, :] = val


def _hg_scan_fwd(u, lb, name, pipe=None):
    T, W = u.shape
    H = W // (4 * HEAD)
    blk = min(T, HG_BLOCK)
    nb, nsb = T // blk, blk // SUB

    def body(u_ref, lb_ref, o_ref, st_ref, state, k_sc, b_sc, v_sc):
        @pl.when(pl.program_id(1) == 0)
        def _():
            state[...] = jnp.zeros_like(state)

        q, v, sg, sgm, f, kin, sq, qin = _hg_gates(u_ref[...], lb_ref[...])
        tri, ones, _ = _hg_block_mats(blk)
        logf = jnp.log(f)
        brel = _tri_dot(tri, logf)
        btot = _tri_dot(ones, logf)
        _hg_stage(((k_sc, kin), (b_sc, brel), (v_sc, v)), blk)
        sub_row = lax.broadcasted_iota(jnp.int32, (blk, HEAD), 0) % SUB
        o = jnp.zeros((blk, HEAD), F32)
        for d in range(SUB):
            win = slice(SUB - d, SUB - d + blk)
            e = jnp.exp(jnp.where(sub_row >= d, brel - b_sc[win, :], NEG))
            o = o + jnp.sum(qin * e * k_sc[win, :], axis=-1, keepdims=True) * v_sc[win, :]
        ab = (qin * jnp.exp(brel)).astype(BF16)
        kdb = (kin * jnp.exp(btot - brel)).astype(BF16)
        vb = v.astype(BF16)
        ebt = jnp.exp(btot)
        st = state[...]
        st_ref[...] = st
        for i in range(nsb):
            sl = slice(i * SUB, (i + 1) * SUB)
            o_ref[sl, :] = o[sl] + _dot(ab[sl], st.astype(BF16), NT)
            st = ebt[i * SUB:i * SUB + 1, :] * st + _dot(vb[sl], kdb[sl], TN)
        state[...] = st

    return _carry_call(
        body, grid=(H, nb),
        in_specs=[pl.BlockSpec((blk, 4 * HEAD), lambda h, c: (c, h)), pl.BlockSpec((1, HEAD), lambda h, c: (0, h))],
        out_specs=[pl.BlockSpec((blk, HEAD), lambda h, c: (c, h)), pl.BlockSpec((None, None, HEAD, HEAD), lambda h, c: (h, c, 0, 0))],
        out_shape=[jax.ShapeDtypeStruct((T, H * HEAD), F32), jax.ShapeDtypeStruct((H, nb, HEAD, HEAD), F32)],
        scratch_shapes=[pltpu.VMEM((HEAD, HEAD), F32)] + [pltpu.VMEM((SUB + blk, HEAD), F32)] * 3,
        sem=("parallel", "arbitrary"), name=name, args=(u, lb), pipe=pipe,
    )


def _hg_scan_bwd(u, lb, do, dog, states, name, pipe=None):
    T, W = u.shape
    H = W // (4 * HEAD)
    blk = min(T, HG_BLOCK)
    NC, nsb = T // blk, blk // SUB

    def body(u_ref, lb_ref, do_ref, dog_ref, st_ref, du_ref, dlb_ref, dstate, s_all, k_sc, b_sc, v_sc, dk_sc, dbn_sc,
             dv_sc, da_sc, dkd_sc, dvs_sc, dbt_sc):
        @pl.when(pl.program_id(1) == 0)
        def _():
            dstate[...] = jnp.zeros_like(dstate)
            dlb_ref[...] = jnp.zeros_like(dlb_ref)

        lb = lb_ref[...]
        q, v, sg, sgm, f, kin, sq, qin = _hg_gates(u_ref[...], lb)
        tri, ones, tri_t = _hg_block_mats(blk)
        logf = jnp.log(f)
        brel = _tri_dot(tri, logf)
        btot = _tri_dot(ones, logf)
        eb, ekd, ebt = jnp.exp(brel), jnp.exp(btot - brel), jnp.exp(btot)
        a, kd = qin * eb, kin * ekd
        ab, kdb, vb = a.astype(BF16), kd.astype(BF16), v.astype(BF16)
        do = do_ref[...]
        dob = do.astype(BF16)
        st = st_ref[...]
        for i in range(nsb):
            sl = slice(i * SUB, (i + 1) * SUB)
            s_all[i] = st
            st = ebt[i * SUB:i * SUB + 1, :] * st + _dot(vb[sl], kdb[sl], TN)
        ds = dstate[...]
        for i in reversed(range(nsb)):
            sl = slice(i * SUB, (i + 1) * SUB)
            st_i = s_all[i]
            dsb = ds.astype(BF16)
            e_i = ebt[i * SUB:i * SUB + 1, :]
            da_sc[sl, :] = _dot(dob[sl], st_i.astype(BF16), NN)
            dvs_sc[sl, :] = _dot(kdb[sl], dsb, NT)
            dkd_sc[sl, :] = _dot(vb[sl], dsb, NN)
            dbt_sc[sl, :] = jnp.broadcast_to(jnp.sum(ds * st_i, axis=0, keepdims=True) * e_i, (SUB, HEAD))
            ds = e_i * ds + _dot(dob[sl], ab[sl], TN)
        dstate[...] = ds
        da, dkd = da_sc[...], dkd_sc[...]
        t1 = dkd * kd
        dqin = da * eb
        dbrel = da * a - t1
        dkin = dkd * ekd
        dbtot = dbt_sc[...] + _tri_dot(ones, t1)
        _hg_stage(((k_sc, kin), (b_sc, brel), (v_sc, v)), blk)
        for sc in (dk_sc, dbn_sc, dv_sc):
            sc[...] = jnp.zeros_like(sc)
        sub_row = lax.broadcasted_iota(jnp.int32, (blk, HEAD), 0) % SUB
        for d in range(SUB):
            win = slice(SUB - d, SUB - d + blk)
            ks = k_sc[win, :]
            e = jnp.exp(jnp.where(sub_row >= d, brel - b_sc[win, :], NEG))
            qe = qin * e
            col = jnp.sum(qe * ks, axis=-1, keepdims=True)
            dcol = jnp.sum(do * v_sc[win, :], axis=-1, keepdims=True)
            dqe = dcol * qe
            g = dqe * ks
            dqin = dqin + dcol * (e * ks)
            dbrel = dbrel + g
            dk_sc[win, :] += dqe
            dbn_sc[win, :] += g
            dv_sc[win, :] += col * do
        dkin = dkin + dk_sc[SUB:SUB + blk, :]
        dbrel = dbrel - dbn_sc[SUB:SUB + blk, :]
        dv = dvs_sc[...] + dv_sc[SUB:SUB + blk, :]
        dlogf = _tri_dot(tri_t, dbrel) + dbtot
        diff = dlogf / f - dkin
        dlb_ref[...] += jnp.sum(sgm * diff, axis=0, keepdims=True)
        du_ref[:, 0:HEAD] = (dqin * (sq * (1.0 + q * (1.0 - sq)))).astype(BF16)
        du_ref[:, HEAD:2 * HEAD] = ((1.0 - lb) * sg * sgm * diff).astype(BF16)
        du_ref[:, 2 * HEAD:3 * HEAD] = dv.astype(BF16)
        du_ref[:, 3 * HEAD:4 * HEAD] = dog_ref[...]

    rev = lambda h, c: (NC - 1 - c, h)
    return _carry_call(
        body, grid=(H, NC),
        in_specs=[
            pl.BlockSpec((blk, 4 * HEAD), rev), pl.BlockSpec((1, HEAD), lambda h, c: (0, h)),
            pl.BlockSpec((blk, HEAD), rev), pl.BlockSpec((blk, HEAD), rev),
            pl.BlockSpec((None, None, HEAD, HEAD), lambda h, c: (h, NC - 1 - c, 0, 0)),
        ],
        out_specs=[pl.BlockSpec((blk, 4 * HEAD), rev), pl.BlockSpec((1, HEAD), lambda h, c: (0, h))],
        out_shape=[jax.ShapeDtypeStruct((T, W), BF16), jax.ShapeDtypeStruct((1, H * HEAD), F32)],
        scratch_shapes=[pltpu.VMEM((HEAD, HEAD), F32), pltpu.VMEM((nsb, HEAD, HEAD), F32)]
        + [pltpu.VMEM((SUB + blk, HEAD), F32)] * 6 + [pltpu.VMEM((blk, HEAD), F32)] * 4,
        sem=("parallel", "arbitrary"), name=name, args=(u, lb, do, dog, states), pipe=pipe,
    )


def _hg_tail_fwd(o_raw, u, g, name):
    T, D = o_raw.shape
    H = D // HEAD
    tr = min(T, 512)
    blk = pl.BlockSpec((tr, HEAD), lambda h, i: (i, h))

    def body(o_ref, og_ref, g_ref, out_ref):
        o, og = o_ref[...], og_ref[...]
        out_ref[...] = (o * _rstd(o) * g_ref[...] * (og * _sigmoid(og))).astype(BF16)

    return pl.pallas_call(
        body, grid=(H, T // tr),
        in_specs=[blk, pl.BlockSpec((tr, HEAD), lambda h, i: (i, 4 * h + 3)), pl.BlockSpec((1, HEAD), lambda h, i: (0, h))],
        out_specs=blk, out_shape=jax.ShapeDtypeStruct((T, D), BF16),
        compiler_params=_params(("parallel", "parallel")), name=name,
    )(o_raw, u, g)


def _hg_tail_bwd(o_raw, u, g, doa, name):
    T, D = o_raw.shape
    H = D // HEAD
    tr = min(T, 512)
    blk = pl.BlockSpec((tr, HEAD), lambda h, i: (i, h))
    vec = pl.BlockSpec((1, HEAD), lambda h, i: (0, h))

    def body(o_ref, og_ref, g_ref, doa_ref, do_ref, dog_ref, dg_ref):
        @pl.when(pl.program_id(1) == 0)
        def _():
            dg_ref[...] = jnp.zeros_like(dg_ref)

        o, og, doa, g = o_ref[...], og_ref[...], doa_ref[...], g_ref[...]
        sg = _sigmoid(og)
        r = _rstd(o)
        xhat = o * r
        dog_ref[...] = (doa * (xhat * g) * (sg * (1.0 + og * (1.0 - sg)))).astype(BF16)
        dn = doa * (og * sg)
        dg_ref[...] += jnp.sum(dn * xhat, axis=0, keepdims=True)
        dxh = dn * g
        do_ref[...] = r * (dxh - xhat * jnp.mean(dxh * xhat, axis=-1, keepdims=True))

    return pl.pallas_call(
        body, grid=(H, T // tr),
        in_specs=[blk, pl.BlockSpec((tr, HEAD), lambda h, i: (i, 4 * h + 3)), vec, blk],
        out_specs=[blk, blk, vec],
        out_shape=[jax.ShapeDtypeStruct((T, D), F32), jax.ShapeDtypeStruct((T, D), BF16), jax.ShapeDtypeStruct((1, D), F32)],
        compiler_params=_params(("parallel", "arbitrary")), name=name,
    )(o_raw, u, g, doa)


def _lb_fwd(logits, name):
    def body(l_ref, o_ref):
        l0, l1 = l_ref[0:1, :], l_ref[1:2, :]
        m = jnp.maximum(l0, l1)
        e0, e1 = jnp.exp(l0 - m), jnp.exp(l1 - m)
        o_ref[...] = e0 / (e0 + e1)

    D = logits.shape[1]
    return pl.pallas_call(body, out_shape=jax.ShapeDtypeStruct((1, D), F32), name=name)(logits)


def _lb_bwd(lb, dlb, name):
    def body(lb_ref, d_ref, o_ref):
        lb = lb_ref[...]
        d0 = d_ref[...] * lb * (1.0 - lb)
        o_ref[0:1, :] = d0
        o_ref[1:2, :] = -d0

    D = lb.shape[1]
    return pl.pallas_call(body, out_shape=jax.ShapeDtypeStruct((2, D), F32), name=name)(lb, dlb)


def _slots(w):
    return w.shape[0] if w.ndim == 3 else 0


def _ffn_fwd(x, p, tag, carry):
    mm = lambda a, b, mode, dt, name, **kw: _mm(a, b, mode, dt, name, pipe=carry.get(name), **kw)
    hb = _norm_fwd(x, p["pre_g"], f"{tag}_pre_norm", pipe=carry.get(f"{tag}_pre_norm"))
    a = mm(hb, p["w_gate"], "nn", F32, f"{tag}_gate")
    b = mm(hb, p["w_up"], "nn", F32, f"{tag}_up")
    sb = _swiglu_fwd(a, b, f"{tag}_swiglu", pipe=carry.get(f"{tag}_swiglu"))
    y = mm(sb, p["w_down"], "nn", F32, f"{tag}_down")
    xo = _postnorm_fwd(x, y, p["post_g"], FFN_RESIDUAL_WEIGHT, f"{tag}_post_norm", pipe=carry.get(f"{tag}_post_norm"))
    return xo, (x, hb, a, b, sb, y)


def _ffn_bwd(dxo, p, saved, tag, carry):
    mm = lambda a, b, mode, dt, name, **kw: _mm(a, b, mode, dt, name, pipe=carry.get(name), **kw)
    x, hb, a, b, sb, y = saved
    dyb, dpost = _norm_bwd(y, p["post_g"], dxo, FFN_RESIDUAL_WEIGHT, BF16, f"{tag}_post_norm_bwd")
    dw_down = mm(sb, dyb, "tn", BF16, f"{tag}_down_dw")
    ds = mm(dyb, p["w_down"], "nt", F32, f"{tag}_down_dx")
    dab, dbb = _swiglu_bwd(a, b, ds, f"{tag}_swiglu_bwd")
    dw_gate = mm(hb, dab, "tn", BF16, f"{tag}_gate_dw", out_slots=_slots(p["w_gate"]))
    dw_up = mm(hb, dbb, "tn", BF16, f"{tag}_up_dw", out_slots=_slots(p["w_up"]))
    dh = mm(dab, p["w_gate"], "nt", F32, f"{tag}_gate_dx")
    dh = mm(dbb, p["w_up"], "nt", F32, f"{tag}_up_dx", add=dh)
    dx, dpre = _norm_bwd(x, p["pre_g"], dh, 1.0, F32, f"{tag}_pre_norm_bwd", res=dxo)
    return dx, {"pre_g": dpre, "w_gate": dw_gate, "w_up": dw_up, "w_down": dw_down, "post_g": dpost}


def _mixer_fwd(x, cos, sin, p, carry):
    scale = (HEAD + ROPE) ** -0.5
    hb = _norm_fwd(x, p["pre_g"], "mix_pre_norm")
    u = _mm(hb, p["w_hg"], "nn", F32, "mix_in_hg", pipe=carry.get("mix_in_hg"))
    cq = _mm(hb, p["w_cq"], "nn", F32, "mix_in_cq")
    ckv = _mm(hb, p["w_ckv"], "nn", F32, "mix_in_ckv")
    kpe = _mm(hb, p["w_kpe"], "nn", F32, "mix_in_kpe")
    ga = _mm(hb, p["w_ga"], "nn", F32, "mix_in_ga")
    gb = _mm(hb, p["w_gb"], "nn", F32, "mix_in_gb")
    lb = _lb_fwd(p["lb_logits"], "hg_lb")
    o_raw, states = _hg_scan_fwd(u, lb, "hg_scan", pipe=carry.get("hg_scan"))
    oa = _hg_tail_fwd(o_raw, u, p["hg_norm_g"], "hg_tail")
    ya = _mm(oa, p["w_branch_a"], "nn", F32, "mix_branch_a")
    cqn = _norm_fwd(cq, p["q_norm_g"], "mla_q_norm")
    qn = _mm(cqn, p["w_qn"], "nn", BF16, "mla_q_up_nope")
    qpe = _rope_q_fwd(_mm(cqn, p["w_qpe"], "nn", F32, "mla_q_up_pe"), cos, sin, "mla_rope_q")
    ckvn = _norm_fwd(ckv, p["kv_norm_g"], "mla_kv_norm")
    kn = _mm(ckvn, p["w_kn"], "nn", BF16, "mla_k_up")
    vv = _mm(ckvn, p["w_vv"], "nn", BF16, "mla_v_up")
    kpe2 = _rope_k_fwd(kpe, cos, sin, "mla_rope_k")
    ob = _attn_fwd(qn, kn, vv, scale, "mla_attn", qpe=qpe, kpe=kpe2, causal=True, pipe=carry.get("mla_attn"))
    yb = _mm(ob, p["w_branch_b"], "nn", F32, "mix_branch_b")
    ym = _merge_fwd(ga, gb, ya, yb, "mix_merge")
    z = _mm(ym, p["w_out"], "nn", F32, "mix_out")
    xo = _postnorm_fwd(x, z, p["post_g"], 1.0, "mix_post_norm")
    saved = (x, hb, u, cq, ckv, ga, gb, lb, o_raw, states, oa, ya, cqn, qn, qpe, ckvn, kn, vv, kpe2, ob, yb, ym, z)
    return xo, saved


def _mixer_bwd(dxo, cos, sin, p, saved, carry, on_early):
    x, hb, u, cq, ckv, ga, gb, lb, o_raw, states, oa, ya, cqn, qn, qpe, ckvn, kn, vv, kpe2, ob, yb, ym, z = saved
    scale = (HEAD + ROPE) ** -0.5
    g = {}
    dzb, g["post_g"] = _norm_bwd(z, p["post_g"], dxo, 1.0, BF16, "mix_post_norm_bwd")
    g["w_out"] = _mm(ym, dzb, "tn", BF16, "mix_out_dw")
    dym = _mm(dzb, p["w_out"], "nt", F32, "mix_out_dx")
    dya, dyb, dga, dgb = _merge_bwd(ga, gb, ya, yb, dym, "mix_merge_bwd")
    g["w_branch_a"] = _mm(oa, dya, "tn", BF16, "mix_branch_a_dw")
    doa = _mm(dya, p["w_branch_a"], "nt", F32, "mix_branch_a_dx")
    do_raw, dog, g["hg_norm_g"] = _hg_tail_bwd(o_raw, u, p["hg_norm_g"], doa, "hg_tail_bwd")
    du, dlb = _hg_scan_bwd(u, lb, do_raw, dog, states, "hg_scan_bwd", pipe=carry.get("hg_scan_bwd"))
    g["lb_logits"] = _lb_bwd(lb, dlb, "hg_lb_bwd")
    g["w_branch_b"] = _mm(ob, dyb, "tn", BF16, "mix_branch_b_dw")
    on_early({k: g[k] for k in ("w_out", "w_branch_a", "w_branch_b")})
    dob = _mm(dyb, p["w_branch_b"], "nt", BF16, "mix_branch_b_dx")
    dqn, dkn, dvv, dqpe_h, dkpe_h = _attn_bwd(qn, kn, vv, dob, scale, "mla_attn_bwd", qpe=qpe, kpe=kpe2, causal=True,
                                              pipe=carry.get("mla_attn_bwd"))
    dqpe = _rope_q_bwd(dqpe_h, cos, sin, "mla_rope_q_bwd")
    dkpe = _rope_k_bwd(dkpe_h, cos, sin, "mla_rope_k_bwd")
    g["w_qn"] = _mm(cqn, dqn, "tn", BF16, "mla_q_up_nope_dw")
    g["w_qpe"] = _mm(cqn, dqpe, "tn", BF16, "mla_q_up_pe_dw")
    dcqn = _mm(dqn, p["w_qn"], "nt", F32, "mla_q_up_nope_dx")
    dcqn = _mm(dqpe, p["w_qpe"], "nt", F32, "mla_q_up_pe_dx", add=dcqn)
    dcq, g["q_norm_g"] = _norm_bwd(cq, p["q_norm_g"], dcqn, 1.0, BF16, "mla_q_norm_bwd")
    g["w_kn"] = _mm(ckvn, dkn, "tn", BF16, "mla_k_up_dw")
    g["w_vv"] = _mm(ckvn, dvv, "tn", BF16, "mla_v_up_dw")
    dckvn = _mm(dkn, p["w_kn"], "nt", F32, "mla_k_up_dx")
    dckvn = _mm(dvv, p["w_vv"], "nt", F32, "mla_v_up_dx", add=dckvn)
    dckv, g["kv_norm_g"] = _norm_bwd(ckv, p["kv_norm_g"], dckvn, 1.0, BF16, "mla_kv_norm_bwd")
    dh = None
    for key, d in (("w_hg", du), ("w_cq", dcq), ("w_ckv", dckv), ("w_kpe", dkpe), ("w_ga", dga), ("w_gb", dgb)):
        g[key] = _mm(hb, d, "tn", BF16, f"mix_in_{key}_dw")
        dh = _mm(d, p[key], "nt", F32, f"mix_in_{key}_dx", add=dh)
    dx, g["pre_g"] = _norm_bwd(x, p["pre_g"], dh, 1.0, F32, "mix_pre_norm_bwd", res=dxo)
    return dx, g


def _xa_fwd(x, mem, p):
    scale = HEAD ** -0.5
    hb = _norm_fwd(x, p["pre_g"], "xa_pre_norm")
    mb = _norm_fwd(mem, p["mem_g"], "xa_mem_norm")
    q = _mm(hb, p["w_q"], "nn", BF16, "xa_q")
    k = _mm(mb, p["w_k"], "nn", BF16, "xa_k")
    v = _mm(mb, p["w_v"], "nn", BF16, "xa_v")
    o = _attn_fwd(q, k, v, scale, "xa_attn")
    z = _mm(o, p["w_o"], "nn", F32, "xa_o")
    xo = _postnorm_fwd(x, z, p["post_g"], 1.0, "xa_post_norm")
    return xo, (x, mem, hb, mb, q, k, v, o, z)


def _xa_bwd(dxo, p, saved):
    x, mem, hb, mb, q, k, v, o, z = saved
    scale = HEAD ** -0.5
    g = {}
    dzb, g["post_g"] = _norm_bwd(z, p["post_g"], dxo, 1.0, BF16, "xa_post_norm_bwd")
    g["w_o"] = _mm(o, dzb, "tn", BF16, "xa_o_dw", out_slots=_slots(p["w_o"]))
    do = _mm(dzb, p["w_o"], "nt", BF16, "xa_o_dx")
    dq, dk, dv = _attn_bwd(q, k, v, do, scale, "xa_attn_bwd")
    g["w_q"] = _mm(hb, dq, "tn", BF16, "xa_q_dw")
    g["w_k"] = _mm(mb, dk, "tn", BF16, "xa_k_dw")
    g["w_v"] = _mm(mb, dv, "tn", BF16, "xa_v_dw")
    dh = _mm(dq, p["w_q"], "nt", F32, "xa_q_dx")
    dm = _mm(dk, p["w_k"], "nt", F32, "xa_k_dx")
    dm = _mm(dv, p["w_v"], "nt", F32, "xa_v_dx", add=dm)
    _, g["mem_g"] = _norm_bwd(mem, p["mem_g"], dm, 1.0, BF16, "xa_mem_norm_bwd")
    dx, g["pre_g"] = _norm_bwd(x, p["pre_g"], dh, 1.0, F32, "xa_pre_norm_bwd", res=dxo)
    return dx, g


def _local_step(x, mem, cos, sin, target, params_of, carry, on_grads):
    p1 = params_of("ffn1")
    x1, s1 = _ffn_fwd(x, p1, "ffn1", carry)
    p2 = params_of("mix")
    x2, s2 = _mixer_fwd(x1, cos, sin, p2, carry)
    p3 = params_of("xa")
    x3, s3 = _xa_fwd(x2, mem, p3)
    p4 = params_of("ffn2")
    x4, s4 = _ffn_fwd(x3, p4, "ffn2", carry)
    dy, sq_err = _loss_head(x4, target, "loss_head")
    loss = 0.5 / x.shape[1] * jnp.sum(sq_err)
    dx, g4 = _ffn_bwd(dy, p4, s4, "ffn2", carry)
    on_grads("ffn2", g4)
    dx, g3 = _xa_bwd(dx, p3, s3)
    on_grads("xa", g3)
    dx, g2 = _mixer_bwd(dx, cos, sin, p2, s2, carry, lambda g: on_grads("mix_early", g))
    on_grads("mix", g2)
    dx, g1 = _ffn_bwd(dx, p1, s1, "ffn1", carry)
    on_grads("ffn1", g1)
    return loss, dx


def _split_w_in(w_in):
    D = w_in.shape[0]
    H = D // HEAD
    lora = (w_in.shape[1] - 6 * D - ROPE) // 2
    o = 4 * D
    w_hg = w_in[:, :o].reshape(D, 4, H, HEAD).transpose(0, 2, 1, 3).reshape(D, 4 * D)
    w_cq, w_ckv = w_in[:, o:o + lora], w_in[:, o + lora:o + 2 * lora]
    o += 2 * lora
    w_kpe = jnp.pad(w_in[:, o:o + ROPE], ((0, 0), (0, HEAD - ROPE)))
    o += ROPE
    return {"w_hg": w_hg, "w_cq": w_cq, "w_ckv": w_ckv, "w_kpe": w_kpe, "w_ga": w_in[:, o:o + D], "w_gb": w_in[:, o + D:o + 2 * D]}


def _merge_w_in(g):
    D = g["w_ga"].shape[0]
    H = D // HEAD
    hg = g["w_hg"].reshape(D, H, 4, HEAD).transpose(0, 2, 1, 3).reshape(D, 4 * D)
    return jnp.concatenate([hg, g["w_cq"], g["w_ckv"], g["w_kpe"][:, :ROPE], g["w_ga"], g["w_gb"]], axis=1)


def _split_heads(w, rest):
    K, N = w.shape
    w3 = w.reshape(K, N // (HEAD + rest), HEAD + rest)
    return w3[:, :, :HEAD].reshape(K, -1), w3[:, :, HEAD:].reshape(K, -1)


def _merge_heads(a, b, rest):
    K = a.shape[0]
    H = a.shape[1] // HEAD
    return jnp.concatenate([a.reshape(K, H, HEAD), b.reshape(K, H, rest)], axis=2).reshape(K, H * (HEAD + rest))


BLOCK_WEIGHTS = {
    "ffn1": ("ffn1_w_gate", "ffn1_w_up", "ffn1_w_down"),
    "mix": ("w_in", "mla_w_q_up", "mla_w_kv_up", "w_branch_a", "w_branch_b", "w_out"),
    "xa": ("xa_w_q", "xa_w_k", "xa_w_v", "xa_w_o"),
    "ffn2": ("ffn2_w_gate", "ffn2_w_up", "ffn2_w_down"),
}


def _block_params(block, w, small):
    if block in ("ffn1", "ffn2"):
        return {"pre_g": small[f"{block}_pre_g"], "w_gate": w[f"{block}_w_gate"], "w_up": w[f"{block}_w_up"],
                "w_down": w[f"{block}_w_down"], "post_g": small[f"{block}_post_g"]}
    if block == "xa":
        return {"pre_g": small["xa_pre_g"], "mem_g": small["xa_mem_g"], "post_g": small["xa_post_g"],
                "w_q": w["xa_w_q"], "w_k": w["xa_w_k"], "w_v": w["xa_w_v"], "w_o": w["xa_w_o"]}
    mix = _split_w_in(w["w_in"])
    mix["w_qn"], mix["w_qpe"] = _split_heads(w["mla_w_q_up"], ROPE)
    mix["w_kn"], mix["w_vv"] = _split_heads(w["mla_w_kv_up"], HEAD)
    mix.update(w_branch_a=w["w_branch_a"], w_branch_b=w["w_branch_b"], w_out=w["w_out"], pre_g=small["mix_pre_g"],
               post_g=small["mix_post_g"], hg_norm_g=small["hg_norm_g"], q_norm_g=small["mla_q_norm_g"],
               kv_norm_g=small["mla_kv_norm_g"], lb_logits=small["hgrn_lb_logits"])
    return mix


def _block_grads(block, g):
    if block in ("ffn1", "ffn2"):
        return {f"{block}_{k}": g[k] for k in ("pre_g", "w_gate", "w_up", "w_down", "post_g")}
    if block == "xa":
        return {f"xa_{k}": g[k] for k in ("pre_g", "mem_g", "post_g", "w_q", "w_k", "w_v", "w_o")}
    if block == "mix_early":
        return dict(g)
    return dict(w_in=_merge_w_in(g), mla_w_q_up=_merge_heads(g["w_qn"], g["w_qpe"], ROPE),
                mla_w_kv_up=_merge_heads(g["w_kn"], g["w_vv"], HEAD), mix_pre_g=g["pre_g"], mix_post_g=g["post_g"],
                hg_norm_g=g["hg_norm_g"], mla_q_norm_g=g["q_norm_g"], mla_kv_norm_g=g["kv_norm_g"],
                hgrn_lb_logits=g["lb_logits"])


def _rope_tables(positions):
    inv_freq = 1.0 / (ROPE_THETA ** (jnp.arange(0, ROPE, 2, dtype=F32) / ROPE))
    ang = positions.astype(F32)[:, None] * inv_freq
    return jnp.tile(jnp.cos(ang), (1, 4)), jnp.tile(jnp.sin(ang), (1, 4))


def _adamw(w, g, m, v, name):
    bc1 = 1.0 - ADAM_B1 ** ADAM_STEP
    bc2 = 1.0 - ADAM_B2 ** ADAM_STEP

    def body(w_ref, g_ref, m_ref, v_ref, go_ref, d_ref, mo_ref, vo_ref):
        g = g_ref[...]
        m = ADAM_B1 * m_ref[...] + (1.0 - ADAM_B1) * g
        v = ADAM_B2 * v_ref[...] + (1.0 - ADAM_B2) * (g * g)
        go_ref[...] = g
        mo_ref[...] = m
        vo_ref[...] = v
        d_ref[...] = -ADAM_LR * ((m / bc1) / (jnp.sqrt(v / bc2) + ADAM_EPS) + ADAM_WD * w_ref[...])

    return _rows(body, [(w, "row"), (g, "row"), (m, "row"), (v, "row")], [(w.shape, F32, "row")] * 4, name)


ANY = pl.BlockSpec(memory_space=pl.ANY)
COMM_AXES = ("x", "y", "c")


def _place():
    x, y, c = (lax.axis_index(n) for n in COMM_AXES)
    chips = [(1 - x, y), (x, 1 - y), (1 - x, 1 - y)]
    return x, y, c, 2 * x + y, (x, y, 1 - c), chips


def _remote(src, dst, send_sems, recv_sems, j, to):
    return pltpu.make_async_remote_copy(src_ref=src, dst_ref=dst, send_sem=send_sems.at[j], recv_sem=recv_sems.at[j],
                                        device_id=to, device_id_type=MESH)


def _dma_sems(n):
    return [pltpu.SemaphoreType.DMA((n,)), pltpu.SemaphoreType.DMA((n,))]


def _all_gather(shards, whole, name):
    n = len(shards)

    def body(*refs):
        srcs, outs, send_sems, recv_sems = refs[:n], refs[n:2 * n], refs[2 * n], refs[2 * n + 1]
        x, y, c, me, sibling, chips = _place()
        sent = []

        def start(cp):
            cp.start()
            sent.append(cp)

        def rows(w, h):
            hr = srcs[w].shape[0] // 2
            return pl.ds(h * hr, hr)

        gathered = [w for w in range(n) if whole[w]]
        for w in gathered:
            for j, (cx, cy) in enumerate(chips):
                start(_remote(srcs[w].at[rows(w, c)], outs[w].at[me, rows(w, c)], send_sems, recv_sems, 7 * w + j, (cx, cy, c)))
        for w in range(n):
            start(_remote(srcs[w], outs[w].at[me], send_sems, recv_sems, 7 * w + 6, sibling))
        for w in gathered:
            for j, (cx, cy) in enumerate(chips):
                blk = outs[w].at[2 * cx + cy, rows(w, c)]
                _remote(srcs[w].at[rows(w, c)], blk, send_sems, recv_sems, 7 * w + j, (cx, cy, c)).wait_recv()
                start(_remote(blk, blk, send_sems, recv_sems, 7 * w + 3 + j, sibling))
        for w in gathered:
            for j, (cx, cy) in enumerate(chips):
                blk = outs[w].at[2 * cx + cy, rows(w, 1 - c)]
                _remote(blk, blk, send_sems, recv_sems, 7 * w + 3 + j, sibling).wait_recv()
        for w in range(n):
            _remote(srcs[w], outs[w].at[me], send_sems, recv_sems, 7 * w + 6, sibling).wait_recv()
        for cp in sent:
            cp.wait_send()

    return pl.pallas_call(
        body, in_specs=[ANY] * n, out_specs=[ANY] * n,
        out_shape=[jax.ShapeDtypeStruct((N_CHIP,) + s.shape, s.dtype) for s in shards],
        scratch_shapes=_dma_sems(7 * n), name=name,
    )(*shards)


def _rs_swap(grads, name):
    n = len(grads)

    def body(*refs):
        gs, outs, send_sems, recv_sems = refs[:n], refs[n:2 * n], refs[2 * n], refs[2 * n + 1]
        x, y, c, me, sibling, chips = _place()
        cps = []
        for w in range(n):
            hr = gs[w].shape[1] // 2
            cps.append(_remote(gs[w].at[:, pl.ds((1 - c) * hr, hr)], outs[w], send_sems, recv_sems, w, sibling))
            cps[-1].start()
        for cp in cps:
            cp.wait()

    return pl.pallas_call(
        body, in_specs=[ANY] * n, out_specs=[ANY] * n,
        out_shape=[jax.ShapeDtypeStruct((g.shape[0], g.shape[1] // 2, g.shape[2]), g.dtype) for g in grads],
        scratch_shapes=_dma_sems(n), name=name,
    )(*grads)


def _sum_rows(hr, row_bytes):
    return _tile(hr, max(16, ROW_BUDGET // (2 * row_bytes) // 16 * 16), 16)


def _rs_pair_sum(g, got, c, name):
    S, r, cw = g.shape
    hr = r // 2
    tr = _sum_rows(hr, 3 * cw * 2)
    nrb = hr // tr

    def body(c_ref, a_ref, b_ref, o_ref):
        o_ref[...] = (a_ref[...].astype(F32) + b_ref[...].astype(F32)).astype(BF16)

    return pl.pallas_call(
        body,
        grid_spec=pltpu.PrefetchScalarGridSpec(
            num_scalar_prefetch=1, grid=(S, nrb),
            in_specs=[pl.BlockSpec((None, tr, cw), lambda k, i, c_ref: (k, c_ref[0] * nrb + i, 0)),
                      pl.BlockSpec((None, tr, cw), lambda k, i, c_ref: (k, i, 0))],
            out_specs=pl.BlockSpec((None, tr, cw), lambda k, i, c_ref: (k, i, 0)),
        ),
        out_shape=jax.ShapeDtypeStruct((S, hr, cw), BF16),
        compiler_params=_params(("parallel", "parallel")), name=name,
    )(c, g, got)


def _rs_chip_sum(pair, got, place, name):
    S, hr, cw = pair.shape
    tr = _sum_rows(hr, cw * (4 * 2 + 4))

    def body(p_ref, a_ref, z_ref, o_ref):
        o_ref[...] = a_ref[...].astype(F32) + z_ref[0].astype(F32) + z_ref[1].astype(F32) + z_ref[2].astype(F32)

    return pl.pallas_call(
        body,
        grid_spec=pltpu.PrefetchScalarGridSpec(
            num_scalar_prefetch=1, grid=(hr // tr,),
            in_specs=[pl.BlockSpec((None, tr, cw), lambda i, p_ref: (p_ref[0], i, 0)),
                      pl.BlockSpec((3, tr, cw), lambda i, p_ref: (0, i, 0))],
            out_specs=pl.BlockSpec((None, tr, cw), lambda i, p_ref: (p_ref[1], i, 0)),
        ),
        out_shape=jax.ShapeDtypeStruct((2, hr, cw), F32),
        compiler_params=_params(("parallel",)), name=name,
    )(place, pair, got)


def _rs_share(halves, name):
    n = len(halves)

    def body(*refs):
        outs, send_sems, recv_sems = refs[n:2 * n], refs[2 * n], refs[2 * n + 1]
        x, y, c, me, sibling, chips = _place()
        cps = []
        for w in range(n):
            cps.append(_remote(outs[w].at[c], outs[w].at[c], send_sems, recv_sems, w, sibling))
            cps[-1].start()
        for w in range(n):
            _remote(outs[w].at[1 - c], outs[w].at[1 - c], send_sems, recv_sems, w, sibling).wait_recv()
        for cp in cps:
            cp.wait_send()

    return pl.pallas_call(
        body, in_specs=[ANY] * n, out_specs=[ANY] * n,
        out_shape=[jax.ShapeDtypeStruct(h.shape, h.dtype) for h in halves],
        input_output_aliases={i: i for i in range(n)},
        scratch_shapes=_dma_sems(n), name=name,
    )(*halves)


def _rs_begin(names, grads, core, fractions, tag):
    slotted = [_to_slots(n, grads[n]) for n in names]
    got = _rs_swap(slotted, f"grads_sibling_swap_{tag}")
    pairs = [_rs_pair_sum(s, t, core.reshape(1), f"grads_pair_sum_{n}") for n, s, t in zip(names, slotted, got)]
    lands = [lax.empty((N_CHIP - 1,) + p.shape[1:], p.dtype) for p in pairs]
    return pairs, _Pipe("reduce", pairs, lands, fractions)


def _rs_end(names, pairs, pipe, place, shapes, tag):
    _pipe_flush(pipe, f"grads_chip_exchange_rest_{tag}")
    halves = [_rs_chip_sum(p, o, place, f"grads_chip_sum_{n}") for n, p, o in zip(names, pairs, pipe.lands)]
    both = _rs_share(halves, f"grads_sibling_share_{tag}")
    return {n: b.reshape(s) for n, b, s in zip(names, both, shapes)}


def _all_reduce_small(s, name):
    flips = [(dx, dy, dc) for dx in (0, 1) for dy in (0, 1) for dc in (0, 1) if (dx, dy, dc) != (0, 0, 0)]

    def body(s_ref, o_ref, buf, send_sems, recv_sems):
        x, y, c = (lax.axis_index(n) for n in COMM_AXES)
        me = 4 * x + 2 * y + c
        buf[me] = s_ref[...]
        peers = [((1 - x) if dx else x, (1 - y) if dy else y, (1 - c) if dc else c) for dx, dy, dc in flips]
        sent = [_remote(s_ref, buf.at[me], send_sems, recv_sems, j, p) for j, p in enumerate(peers)]
        for cp in sent:
            cp.start()
        for j, (px, py, pc) in enumerate(peers):
            _remote(s_ref, buf.at[4 * px + 2 * py + pc], send_sems, recv_sems, j, (px, py, pc)).wait_recv()
        for cp in sent:
            cp.wait_send()
        acc = buf[0]
        for d in range(1, N_DEV):
            acc = acc + buf[d]
        o_ref[...] = acc

    vmem = pl.BlockSpec(memory_space=pltpu.VMEM)
    return pl.pallas_call(
        body, in_specs=[vmem], out_specs=vmem, out_shape=jax.ShapeDtypeStruct(s.shape, F32),
        scratch_shapes=[pltpu.VMEM((N_DEV,) + s.shape, F32), pltpu.SemaphoreType.DMA((7,)), pltpu.SemaphoreType.DMA((7,))],
        name=name,
    )(s)


BIG = {
    "ffn1_w_gate": 1, "ffn1_w_up": 1, "ffn1_w_down": 0, "w_in": 1, "mla_w_q_up": 1, "mla_w_kv_up": 1,
    "w_branch_a": 0, "w_branch_b": 0, "w_out": 0, "xa_w_q": 0, "xa_w_k": 0, "xa_w_v": 0, "xa_w_o": 1,
    "ffn2_w_gate": 1, "ffn2_w_up": 1, "ffn2_w_down": 0,
}
WEIGHTS = [
    "hgrn_lb_logits", "ffn1_pre_g", "ffn1_w_gate", "ffn1_w_up", "ffn1_w_down", "ffn1_post_g", "mix_pre_g", "w_in",
    "hg_norm_g", "mla_q_norm_g", "mla_w_q_up", "mla_kv_norm_g", "mla_w_kv_up", "w_branch_a", "w_branch_b", "w_out",
    "mix_post_g", "xa_pre_g", "xa_mem_g", "xa_w_q", "xa_w_k", "xa_w_v", "xa_w_o", "xa_post_g", "ffn2_pre_g",
    "ffn2_w_gate", "ffn2_w_up", "ffn2_w_down", "ffn2_post_g",
]
SMALL = [n for n in WEIGHTS if n not in BIG]
SLOTTED = ("ffn1_w_gate", "ffn1_w_up", "ffn2_w_gate", "ffn2_w_up", "xa_w_o")
MIX_EARLY = ("w_out", "w_branch_a", "w_branch_b")


def _from_slots(name, g):
    S, r, cw = g.shape
    if BIG[name] == 0:
        return g.reshape(S * r, cw)
    return g if name in SLOTTED else g.transpose(1, 0, 2).reshape(r, S * cw)


def _to_slots(name, g):
    if g.ndim == 3:
        return g
    if BIG[name] == 0:
        return g.reshape(N_CHIP, g.shape[0] // N_CHIP, g.shape[1])
    return g.reshape(g.shape[0], N_CHIP, g.shape[1] // N_CHIP).transpose(1, 0, 2)


def _pack_small(vals, width):
    rows = [jnp.pad(v, ((0, 0), (0, width - v.shape[1]))) for v in vals]
    s = jnp.concatenate(rows, axis=0)
    return jnp.pad(s, ((0, -s.shape[0] % 8), (0, 0)))


def _unpack_small(s, shapes):
    out, o = [], 0
    for r, w in shapes:
        out.append(s[o:o + r, :w])
        o += r
    return out


def kernel(x, mem, positions, hgrn_lb_logits, ffn1_pre_g, ffn1_w_gate, ffn1_w_up, ffn1_w_down, ffn1_post_g, mix_pre_g, w_in, hg_norm_g, mla_q_norm_g, mla_w_q_up, mla_kv_norm_g, mla_w_kv_up, w_branch_a, w_branch_b, w_out, mix_post_g, xa_pre_g, xa_mem_g, xa_w_q, xa_w_k, xa_w_v, xa_w_o, xa_post_g, ffn2_pre_g, ffn2_w_gate, ffn2_w_up, ffn2_w_down, ffn2_post_g, loss_target, m_hgrn_lb_logits, m_ffn1_pre_g, m_ffn1_w_gate, m_ffn1_w_up, m_ffn1_w_down, m_ffn1_post_g, m_mix_pre_g, m_w_in, m_hg_norm_g, m_mla_q_norm_g, m_mla_w_q_up, m_mla_kv_norm_g, m_mla_w_kv_up, m_w_branch_a, m_w_branch_b, m_w_out, m_mix_post_g, m_xa_pre_g, m_xa_mem_g, m_xa_w_q, m_xa_w_k, m_xa_w_v, m_xa_w_o, m_xa_post_g, m_ffn2_pre_g, m_ffn2_w_gate, m_ffn2_w_up, m_ffn2_w_down, m_ffn2_post_g, v_hgrn_lb_logits, v_ffn1_pre_g, v_ffn1_w_gate, v_ffn1_w_up, v_ffn1_w_down, v_ffn1_post_g, v_mix_pre_g, v_w_in, v_hg_norm_g, v_mla_q_norm_g, v_mla_w_q_up, v_mla_kv_norm_g, v_mla_w_kv_up, v_w_branch_a, v_w_branch_b, v_w_out, v_mix_post_g, v_xa_pre_g, v_xa_mem_g, v_xa_w_q, v_xa_w_k, v_xa_w_v, v_xa_w_o, v_xa_post_g, v_ffn2_pre_g, v_ffn2_w_gate, v_ffn2_w_up, v_ffn2_w_down, v_ffn2_post_g):
    a = dict(locals())
    big = list(BIG)
    small = {n: a[n] for n in SMALL}
    core = lax.axis_index("c").astype(jnp.int32)
    place = jnp.stack([(2 * lax.axis_index("x") + lax.axis_index("y")).astype(jnp.int32), core])

    shards = {n: a[n][0].astype(BF16) for n in big}
    whole = [n in BLOCK_WEIGHTS["ffn1"] for n in big]
    lands = dict(zip(big, _all_gather([shards[n] for n in big], whole, "weights_all_gather")))
    late = BLOCK_WEIGHTS["xa"] + BLOCK_WEIGHTS["ffn2"]
    gather = lambda names, fr: _Pipe("gather", [shards[n] for n in names], [lands[n] for n in names], fr)
    pipe_mix, pipe_late = gather(BLOCK_WEIGHTS["mix"], (2, 10, 10, 6, 10, 3)), gather(late, (2, 5, 3))
    carry = {f"ffn1_{k}": pipe_mix for k in ("pre_norm", "gate", "up", "swiglu", "down", "post_norm")}
    carry.update({k: pipe_late for k in ("mix_in_hg", "hg_scan", "mla_attn")})

    def params_of(block):
        for first, names, pipe in (("mix", BLOCK_WEIGHTS["mix"], pipe_mix), ("xa", late, pipe_late)):
            if block == first:
                _pipe_flush(pipe, f"weights_gather_rest_{block}")
                lands.update(zip(names, pipe.lands))
        return _block_params(block, {n: _from_slots(n, lands[n]) for n in BLOCK_WEIGHTS[block]}, small)

    g_small, g_big, open_groups, held = {}, {}, {}, {}

    def begin(tag, names, fractions, carriers):
        pairs, pipe = _rs_begin(names, held, core, fractions, tag)
        open_groups[tag] = (names, pairs, pipe)
        carry.update({c: pipe for c in carriers})

    def end(tag):
        names, pairs, pipe = open_groups.pop(tag)
        g_big.update(_rs_end(names, pairs, pipe, place, [a[n].shape[1:] for n in names], tag))

    def on_grads(block, g):
        for n, v in _block_grads(block, g).items():
            (held if n in BIG else g_small)[n] = v
        if block == "ffn2":
            begin("ffn2", BLOCK_WEIGHTS["ffn2"], (1,), ("hg_scan_bwd",))
        elif block == "mix_early":
            begin("early", BLOCK_WEIGHTS["xa"] + MIX_EARLY, (1,), ("mla_attn_bwd",))
        elif block == "mix":
            end("ffn2")
            end("early")
            begin("mid", [n for n in BLOCK_WEIGHTS["mix"] if n not in MIX_EARLY], (1,) * 6,
                  [f"ffn1_{k}" for k in ("down_dw", "down_dx", "gate_dw", "up_dw", "gate_dx", "up_dx")])
        elif block == "ffn1":
            end("mid")
            begin("ffn1", BLOCK_WEIGHTS["ffn1"], (1,), ())
            end("ffn1")

    cos, sin = _rope_tables(positions[0])
    loss_part, grad_x = _local_step(x[0], mem[0], cos, sin, loss_target[0], params_of, carry, on_grads)

    small_shapes = [a[n].shape for n in SMALL]
    width = max(s[1] for s in small_shapes)
    gs = _all_reduce_small(_pack_small([g_small[n] for n in SMALL], width), "small_grads_all_reduce")

    out_g, out_d, out_m, out_v = {}, {}, {}, {}
    for n in big:
        res = _adamw(a[n][0], g_big[n], a["m_" + n][0], a["v_" + n][0], f"adamw_{n}")
        out_g[n], out_d[n], out_m[n], out_v[n] = (t.reshape(a[n].shape) for t in res)
    sw, sm, sv = (_pack_small([a[p + n] for n in SMALL], width) for p in ("", "m_", "v_"))
    for t, dst in zip(_adamw(sw, gs, sm, sv, "adamw_small"), (out_g, out_d, out_m, out_v)):
        dst.update(zip(SMALL, _unpack_small(t, small_shapes)))

    loss = lax.psum(loss_part, COMM_AXES)
    return (loss, grad_x[None], *[out_g[n] for n in WEIGHTS], *[out_d[n] for n in WEIGHTS],
            *[out_m[n] for n in WEIGHTS], *[out_v[n] for n in WEIGHTS])
```

```python
import functools

import jax
import jax.numpy as jnp
from jax import lax
from jax.experimental import pallas as pl
from jax.experimental.pallas import tpu as pltpu

F32 = jnp.float32
BF16 = jnp.bfloat16
EPS = 1e-6
HEAD = 128
ROPE = 64
CHUNK = 64
SUB = 16
HG_BLOCK = 256
ROPE_THETA = 10000.0
FFN_RESIDUAL_WEIGHT = 0.5
ADAM_LR, ADAM_B1, ADAM_B2, ADAM_EPS, ADAM_WD, ADAM_STEP = 0.001, 0.9, 0.999, 1e-08, 0.01, 10
VMEM_LIMIT = 56 * 2**20
ROW_BUDGET = 20 * 2**20
NEG = -1e30
MESH = pl.DeviceIdType.MESH
N_CHIP = 4
N_DEV = 8


def _params(sem):
    return pltpu.CompilerParams(dimension_semantics=sem, vmem_limit_bytes=VMEM_LIMIT)


def _tile(n, cap, mult):
    if n <= cap:
        return n
    t = (cap // mult) * mult
    while t >= mult:
        if n % t == 0:
            return t
        t -= mult
    raise ValueError(f"no tile for {n} under {cap}")


def _split_rows(n, fractions):
    if n % 16:
        return [(0, n)] + [(n, 0)] * (len(fractions) - 1)
    units, total, acc, cuts = n // 16, sum(fractions), 0, [0]
    for f in fractions[:-1]:
        acc += f
        cuts.append(round(units * acc / total))
    cuts.append(units)
    return [(16 * lo, 16 * (hi - lo)) for lo, hi in zip(cuts, cuts[1:])]


class _Pipe:
    def __init__(self, kind, srcs, lands, fractions):
        self.kind, self.srcs, self.lands = kind, list(srcs), list(lands)
        per_w = [_split_rows(s.shape[0] // 2 if kind == "gather" else s.shape[1], fractions) for s in srcs]
        self.parts = [[pw[i] for pw in per_w] for i in range(len(fractions))]
        self.taken = 0
        self.sems = (6 if kind == "gather" else 3) * len(srcs)

    def take(self):
        self.taken += 1
        return self.parts[self.taken - 1]

    def rest(self):
        left = self.parts[self.taken:]
        self.taken = len(self.parts)
        return [(left[0][w][0], sum(p[w][1] for p in left)) for w in range(len(self.srcs))] if left else None


def _pipe_copies(kind, rows, lands, srcs, send_sems, recv_sems):
    x, y, c, me, sibling, chips = _place()
    out = []
    for w, (r0, nr) in enumerate(rows):
        for j, (cx, cy) in enumerate(chips if nr else []):
            k, to = 2 * cx + cy, (cx, cy, c)
            if kind == "gather":
                rs = pl.ds(c * (srcs[w].shape[0] // 2) + r0, nr)
                src, dst, got, j0 = srcs[w].at[rs], lands[w].at[me, rs], lands[w].at[k, rs], 6 * w + j
            else:
                rs = pl.ds(r0, nr)
                src, dst, got, j0 = srcs[w].at[k, rs], lands[w].at[j, rs], lands[w].at[j, rs], 3 * w + j
            out.append((_remote(src, dst, send_sems, recv_sems, j0, to), _remote(src, got, send_sems, recv_sems, j0, to), w, j, k))
    return out


def _pipe_start(kind, rows, lands, srcs, send_sems, recv_sems):
    for send, _, _, _, _ in _pipe_copies(kind, rows, lands, srcs, send_sems, recv_sems):
        send.start()


def _pipe_finish(kind, rows, lands, srcs, send_sems, recv_sems):
    x, y, c, me, sibling, chips = _place()
    copies = _pipe_copies(kind, rows, lands, srcs, send_sems, recv_sems)
    passed = []
    for _, arrival, _, _, _ in copies:
        arrival.wait_recv()
    if kind == "gather":
        for w, (r0, nr) in enumerate(rows):
            hr = srcs[w].shape[0] // 2
            for j, (cx, cy) in enumerate(chips if nr else []):
                blk = lands[w].at[2 * cx + cy, pl.ds(c * hr + r0, nr)]
                passed.append(_remote(blk, blk, send_sems, recv_sems, 6 * w + 3 + j, sibling))
                passed[-1].start()
        for w, (r0, nr) in enumerate(rows):
            hr = srcs[w].shape[0] // 2
            for j, (cx, cy) in enumerate(chips if nr else []):
                blk = lands[w].at[2 * cx + cy, pl.ds((1 - c) * hr + r0, nr)]
                _remote(blk, blk, send_sems, recv_sems, 6 * w + 3 + j, sibling).wait_recv()
    for send, _, _, _, _ in copies:
        send.wait_send()
    for cp in passed:
        cp.wait_send()


def _carry_call(body, *, grid, in_specs, out_specs, out_shape, scratch_shapes, sem, name, args, pipe=None):
    single = not isinstance(out_shape, (list, tuple))
    if single:
        out_specs, out_shape = [out_specs], [out_shape]
    if pipe is None or pipe.taken >= len(pipe.parts):
        res = pl.pallas_call(body, grid=grid, in_specs=in_specs, out_specs=out_specs, out_shape=out_shape,
                             scratch_shapes=scratch_shapes, compiler_params=_params(sem), name=name)(*args)
        return res[0] if single else res
    rows, kind = pipe.take(), pipe.kind
    n_in, n_out, n_l, n_s, n_scr = len(args), len(out_shape), len(pipe.lands), len(pipe.srcs), len(scratch_shapes)

    def wrapped(*refs):
        ins, srcs = refs[:n_in], refs[n_in + n_l:n_in + n_l + n_s]
        o0 = n_in + n_l + n_s
        outs, lands = refs[o0:o0 + n_out], refs[o0 + n_out:o0 + n_out + n_l]
        scr = refs[o0 + n_out + n_l:o0 + n_out + n_l + n_scr]
        send_sems, recv_sems = refs[-2], refs[-1]
        ids = [pl.program_id(ax) for ax in range(len(grid))]
        first = functools.reduce(jnp.logical_and, [i == 0 for i in ids])
        last = functools.reduce(jnp.logical_and, [i == g - 1 for i, g in zip(ids, grid)])

        @pl.when(first)
        def _():
            _pipe_start(kind, rows, lands, srcs, send_sems, recv_sems)

        body(*ins, *outs, *scr)

        @pl.when(last)
        def _():
            _pipe_finish(kind, rows, lands, srcs, send_sems, recv_sems)

    res = pl.pallas_call(
        wrapped, grid=grid, in_specs=list(in_specs) + [ANY] * (n_l + n_s), out_specs=list(out_specs) + [ANY] * n_l,
        out_shape=list(out_shape) + [jax.ShapeDtypeStruct(l.shape, l.dtype) for l in pipe.lands],
        input_output_aliases={n_in + i: n_out + i for i in range(n_l)},
        scratch_shapes=list(scratch_shapes) + _dma_sems(pipe.sems),
        compiler_params=_params(("arbitrary",) * len(grid)), name=name,
    )(*args, *pipe.lands, *pipe.srcs)
    pipe.lands = list(res[n_out:])
    return res[0] if single else list(res[:n_out])


def _pipe_flush(pipe, name):
    rows = pipe.rest()
    if rows is None:
        return
    n_l, n_s, kind = len(pipe.lands), len(pipe.srcs), pipe.kind

    def body(*refs):
        srcs, lands = refs[n_l:n_l + n_s], refs[n_l + n_s:2 * n_l + n_s]
        _pipe_start(kind, rows, lands, srcs, refs[-2], refs[-1])
        _pipe_finish(kind, rows, lands, srcs, refs[-2], refs[-1])

    pipe.lands = list(pl.pallas_call(
        body, in_specs=[ANY] * (n_l + n_s), out_specs=[ANY] * n_l,
        out_shape=[jax.ShapeDtypeStruct(l.shape, l.dtype) for l in pipe.lands],
        input_output_aliases={i: i for i in range(n_l)}, scratch_shapes=_dma_sems(pipe.sems), name=name,
    )(*pipe.lands, *pipe.srcs))


def _sigmoid(x):
    return 1.0 / (1.0 + jnp.exp(-x))


def _dot(a, b, dims):
    return lax.dot_general(a, b, (dims, ((), ())), preferred_element_type=F32)


NN = ((1,), (0,))
NT = ((1,), (1,))
TN = ((0,), (0,))


def _mm(a, b, mode, out_dtype, name, add=None, out_slots=0, pipe=None, tm_cap=1024, tn_cap=512, tk_cap=2816):
    slot_cap = 1408
    b_slots = b.shape[0] if b.ndim == 3 else 0
    bs = (b.shape[1], b_slots * b.shape[2]) if b_slots else b.shape
    if mode == "nn":
        (M, K), (K2, N) = a.shape, bs
    elif mode == "nt":
        (M, K), (N, K2) = a.shape, bs
    else:
        (K, M), (K2, N) = a.shape, bs
    assert K == K2, (a.shape, b.shape, mode)
    tm = _tile(M, tm_cap, 128)
    tn = _tile(N // (b_slots or out_slots), slot_cap, 128) if (out_slots or (b_slots and mode == "nn")) else _tile(N, tn_cap, 128)
    tk = _tile(K // b_slots, slot_cap, 128) if (b_slots and mode == "nt") else _tile(K, tk_cap, 128)
    nk = K // tk
    a_spec = pl.BlockSpec((tk, tm), lambda i, j, k: (k, i)) if mode == "tn" else pl.BlockSpec((tm, tk), lambda i, j, k: (i, k))
    if b_slots and mode == "nn":
        per = b.shape[2] // tn
        b_spec = pl.BlockSpec((None, tk, tn), lambda i, j, k: (j // per, k, j % per))
    elif b_slots:
        per = b.shape[2] // tk
        b_spec = pl.BlockSpec((None, tn, tk), lambda i, j, k: (k // per, j, k % per))
    else:
        b_spec = pl.BlockSpec((tn, tk), lambda i, j, k: (j, k)) if mode == "nt" else pl.BlockSpec((tk, tn), lambda i, j, k: (k, j))
    if out_slots:
        per_o = N // out_slots // tn
        o_spec = pl.BlockSpec((None, tm, tn), lambda i, j, k: (j // per_o, i, j % per_o))
        o_shape = (out_slots, M, N // out_slots)
    else:
        o_spec = pl.BlockSpec((tm, tn), lambda i, j, k: (i, j))
        o_shape = (M, N)
    dims = {"nn": NN, "nt": NT, "tn": TN}[mode]
    has_add = add is not None

    def body(*refs):
        a_ref, b_ref = refs[0], refs[1]
        add_ref = refs[2] if has_add else None
        o_ref = refs[3] if has_add else refs[2]
        p = _dot(a_ref[...].astype(BF16), b_ref[...].astype(BF16), dims)

        def finish(val):
            if has_add:
                val = val + add_ref[...]
            o_ref[...] = val.astype(out_dtype)

        if nk == 1:
            finish(p)
        else:
            acc_ref = refs[-1]
            k = pl.program_id(2)

            @pl.when(k == 0)
            def _():
                acc_ref[...] = p

            @pl.when(k > 0)
            def _():
                acc_ref[...] += p

            @pl.when(k == nk - 1)
            def _():
                finish(acc_ref[...])

    assert not (has_add and out_slots)
    in_specs = [a_spec, b_spec] + ([o_spec] if has_add else [])
    args = (a, b) + ((add,) if has_add else ())
    return _carry_call(
        body, grid=(M // tm, N // tn, nk), in_specs=in_specs, out_specs=o_spec,
        out_shape=jax.ShapeDtypeStruct(o_shape, out_dtype),
        scratch_shapes=[pltpu.VMEM((tm, tn), F32)] if nk > 1 else [],
        sem=("parallel", "parallel", "arbitrary"), name=name, args=args, pipe=pipe,
    )


def _rows(body, ins, outs, name, pipe=None):
    T = next(a.shape[0] for a, k in ins if k == "row")
    per_row = sum(a.shape[1] * a.dtype.itemsize for a, k in ins if k == "row")
    per_row += sum(s[1] * jnp.dtype(d).itemsize for s, d, k in outs if k == "row")
    tr = next(t for t in (512, 256, 128, 64, 32, 16, 8) if T % t == 0 and 2 * t * per_row <= ROW_BUDGET)
    in_specs = [
        pl.BlockSpec((tr, a.shape[1]), lambda i: (i, 0)) if k == "row" else pl.BlockSpec(a.shape, lambda i: (0, 0))
        for a, k in ins
    ]
    out_specs = [
        pl.BlockSpec((tr, s[1]), lambda i: (i, 0)) if k == "row" else pl.BlockSpec(s, lambda i: (0, 0))
        for s, d, k in outs
    ]
    has_acc = any(k == "acc" for _, _, k in outs)
    return _carry_call(
        body, grid=(T // tr,), in_specs=in_specs, out_specs=out_specs,
        out_shape=[jax.ShapeDtypeStruct(s, d) for s, d, k in outs], scratch_shapes=[],
        sem=("arbitrary",) if has_acc else ("parallel",), name=name, args=[a for a, _ in ins], pipe=pipe,
    )


def _rstd(x):
    return lax.rsqrt(jnp.mean(x * x, axis=-1, keepdims=True) + EPS)


def _norm_fwd(x, g, name, pipe=None):
    def body(x_ref, g_ref, o_ref):
        x = x_ref[...]
        o_ref[...] = (x * _rstd(x) * g_ref[...]).astype(BF16)

    return _rows(body, [(x, "row"), (g, "vec")], [(x.shape, BF16, "row")], name, pipe=pipe)[0]


def _postnorm_fwd(x, y, g, weight, name, pipe=None):
    def body(x_ref, y_ref, g_ref, o_ref):
        y = y_ref[...]
        o_ref[...] = x_ref[...] + weight * (y * _rstd(y) * g_ref[...])

    return _rows(body, [(x, "row"), (y, "row"), (g, "vec")], [(x.shape, F32, "row")], name, pipe=pipe)[0]


def _norm_bwd(x, g, dy, weight, out_dtype, name, res=None, pipe=None):
    has_res = res is not None

    def body(*refs):
        x_ref, g_ref, dy_ref = refs[:3]
        res_ref = refs[3] if has_res else None
        dx_ref, dg_ref = refs[-2], refs[-1]

        @pl.when(pl.program_id(0) == 0)
        def _():
            dg_ref[...] = jnp.zeros_like(dg_ref)

        x = x_ref[...]
        dn = dy_ref[...].astype(F32) * weight
        r = _rstd(x)
        xhat = x * r
        dg_ref[...] += jnp.sum(dn * xhat, axis=0, keepdims=True)
        dxh = dn * g_ref[...]
        dx = r * (dxh - xhat * jnp.mean(dxh * xhat, axis=-1, keepdims=True))
        if has_res:
            dx = dx + res_ref[...]
        dx_ref[...] = dx.astype(out_dtype)

    ins = [(x, "row"), (g, "vec"), (dy, "row")] + ([(res, "row")] if has_res else [])
    return _rows(body, ins, [(x.shape, out_dtype, "row"), (g.shape, F32, "acc")], name, pipe=pipe)


def _swiglu_fwd(a, b, name, pipe=None):
    def body(a_ref, b_ref, o_ref):
        a = a_ref[...]
        o_ref[...] = (a * _sigmoid(a) * b_ref[...]).astype(BF16)

    return _rows(body, [(a, "row"), (b, "row")], [(a.shape, BF16, "row")], name, pipe=pipe)[0]


def _swiglu_bwd(a, b, ds, name, pipe=None):
    def body(a_ref, b_ref, ds_ref, da_ref, db_ref):
        a, ds = a_ref[...], ds_ref[...]
        sg = _sigmoid(a)
        da_ref[...] = (ds * b_ref[...] * (sg * (1.0 + a * (1.0 - sg)))).astype(BF16)
        db_ref[...] = (ds * (a * sg)).astype(BF16)

    return _rows(body, [(a, "row"), (b, "row"), (ds, "row")], [(a.shape, BF16, "row"), (a.shape, BF16, "row")], name, pipe=pipe)


def _merge_fwd(ga, gb, ya, yb, name):
    def body(ga_ref, gb_ref, ya_ref, yb_ref, o_ref):
        o_ref[...] = (_sigmoid(ga_ref[...]) * ya_ref[...] + _sigmoid(gb_ref[...]) * yb_ref[...]).astype(BF16)

    return _rows(body, [(ga, "row"), (gb, "row"), (ya, "row"), (yb, "row")], [(ga.shape, BF16, "row")], name)[0]


def _merge_bwd(ga, gb, ya, yb, dy, name):
    def body(ga_ref, gb_ref, ya_ref, yb_ref, dy_ref, dya_ref, dyb_ref, dga_ref, dgb_ref):
        dy = dy_ref[...]
        sa, sb = _sigmoid(ga_ref[...]), _sigmoid(gb_ref[...])
        dya_ref[...] = (dy * sa).astype(BF16)
        dyb_ref[...] = (dy * sb).astype(BF16)
        dga_ref[...] = (dy * ya_ref[...] * (sa * (1.0 - sa))).astype(BF16)
        dgb_ref[...] = (dy * yb_ref[...] * (sb * (1.0 - sb))).astype(BF16)

    ins = [(ga, "row"), (gb, "row"), (ya, "row"), (yb, "row"), (dy, "row")]
    return _rows(body, ins, [(ga.shape, BF16, "row")] * 4, name)


def _loss_head(y, target, name):
    D = y.shape[1]

    def body(y_ref, t_ref, dy_ref, acc_ref):
        @pl.when(pl.program_id(0) == 0)
        def _():
            acc_ref[...] = jnp.zeros_like(acc_ref)

        err = y_ref[...] - t_ref[...]
        dy_ref[...] = err * (1.0 / D)
        acc_ref[...] += jnp.sum(err * err, axis=0, keepdims=True)

    return _rows(body, [(y, "row"), (target, "row")], [(y.shape, F32, "row"), ((1, D), F32, "acc")], name)


def _rot(x):
    lane = lax.broadcasted_iota(jnp.int32, x.shape, 1)
    return jnp.where((lane % ROPE) < ROPE // 2, -pltpu.roll(x, 128 - ROPE // 2, 1), pltpu.roll(x, ROPE // 2, 1))


def _rope_q_fwd(qpe, cos, sin, name):
    T, W = qpe.shape
    tr = min(T, 512)
    blk = pl.BlockSpec((tr, 128), lambda i, j: (i, j))
    tab = pl.BlockSpec((tr, 128), lambda i, j: (i, 0))

    def body(x_ref, c_ref, s_ref, o_ref):
        x = x_ref[...]
        o_ref[...] = (x * c_ref[...] + _rot(x) * s_ref[...]).astype(BF16)

    return pl.pallas_call(
        body, grid=(T // tr, W // 128), in_specs=[blk, tab, tab], out_specs=blk,
        out_shape=jax.ShapeDtypeStruct((T, W), BF16), compiler_params=_params(("parallel", "parallel")), name=name,
    )(qpe, cos, sin)


def _rope_q_bwd(dq_heads, cos, sin, name):
    T, W = dq_heads.shape
    tr = min(T, 512)
    even = pl.BlockSpec((tr, 128), lambda i, j: (i, 2 * j))
    odd = pl.BlockSpec((tr, 128), lambda i, j: (i, 2 * j + 1))
    tab = pl.BlockSpec((tr, 128), lambda i, j: (i, 0))

    def body(a_ref, b_ref, c_ref, s_ref, o_ref):
        d = a_ref[...] + b_ref[...]
        o_ref[...] = (d * c_ref[...] - _rot(d * s_ref[...])).astype(BF16)

    return pl.pallas_call(
        body, grid=(T // tr, W // 256), in_specs=[even, odd, tab, tab],
        out_specs=pl.BlockSpec((tr, 128), lambda i, j: (i, j)),
        out_shape=jax.ShapeDtypeStruct((T, W // 2), BF16), compiler_params=_params(("parallel", "parallel")), name=name,
    )(dq_heads, dq_heads, cos, sin)


def _rope_k_fwd(kpe, cos, sin, name):
    def body(x_ref, c_ref, s_ref, o_ref):
        x = x_ref[...]
        y = x * c_ref[...] + _rot(x) * s_ref[...]
        o_ref[...] = (y + pltpu.roll(y, ROPE, 1)).astype(BF16)

    return _rows(body, [(kpe, "row"), (cos, "row"), (sin, "row")], [(kpe.shape, BF16, "row")], name)[0]


def _rope_k_bwd(dk_heads, cos, sin, name):
    T, W = dk_heads.shape

    def body(d_ref, c_ref, s_ref, o_ref):
        d = d_ref[:, 0:128]
        for h in range(1, W // 128):
            d = d + d_ref[:, h * 128:(h + 1) * 128]
        d = d + pltpu.roll(d, ROPE, 1)
        dx = d * c_ref[...] - _rot(d * s_ref[...])
        lane = lax.broadcasted_iota(jnp.int32, dx.shape, 1)
        o_ref[...] = jnp.where(lane < ROPE, dx, 0.0).astype(BF16)

    return _rows(body, [(dk_heads, "row"), (cos, "row"), (sin, "row")], [((T, 128), BF16, "row")], name)[0]


def _attn_probs(q, k, qpe, kpe, scale, causal, q0):
    s = _dot(q, k, NT)
    if qpe is not None:
        s = s + _dot(qpe, kpe, NT)
    s = s * scale
    if causal:
        row = q0 + lax.broadcasted_iota(jnp.int32, s.shape, 0)
        col = lax.broadcasted_iota(jnp.int32, s.shape, 1)
        s = jnp.where((col // CHUNK) <= (row // CHUNK), s, NEG)
    p = jnp.exp(s - jnp.max(s, axis=-1, keepdims=True))
    return p / jnp.sum(p, axis=-1, keepdims=True)


def _pe_mask(x, h):
    lane = lax.broadcasted_iota(jnp.int32, x.shape, 1)
    return jnp.where((lane // ROPE) == (h % 2), x, jnp.zeros_like(x))


def _attn_fwd(q, k, v, scale, name, qpe=None, kpe=None, causal=False, pipe=None):
    T, W = q.shape
    Tk = k.shape[0]
    H = W // HEAD
    tq = min(T, 256)
    nq = T // tq
    has_pe = qpe is not None
    qs = pl.BlockSpec((tq, HEAD), lambda h, i: (i, h))
    ks = pl.BlockSpec((Tk, HEAD), lambda h, i: (0, h))
    in_specs, args = [qs, ks, ks], [q, k, v]
    if has_pe:
        in_specs += [pl.BlockSpec((tq, HEAD), lambda h, i: (i, h // 2)), pl.BlockSpec((Tk, HEAD), lambda h, i: (0, 0))]
        args += [qpe, kpe]

    def body(*refs):
        q_ref, k_ref, v_ref = refs[:3]
        o_ref = refs[-1]
        h, i = pl.program_id(0), pl.program_id(1)

        def compute(klen):
            qp = _pe_mask(refs[3][...], h) if has_pe else None
            kp = refs[4][0:klen, :] if has_pe else None
            p = _attn_probs(q_ref[...], k_ref[0:klen, :], qp, kp, scale, causal, i * tq)
            o_ref[...] = _dot(p.astype(BF16), v_ref[0:klen, :], NN).astype(BF16)

        if causal:
            for qi in range(nq):
                pl.when(i == qi)(functools.partial(compute, (qi + 1) * tq))
        else:
            compute(Tk)

    return _carry_call(
        body, grid=(H, nq), in_specs=in_specs, out_specs=qs, out_shape=jax.ShapeDtypeStruct((T, W), BF16),
        scratch_shapes=[], sem=("parallel", "parallel"), name=name, args=args, pipe=pipe,
    )


def _attn_bwd(q, k, v, do, scale, name, qpe=None, kpe=None, causal=False, pipe=None):
    T, W = q.shape
    Tk = k.shape[0]
    H = W // HEAD
    tq = min(T, 256)
    nq = T // tq
    has_pe = qpe is not None
    qs = pl.BlockSpec((tq, HEAD), lambda h, i: (i, h))
    ks = pl.BlockSpec((Tk, HEAD), lambda h, i: (0, h))
    in_specs, args = [qs, ks, ks, qs], [q, k, v, do]
    out_specs = [qs, ks, ks]
    out_shape = [jax.ShapeDtypeStruct((T, W), BF16), jax.ShapeDtypeStruct((Tk, W), BF16), jax.ShapeDtypeStruct((Tk, W), BF16)]
    scratch = [pltpu.VMEM((Tk, HEAD), F32), pltpu.VMEM((Tk, HEAD), F32)]
    if has_pe:
        in_specs += [pl.BlockSpec((tq, HEAD), lambda h, i: (i, h // 2)), pl.BlockSpec((Tk, HEAD), lambda h, i: (0, 0))]
        args += [qpe, kpe]
        out_specs += [qs, ks]
        out_shape += [jax.ShapeDtypeStruct((T, W), F32), jax.ShapeDtypeStruct((Tk, W), F32)]
        scratch += [pltpu.VMEM((Tk, HEAD), F32)]
    n_in = len(in_specs)

    def body(*refs):
        q_ref, k_ref, v_ref, do_ref = refs[:4]
        outs = refs[n_in:n_in + len(out_specs)]
        accs = refs[n_in + len(out_specs):]
        dq_ref, dk_ref, dv_ref = outs[:3]
        h, i = pl.program_id(0), pl.program_id(1)

        @pl.when(i == 0)
        def _():
            for acc in accs:
                acc[...] = jnp.zeros_like(acc)

        def compute(klen):
            qp = _pe_mask(refs[4][...], h) if has_pe else None
            kp = refs[5][0:klen, :] if has_pe else None
            qv, kv, vv, dov = q_ref[...], k_ref[0:klen, :], v_ref[0:klen, :], do_ref[...]
            p = _attn_probs(qv, kv, qp, kp, scale, causal, i * tq)
            dp = _dot(dov, vv, NT)
            ds = (p * (dp - jnp.sum(p * dp, axis=-1, keepdims=True)) * scale).astype(BF16)
            dq_ref[...] = _dot(ds, kv, NN).astype(BF16)
            accs[0][0:klen, :] += _dot(ds, qv, TN)
            accs[1][0:klen, :] += _dot(p.astype(BF16), dov, TN)
            if has_pe:
                outs[3][...] = _pe_mask(_dot(ds, kp, NN), h)
                accs[2][0:klen, :] += _dot(ds, qp, TN)

        if causal:
            for qi in range(nq):
                pl.when(i == qi)(functools.partial(compute, (qi + 1) * tq))
        else:
            compute(Tk)

        @pl.when(i == nq - 1)
        def _():
            dk_ref[...] = accs[0][...].astype(BF16)
            dv_ref[...] = accs[1][...].astype(BF16)
            if has_pe:
                outs[4][...] = accs[2][...]

    return _carry_call(
        body, grid=(H, nq), in_specs=in_specs, out_specs=out_specs, out_shape=out_shape, scratch_shapes=scratch,
        sem=("parallel", "arbitrary"), name=name, args=args, pipe=pipe,
    )


def _split3(x):
    hi = x.astype(BF16)
    r1 = x - hi.astype(F32)
    mid = r1.astype(BF16)
    lo = (r1 - mid.astype(F32)).astype(BF16)
    return hi, mid, lo


def _tri_dot(tri, x):
    hi, mid, lo = _split3(x)
    return _dot(tri, hi, NN) + _dot(tri, mid, NN) + _dot(tri, lo, NN)


def _hg_gates(u, lb):
    q, fr, v = u[:, 0:HEAD], u[:, HEAD:2 * HEAD], u[:, 2 * HEAD:3 * HEAD]
    sg = 1.0 / (1.0 + jnp.exp(-fr))
    sgm = 1.0 / (1.0 + jnp.exp(fr))
    f = lb + (1.0 - lb) * sg
    kin = (1.0 - lb) * sgm
    sq = _sigmoid(q)
    return q, v, sg, sgm, f, kin, sq, q * sq


def _hg_block_mats(blk):
    t = lax.broadcasted_iota(jnp.int32, (blk, blk), 0)
    s = lax.broadcasted_iota(jnp.int32, (blk, blk), 1)
    same = (t // SUB) == (s // SUB)
    one = lambda m: jnp.where(m, 1.0, 0.0).astype(BF16)
    return one(same & (s <= t)), one(same), one(same & (s >= t))


def _hg_stage(pairs, blk):
    for sc, val in pairs:
        sc[0:SUB, :] = jnp.zeros((SUB, HEAD), F32)
        sc[SUB:SUB + blk, :] = val


def _hg_scan_fwd(u, lb, name, pipe=None):
    T, W = u.shape
    H = W // (4 * HEAD)
    blk = min(T, HG_BLOCK)
    nb, nsb = T // blk, blk // SUB

    def body(u_ref, lb_ref, o_ref, st_ref, state, k_sc, b_sc, v_sc):
        @pl.when(pl.program_id(1) == 0)
        def _():
            state[...] = jnp.zeros_like(state)

        q, v, sg, sgm, f, kin, sq, qin = _hg_gates(u_ref[...], lb_ref[...])
        tri, ones, _ = _hg_block_mats(blk)
        logf = jnp.log(f)
        brel = _tri_dot(tri, logf)
        btot = _tri_dot(ones, logf)
        _hg_stage(((k_sc, kin), (b_sc, brel), (v_sc, v)), blk)
        sub_row = lax.broadcasted_iota(jnp.int32, (blk, HEAD), 0) % SUB
        o = jnp.zeros((blk, HEAD), F32)
        for d in range(SUB):
            win = slice(SUB - d, SUB - d + blk)
            e = jnp.exp(jnp.where(sub_row >= d, brel - b_sc[win, :], NEG))
            o = o + jnp.sum(qin * e * k_sc[win, :], axis=-1, keepdims=True) * v_sc[win, :]
        ab = (qin * jnp.exp(brel)).astype(BF16)
        kdb = (kin * jnp.exp(btot - brel)).astype(BF16)
        vb = v.astype(BF16)
        ebt = jnp.exp(btot)
        st = state[...]
        st_ref[...] = st
        for i in range(nsb):
            sl = slice(i * SUB, (i + 1) * SUB)
            o_ref[sl, :] = o[sl] + _dot(ab[sl], st.astype(BF16), NT)
            st = ebt[i * SUB:i * SUB + 1, :] * st + _dot(vb[sl], kdb[sl], TN)
        state[...] = st

    return _carry_call(
        body, grid=(H, nb),
        in_specs=[pl.BlockSpec((blk, 4 * HEAD), lambda h, c: (c, h)), pl.BlockSpec((1, HEAD), lambda h, c: (0, h))],
        out_specs=[pl.BlockSpec((blk, HEAD), lambda h, c: (c, h)), pl.BlockSpec((None, None, HEAD, HEAD), lambda h, c: (h, c, 0, 0))],
        out_shape=[jax.ShapeDtypeStruct((T, H * HEAD), F32), jax.ShapeDtypeStruct((H, nb, HEAD, HEAD), F32)],
        scratch_shapes=[pltpu.VMEM((HEAD, HEAD), F32)] + [pltpu.VMEM((SUB + blk, HEAD), F32)] * 3,
        sem=("parallel", "arbitrary"), name=name, args=(u, lb), pipe=pipe,
    )


def _hg_scan_bwd(u, lb, do, dog, states, name, pipe=None):
    T, W = u.shape
    H = W // (4 * HEAD)
    blk = min(T, HG_BLOCK)
    NC, nsb = T // blk, blk // SUB

    def body(u_ref, lb_ref, do_ref, dog_ref, st_ref, du_ref, dlb_ref, dstate, s_all, k_sc, b_sc, v_sc, dk_sc, dbn_sc,
             dv_sc, da_sc, dkd_sc, dvs_sc, dbt_sc):
        @pl.when(pl.program_id(1) == 0)
        def _():
            dstate[...] = jnp.zeros_like(dstate)
            dlb_ref[...] = jnp.zeros_like(dlb_ref)

        lb = lb_ref[...]
        q, v, sg, sgm, f, kin, sq, qin = _hg_gates(u_ref[...], lb)
        tri, ones, tri_t = _hg_block_mats(blk)
        logf = jnp.log(f)
        brel = _tri_dot(tri, logf)
        btot = _tri_dot(ones, logf)
        eb, ekd, ebt = jnp.exp(brel), jnp.exp(btot - brel), jnp.exp(btot)
        a, kd = qin * eb, kin * ekd
        ab, kdb, vb = a.astype(BF16), kd.astype(BF16), v.astype(BF16)
        do = do_ref[...]
        dob = do.astype(BF16)
        st = st_ref[...]
        for i in range(nsb):
            sl = slice(i * SUB, (i + 1) * SUB)
            s_all[i] = st
            st = ebt[i * SUB:i * SUB + 1, :] * st + _dot(vb[sl], kdb[sl], TN)
        ds = dstate[...]
        for i in reversed(range(nsb)):
            sl = slice(i * SUB, (i + 1) * SUB)
            st_i = s_all[i]
            dsb = ds.astype(BF16)
            e_i = ebt[i * SUB:i * SUB + 1, :]
            da_sc[sl, :] = _dot(dob[sl], st_i.astype(BF16), NN)
            dvs_sc[sl, :] = _dot(kdb[sl], dsb, NT)
            dkd_sc[sl, :] = _dot(vb[sl], dsb, NN)
            dbt_sc[sl, :] = jnp.broadcast_to(jnp.sum(ds * st_i, axis=0, keepdims=True) * e_i, (SUB, HEAD))
            ds = e_i * ds + _dot(dob[sl], ab[sl], TN)
        dstate[...] = ds
        da, dkd = da_sc[...], dkd_sc[...]
        t1 = dkd * kd
        dqin = da * eb
        dbrel = da * a - t1
        dkin = dkd * ekd
        dbtot = dbt_sc[...] + _tri_dot(ones, t1)
        _hg_stage(((k_sc, kin), (b_sc, brel), (v_sc, v)), blk)
        for sc in (dk_sc, dbn_sc, dv_sc):
            sc[...] = jnp.zeros_like(sc)
        sub_row = lax.broadcasted_iota(jnp.int32, (blk, HEAD), 0) % SUB
        for d in range(SUB):
            win = slice(SUB - d, SUB - d + blk)
            ks = k_sc[win, :]
            e = jnp.exp(jnp.where(sub_row >= d, brel - b_sc[win, :], NEG))
            qe = qin * e
            col = jnp.sum(qe * ks, axis=-1, keepdims=True)
            dcol = jnp.sum(do * v_sc[win, :], axis=-1, keepdims=True)
            dqe = dcol * qe
            g = dqe * ks
            dqin = dqin + dcol * (e * ks)
            dbrel = dbrel + g
            dk_sc[win, :] += dqe
            dbn_sc[win, :] += g
            dv_sc[win, :] += col * do
        dkin = dkin + dk_sc[SUB:SUB + blk, :]
        dbrel = dbrel - dbn_sc[SUB:SUB + blk, :]
        dv = dvs_sc[...] + dv_sc[SUB:SUB + blk, :]
        dlogf = _tri_dot(tri_t, dbrel) + dbtot
        diff = dlogf / f - dkin
        dlb_ref[...] += jnp.sum(sgm * diff, axis=0, keepdims=True)
        du_ref[:, 0:HEAD] = (dqin * (sq * (1.0 + q * (1.0 - sq)))).astype(BF16)
        du_ref[:, HEAD:2 * HEAD] = ((1.0 - lb) * sg * sgm * diff).astype(BF16)
        du_ref[:, 2 * HEAD:3 * HEAD] = dv.astype(BF16)
        du_ref[:, 3 * HEAD:4 * HEAD] = dog_ref[...]

    rev = lambda h, c: (NC - 1 - c, h)
    return _carry_call(
        body, grid=(H, NC),
        in_specs=[
            pl.BlockSpec((blk, 4 * HEAD), rev), pl.BlockSpec((1, HEAD), lambda h, c: (0, h)),
            pl.BlockSpec((blk, HEAD), rev), pl.BlockSpec((blk, HEAD), rev),
            pl.BlockSpec((None, None, HEAD, HEAD), lambda h, c: (h, NC - 1 - c, 0, 0)),
        ],
        out_specs=[pl.BlockSpec((blk, 4 * HEAD), rev), pl.BlockSpec((1, HEAD), lambda h, c: (0, h))],
        out_shape=[jax.ShapeDtypeStruct((T, W), BF16), jax.ShapeDtypeStruct((1, H * HEAD), F32)],
        scratch_shapes=[pltpu.VMEM((HEAD, HEAD), F32), pltpu.VMEM((nsb, HEAD, HEAD), F32)]
        + [pltpu.VMEM((SUB + blk, HEAD), F32)] * 6 + [pltpu.VMEM((blk, HEAD), F32)] * 4,
        sem=("parallel", "arbitrary"), name=name, args=(u, lb, do, dog, states), pipe=pipe,
    )


def _hg_tail_fwd(o_raw, u, g, name):
    T, D = o_raw.shape
    H = D // HEAD
    tr = min(T, 512)
    blk = pl.BlockSpec((tr, HEAD), lambda h, i: (i, h))

    def body(o_ref, og_ref, g_ref, out_ref):
        o, og = o_ref[...], og_ref[...]
        out_ref[...] = (o * _rstd(o) * g_ref[...] * (og * _sigmoid(og))).astype(BF16)

    return pl.pallas_call(
        body, grid=(H, T // tr),
        in_specs=[blk, pl.BlockSpec((tr, HEAD), lambda h, i: (i, 4 * h + 3)), pl.BlockSpec((1, HEAD), lambda h, i: (0, h))],
        out_specs=blk, out_shape=jax.ShapeDtypeStruct((T, D), BF16),
        compiler_params=_params(("parallel", "parallel")), name=name,
    )(o_raw, u, g)


def _hg_tail_bwd(o_raw, u, g, doa, name):
    T, D = o_raw.shape
    H = D // HEAD
    tr = min(T, 512)
    blk = pl.BlockSpec((tr, HEAD), lambda h, i: (i, h))
    vec = pl.BlockSpec((1, HEAD), lambda h, i: (0, h))

    def body(o_ref, og_ref, g_ref, doa_ref, do_ref, dog_ref, dg_ref):
        @pl.when(pl.program_id(1) == 0)
        def _():
            dg_ref[...] = jnp.zeros_like(dg_ref)

        o, og, doa, g = o_ref[...], og_ref[...], doa_ref[...], g_ref[...]
        sg = _sigmoid(og)
        r = _rstd(o)
        xhat = o * r
        dog_ref[...] = (doa * (xhat * g) * (sg * (1.0 + og * (1.0 - sg)))).astype(BF16)
        dn = doa * (og * sg)
        dg_ref[...] += jnp.sum(dn * xhat, axis=0, keepdims=True)
        dxh = dn * g
        do_ref[...] = r * (dxh - xhat * jnp.mean(dxh * xhat, axis=-1, keepdims=True))

    return pl.pallas_call(
        body, grid=(H, T // tr),
        in_specs=[blk, pl.BlockSpec((tr, HEAD), lambda h, i: (i, 4 * h + 3)), vec, blk],
        out_specs=[blk, blk, vec],
        out_shape=[jax.ShapeDtypeStruct((T, D), F32), jax.ShapeDtypeStruct((T, D), BF16), jax.ShapeDtypeStruct((1, D), F32)],
        compiler_params=_params(("parallel", "arbitrary")), name=name,
    )(o_raw, u, g, doa)


def _lb_fwd(logits, name):
    def body(l_ref, o_ref):
        l0, l1 = l_ref[0:1, :], l_ref[1:2, :]
        m = jnp.maximum(l0, l1)
        e0, e1 = jnp.exp(l0 - m), jnp.exp(l1 - m)
        o_ref[...] = e0 / (e0 + e1)

    D = logits.shape[1]
    return pl.pallas_call(body, out_shape=jax.ShapeDtypeStruct((1, D), F32), name=name)(logits)


def _lb_bwd(lb, dlb, name):
    def body(lb_ref, d_ref, o_ref):
        lb = lb_ref[...]
        d0 = d_ref[...] * lb * (1.0 - lb)
        o_ref[0:1, :] = d0
        o_ref[1:2, :] = -d0

    D = lb.shape[1]
    return pl.pallas_call(body, out_shape=jax.ShapeDtypeStruct((2, D), F32), name=name)(lb, dlb)


def _slots(w):
    return w.shape[0] if w.ndim == 3 else 0


def _ffn_fwd(x, p, tag, carry):
    mm = lambda a, b, mode, dt, name, **kw: _mm(a, b, mode, dt, name, pipe=carry.get(name), **kw)
    hb = _norm_fwd(x, p["pre_g"], f"{tag}_pre_norm", pipe=carry.get(f"{tag}_pre_norm"))
    a = mm(hb, p["w_gate"], "nn", F32, f"{tag}_gate")
    b = mm(hb, p["w_up"], "nn", F32, f"{tag}_up")
    sb = _swiglu_fwd(a, b, f"{tag}_swiglu", pipe=carry.get(f"{tag}_swiglu"))
    y = mm(sb, p["w_down"], "nn", F32, f"{tag}_down")
    xo = _postnorm_fwd(x, y, p["post_g"], FFN_RESIDUAL_WEIGHT, f"{tag}_post_norm", pipe=carry.get(f"{tag}_post_norm"))
    return xo, (x, hb, a, b, sb, y)


def _ffn_bwd(dxo, p, saved, tag, carry, on_dw):
    mm = lambda a, b, mode, dt, name, **kw: _mm(a, b, mode, dt, name, pipe=carry.get(name), **kw)
    x, hb, a, b, sb, y = saved
    dyb, dpost = _norm_bwd(y, p["post_g"], dxo, FFN_RESIDUAL_WEIGHT, BF16, f"{tag}_post_norm_bwd",
                           pipe=carry.get(f"{tag}_post_norm_bwd"))
    dw_down = mm(sb, dyb, "tn", BF16, f"{tag}_down_dw")
    ds = mm(dyb, p["w_down"], "nt", F32, f"{tag}_down_dx")
    dab, dbb = _swiglu_bwd(a, b, ds, f"{tag}_swiglu_bwd", pipe=carry.get(f"{tag}_swiglu_bwd"))
    dw_gate = mm(hb, dab, "tn", BF16, f"{tag}_gate_dw", out_slots=_slots(p["w_gate"]))
    dw_up = mm(hb, dbb, "tn", BF16, f"{tag}_up_dw", out_slots=_slots(p["w_up"]))
    on_dw({"w_gate": dw_gate, "w_up": dw_up, "w_down": dw_down})
    dh = mm(dab, p["w_gate"], "nt", F32, f"{tag}_gate_dx")
    dh = mm(dbb, p["w_up"], "nt", F32, f"{tag}_up_dx", add=dh)
    dx, dpre = _norm_bwd(x, p["pre_g"], dh, 1.0, F32, f"{tag}_pre_norm_bwd", res=dxo, pipe=carry.get(f"{tag}_pre_norm_bwd"))
    return dx, {"pre_g": dpre, "w_gate": dw_gate, "w_up": dw_up, "w_down": dw_down, "post_g": dpost}


def _mixer_fwd(x, cos, sin, p, carry):
    scale = (HEAD + ROPE) ** -0.5
    hb = _norm_fwd(x, p["pre_g"], "mix_pre_norm")
    u = _mm(hb, p["w_hg"], "nn", F32, "mix_in_hg", pipe=carry.get("mix_in_hg"))
    cq = _mm(hb, p["w_cq"], "nn", F32, "mix_in_cq")
    ckv = _mm(hb, p["w_ckv"], "nn", F32, "mix_in_ckv")
    kpe = _mm(hb, p["w_kpe"], "nn", F32, "mix_in_kpe")
    ga = _mm(hb, p["w_ga"], "nn", F32, "mix_in_ga", pipe=carry.get("mix_in_ga"))
    gb = _mm(hb, p["w_gb"], "nn", F32, "mix_in_gb", pipe=carry.get("mix_in_gb"))
    lb = _lb_fwd(p["lb_logits"], "hg_lb")
    o_raw, states = _hg_scan_fwd(u, lb, "hg_scan", pipe=carry.get("hg_scan"))
    oa = _hg_tail_fwd(o_raw, u, p["hg_norm_g"], "hg_tail")
    ya = _mm(oa, p["w_branch_a"], "nn", F32, "mix_branch_a")
    cqn = _norm_fwd(cq, p["q_norm_g"], "mla_q_norm")
    qn = _mm(cqn, p["w_qn"], "nn", BF16, "mla_q_up_nope")
    qpe = _rope_q_fwd(_mm(cqn, p["w_qpe"], "nn", F32, "mla_q_up_pe"), cos, sin, "mla_rope_q")
    ckvn = _norm_fwd(ckv, p["kv_norm_g"], "mla_kv_norm")
    kn = _mm(ckvn, p["w_kn"], "nn", BF16, "mla_k_up")
    vv = _mm(ckvn, p["w_vv"], "nn", BF16, "mla_v_up")
    kpe2 = _rope_k_fwd(kpe, cos, sin, "mla_rope_k")
    ob = _attn_fwd(qn, kn, vv, scale, "mla_attn", qpe=qpe, kpe=kpe2, causal=True, pipe=carry.get("mla_attn"))
    yb = _mm(ob, p["w_branch_b"], "nn", F32, "mix_branch_b")
    ym = _merge_fwd(ga, gb, ya, yb, "mix_merge")
    z = _mm(ym, p["w_out"], "nn", F32, "mix_out")
    xo = _postnorm_fwd(x, z, p["post_g"], 1.0, "mix_post_norm")
    saved = (x, hb, u, cq, ckv, ga, gb, lb, o_raw, states, oa, ya, cqn, qn, qpe, ckvn, kn, vv, kpe2, ob, yb, ym, z)
    return xo, saved


def _mixer_bwd(dxo, cos, sin, p, saved, carry, on_early):
    x, hb, u, cq, ckv, ga, gb, lb, o_raw, states, oa, ya, cqn, qn, qpe, ckvn, kn, vv, kpe2, ob, yb, ym, z = saved
    scale = (HEAD + ROPE) ** -0.5
    g = {}
    dzb, g["post_g"] = _norm_bwd(z, p["post_g"], dxo, 1.0, BF16, "mix_post_norm_bwd")
    g["w_out"] = _mm(ym, dzb, "tn", BF16, "mix_out_dw")
    dym = _mm(dzb, p["w_out"], "nt", F32, "mix_out_dx")
    dya, dyb, dga, dgb = _merge_bwd(ga, gb, ya, yb, dym, "mix_merge_bwd")
    g["w_branch_a"] = _mm(oa, dya, "tn", BF16, "mix_branch_a_dw")
    doa = _mm(dya, p["w_branch_a"], "nt", F32, "mix_branch_a_dx")
    do_raw, dog, g["hg_norm_g"] = _hg_tail_bwd(o_raw, u, p["hg_norm_g"], doa, "hg_tail_bwd")
    du, dlb = _hg_scan_bwd(u, lb, do_raw, dog, states, "hg_scan_bwd", pipe=carry.get("hg_scan_bwd"))
    g["lb_logits"] = _lb_bwd(lb, dlb, "hg_lb_bwd")
    g["w_branch_b"] = _mm(ob, dyb, "tn", BF16, "mix_branch_b_dw")
    on_early({k: g[k] for k in ("w_out", "w_branch_a", "w_branch_b")})
    dob = _mm(dyb, p["w_branch_b"], "nt", BF16, "mix_branch_b_dx")
    dqn, dkn, dvv, dqpe_h, dkpe_h = _attn_bwd(qn, kn, vv, dob, scale, "mla_attn_bwd", qpe=qpe, kpe=kpe2, causal=True,
                                              pipe=carry.get("mla_attn_bwd"))
    dqpe = _rope_q_bwd(dqpe_h, cos, sin, "mla_rope_q_bwd")
    dkpe = _rope_k_bwd(dkpe_h, cos, sin, "mla_rope_k_bwd")
    g["w_qn"] = _mm(cqn, dqn, "tn", BF16, "mla_q_up_nope_dw")
    g["w_qpe"] = _mm(cqn, dqpe, "tn", BF16, "mla_q_up_pe_dw")
    dcqn = _mm(dqn, p["w_qn"], "nt", F32, "mla_q_up_nope_dx")
    dcqn = _mm(dqpe, p["w_qpe"], "nt", F32, "mla_q_up_pe_dx", add=dcqn)
    dcq, g["q_norm_g"] = _norm_bwd(cq, p["q_norm_g"], dcqn, 1.0, BF16, "mla_q_norm_bwd")
    g["w_kn"] = _mm(ckvn, dkn, "tn", BF16, "mla_k_up_dw")
    g["w_vv"] = _mm(ckvn, dvv, "tn", BF16, "mla_v_up_dw")
    dckvn = _mm(dkn, p["w_kn"], "nt", F32, "mla_k_up_dx")
    dckvn = _mm(dvv, p["w_vv"], "nt", F32, "mla_v_up_dx", add=dckvn)
    dckv, g["kv_norm_g"] = _norm_bwd(ckv, p["kv_norm_g"], dckvn, 1.0, BF16, "mla_kv_norm_bwd")
    dh = None
    for key, d in (("w_hg", du), ("w_cq", dcq), ("w_ckv", dckv), ("w_kpe", dkpe), ("w_ga", dga), ("w_gb", dgb)):
        g[key] = _mm(hb, d, "tn", BF16, f"mix_in_{key}_dw")
        dh = _mm(d, p[key], "nt", F32, f"mix_in_{key}_dx", add=dh)
    dx, g["pre_g"] = _norm_bwd(x, p["pre_g"], dh, 1.0, F32, "mix_pre_norm_bwd", res=dxo)
    return dx, g


def _xa_fwd(x, mem, p):
    scale = HEAD ** -0.5
    hb = _norm_fwd(x, p["pre_g"], "xa_pre_norm")
    mb = _norm_fwd(mem, p["mem_g"], "xa_mem_norm")
    q = _mm(hb, p["w_q"], "nn", BF16, "xa_q")
    k = _mm(mb, p["w_k"], "nn", BF16, "xa_k")
    v = _mm(mb, p["w_v"], "nn", BF16, "xa_v")
    o = _attn_fwd(q, k, v, scale, "xa_attn")
    z = _mm(o, p["w_o"], "nn", F32, "xa_o")
    xo = _postnorm_fwd(x, z, p["post_g"], 1.0, "xa_post_norm")
    return xo, (x, mem, hb, mb, q, k, v, o, z)


def _xa_bwd(dxo, p, saved):
    x, mem, hb, mb, q, k, v, o, z = saved
    scale = HEAD ** -0.5
    g = {}
    dzb, g["post_g"] = _norm_bwd(z, p["post_g"], dxo, 1.0, BF16, "xa_post_norm_bwd")
    g["w_o"] = _mm(o, dzb, "tn", BF16, "xa_o_dw", out_slots=_slots(p["w_o"]))
    do = _mm(dzb, p["w_o"], "nt", BF16, "xa_o_dx")
    dq, dk, dv = _attn_bwd(q, k, v, do, scale, "xa_attn_bwd")
    g["w_q"] = _mm(hb, dq, "tn", BF16, "xa_q_dw")
    g["w_k"] = _mm(mb, dk, "tn", BF16, "xa_k_dw")
    g["w_v"] = _mm(mb, dv, "tn", BF16, "xa_v_dw")
    dh = _mm(dq, p["w_q"], "nt", F32, "xa_q_dx")
    dm = _mm(dk, p["w_k"], "nt", F32, "xa_k_dx")
    dm = _mm(dv, p["w_v"], "nt", F32, "xa_v_dx", add=dm)
    _, g["mem_g"] = _norm_bwd(mem, p["mem_g"], dm, 1.0, BF16, "xa_mem_norm_bwd")
    dx, g["pre_g"] = _norm_bwd(x, p["pre_g"], dh, 1.0, F32, "xa_pre_norm_bwd", res=dxo)
    return dx, g


def _local_step(x, mem, cos, sin, target, params_of, carry, on_grads):
    p1 = params_of("ffn1")
    x1, s1 = _ffn_fwd(x, p1, "ffn1", carry)
    p2 = params_of("mix")
    x2, s2 = _mixer_fwd(x1, cos, sin, p2, carry)
    p3 = params_of("xa")
    x3, s3 = _xa_fwd(x2, mem, p3)
    p4 = params_of("ffn2")
    x4, s4 = _ffn_fwd(x3, p4, "ffn2", carry)
    dy, sq_err = _loss_head(x4, target, "loss_head")
    loss = 0.5 / x.shape[1] * jnp.sum(sq_err)
    dx, g4 = _ffn_bwd(dy, p4, s4, "ffn2", carry, lambda g: on_grads("ffn2_dw", g))
    on_grads("ffn2", g4)
    dx, g3 = _xa_bwd(dx, p3, s3)
    on_grads("xa", g3)
    dx, g2 = _mixer_bwd(dx, cos, sin, p2, s2, carry, lambda g: on_grads("mix_early", g))
    on_grads("mix", g2)
    dx, g1 = _ffn_bwd(dx, p1, s1, "ffn1", carry, lambda g: on_grads("ffn1_dw", g))
    on_grads("ffn1", g1)
    return loss, dx


def _split_w_in(w_in):
    D = w_in.shape[0]
    H = D // HEAD
    lora = (w_in.shape[1] - 6 * D - ROPE) // 2
    o = 4 * D
    w_hg = w_in[:, :o].reshape(D, 4, H, HEAD).transpose(0, 2, 1, 3).reshape(D, 4 * D)
    w_cq, w_ckv = w_in[:, o:o + lora], w_in[:, o + lora:o + 2 * lora]
    o += 2 * lora
    w_kpe = jnp.pad(w_in[:, o:o + ROPE], ((0, 0), (0, HEAD - ROPE)))
    o += ROPE
    return {"w_hg": w_hg, "w_cq": w_cq, "w_ckv": w_ckv, "w_kpe": w_kpe, "w_ga": w_in[:, o:o + D], "w_gb": w_in[:, o + D:o + 2 * D]}


def _merge_w_in(g):
    D = g["w_ga"].shape[0]
    H = D // HEAD
    hg = g["w_hg"].reshape(D, H, 4, HEAD).transpose(0, 2, 1, 3).reshape(D, 4 * D)
    return jnp.concatenate([hg, g["w_cq"], g["w_ckv"], g["w_kpe"][:, :ROPE], g["w_ga"], g["w_gb"]], axis=1)


def _split_heads(w, rest):
    K, N = w.shape
    w3 = w.reshape(K, N // (HEAD + rest), HEAD + rest)
    return w3[:, :, :HEAD].reshape(K, -1), w3[:, :, HEAD:].reshape(K, -1)


def _merge_heads(a, b, rest):
    K = a.shape[0]
    H = a.shape[1] // HEAD
    return jnp.concatenate([a.reshape(K, H, HEAD), b.reshape(K, H, rest)], axis=2).reshape(K, H * (HEAD + rest))


BLOCK_WEIGHTS = {
    "ffn1": ("ffn1_w_gate", "ffn1_w_up", "ffn1_w_down"),
    "mix": ("w_in", "mla_w_q_up", "mla_w_kv_up", "w_branch_a", "w_branch_b", "w_out"),
    "xa": ("xa_w_q", "xa_w_k", "xa_w_v", "xa_w_o"),
    "ffn2": ("ffn2_w_gate", "ffn2_w_up", "ffn2_w_down"),
}


def _block_params(block, w, small):
    if block in ("ffn1", "ffn2"):
        return {"pre_g": small[f"{block}_pre_g"], "w_gate": w[f"{block}_w_gate"], "w_up": w[f"{block}_w_up"],
                "w_down": w[f"{block}_w_down"], "post_g": small[f"{block}_post_g"]}
    if block == "xa":
        return {"pre_g": small["xa_pre_g"], "mem_g": small["xa_mem_g"], "post_g": small["xa_post_g"],
                "w_q": w["xa_w_q"], "w_k": w["xa_w_k"], "w_v": w["xa_w_v"], "w_o": w["xa_w_o"]}
    mix = _split_w_in(w["w_in"])
    mix["w_qn"], mix["w_qpe"] = _split_heads(w["mla_w_q_up"], ROPE)
    mix["w_kn"], mix["w_vv"] = _split_heads(w["mla_w_kv_up"], HEAD)
    mix.update(w_branch_a=w["w_branch_a"], w_branch_b=w["w_branch_b"], w_out=w["w_out"], pre_g=small["mix_pre_g"],
               post_g=small["mix_post_g"], hg_norm_g=small["hg_norm_g"], q_norm_g=small["mla_q_norm_g"],
               kv_norm_g=small["mla_kv_norm_g"], lb_logits=small["hgrn_lb_logits"])
    return mix


def _block_grads(block, g):
    if block in ("ffn1", "ffn2"):
        return {f"{block}_{k}": g[k] for k in ("pre_g", "w_gate", "w_up", "w_down", "post_g")}
    if block == "xa":
        return {f"xa_{k}": g[k] for k in ("pre_g", "mem_g", "post_g", "w_q", "w_k", "w_v", "w_o")}
    if block in ("ffn1_dw", "ffn2_dw"):
        return {f"{block[:4]}_{k}": v for k, v in g.items()}
    if block == "mix_early":
        return dict(g)
    return dict(w_in=_merge_w_in(g), mla_w_q_up=_merge_heads(g["w_qn"], g["w_qpe"], ROPE),
                mla_w_kv_up=_merge_heads(g["w_kn"], g["w_vv"], HEAD), mix_pre_g=g["pre_g"], mix_post_g=g["post_g"],
                hg_norm_g=g["hg_norm_g"], mla_q_norm_g=g["q_norm_g"], mla_kv_norm_g=g["kv_norm_g"],
                hgrn_lb_logits=g["lb_logits"])


def _rope_tables(positions):
    inv_freq = 1.0 / (ROPE_THETA ** (jnp.arange(0, ROPE, 2, dtype=F32) / ROPE))
    ang = positions.astype(F32)[:, None] * inv_freq
    return jnp.tile(jnp.cos(ang), (1, 4)), jnp.tile(jnp.sin(ang), (1, 4))


def _adamw(w, g, m, v, name):
    bc1 = 1.0 - ADAM_B1 ** ADAM_STEP
    bc2 = 1.0 - ADAM_B2 ** ADAM_STEP

    def body(w_ref, g_ref, m_ref, v_ref, go_ref, d_ref, mo_ref, vo_ref):
        g = g_ref[...]
        m = ADAM_B1 * m_ref[...] + (1.0 - ADAM_B1) * g
        v = ADAM_B2 * v_ref[...] + (1.0 - ADAM_B2) * (g * g)
        go_ref[...] = g
        mo_ref[...] = m
        vo_ref[...] = v
        d_ref[...] = -ADAM_LR * ((m / bc1) / (jnp.sqrt(v / bc2) + ADAM_EPS) + ADAM_WD * w_ref[...])

    return _rows(body, [(w, "row"), (g, "row"), (m, "row"), (v, "row")], [(w.shape, F32, "row")] * 4, name)


ANY = pl.BlockSpec(memory_space=pl.ANY)
COMM_AXES = ("x", "y", "c")


def _place():
    x, y, c = (lax.axis_index(n) for n in COMM_AXES)
    chips = [(1 - x, y), (x, 1 - y), (1 - x, 1 - y)]
    return x, y, c, 2 * x + y, (x, y, 1 - c), chips


def _remote(src, dst, send_sems, recv_sems, j, to):
    return pltpu.make_async_remote_copy(src_ref=src, dst_ref=dst, send_sem=send_sems.at[j], recv_sem=recv_sems.at[j],
                                        device_id=to, device_id_type=MESH)


def _dma_sems(n):
    return [pltpu.SemaphoreType.DMA((n,)), pltpu.SemaphoreType.DMA((n,))]


def _all_gather(shards, whole, name):
    n = len(shards)

    def body(*refs):
        srcs, outs, send_sems, recv_sems = refs[:n], refs[n:2 * n], refs[2 * n], refs[2 * n + 1]
        x, y, c, me, sibling, chips = _place()
        sent = []

        def start(cp):
            cp.start()
            sent.append(cp)

        def rows(w, h):
            hr = srcs[w].shape[0] // 2
            return pl.ds(h * hr, hr)

        gathered = [w for w in range(n) if whole[w]]
        for w in gathered:
            for j, (cx, cy) in enumerate(chips):
                start(_remote(srcs[w].at[rows(w, c)], outs[w].at[me, rows(w, c)], send_sems, recv_sems, 7 * w + j, (cx, cy, c)))
        for w in range(n):
            start(_remote(srcs[w], outs[w].at[me], send_sems, recv_sems, 7 * w + 6, sibling))
        for w in gathered:
            for j, (cx, cy) in enumerate(chips):
                blk = outs[w].at[2 * cx + cy, rows(w, c)]
                _remote(srcs[w].at[rows(w, c)], blk, send_sems, recv_sems, 7 * w + j, (cx, cy, c)).wait_recv()
                start(_remote(blk, blk, send_sems, recv_sems, 7 * w + 3 + j, sibling))
        for w in gathered:
            for j, (cx, cy) in enumerate(chips):
                blk = outs[w].at[2 * cx + cy, rows(w, 1 - c)]
                _remote(blk, blk, send_sems, recv_sems, 7 * w + 3 + j, sibling).wait_recv()
        for w in range(n):
            _remote(srcs[w], outs[w].at[me], send_sems, recv_sems, 7 * w + 6, sibling).wait_recv()
        for cp in sent:
            cp.wait_send()

    return pl.pallas_call(
        body, in_specs=[ANY] * n, out_specs=[ANY] * n,
        out_shape=[jax.ShapeDtypeStruct((N_CHIP,) + s.shape, s.dtype) for s in shards],
        scratch_shapes=_dma_sems(7 * n), name=name,
    )(*shards)


def _rs_swap(grads, name):
    n = len(grads)

    def body(*refs):
        gs, outs, send_sems, recv_sems = refs[:n], refs[n:2 * n], refs[2 * n], refs[2 * n + 1]
        x, y, c, me, sibling, chips = _place()
        cps = []
        for w in range(n):
            hr = gs[w].shape[1] // 2
            cps.append(_remote(gs[w].at[:, pl.ds((1 - c) * hr, hr)], outs[w], send_sems, recv_sems, w, sibling))
            cps[-1].start()
        for cp in cps:
            cp.wait()

    return pl.pallas_call(
        body, in_specs=[ANY] * n, out_specs=[ANY] * n,
        out_shape=[jax.ShapeDtypeStruct((g.shape[0], g.shape[1] // 2, g.shape[2]), g.dtype) for g in grads],
        scratch_shapes=_dma_sems(n), name=name,
    )(*grads)


def _sum_rows(hr, row_bytes):
    return _tile(hr, max(16, ROW_BUDGET // (2 * row_bytes) // 16 * 16), 16)


def _rs_pair_sum(g, got, c, name):
    S, r, cw = g.shape
    hr = r // 2
    tr = _sum_rows(hr, 3 * cw * 2)
    nrb = hr // tr

    def body(c_ref, a_ref, b_ref, o_ref):
        o_ref[...] = (a_ref[...].astype(F32) + b_ref[...].astype(F32)).astype(BF16)

    return pl.pallas_call(
        body,
        grid_spec=pltpu.PrefetchScalarGridSpec(
            num_scalar_prefetch=1, grid=(S, nrb),
            in_specs=[pl.BlockSpec((None, tr, cw), lambda k, i, c_ref: (k, c_ref[0] * nrb + i, 0)),
                      pl.BlockSpec((None, tr, cw), lambda k, i, c_ref: (k, i, 0))],
            out_specs=pl.BlockSpec((None, tr, cw), lambda k, i, c_ref: (k, i, 0)),
        ),
        out_shape=jax.ShapeDtypeStruct((S, hr, cw), BF16),
        compiler_params=_params(("parallel", "parallel")), name=name,
    )(c, g, got)


def _rs_chip_sum(pair, got, place, name):
    S, hr, cw = pair.shape
    tr = _sum_rows(hr, cw * (4 * 2 + 4))

    def body(p_ref, a_ref, z_ref, o_ref):
        o_ref[...] = a_ref[...].astype(F32) + z_ref[0].astype(F32) + z_ref[1].astype(F32) + z_ref[2].astype(F32)

    return pl.pallas_call(
        body,
        grid_spec=pltpu.PrefetchScalarGridSpec(
            num_scalar_prefetch=1, grid=(hr // tr,),
            in_specs=[pl.BlockSpec((None, tr, cw), lambda i, p_ref: (p_ref[0], i, 0)),
                      pl.BlockSpec((3, tr, cw), lambda i, p_ref: (0, i, 0))],
            out_specs=pl.BlockSpec((None, tr, cw), lambda i, p_ref: (p_ref[1], i, 0)),
        ),
        out_shape=jax.ShapeDtypeStruct((2, hr, cw), F32),
        compiler_params=_params(("parallel",)), name=name,
    )(place, pair, got)


def _rs_share(halves, name):
    n = len(halves)

    def body(*refs):
        outs, send_sems, recv_sems = refs[n:2 * n], refs[2 * n], refs[2 * n + 1]
        x, y, c, me, sibling, chips = _place()
        cps = []
        for w in range(n):
            cps.append(_remote(outs[w].at[c], outs[w].at[c], send_sems, recv_sems, w, sibling))
            cps[-1].start()
        for w in range(n):
            _remote(outs[w].at[1 - c], outs[w].at[1 - c], send_sems, recv_sems, w, sibling).wait_recv()
        for cp in cps:
            cp.wait_send()

    return pl.pallas_call(
        body, in_specs=[ANY] * n, out_specs=[ANY] * n,
        out_shape=[jax.ShapeDtypeStruct(h.shape, h.dtype) for h in halves],
        input_output_aliases={i: i for i in range(n)},
        scratch_shapes=_dma_sems(n), name=name,
    )(*halves)


def _rs_begin(names, grads, core, fractions, tag):
    slotted = [_to_slots(n, grads[n]) for n in names]
    got = _rs_swap(slotted, f"grads_sibling_swap_{tag}")
    pairs = [_rs_pair_sum(s, t, core.reshape(1), f"grads_pair_sum_{n}") for n, s, t in zip(names, slotted, got)]
    lands = [lax.empty((N_CHIP - 1,) + p.shape[1:], p.dtype) for p in pairs]
    return pairs, _Pipe("reduce", pairs, lands, fractions)


def _rs_end(names, pairs, pipe, place, shapes, tag):
    _pipe_flush(pipe, f"grads_chip_exchange_rest_{tag}")
    halves = [_rs_chip_sum(p, o, place, f"grads_chip_sum_{n}") for n, p, o in zip(names, pairs, pipe.lands)]
    both = _rs_share(halves, f"grads_sibling_share_{tag}")
    return {n: b.reshape(s) for n, b, s in zip(names, both, shapes)}


def _all_reduce_small(s, name):
    flips = [(dx, dy, dc) for dx in (0, 1) for dy in (0, 1) for dc in (0, 1) if (dx, dy, dc) != (0, 0, 0)]

    def body(s_ref, o_ref, buf, send_sems, recv_sems):
        x, y, c = (lax.axis_index(n) for n in COMM_AXES)
        me = 4 * x + 2 * y + c
        buf[me] = s_ref[...]
        peers = [((1 - x) if dx else x, (1 - y) if dy else y, (1 - c) if dc else c) for dx, dy, dc in flips]
        sent = [_remote(s_ref, buf.at[me], send_sems, recv_sems, j, p) for j, p in enumerate(peers)]
        for cp in sent:
            cp.start()
        for j, (px, py, pc) in enumerate(peers):
            _remote(s_ref, buf.at[4 * px + 2 * py + pc], send_sems, recv_sems, j, (px, py, pc)).wait_recv()
        for cp in sent:
            cp.wait_send()
        acc = buf[0]
        for d in range(1, N_DEV):
            acc = acc + buf[d]
        o_ref[...] = acc

    vmem = pl.BlockSpec(memory_space=pltpu.VMEM)
    return pl.pallas_call(
        body, in_specs=[vmem], out_specs=vmem, out_shape=jax.ShapeDtypeStruct(s.shape, F32),
        scratch_shapes=[pltpu.VMEM((N_DEV,) + s.shape, F32), pltpu.SemaphoreType.DMA((7,)), pltpu.SemaphoreType.DMA((7,))],
        name=name,
    )(s)


BIG = {
    "ffn1_w_gate": 1, "ffn1_w_up": 1, "ffn1_w_down": 0, "w_in": 1, "mla_w_q_up": 1, "mla_w_kv_up": 1,
    "w_branch_a": 0, "w_branch_b": 0, "w_out": 0, "xa_w_q": 0, "xa_w_k": 0, "xa_w_v": 0, "xa_w_o": 1,
    "ffn2_w_gate": 1, "ffn2_w_up": 1, "ffn2_w_down": 0,
}
WEIGHTS = [
    "hgrn_lb_logits", "ffn1_pre_g", "ffn1_w_gate", "ffn1_w_up", "ffn1_w_down", "ffn1_post_g", "mix_pre_g", "w_in",
    "hg_norm_g", "mla_q_norm_g", "mla_w_q_up", "mla_kv_norm_g", "mla_w_kv_up", "w_branch_a", "w_branch_b", "w_out",
    "mix_post_g", "xa_pre_g", "xa_mem_g", "xa_w_q", "xa_w_k", "xa_w_v", "xa_w_o", "xa_post_g", "ffn2_pre_g",
    "ffn2_w_gate", "ffn2_w_up", "ffn2_w_down", "ffn2_post_g",
]
SMALL = [n for n in WEIGHTS if n not in BIG]
SLOTTED = ("ffn1_w_gate", "ffn1_w_up", "ffn2_w_gate", "ffn2_w_up", "xa_w_o")
MIX_EARLY = ("w_out", "w_branch_a", "w_branch_b")


def _from_slots(name, g):
    S, r, cw = g.shape
    if BIG[name] == 0:
        return g.reshape(S * r, cw)
    return g if name in SLOTTED else g.transpose(1, 0, 2).reshape(r, S * cw)


def _to_slots(name, g):
    if g.ndim == 3:
        return g
    if BIG[name] == 0:
        return g.reshape(N_CHIP, g.shape[0] // N_CHIP, g.shape[1])
    return g.reshape(g.shape[0], N_CHIP, g.shape[1] // N_CHIP).transpose(1, 0, 2)


def _pack_small(vals, width):
    rows = [jnp.pad(v, ((0, 0), (0, width - v.shape[1]))) for v in vals]
    s = jnp.concatenate(rows, axis=0)
    return jnp.pad(s, ((0, -s.shape[0] % 8), (0, 0)))


def _unpack_small(s, shapes):
    out, o = [], 0
    for r, w in shapes:
        out.append(s[o:o + r, :w])
        o += r
    return out


def kernel(x, mem, positions, hgrn_lb_logits, ffn1_pre_g, ffn1_w_gate, ffn1_w_up, ffn1_w_down, ffn1_post_g, mix_pre_g, w_in, hg_norm_g, mla_q_norm_g, mla_w_q_up, mla_kv_norm_g, mla_w_kv_up, w_branch_a, w_branch_b, w_out, mix_post_g, xa_pre_g, xa_mem_g, xa_w_q, xa_w_k, xa_w_v, xa_w_o, xa_post_g, ffn2_pre_g, ffn2_w_gate, ffn2_w_up, ffn2_w_down, ffn2_post_g, loss_target, m_hgrn_lb_logits, m_ffn1_pre_g, m_ffn1_w_gate, m_ffn1_w_up, m_ffn1_w_down, m_ffn1_post_g, m_mix_pre_g, m_w_in, m_hg_norm_g, m_mla_q_norm_g, m_mla_w_q_up, m_mla_kv_norm_g, m_mla_w_kv_up, m_w_branch_a, m_w_branch_b, m_w_out, m_mix_post_g, m_xa_pre_g, m_xa_mem_g, m_xa_w_q, m_xa_w_k, m_xa_w_v, m_xa_w_o, m_xa_post_g, m_ffn2_pre_g, m_ffn2_w_gate, m_ffn2_w_up, m_ffn2_w_down, m_ffn2_post_g, v_hgrn_lb_logits, v_ffn1_pre_g, v_ffn1_w_gate, v_ffn1_w_up, v_ffn1_w_down, v_ffn1_post_g, v_mix_pre_g, v_w_in, v_hg_norm_g, v_mla_q_norm_g, v_mla_w_q_up, v_mla_kv_norm_g, v_mla_w_kv_up, v_w_branch_a, v_w_branch_b, v_w_out, v_mix_post_g, v_xa_pre_g, v_xa_mem_g, v_xa_w_q, v_xa_w_k, v_xa_w_v, v_xa_w_o, v_xa_post_g, v_ffn2_pre_g, v_ffn2_w_gate, v_ffn2_w_up, v_ffn2_w_down, v_ffn2_post_g):
    a = dict(locals())
    big = list(BIG)
    small = {n: a[n] for n in SMALL}
    core = lax.axis_index("c").astype(jnp.int32)
    place = jnp.stack([(2 * lax.axis_index("x") + lax.axis_index("y")).astype(jnp.int32), core])

    shards = {n: a[n][0].astype(BF16) for n in big}
    whole = [n in BLOCK_WEIGHTS["ffn1"] for n in big]
    lands = dict(zip(big, _all_gather([shards[n] for n in big], whole, "weights_all_gather")))
    late = BLOCK_WEIGHTS["xa"] + BLOCK_WEIGHTS["ffn2"]
    gather = lambda names, fr: _Pipe("gather", [shards[n] for n in names], [lands[n] for n in names], fr)
    pipe_mix, pipe_late = gather(BLOCK_WEIGHTS["mix"], (2, 10, 10, 6, 10, 3)), gather(late, (6, 2, 2, 18, 10))
    carry = {f"ffn1_{k}": pipe_mix for k in ("pre_norm", "gate", "up", "swiglu", "down", "post_norm")}
    carry.update({k: pipe_late for k in ("mix_in_hg", "mix_in_ga", "mix_in_gb", "hg_scan", "mla_attn")})

    def params_of(block):
        for first, names, pipe in (("mix", BLOCK_WEIGHTS["mix"], pipe_mix), ("xa", late, pipe_late)):
            if block == first:
                _pipe_flush(pipe, f"weights_gather_rest_{block}")
                lands.update(zip(names, pipe.lands))
        return _block_params(block, {n: _from_slots(n, lands[n]) for n in BLOCK_WEIGHTS[block]}, small)

    g_small, g_big, open_groups, held = {}, {}, {}, {}

    def begin(tag, names, fractions, carriers):
        pairs, pipe = _rs_begin(names, held, core, fractions, tag)
        open_groups[tag] = (names, pairs, pipe)
        carry.update({c: pipe for c in carriers})

    def end(tag):
        names, pairs, pipe = open_groups.pop(tag)
        g_big.update(_rs_end(names, pairs, pipe, place, [a[n].shape[1:] for n in names], tag))

    def on_grads(block, g):
        for n, v in _block_grads(block, g).items():
            (held if n in BIG else g_small)[n] = v
        if block == "ffn2":
            begin("ffn2", BLOCK_WEIGHTS["ffn2"], (1,), ("hg_scan_bwd",))
        elif block == "mix_early":
            begin("early", BLOCK_WEIGHTS["xa"] + MIX_EARLY, (1,), ("mla_attn_bwd",))
        elif block == "mix":
            end("ffn2")
            end("early")
            begin("mid", [n for n in BLOCK_WEIGHTS["mix"] if n not in MIX_EARLY], (2, 6, 6, 6, 6, 6),
                  [f"ffn1_{k}" for k in ("post_norm_bwd", "down_dw", "down_dx", "swiglu_bwd", "gate_dw", "up_dw")])
        elif block == "ffn1_dw":
            begin("ffn1", BLOCK_WEIGHTS["ffn1"], (8, 9, 3), [f"ffn1_{k}" for k in ("gate_dx", "up_dx", "pre_norm_bwd")])
        elif block == "ffn1":
            end("mid")
            end("ffn1")

    cos, sin = _rope_tables(positions[0])
    loss_part, grad_x = _local_step(x[0], mem[0], cos, sin, loss_target[0], params_of, carry, on_grads)

    small_shapes = [a[n].shape for n in SMALL]
    width = max(s[1] for s in small_shapes)
    gs = _all_reduce_small(_pack_small([g_small[n] for n in SMALL], width), "small_grads_all_reduce")

    out_g, out_d, out_m, out_v = {}, {}, {}, {}
    for n in big:
        res = _adamw(a[n][0], g_big[n], a["m_" + n][0], a["v_" + n][0], f"adamw_{n}")
        out_g[n], out_d[n], out_m[n], out_v[n] = (t.reshape(a[n].shape) for t in res)
    sw, sm, sv = (_pack_small([a[p + n] for n in SMALL], width) for p in ("", "m_", "v_"))
    for t, dst in zip(_adamw(sw, gs, sm, sv, "adamw_small"), (out_g, out_d, out_m, out_v)):
        dst.update(zip(SMALL, _unpack_small(t, small_shapes)))

    loss = lax.psum(loss_part, COMM_AXES)
    return (loss, grad_x[None], *[out_g[n] for n in WEIGHTS], *[out_d[n] for n in WEIGHTS],
            *[out_m[n] for n in WEIGHTS], *[out_v[n] for n in WEIGHTS])
```

```python
import functools

import jax
import jax.numpy as jnp
from jax import lax
from jax.experimental import pallas as pl
from jax.experimental.pallas import tpu as pltpu

F32 = jnp.float32
BF16 = jnp.bfloat16
EPS = 1e-6
HEAD = 128
ROPE = 64
CHUNK = 64
SUB = 16
HG_BLOCK = 256
ROPE_THETA = 10000.0
FFN_RESIDUAL_WEIGHT = 0.5
ADAM_LR, ADAM_B1, ADAM_B2, ADAM_EPS, ADAM_WD, ADAM_STEP = 0.001, 0.9, 0.999, 1e-08, 0.01, 10
VMEM_LIMIT = 56 * 2**20
ROW_BUDGET = 20 * 2**20
NEG = -1e30
MESH = pl.DeviceIdType.MESH
N_CHIP = 4
N_DEV = 8


def _params(sem):
    return pltpu.CompilerParams(dimension_semantics=sem, vmem_limit_bytes=VMEM_LIMIT)


def _tile(n, cap, mult):
    if n <= cap:
        return n
    t = (cap // mult) * mult
    while t >= mult:
        if n % t == 0:
            return t
        t -= mult
    raise ValueError(f"no tile for {n} under {cap}")


def _split_rows(n, fractions):
    if n % 16:
        return [(0, n)] + [(n, 0)] * (len(fractions) - 1)
    units, total, acc, cuts = n // 16, sum(fractions), 0, [0]
    for f in fractions[:-1]:
        acc += f
        cuts.append(round(units * acc / total))
    cuts.append(units)
    return [(16 * lo, 16 * (hi - lo)) for lo, hi in zip(cuts, cuts[1:])]


class _Pipe:
    def __init__(self, kind, srcs, lands, fractions):
        self.kind, self.srcs, self.lands = kind, list(srcs), list(lands)
        per_w = [_split_rows(s.shape[0] // 2 if kind == "gather" else s.shape[1], fractions) for s in srcs]
        self.parts = [[pw[i] for pw in per_w] for i in range(len(fractions))]
        self.taken = 0
        self.sems = (6 if kind == "gather" else 3) * len(srcs)

    def take(self):
        self.taken += 1
        return self.parts[self.taken - 1]

    def rest(self):
        left = self.parts[self.taken:]
        self.taken = len(self.parts)
        return [(left[0][w][0], sum(p[w][1] for p in left)) for w in range(len(self.srcs))] if left else None


def _pipe_copies(kind, rows, lands, srcs, send_sems, recv_sems):
    x, y, c, me, sibling, chips = _place()
    out = []
    for w, (r0, nr) in enumerate(rows):
        for j, (cx, cy) in enumerate(chips if nr else []):
            k, to = 2 * cx + cy, (cx, cy, c)
            if kind == "gather":
                rs = pl.ds(c * (srcs[w].shape[0] // 2) + r0, nr)
                src, dst, got, j0 = srcs[w].at[rs], lands[w].at[me, rs], lands[w].at[k, rs], 6 * w + j
            else:
                rs = pl.ds(r0, nr)
                src, dst, got, j0 = srcs[w].at[k, rs], lands[w].at[j, rs], lands[w].at[j, rs], 3 * w + j
            out.append((_remote(src, dst, send_sems, recv_sems, j0, to), _remote(src, got, send_sems, recv_sems, j0, to), w, j, k))
    return out


def _pipe_start(kind, rows, lands, srcs, send_sems, recv_sems):
    for send, _, _, _, _ in _pipe_copies(kind, rows, lands, srcs, send_sems, recv_sems):
        send.start()


def _pipe_finish(kind, rows, lands, srcs, send_sems, recv_sems):
    x, y, c, me, sibling, chips = _place()
    copies = _pipe_copies(kind, rows, lands, srcs, send_sems, recv_sems)
    passed = []
    for _, arrival, _, _, _ in copies:
        arrival.wait_recv()
    if kind == "gather":
        for w, (r0, nr) in enumerate(rows):
            hr = srcs[w].shape[0] // 2
            for j, (cx, cy) in enumerate(chips if nr else []):
                blk = lands[w].at[2 * cx + cy, pl.ds(c * hr + r0, nr)]
                passed.append(_remote(blk, blk, send_sems, recv_sems, 6 * w + 3 + j, sibling))
                passed[-1].start()
        for w, (r0, nr) in enumerate(rows):
            hr = srcs[w].shape[0] // 2
            for j, (cx, cy) in enumerate(chips if nr else []):
                blk = lands[w].at[2 * cx + cy, pl.ds((1 - c) * hr + r0, nr)]
                _remote(blk, blk, send_sems, recv_sems, 6 * w + 3 + j, sibling).wait_recv()
    for send, _, _, _, _ in copies:
        send.wait_send()
    for cp in passed:
        cp.wait_send()


def _carry_call(body, *, grid, in_specs, out_specs, out_shape, scratch_shapes, sem, name, args, pipe=None):
    single = not isinstance(out_shape, (list, tuple))
    if single:
        out_specs, out_shape = [out_specs], [out_shape]
    if pipe is None or pipe.taken >= len(pipe.parts):
        res = pl.pallas_call(body, grid=grid, in_specs=in_specs, out_specs=out_specs, out_shape=out_shape,
                             scratch_shapes=scratch_shapes, compiler_params=_params(sem), name=name)(*args)
        return res[0] if single else res
    rows, kind = pipe.take(), pipe.kind
    n_in, n_out, n_l, n_s, n_scr = len(args), len(out_shape), len(pipe.lands), len(pipe.srcs), len(scratch_shapes)

    def wrapped(*refs):
        ins, srcs = refs[:n_in], refs[n_in + n_l:n_in + n_l + n_s]
        o0 = n_in + n_l + n_s
        outs, lands = refs[o0:o0 + n_out], refs[o0 + n_out:o0 + n_out + n_l]
        scr = refs[o0 + n_out + n_l:o0 + n_out + n_l + n_scr]
        send_sems, recv_sems = refs[-2], refs[-1]
        ids = [pl.program_id(ax) for ax in range(len(grid))]
        first = functools.reduce(jnp.logical_and, [i == 0 for i in ids])
        last = functools.reduce(jnp.logical_and, [i == g - 1 for i, g in zip(ids, grid)])

        @pl.when(first)
        def _():
            _pipe_start(kind, rows, lands, srcs, send_sems, recv_sems)

        body(*ins, *outs, *scr)

        @pl.when(last)
        def _():
            _pipe_finish(kind, rows, lands, srcs, send_sems, recv_sems)

    res = pl.pallas_call(
        wrapped, grid=grid, in_specs=list(in_specs) + [ANY] * (n_l + n_s), out_specs=list(out_specs) + [ANY] * n_l,
        out_shape=list(out_shape) + [jax.ShapeDtypeStruct(l.shape, l.dtype) for l in pipe.lands],
        input_output_aliases={n_in + i: n_out + i for i in range(n_l)},
        scratch_shapes=list(scratch_shapes) + _dma_sems(pipe.sems),
        compiler_params=_params(("arbitrary",) * len(grid)), name=name,
    )(*args, *pipe.lands, *pipe.srcs)
    pipe.lands = list(res[n_out:])
    return res[0] if single else list(res[:n_out])


def _pipe_flush(pipe, name):
    rows = pipe.rest()
    if rows is None:
        return
    n_l, n_s, kind = len(pipe.lands), len(pipe.srcs), pipe.kind

    def body(*refs):
        srcs, lands = refs[n_l:n_l + n_s], refs[n_l + n_s:2 * n_l + n_s]
        _pipe_start(kind, rows, lands, srcs, refs[-2], refs[-1])
        _pipe_finish(kind, rows, lands, srcs, refs[-2], refs[-1])

    pipe.lands = list(pl.pallas_call(
        body, in_specs=[ANY] * (n_l + n_s), out_specs=[ANY] * n_l,
        out_shape=[jax.ShapeDtypeStruct(l.shape, l.dtype) for l in pipe.lands],
        input_output_aliases={i: i for i in range(n_l)}, scratch_shapes=_dma_sems(pipe.sems), name=name,
    )(*pipe.lands, *pipe.srcs))


def _sigmoid(x):
    return 1.0 / (1.0 + jnp.exp(-x))


def _dot(a, b, dims):
    return lax.dot_general(a, b, (dims, ((), ())), preferred_element_type=F32)


NN = ((1,), (0,))
NT = ((1,), (1,))
TN = ((0,), (0,))


def _mm(a, b, mode, out_dtype, name, add=None, out_slots=0, pipe=None, tm_cap=1024, tn_cap=512, tk_cap=2816):
    slot_cap = 1408
    b_slots = b.shape[0] if b.ndim == 3 else 0
    bs = (b.shape[1], b_slots * b.shape[2]) if b_slots else b.shape
    if mode == "nn":
        (M, K), (K2, N) = a.shape, bs
    elif mode == "nt":
        (M, K), (N, K2) = a.shape, bs
    else:
        (K, M), (K2, N) = a.shape, bs
    assert K == K2, (a.shape, b.shape, mode)
    tm = _tile(M, tm_cap, 128)
    tn = _tile(N // (b_slots or out_slots), slot_cap, 128) if (out_slots or (b_slots and mode == "nn")) else _tile(N, tn_cap, 128)
    tk = _tile(K // b_slots, slot_cap, 128) if (b_slots and mode == "nt") else _tile(K, tk_cap, 128)
    nk = K // tk
    a_spec = pl.BlockSpec((tk, tm), lambda i, j, k: (k, i)) if mode == "tn" else pl.BlockSpec((tm, tk), lambda i, j, k: (i, k))
    if b_slots and mode == "nn":
        per = b.shape[2] // tn
        b_spec = pl.BlockSpec((None, tk, tn), lambda i, j, k: (j // per, k, j % per))
    elif b_slots:
        per = b.shape[2] // tk
        b_spec = pl.BlockSpec((None, tn, tk), lambda i, j, k: (k // per, j, k % per))
    else:
        b_spec = pl.BlockSpec((tn, tk), lambda i, j, k: (j, k)) if mode == "nt" else pl.BlockSpec((tk, tn), lambda i, j, k: (k, j))
    if out_slots:
        per_o = N // out_slots // tn
        o_spec = pl.BlockSpec((None, tm, tn), lambda i, j, k: (j // per_o, i, j % per_o))
        o_shape = (out_slots, M, N // out_slots)
    else:
        o_spec = pl.BlockSpec((tm, tn), lambda i, j, k: (i, j))
        o_shape = (M, N)
    dims = {"nn": NN, "nt": NT, "tn": TN}[mode]
    has_add = add is not None

    def body(*refs):
        a_ref, b_ref = refs[0], refs[1]
        add_ref = refs[2] if has_add else None
        o_ref = refs[3] if has_add else refs[2]
        p = _dot(a_ref[...].astype(BF16), b_ref[...].astype(BF16), dims)

        def finish(val):
            if has_add:
                val = val + add_ref[...]
            o_ref[...] = val.astype(out_dtype)

        if nk == 1:
            finish(p)
        else:
            acc_ref = refs[-1]
            k = pl.program_id(2)

            @pl.when(k == 0)
            def _():
                acc_ref[...] = p

            @pl.when(k > 0)
            def _():
                acc_ref[...] += p

            @pl.when(k == nk - 1)
            def _():
                finish(acc_ref[...])

    assert not (has_add and out_slots)
    in_specs = [a_spec, b_spec] + ([o_spec] if has_add else [])
    args = (a, b) + ((add,) if has_add else ())
    return _carry_call(
        body, grid=(M // tm, N // tn, nk), in_specs=in_specs, out_specs=o_spec,
        out_shape=jax.ShapeDtypeStruct(o_shape, out_dtype),
        scratch_shapes=[pltpu.VMEM((tm, tn), F32)] if nk > 1 else [],
        sem=("parallel", "parallel", "arbitrary"), name=name, args=args, pipe=pipe,
    )


def _rows(body, ins, outs, name, pipe=None):
    T = next(a.shape[0] for a, k in ins if k == "row")
    per_row = sum(a.shape[1] * a.dtype.itemsize for a, k in ins if k == "row")
    per_row += sum(s[1] * jnp.dtype(d).itemsize for s, d, k in outs if k == "row")
    tr = next(t for t in range(512, 0, -8) if T % t == 0 and 2 * t * per_row <= ROW_BUDGET)
    in_specs = [
        pl.BlockSpec((tr, a.shape[1]), lambda i: (i, 0)) if k == "row" else pl.BlockSpec(a.shape, lambda i: (0, 0))
        for a, k in ins
    ]
    out_specs = [
        pl.BlockSpec((tr, s[1]), lambda i: (i, 0)) if k == "row" else pl.BlockSpec(s, lambda i: (0, 0))
        for s, d, k in outs
    ]
    has_acc = any(k == "acc" for _, _, k in outs)
    return _carry_call(
        body, grid=(T // tr,), in_specs=in_specs, out_specs=out_specs,
        out_shape=[jax.ShapeDtypeStruct(s, d) for s, d, k in outs], scratch_shapes=[],
        sem=("arbitrary",) if has_acc else ("parallel",), name=name, args=[a for a, _ in ins], pipe=pipe,
    )


def _rstd(x):
    return lax.rsqrt(jnp.mean(x * x, axis=-1, keepdims=True) + EPS)


def _norm_fwd(x, g, name, pipe=None):
    def body(x_ref, g_ref, o_ref):
        x = x_ref[...]
        o_ref[...] = (x * _rstd(x) * g_ref[...]).astype(BF16)

    return _rows(body, [(x, "row"), (g, "vec")], [(x.shape, BF16, "row")], name, pipe=pipe)[0]


def _postnorm_fwd(x, y, g, weight, name, pipe=None):
    def body(x_ref, y_ref, g_ref, o_ref):
        y = y_ref[...]
        o_ref[...] = x_ref[...] + weight * (y * _rstd(y) * g_ref[...])

    return _rows(body, [(x, "row"), (y, "row"), (g, "vec")], [(x.shape, F32, "row")], name, pipe=pipe)[0]


def _norm_bwd(x, g, dy, weight, out_dtype, name, res=None, pipe=None):
    has_res = res is not None

    def body(*refs):
        x_ref, g_ref, dy_ref = refs[:3]
        res_ref = refs[3] if has_res else None
        dx_ref, dg_ref = refs[-2], refs[-1]

        @pl.when(pl.program_id(0) == 0)
        def _():
            dg_ref[...] = jnp.zeros_like(dg_ref)

        x = x_ref[...]
        dn = dy_ref[...].astype(F32) * weight
        r = _rstd(x)
        xhat = x * r
        dg_ref[...] += jnp.sum(dn * xhat, axis=0, keepdims=True)
        dxh = dn * g_ref[...]
        dx = r * (dxh - xhat * jnp.mean(dxh * xhat, axis=-1, keepdims=True))
        if has_res:
            dx = dx + res_ref[...]
        dx_ref[...] = dx.astype(out_dtype)

    ins = [(x, "row"), (g, "vec"), (dy, "row")] + ([(res, "row")] if has_res else [])
    return _rows(body, ins, [(x.shape, out_dtype, "row"), (g.shape, F32, "acc")], name, pipe=pipe)


def _swiglu_fwd(a, b, name, pipe=None):
    def body(a_ref, b_ref, o_ref):
        a = a_ref[...]
        o_ref[...] = (a * _sigmoid(a) * b_ref[...]).astype(BF16)

    return _rows(body, [(a, "row"), (b, "row")], [(a.shape, BF16, "row")], name, pipe=pipe)[0]


def _swiglu_bwd(a, b, ds, name, pipe=None):
    def body(a_ref, b_ref, ds_ref, da_ref, db_ref):
        a, ds = a_ref[...], ds_ref[...]
        sg = _sigmoid(a)
        da_ref[...] = (ds * b_ref[...] * (sg * (1.0 + a * (1.0 - sg)))).astype(BF16)
        db_ref[...] = (ds * (a * sg)).astype(BF16)

    return _rows(body, [(a, "row"), (b, "row"), (ds, "row")], [(a.shape, BF16, "row"), (a.shape, BF16, "row")], name, pipe=pipe)


def _merge_fwd(ga, gb, ya, yb, name):
    def body(ga_ref, gb_ref, ya_ref, yb_ref, o_ref):
        o_ref[...] = (_sigmoid(ga_ref[...]) * ya_ref[...] + _sigmoid(gb_ref[...]) * yb_ref[...]).astype(BF16)

    return _rows(body, [(ga, "row"), (gb, "row"), (ya, "row"), (yb, "row")], [(ga.shape, BF16, "row")], name)[0]


def _merge_bwd(ga, gb, ya, yb, dy, name):
    def body(ga_ref, gb_ref, ya_ref, yb_ref, dy_ref, dya_ref, dyb_ref, dga_ref, dgb_ref):
        dy = dy_ref[...]
        sa, sb = _sigmoid(ga_ref[...]), _sigmoid(gb_ref[...])
        dya_ref[...] = (dy * sa).astype(BF16)
        dyb_ref[...] = (dy * sb).astype(BF16)
        dga_ref[...] = (dy * ya_ref[...] * (sa * (1.0 - sa))).astype(BF16)
        dgb_ref[...] = (dy * yb_ref[...] * (sb * (1.0 - sb))).astype(BF16)

    ins = [(ga, "row"), (gb, "row"), (ya, "row"), (yb, "row"), (dy, "row")]
    return _rows(body, ins, [(ga.shape, BF16, "row")] * 4, name)


def _loss_head(y, target, name):
    D = y.shape[1]

    def body(y_ref, t_ref, dy_ref, acc_ref):
        @pl.when(pl.program_id(0) == 0)
        def _():
            acc_ref[...] = jnp.zeros_like(acc_ref)

        err = y_ref[...] - t_ref[...]
        dy_ref[...] = err * (1.0 / D)
        acc_ref[...] += jnp.sum(err * err, axis=0, keepdims=True)

    return _rows(body, [(y, "row"), (target, "row")], [(y.shape, F32, "row"), ((1, D), F32, "acc")], name)


def _rot(x):
    lane = lax.broadcasted_iota(jnp.int32, x.shape, 1)
    return jnp.where((lane % ROPE) < ROPE // 2, -pltpu.roll(x, 128 - ROPE // 2, 1), pltpu.roll(x, ROPE // 2, 1))


def _rope_q_fwd(qpe, cos, sin, name):
    T, W = qpe.shape
    tr = min(T, 512)
    blk = pl.BlockSpec((tr, 128), lambda i, j: (i, j))
    tab = pl.BlockSpec((tr, 128), lambda i, j: (i, 0))

    def body(x_ref, c_ref, s_ref, o_ref):
        x = x_ref[...]
        o_ref[...] = (x * c_ref[...] + _rot(x) * s_ref[...]).astype(BF16)

    return pl.pallas_call(
        body, grid=(T // tr, W // 128), in_specs=[blk, tab, tab], out_specs=blk,
        out_shape=jax.ShapeDtypeStruct((T, W), BF16), compiler_params=_params(("parallel", "parallel")), name=name,
    )(qpe, cos, sin)


def _rope_q_bwd(dq_heads, cos, sin, name):
    T, W = dq_heads.shape
    tr = min(T, 512)
    even = pl.BlockSpec((tr, 128), lambda i, j: (i, 2 * j))
    odd = pl.BlockSpec((tr, 128), lambda i, j: (i, 2 * j + 1))
    tab = pl.BlockSpec((tr, 128), lambda i, j: (i, 0))

    def body(a_ref, b_ref, c_ref, s_ref, o_ref):
        d = a_ref[...] + b_ref[...]
        o_ref[...] = (d * c_ref[...] - _rot(d * s_ref[...])).astype(BF16)

    return pl.pallas_call(
        body, grid=(T // tr, W // 256), in_specs=[even, odd, tab, tab],
        out_specs=pl.BlockSpec((tr, 128), lambda i, j: (i, j)),
        out_shape=jax.ShapeDtypeStruct((T, W // 2), BF16), compiler_params=_params(("parallel", "parallel")), name=name,
    )(dq_heads, dq_heads, cos, sin)


def _rope_k_fwd(kpe, cos, sin, name):
    def body(x_ref, c_ref, s_ref, o_ref):
        x = x_ref[...]
        y = x * c_ref[...] + _rot(x) * s_ref[...]
        o_ref[...] = (y + pltpu.roll(y, ROPE, 1)).astype(BF16)

    return _rows(body, [(kpe, "row"), (cos, "row"), (sin, "row")], [(kpe.shape, BF16, "row")], name)[0]


def _rope_k_bwd(dk_heads, cos, sin, name):
    T, W = dk_heads.shape

    def body(d_ref, c_ref, s_ref, o_ref):
        d = d_ref[:, 0:128]
        for h in range(1, W // 128):
            d = d + d_ref[:, h * 128:(h + 1) * 128]
        d = d + pltpu.roll(d, ROPE, 1)
        dx = d * c_ref[...] - _rot(d * s_ref[...])
        lane = lax.broadcasted_iota(jnp.int32, dx.shape, 1)
        o_ref[...] = jnp.where(lane < ROPE, dx, 0.0).astype(BF16)

    return _rows(body, [(dk_heads, "row"), (cos, "row"), (sin, "row")], [((T, 128), BF16, "row")], name)[0]


def _attn_probs(q, k, qpe, kpe, scale, causal, q0):
    s = _dot(q, k, NT)
    if qpe is not None:
        s = s + _dot(qpe, kpe, NT)
    s = s * scale
    if causal:
        row = q0 + lax.broadcasted_iota(jnp.int32, s.shape, 0)
        col = lax.broadcasted_iota(jnp.int32, s.shape, 1)
        s = jnp.where((col // CHUNK) <= (row // CHUNK), s, NEG)
    p = jnp.exp(s - jnp.max(s, axis=-1, keepdims=True))
    return p / jnp.sum(p, axis=-1, keepdims=True)


def _pe_mask(x, h):
    lane = lax.broadcasted_iota(jnp.int32, x.shape, 1)
    return jnp.where((lane // ROPE) == (h % 2), x, jnp.zeros_like(x))


def _attn_fwd(q, k, v, scale, name, qpe=None, kpe=None, causal=False, pipe=None):
    T, W = q.shape
    Tk = k.shape[0]
    H = W // HEAD
    tq = min(T, 256)
    nq = T // tq
    has_pe = qpe is not None
    qs = pl.BlockSpec((tq, HEAD), lambda h, i: (i, h))
    ks = pl.BlockSpec((Tk, HEAD), lambda h, i: (0, h))
    in_specs, args = [qs, ks, ks], [q, k, v]
    if has_pe:
        in_specs += [pl.BlockSpec((tq, HEAD), lambda h, i: (i, h // 2)), pl.BlockSpec((Tk, HEAD), lambda h, i: (0, 0))]
        args += [qpe, kpe]

    def body(*refs):
        q_ref, k_ref, v_ref = refs[:3]
        o_ref = refs[-1]
        h, i = pl.program_id(0), pl.program_id(1)

        def compute(klen):
            qp = _pe_mask(refs[3][...], h) if has_pe else None
            kp = refs[4][0:klen, :] if has_pe else None
            p = _attn_probs(q_ref[...], k_ref[0:klen, :], qp, kp, scale, causal, i * tq)
            o_ref[...] = _dot(p.astype(BF16), v_ref[0:klen, :], NN).astype(BF16)

        if causal:
            for qi in range(nq):
                pl.when(i == qi)(functools.partial(compute, (qi + 1) * tq))
        else:
            compute(Tk)

    return _carry_call(
        body, grid=(H, nq), in_specs=in_specs, out_specs=qs, out_shape=jax.ShapeDtypeStruct((T, W), BF16),
        scratch_shapes=[], sem=("parallel", "parallel"), name=name, args=args, pipe=pipe,
    )


def _attn_bwd(q, k, v, do, scale, name, qpe=None, kpe=None, causal=False, pipe=None):
    T, W = q.shape
    Tk = k.shape[0]
    H = W // HEAD
    tq = min(T, 256)
    nq = T // tq
    has_pe = qpe is not None
    qs = pl.BlockSpec((tq, HEAD), lambda h, i: (i, h))
    ks = pl.BlockSpec((Tk, HEAD), lambda h, i: (0, h))
    in_specs, args = [qs, ks, ks, qs], [q, k, v, do]
    out_specs = [qs, ks, ks]
    out_shape = [jax.ShapeDtypeStruct((T, W), BF16), jax.ShapeDtypeStruct((Tk, W), BF16), jax.ShapeDtypeStruct((Tk, W), BF16)]
    scratch = [pltpu.VMEM((Tk, HEAD), F32), pltpu.VMEM((Tk, HEAD), F32)]
    if has_pe:
        in_specs += [pl.BlockSpec((tq, HEAD), lambda h, i: (i, h // 2)), pl.BlockSpec((Tk, HEAD), lambda h, i: (0, 0))]
        args += [qpe, kpe]
        out_specs += [qs, ks]
        out_shape += [jax.ShapeDtypeStruct((T, W), F32), jax.ShapeDtypeStruct((Tk, W), F32)]
        scratch += [pltpu.VMEM((Tk, HEAD), F32)]
    n_in = len(in_specs)

    def body(*refs):
        q_ref, k_ref, v_ref, do_ref = refs[:4]
        outs = refs[n_in:n_in + len(out_specs)]
        accs = refs[n_in + len(out_specs):]
        dq_ref, dk_ref, dv_ref = outs[:3]
        h, i = pl.program_id(0), pl.program_id(1)

        @pl.when(i == 0)
        def _():
            for acc in accs:
                acc[...] = jnp.zeros_like(acc)

        def compute(klen):
            qp = _pe_mask(refs[4][...], h) if has_pe else None
            kp = refs[5][0:klen, :] if has_pe else None
            qv, kv, vv, dov = q_ref[...], k_ref[0:klen, :], v_ref[0:klen, :], do_ref[...]
            p = _attn_probs(qv, kv, qp, kp, scale, causal, i * tq)
            dp = _dot(dov, vv, NT)
            ds = (p * (dp - jnp.sum(p * dp, axis=-1, keepdims=True)) * scale).astype(BF16)
            dq_ref[...] = _dot(ds, kv, NN).astype(BF16)
            accs[0][0:klen, :] += _dot(ds, qv, TN)
            accs[1][0:klen, :] += _dot(p.astype(BF16), dov, TN)
            if has_pe:
                outs[3][...] = _pe_mask(_dot(ds, kp, NN), h)
                accs[2][0:klen, :] += _dot(ds, qp, TN)

        if causal:
            for qi in range(nq):
                pl.when(i == qi)(functools.partial(compute, (qi + 1) * tq))
        else:
            compute(Tk)

        @pl.when(i == nq - 1)
        def _():
            dk_ref[...] = accs[0][...].astype(BF16)
            dv_ref[...] = accs[1][...].astype(BF16)
            if has_pe:
                outs[4][...] = accs[2][...]

    return _carry_call(
        body, grid=(H, nq), in_specs=in_specs, out_specs=out_specs, out_shape=out_shape, scratch_shapes=scratch,
        sem=("parallel", "arbitrary"), name=name, args=args, pipe=pipe,
    )


def _split3(x):
    hi = x.astype(BF16)
    r1 = x - hi.astype(F32)
    mid = r1.astype(BF16)
    lo = (r1 - mid.astype(F32)).astype(BF16)
    return hi, mid, lo


def _tri_dot(tri, x):
    hi, mid, lo = _split3(x)
    return _dot(tri, hi, NN) + _dot(tri, mid, NN) + _dot(tri, lo, NN)


def _hg_gates(u, lb):
    q, fr, v = u[:, 0:HEAD], u[:, HEAD:2 * HEAD], u[:, 2 * HEAD:3 * HEAD]
    sg = 1.0 / (1.0 + jnp.exp(-fr))
    sgm = 1.0 / (1.0 + jnp.exp(fr))
    f = lb + (1.0 - lb) * sg
    kin = (1.0 - lb) * sgm
    sq = _sigmoid(q)
    return q, v, sg, sgm, f, kin, sq, q * sq


def _hg_block_mats(blk):
    t = lax.broadcasted_iota(jnp.int32, (blk, blk), 0)
    s = lax.broadcasted_iota(jnp.int32, (blk, blk), 1)
    same = (t // SUB) == (s // SUB)
    one = lambda m: jnp.where(m, 1.0, 0.0).astype(BF16)
    return one(same & (s <= t)), one(same), one(same & (s >= t))


def _hg_stage(pairs, blk):
    for sc, val in pairs:
        sc[0:SUB, :] = jnp.zeros((SUB, HEAD), F32)
        sc[SUB:SUB + blk, :] = val


def _hg_scan_fwd(u, lb, name, pipe=None):
    T, W = u.shape
    H = W // (4 * HEAD)
    blk = min(T, HG_BLOCK)
    nb, nsb = T // blk, blk // SUB

    def body(u_ref, lb_ref, o_ref, st_ref, state, k_sc, b_sc, v_sc):
        @pl.when(pl.program_id(1) == 0)
        def _():
            state[...] = jnp.zeros_like(state)

        q, v, sg, sgm, f, kin, sq, qin = _hg_gates(u_ref[...], lb_ref[...])
        tri, ones, _ = _hg_block_mats(blk)
        logf = jnp.log(f)
        brel = _tri_dot(tri, logf)
        btot = _tri_dot(ones, logf)
        _hg_stage(((k_sc, kin), (b_sc, brel), (v_sc, v)), blk)
        sub_row = lax.broadcasted_iota(jnp.int32, (blk, HEAD), 0) % SUB
        o = jnp.zeros((blk, HEAD), F32)
        for d in range(SUB):
            win = slice(SUB - d, SUB - d + blk)
            e = jnp.exp(jnp.where(sub_row >= d, brel - b_sc[win, :], NEG))
            o = o + jnp.sum(qin * e * k_sc[win, :], axis=-1, keepdims=True) * v_sc[win, :]
        ab = (qin * jnp.exp(brel)).astype(BF16)
        kdb = (kin * jnp.exp(btot - brel)).astype(BF16)
        vb = v.astype(BF16)
        ebt = jnp.exp(btot)
        st = state[...]
        st_ref[...] = st
        for i in range(nsb):
            sl = slice(i * SUB, (i + 1) * SUB)
            o_ref[sl, :] = o[sl] + _dot(ab[sl], st.astype(BF16), NT)
            st = ebt[i * SUB:i * SUB + 1, :] * st + _dot(vb[sl], kdb[sl], TN)
        state[...] = st

    return _carry_call(
        body, grid=(H, nb),
        in_specs=[pl.BlockSpec((blk, 4 * HEAD), lambda h, c: (c, h)), pl.BlockSpec((1, HEAD), lambda h, c: (0, h))],
        out_specs=[pl.BlockSpec((blk, HEAD), lambda h, c: (c, h)), pl.BlockSpec((None, None, HEAD, HEAD), lambda h, c: (h, c, 0, 0))],
        out_shape=[jax.ShapeDtypeStruct((T, H * HEAD), F32), jax.ShapeDtypeStruct((H, nb, HEAD, HEAD), F32)],
        scratch_shapes=[pltpu.VMEM((HEAD, HEAD), F32)] + [pltpu.VMEM((SUB + blk, HEAD), F32)] * 3,
        sem=("parallel", "arbitrary"), name=name, args=(u, lb), pipe=pipe,
    )


def _hg_scan_bwd(u, lb, do, dog, states, name, pipe=None):
    T, W = u.shape
    H = W // (4 * HEAD)
    blk = min(T, HG_BLOCK)
    NC, nsb = T // blk, blk // SUB

    def body(u_ref, lb_ref, do_ref, dog_ref, st_ref, du_ref, dlb_ref, dstate, s_all, k_sc, b_sc, v_sc, dk_sc, dbn_sc,
             dv_sc, da_sc, dkd_sc, dvs_sc, dbt_sc):
        @pl.when(pl.program_id(1) == 0)
        def _():
            dstate[...] = jnp.zeros_like(dstate)
            dlb_ref[...] = jnp.zeros_like(dlb_ref)

        lb = lb_ref[...]
        q, v, sg, sgm, f, kin, sq, qin = _hg_gates(u_ref[...], lb)
        tri, ones, tri_t = _hg_block_mats(blk)
        logf = jnp.log(f)
        brel = _tri_dot(tri, logf)
        btot = _tri_dot(ones, logf)
        eb, ekd, ebt = jnp.exp(brel), jnp.exp(btot - brel), jnp.exp(btot)
        a, kd = qin * eb, kin * ekd
        ab, kdb, vb = a.astype(BF16), kd.astype(BF16), v.astype(BF16)
        do = do_ref[...]
        dob = do.astype(BF16)
        st = st_ref[...]
        for i in range(nsb):
            sl = slice(i * SUB, (i + 1) * SUB)
            s_all[i] = st
            st = ebt[i * SUB:i * SUB + 1, :] * st + _dot(vb[sl], kdb[sl], TN)
        ds = dstate[...]
        for i in reversed(range(nsb)):
            sl = slice(i * SUB, (i + 1) * SUB)
            st_i = s_all[i]
            dsb = ds.astype(BF16)
            e_i = ebt[i * SUB:i * SUB + 1, :]
            da_sc[sl, :] = _dot(dob[sl], st_i.astype(BF16), NN)
            dvs_sc[sl, :] = _dot(kdb[sl], dsb, NT)
            dkd_sc[sl, :] = _dot(vb[sl], dsb, NN)
            dbt_sc[sl, :] = jnp.broadcast_to(jnp.sum(ds * st_i, axis=0, keepdims=True) * e_i, (SUB, HEAD))
            ds = e_i * ds + _dot(dob[sl], ab[sl], TN)
        dstate[...] = ds
        da, dkd = da_sc[...], dkd_sc[...]
        t1 = dkd * kd
        dqin = da * eb
        dbrel = da * a - t1
        dkin = dkd * ekd
        dbtot = dbt_sc[...] + _tri_dot(ones, t1)
        _hg_stage(((k_sc, kin), (b_sc, brel), (v_sc, v)), blk)
        for sc in (dk_sc, dbn_sc, dv_sc):
            sc[...] = jnp.zeros_like(sc)
        sub_row = lax.broadcasted_iota(jnp.int32, (blk, HEAD), 0) % SUB
        for d in range(SUB):
            win = slice(SUB - d, SUB - d + blk)
            ks = k_sc[win, :]
            e = jnp.exp(jnp.where(sub_row >= d, brel - b_sc[win, :], NEG))
            qe = qin * e
            col = jnp.sum(qe * ks, axis=-1, keepdims=True)
            dcol = jnp.sum(do * v_sc[win, :], axis=-1, keepdims=True)
            dqe = dcol * qe
            g = dqe * ks
            dqin = dqin + dcol * (e * ks)
            dbrel = dbrel + g
            dk_sc[win, :] += dqe
            dbn_sc[win, :] += g
            dv_sc[win, :] += col * do
        dkin = dkin + dk_sc[SUB:SUB + blk, :]
        dbrel = dbrel - dbn_sc[SUB:SUB + blk, :]
        dv = dvs_sc[...] + dv_sc[SUB:SUB + blk, :]
        dlogf = _tri_dot(tri_t, dbrel) + dbtot
        diff = dlogf / f - dkin
        dlb_ref[...] += jnp.sum(sgm * diff, axis=0, keepdims=True)
        du_ref[:, 0:HEAD] = (dqin * (sq * (1.0 + q * (1.0 - sq)))).astype(BF16)
        du_ref[:, HEAD:2 * HEAD] = ((1.0 - lb) * sg * sgm * diff).astype(BF16)
        du_ref[:, 2 * HEAD:3 * HEAD] = dv.astype(BF16)
        du_ref[:, 3 * HEAD:4 * HEAD] = dog_ref[...]

    rev = lambda h, c: (NC - 1 - c, h)
    return _carry_call(
        body, grid=(H, NC),
        in_specs=[
            pl.BlockSpec((blk, 4 * HEAD), rev), pl.BlockSpec((1, HEAD), lambda h, c: (0, h)),
            pl.BlockSpec((blk, HEAD), rev), pl.BlockSpec((blk, HEAD), rev),
            pl.BlockSpec((None, None, HEAD, HEAD), lambda h, c: (h, NC - 1 - c, 0, 0)),
        ],
        out_specs=[pl.BlockSpec((blk, 4 * HEAD), rev), pl.BlockSpec((1, HEAD), lambda h, c: (0, h))],
        out_shape=[jax.ShapeDtypeStruct((T, W), BF16), jax.ShapeDtypeStruct((1, H * HEAD), F32)],
        scratch_shapes=[pltpu.VMEM((HEAD, HEAD), F32), pltpu.VMEM((nsb, HEAD, HEAD), F32)]
        + [pltpu.VMEM((SUB + blk, HEAD), F32)] * 6 + [pltpu.VMEM((blk, HEAD), F32)] * 4,
        sem=("parallel", "arbitrary"), name=name, args=(u, lb, do, dog, states), pipe=pipe,
    )


def _hg_tail_fwd(o_raw, u, g, name):
    T, D = o_raw.shape
    H = D // HEAD
    tr = min(T, 512)
    blk = pl.BlockSpec((tr, HEAD), lambda h, i: (i, h))

    def body(o_ref, og_ref, g_ref, out_ref):
        o, og = o_ref[...], og_ref[...]
        out_ref[...] = (o * _rstd(o) * g_ref[...] * (og * _sigmoid(og))).astype(BF16)

    return pl.pallas_call(
        body, grid=(H, T // tr),
        in_specs=[blk, pl.BlockSpec((tr, HEAD), lambda h, i: (i, 4 * h + 3)), pl.BlockSpec((1, HEAD), lambda h, i: (0, h))],
        out_specs=blk, out_shape=jax.ShapeDtypeStruct((T, D), BF16),
        compiler_params=_params(("parallel", "parallel")), name=name,
    )(o_raw, u, g)


def _hg_tail_bwd(o_raw, u, g, doa, name):
    T, D = o_raw.shape
    H = D // HEAD
    tr = min(T, 512)
    blk = pl.BlockSpec((tr, HEAD), lambda h, i: (i, h))
    vec = pl.BlockSpec((1, HEAD), lambda h, i: (0, h))

    def body(o_ref, og_ref, g_ref, doa_ref, do_ref, dog_ref, dg_ref):
        @pl.when(pl.program_id(1) == 0)
        def _():
            dg_ref[...] = jnp.zeros_like(dg_ref)

        o, og, doa, g = o_ref[...], og_ref[...], doa_ref[...], g_ref[...]
        sg = _sigmoid(og)
        r = _rstd(o)
        xhat = o * r
        dog_ref[...] = (doa * (xhat * g) * (sg * (1.0 + og * (1.0 - sg)))).astype(BF16)
        dn = doa * (og * sg)
        dg_ref[...] += jnp.sum(dn * xhat, axis=0, keepdims=True)
        dxh = dn * g
        do_ref[...] = r * (dxh - xhat * jnp.mean(dxh * xhat, axis=-1, keepdims=True))

    return pl.pallas_call(
        body, grid=(H, T // tr),
        in_specs=[blk, pl.BlockSpec((tr, HEAD), lambda h, i: (i, 4 * h + 3)), vec, blk],
        out_specs=[blk, blk, vec],
        out_shape=[jax.ShapeDtypeStruct((T, D), F32), jax.ShapeDtypeStruct((T, D), BF16), jax.ShapeDtypeStruct((1, D), F32)],
        compiler_params=_params(("parallel", "arbitrary")), name=name,
    )(o_raw, u, g, doa)


def _lb_fwd(logits, name):
    def body(l_ref, o_ref):
        l0, l1 = l_ref[0:1, :], l_ref[1:2, :]
        m = jnp.maximum(l0, l1)
        e0, e1 = jnp.exp(l0 - m), jnp.exp(l1 - m)
        o_ref[...] = e0 / (e0 + e1)

    D = logits.shape[1]
    return pl.pallas_call(body, out_shape=jax.ShapeDtypeStruct((1, D), F32), name=name)(logits)


def _lb_bwd(lb, dlb, name):
    def body(lb_ref, d_ref, o_ref):
        lb = lb_ref[...]
        d0 = d_ref[...] * lb * (1.0 - lb)
        o_ref[0:1, :] = d0
        o_ref[1:2, :] = -d0

    D = lb.shape[1]
    return pl.pallas_call(body, out_shape=jax.ShapeDtypeStruct((2, D), F32), name=name)(lb, dlb)


def _slots(w):
    return w.shape[0] if w.ndim == 3 else 0


def _ffn_fwd(x, p, tag, carry):
    mm = lambda a, b, mode, dt, name, **kw: _mm(a, b, mode, dt, name, pipe=carry.get(name), **kw)
    hb = _norm_fwd(x, p["pre_g"], f"{tag}_pre_norm", pipe=carry.get(f"{tag}_pre_norm"))
    a = mm(hb, p["w_gate"], "nn", F32, f"{tag}_gate")
    b = mm(hb, p["w_up"], "nn", F32, f"{tag}_up")
    sb = _swiglu_fwd(a, b, f"{tag}_swiglu", pipe=carry.get(f"{tag}_swiglu"))
    y = mm(sb, p["w_down"], "nn", F32, f"{tag}_down")
    xo = _postnorm_fwd(x, y, p["post_g"], FFN_RESIDUAL_WEIGHT, f"{tag}_post_norm", pipe=carry.get(f"{tag}_post_norm"))
    return xo, (x, hb, a, b, sb, y)


def _ffn_bwd(dxo, p, saved, tag, carry, on_dw):
    mm = lambda a, b, mode, dt, name, **kw: _mm(a, b, mode, dt, name, pipe=carry.get(name), **kw)
    x, hb, a, b, sb, y = saved
    dyb, dpost = _norm_bwd(y, p["post_g"], dxo, FFN_RESIDUAL_WEIGHT, BF16, f"{tag}_post_norm_bwd",
                           pipe=carry.get(f"{tag}_post_norm_bwd"))
    dw_down = mm(sb, dyb, "tn", BF16, f"{tag}_down_dw")
    on_dw("down", {"w_down": dw_down})
    ds = mm(dyb, p["w_down"], "nt", F32, f"{tag}_down_dx")
    dab, dbb = _swiglu_bwd(a, b, ds, f"{tag}_swiglu_bwd", pipe=carry.get(f"{tag}_swiglu_bwd"))
    dw_gate = mm(hb, dab, "tn", BF16, f"{tag}_gate_dw", out_slots=_slots(p["w_gate"]))
    dw_up = mm(hb, dbb, "tn", BF16, f"{tag}_up_dw", out_slots=_slots(p["w_up"]))
    on_dw("gate_up", {"w_gate": dw_gate, "w_up": dw_up})
    dh = mm(dab, p["w_gate"], "nt", F32, f"{tag}_gate_dx")
    dh = mm(dbb, p["w_up"], "nt", F32, f"{tag}_up_dx", add=dh)
    dx, dpre = _norm_bwd(x, p["pre_g"], dh, 1.0, F32, f"{tag}_pre_norm_bwd", res=dxo, pipe=carry.get(f"{tag}_pre_norm_bwd"))
    return dx, {"pre_g": dpre, "w_gate": dw_gate, "w_up": dw_up, "w_down": dw_down, "post_g": dpost}


def _mixer_fwd(x, cos, sin, p, carry):
    scale = (HEAD + ROPE) ** -0.5
    hb = _norm_fwd(x, p["pre_g"], "mix_pre_norm")
    u = _mm(hb, p["w_hg"], "nn", F32, "mix_in_hg", pipe=carry.get("mix_in_hg"))
    cq = _mm(hb, p["w_cq"], "nn", F32, "mix_in_cq")
    ckv = _mm(hb, p["w_ckv"], "nn", F32, "mix_in_ckv")
    kpe = _mm(hb, p["w_kpe"], "nn", F32, "mix_in_kpe")
    ga = _mm(hb, p["w_ga"], "nn", F32, "mix_in_ga", pipe=carry.get("mix_in_ga"))
    gb = _mm(hb, p["w_gb"], "nn", F32, "mix_in_gb", pipe=carry.get("mix_in_gb"))
    lb = _lb_fwd(p["lb_logits"], "hg_lb")
    o_raw, states = _hg_scan_fwd(u, lb, "hg_scan", pipe=carry.get("hg_scan"))
    oa = _hg_tail_fwd(o_raw, u, p["hg_norm_g"], "hg_tail")
    ya = _mm(oa, p["w_branch_a"], "nn", F32, "mix_branch_a")
    cqn = _norm_fwd(cq, p["q_norm_g"], "mla_q_norm")
    qn = _mm(cqn, p["w_qn"], "nn", BF16, "mla_q_up_nope")
    qpe = _rope_q_fwd(_mm(cqn, p["w_qpe"], "nn", F32, "mla_q_up_pe"), cos, sin, "mla_rope_q")
    ckvn = _norm_fwd(ckv, p["kv_norm_g"], "mla_kv_norm")
    kn = _mm(ckvn, p["w_kn"], "nn", BF16, "mla_k_up")
    vv = _mm(ckvn, p["w_vv"], "nn", BF16, "mla_v_up")
    kpe2 = _rope_k_fwd(kpe, cos, sin, "mla_rope_k")
    ob = _attn_fwd(qn, kn, vv, scale, "mla_attn", qpe=qpe, kpe=kpe2, causal=True, pipe=carry.get("mla_attn"))
    yb = _mm(ob, p["w_branch_b"], "nn", F32, "mix_branch_b")
    ym = _merge_fwd(ga, gb, ya, yb, "mix_merge")
    z = _mm(ym, p["w_out"], "nn", F32, "mix_out")
    xo = _postnorm_fwd(x, z, p["post_g"], 1.0, "mix_post_norm")
    saved = (x, hb, u, cq, ckv, ga, gb, lb, o_raw, states, oa, ya, cqn, qn, qpe, ckvn, kn, vv, kpe2, ob, yb, ym, z)
    return xo, saved


def _mixer_bwd(dxo, cos, sin, p, saved, carry, on_early, on_late):
    x, hb, u, cq, ckv, ga, gb, lb, o_raw, states, oa, ya, cqn, qn, qpe, ckvn, kn, vv, kpe2, ob, yb, ym, z = saved
    scale = (HEAD + ROPE) ** -0.5
    g = {}
    dzb, g["post_g"] = _norm_bwd(z, p["post_g"], dxo, 1.0, BF16, "mix_post_norm_bwd")
    g["w_out"] = _mm(ym, dzb, "tn", BF16, "mix_out_dw")
    dym = _mm(dzb, p["w_out"], "nt", F32, "mix_out_dx")
    dya, dyb, dga, dgb = _merge_bwd(ga, gb, ya, yb, dym, "mix_merge_bwd")
    g["w_branch_a"] = _mm(oa, dya, "tn", BF16, "mix_branch_a_dw")
    doa = _mm(dya, p["w_branch_a"], "nt", F32, "mix_branch_a_dx")
    do_raw, dog, g["hg_norm_g"] = _hg_tail_bwd(o_raw, u, p["hg_norm_g"], doa, "hg_tail_bwd")
    du, dlb = _hg_scan_bwd(u, lb, do_raw, dog, states, "hg_scan_bwd", pipe=carry.get("hg_scan_bwd"))
    g["lb_logits"] = _lb_bwd(lb, dlb, "hg_lb_bwd")
    g["w_branch_b"] = _mm(ob, dyb, "tn", BF16, "mix_branch_b_dw")
    on_early({k: g[k] for k in ("w_out", "w_branch_a", "w_branch_b")})
    dob = _mm(dyb, p["w_branch_b"], "nt", BF16, "mix_branch_b_dx")
    dqn, dkn, dvv, dqpe_h, dkpe_h = _attn_bwd(qn, kn, vv, dob, scale, "mla_attn_bwd", qpe=qpe, kpe=kpe2, causal=True,
                                              pipe=carry.get("mla_attn_bwd"))
    dqpe = _rope_q_bwd(dqpe_h, cos, sin, "mla_rope_q_bwd")
    dkpe = _rope_k_bwd(dkpe_h, cos, sin, "mla_rope_k_bwd")
    g["w_qn"] = _mm(cqn, dqn, "tn", BF16, "mla_q_up_nope_dw")
    g["w_qpe"] = _mm(cqn, dqpe, "tn", BF16, "mla_q_up_pe_dw")
    dcqn = _mm(dqn, p["w_qn"], "nt", F32, "mla_q_up_nope_dx")
    dcqn = _mm(dqpe, p["w_qpe"], "nt", F32, "mla_q_up_pe_dx", add=dcqn)
    dcq, g["q_norm_g"] = _norm_bwd(cq, p["q_norm_g"], dcqn, 1.0, BF16, "mla_q_norm_bwd")
    g["w_kn"] = _mm(ckvn, dkn, "tn", BF16, "mla_k_up_dw")
    g["w_vv"] = _mm(ckvn, dvv, "tn", BF16, "mla_v_up_dw")
    dckvn = _mm(dkn, p["w_kn"], "nt", F32, "mla_k_up_dx")
    dckvn = _mm(dvv, p["w_vv"], "nt", F32, "mla_v_up_dx", add=dckvn)
    dckv, g["kv_norm_g"] = _norm_bwd(ckv, p["kv_norm_g"], dckvn, 1.0, BF16, "mla_kv_norm_bwd")
    parts = (("w_hg", du), ("w_cq", dcq), ("w_ckv", dckv), ("w_kpe", dkpe), ("w_ga", dga), ("w_gb", dgb))
    for key, d in parts:
        g[key] = _mm(hb, d, "tn", BF16, f"mix_in_{key}_dw")
    on_late(g)
    dh = None
    for key, d in parts:
        dh = _mm(d, p[key], "nt", F32, f"mix_in_{key}_dx", add=dh, pipe=carry.get(f"mix_in_{key}_dx"))
    dx, g["pre_g"] = _norm_bwd(x, p["pre_g"], dh, 1.0, F32, "mix_pre_norm_bwd", res=dxo, pipe=carry.get("mix_pre_norm_bwd"))
    return dx, g


def _xa_fwd(x, mem, p):
    scale = HEAD ** -0.5
    hb = _norm_fwd(x, p["pre_g"], "xa_pre_norm")
    mb = _norm_fwd(mem, p["mem_g"], "xa_mem_norm")
    q = _mm(hb, p["w_q"], "nn", BF16, "xa_q")
    k = _mm(mb, p["w_k"], "nn", BF16, "xa_k")
    v = _mm(mb, p["w_v"], "nn", BF16, "xa_v")
    o = _attn_fwd(q, k, v, scale, "xa_attn")
    z = _mm(o, p["w_o"], "nn", F32, "xa_o")
    xo = _postnorm_fwd(x, z, p["post_g"], 1.0, "xa_post_norm")
    return xo, (x, mem, hb, mb, q, k, v, o, z)


def _xa_bwd(dxo, p, saved):
    x, mem, hb, mb, q, k, v, o, z = saved
    scale = HEAD ** -0.5
    g = {}
    dzb, g["post_g"] = _norm_bwd(z, p["post_g"], dxo, 1.0, BF16, "xa_post_norm_bwd")
    g["w_o"] = _mm(o, dzb, "tn", BF16, "xa_o_dw", out_slots=_slots(p["w_o"]))
    do = _mm(dzb, p["w_o"], "nt", BF16, "xa_o_dx")
    dq, dk, dv = _attn_bwd(q, k, v, do, scale, "xa_attn_bwd")
    g["w_q"] = _mm(hb, dq, "tn", BF16, "xa_q_dw")
    g["w_k"] = _mm(mb, dk, "tn", BF16, "xa_k_dw")
    g["w_v"] = _mm(mb, dv, "tn", BF16, "xa_v_dw")
    dh = _mm(dq, p["w_q"], "nt", F32, "xa_q_dx")
    dm = _mm(dk, p["w_k"], "nt", F32, "xa_k_dx")
    dm = _mm(dv, p["w_v"], "nt", F32, "xa_v_dx", add=dm)
    _, g["mem_g"] = _norm_bwd(mem, p["mem_g"], dm, 1.0, BF16, "xa_mem_norm_bwd")
    dx, g["pre_g"] = _norm_bwd(x, p["pre_g"], dh, 1.0, F32, "xa_pre_norm_bwd", res=dxo)
    return dx, g


def _local_step(x, mem, cos, sin, target, params_of, carry, on_grads):
    p1 = params_of("ffn1")
    x1, s1 = _ffn_fwd(x, p1, "ffn1", carry)
    p2 = params_of("mix")
    x2, s2 = _mixer_fwd(x1, cos, sin, p2, carry)
    p3 = params_of("xa")
    x3, s3 = _xa_fwd(x2, mem, p3)
    p4 = params_of("ffn2")
    x4, s4 = _ffn_fwd(x3, p4, "ffn2", carry)
    dy, sq_err = _loss_head(x4, target, "loss_head")
    loss = 0.5 / x.shape[1] * jnp.sum(sq_err)
    dx, g4 = _ffn_bwd(dy, p4, s4, "ffn2", carry, lambda stage, g: on_grads(f"ffn2_dw_{stage}", g))
    on_grads("ffn2", g4)
    dx, g3 = _xa_bwd(dx, p3, s3)
    on_grads("xa", g3)
    dx, g2 = _mixer_bwd(dx, cos, sin, p2, s2, carry, lambda g: on_grads("mix_early", g), lambda g: on_grads("mix_late", g))
    on_grads("mix", g2)
    dx, g1 = _ffn_bwd(dx, p1, s1, "ffn1", carry, lambda stage, g: on_grads(f"ffn1_dw_{stage}", g))
    on_grads("ffn1", g1)
    return loss, dx


def _split_w_in(w_in):
    D = w_in.shape[0]
    H = D // HEAD
    lora = (w_in.shape[1] - 6 * D - ROPE) // 2
    o = 4 * D
    w_hg = w_in[:, :o].reshape(D, 4, H, HEAD).transpose(0, 2, 1, 3).reshape(D, 4 * D)
    w_cq, w_ckv = w_in[:, o:o + lora], w_in[:, o + lora:o + 2 * lora]
    o += 2 * lora
    w_kpe = jnp.pad(w_in[:, o:o + ROPE], ((0, 0), (0, HEAD - ROPE)))
    o += ROPE
    return {"w_hg": w_hg, "w_cq": w_cq, "w_ckv": w_ckv, "w_kpe": w_kpe, "w_ga": w_in[:, o:o + D], "w_gb": w_in[:, o + D:o + 2 * D]}


def _merge_w_in(g):
    D = g["w_ga"].shape[0]
    H = D // HEAD
    hg = g["w_hg"].reshape(D, H, 4, HEAD).transpose(0, 2, 1, 3).reshape(D, 4 * D)
    return jnp.concatenate([hg, g["w_cq"], g["w_ckv"], g["w_kpe"][:, :ROPE], g["w_ga"], g["w_gb"]], axis=1)


def _split_heads(w, rest):
    K, N = w.shape
    w3 = w.reshape(K, N // (HEAD + rest), HEAD + rest)
    return w3[:, :, :HEAD].reshape(K, -1), w3[:, :, HEAD:].reshape(K, -1)


def _merge_heads(a, b, rest):
    K = a.shape[0]
    H = a.shape[1] // HEAD
    return jnp.concatenate([a.reshape(K, H, HEAD), b.reshape(K, H, rest)], axis=2).reshape(K, H * (HEAD + rest))


BLOCK_WEIGHTS = {
    "ffn1": ("ffn1_w_gate", "ffn1_w_up", "ffn1_w_down"),
    "mix": ("w_in", "mla_w_q_up", "mla_w_kv_up", "w_branch_a", "w_branch_b", "w_out"),
    "xa": ("xa_w_q", "xa_w_k", "xa_w_v", "xa_w_o"),
    "ffn2": ("ffn2_w_gate", "ffn2_w_up", "ffn2_w_down"),
}


def _block_params(block, w, small):
    if block in ("ffn1", "ffn2"):
        return {"pre_g": small[f"{block}_pre_g"], "w_gate": w[f"{block}_w_gate"], "w_up": w[f"{block}_w_up"],
                "w_down": w[f"{block}_w_down"], "post_g": small[f"{block}_post_g"]}
    if block == "xa":
        return {"pre_g": small["xa_pre_g"], "mem_g": small["xa_mem_g"], "post_g": small["xa_post_g"],
                "w_q": w["xa_w_q"], "w_k": w["xa_w_k"], "w_v": w["xa_w_v"], "w_o": w["xa_w_o"]}
    mix = _split_w_in(w["w_in"])
    mix["w_qn"], mix["w_qpe"] = _split_heads(w["mla_w_q_up"], ROPE)
    mix["w_kn"], mix["w_vv"] = _split_heads(w["mla_w_kv_up"], HEAD)
    mix.update(w_branch_a=w["w_branch_a"], w_branch_b=w["w_branch_b"], w_out=w["w_out"], pre_g=small["mix_pre_g"],
               post_g=small["mix_post_g"], hg_norm_g=small["hg_norm_g"], q_norm_g=small["mla_q_norm_g"],
               kv_norm_g=small["mla_kv_norm_g"], lb_logits=small["hgrn_lb_logits"])
    return mix


def _block_grads(block, g):
    if block in ("ffn1", "ffn2"):
        return {f"{block}_{k}": g[k] for k in ("pre_g", "w_gate", "w_up", "w_down", "post_g")}
    if block == "xa":
        return {f"xa_{k}": g[k] for k in ("pre_g", "mem_g", "post_g", "w_q", "w_k", "w_v", "w_o")}
    if block[4:8] == "_dw_":
        return {f"{block[:4]}_{k}": v for k, v in g.items()}
    if block == "mix_early":
        return dict(g)
    if block == "mix_late":
        return dict(w_in=_merge_w_in(g), mla_w_q_up=_merge_heads(g["w_qn"], g["w_qpe"], ROPE),
                    mla_w_kv_up=_merge_heads(g["w_kn"], g["w_vv"], HEAD))
    return dict(mix_pre_g=g["pre_g"], mix_post_g=g["post_g"],
                hg_norm_g=g["hg_norm_g"], mla_q_norm_g=g["q_norm_g"], mla_kv_norm_g=g["kv_norm_g"],
                hgrn_lb_logits=g["lb_logits"])


def _rope_tables(positions):
    inv_freq = 1.0 / (ROPE_THETA ** (jnp.arange(0, ROPE, 2, dtype=F32) / ROPE))
    ang = positions.astype(F32)[:, None] * inv_freq
    return jnp.tile(jnp.cos(ang), (1, 4)), jnp.tile(jnp.sin(ang), (1, 4))


def _adamw(w, g, m, v, name):
    bc1 = 1.0 - ADAM_B1 ** ADAM_STEP
    bc2 = 1.0 - ADAM_B2 ** ADAM_STEP

    def body(w_ref, g_ref, m_ref, v_ref, go_ref, d_ref, mo_ref, vo_ref):
        g = g_ref[...]
        m = ADAM_B1 * m_ref[...] + (1.0 - ADAM_B1) * g
        v = ADAM_B2 * v_ref[...] + (1.0 - ADAM_B2) * (g * g)
        go_ref[...] = g
        mo_ref[...] = m
        vo_ref[...] = v
        d_ref[...] = -ADAM_LR * ((m / bc1) / (jnp.sqrt(v / bc2) + ADAM_EPS) + ADAM_WD * w_ref[...])

    return _rows(body, [(w, "row"), (g, "row"), (m, "row"), (v, "row")], [(w.shape, F32, "row")] * 4, name)


ANY = pl.BlockSpec(memory_space=pl.ANY)
COMM_AXES = ("x", "y", "c")


def _place():
    x, y, c = (lax.axis_index(n) for n in COMM_AXES)
    chips = [(1 - x, y), (x, 1 - y), (1 - x, 1 - y)]
    return x, y, c, 2 * x + y, (x, y, 1 - c), chips


def _remote(src, dst, send_sems, recv_sems, j, to):
    return pltpu.make_async_remote_copy(src_ref=src, dst_ref=dst, send_sem=send_sems.at[j], recv_sem=recv_sems.at[j],
                                        device_id=to, device_id_type=MESH)


def _dma_sems(n):
    return [pltpu.SemaphoreType.DMA((n,)), pltpu.SemaphoreType.DMA((n,))]


def _all_gather(shards, whole, name):
    n = len(shards)

    def body(*refs):
        srcs, outs, send_sems, recv_sems = refs[:n], refs[n:2 * n], refs[2 * n], refs[2 * n + 1]
        x, y, c, me, sibling, chips = _place()
        sent = []

        def start(cp):
            cp.start()
            sent.append(cp)

        def rows(w, h):
            hr = srcs[w].shape[0] // 2
            return pl.ds(h * hr, hr)

        gathered = [w for w in range(n) if whole[w]]
        for w in gathered:
            for j, (cx, cy) in enumerate(chips):
                start(_remote(srcs[w].at[rows(w, c)], outs[w].at[me, rows(w, c)], send_sems, recv_sems, 7 * w + j, (cx, cy, c)))
        for w in range(n):
            start(_remote(srcs[w], outs[w].at[me], send_sems, recv_sems, 7 * w + 6, sibling))
        for w in gathered:
            for j, (cx, cy) in enumerate(chips):
                blk = outs[w].at[2 * cx + cy, rows(w, c)]
                _remote(srcs[w].at[rows(w, c)], blk, send_sems, recv_sems, 7 * w + j, (cx, cy, c)).wait_recv()
                start(_remote(blk, blk, send_sems, recv_sems, 7 * w + 3 + j, sibling))
        for w in gathered:
            for j, (cx, cy) in enumerate(chips):
                blk = outs[w].at[2 * cx + cy, rows(w, 1 - c)]
                _remote(blk, blk, send_sems, recv_sems, 7 * w + 3 + j, sibling).wait_recv()
        for w in range(n):
            _remote(srcs[w], outs[w].at[me], send_sems, recv_sems, 7 * w + 6, sibling).wait_recv()
        for cp in sent:
            cp.wait_send()

    return pl.pallas_call(
        body, in_specs=[ANY] * n, out_specs=[ANY] * n,
        out_shape=[jax.ShapeDtypeStruct((N_CHIP,) + s.shape, s.dtype) for s in shards],
        scratch_shapes=_dma_sems(7 * n), name=name,
    )(*shards)


def _rs_swap(grads, name):
    n = len(grads)

    def body(*refs):
        gs, outs, send_sems, recv_sems = refs[:n], refs[n:2 * n], refs[2 * n], refs[2 * n + 1]
        x, y, c, me, sibling, chips = _place()
        cps = []
        for w in range(n):
            hr = gs[w].shape[1] // 2
            cps.append(_remote(gs[w].at[:, pl.ds((1 - c) * hr, hr)], outs[w], send_sems, recv_sems, w, sibling))
            cps[-1].start()
        for cp in cps:
            cp.wait()

    return pl.pallas_call(
        body, in_specs=[ANY] * n, out_specs=[ANY] * n,
        out_shape=[jax.ShapeDtypeStruct((g.shape[0], g.shape[1] // 2, g.shape[2]), g.dtype) for g in grads],
        scratch_shapes=_dma_sems(n), name=name,
    )(*grads)


def _sum_rows(hr, row_bytes):
    return _tile(hr, max(16, ROW_BUDGET // (2 * row_bytes) // 16 * 16), 16)


def _rs_pair_sum(g, got, c, name):
    S, r, cw = g.shape
    hr = r // 2
    tr = _sum_rows(hr, 3 * cw * 2)
    nrb = hr // tr

    def body(c_ref, a_ref, b_ref, o_ref):
        o_ref[...] = (a_ref[...].astype(F32) + b_ref[...].astype(F32)).astype(BF16)

    return pl.pallas_call(
        body,
        grid_spec=pltpu.PrefetchScalarGridSpec(
            num_scalar_prefetch=1, grid=(S, nrb),
            in_specs=[pl.BlockSpec((None, tr, cw), lambda k, i, c_ref: (k, c_ref[0] * nrb + i, 0)),
                      pl.BlockSpec((None, tr, cw), lambda k, i, c_ref: (k, i, 0))],
            out_specs=pl.BlockSpec((None, tr, cw), lambda k, i, c_ref: (k, i, 0)),
        ),
        out_shape=jax.ShapeDtypeStruct((S, hr, cw), BF16),
        compiler_params=_params(("parallel", "parallel")), name=name,
    )(c, g, got)


def _rs_chip_sum(pair, got, place, name):
    S, hr, cw = pair.shape
    tr = _sum_rows(hr, cw * (4 * 2 + 4))

    def body(p_ref, a_ref, z_ref, o_ref):
        o_ref[...] = a_ref[...].astype(F32) + z_ref[0].astype(F32) + z_ref[1].astype(F32) + z_ref[2].astype(F32)

    return pl.pallas_call(
        body,
        grid_spec=pltpu.PrefetchScalarGridSpec(
            num_scalar_prefetch=1, grid=(hr // tr,),
            in_specs=[pl.BlockSpec((None, tr, cw), lambda i, p_ref: (p_ref[0], i, 0)),
                      pl.BlockSpec((3, tr, cw), lambda i, p_ref: (0, i, 0))],
            out_specs=pl.BlockSpec((None, tr, cw), lambda i, p_ref: (p_ref[1], i, 0)),
        ),
        out_shape=jax.ShapeDtypeStruct((2, hr, cw), F32),
        compiler_params=_params(("parallel",)), name=name,
    )(place, pair, got)


def _rs_share(halves, name):
    n = len(halves)

    def body(*refs):
        outs, send_sems, recv_sems = refs[n:2 * n], refs[2 * n], refs[2 * n + 1]
        x, y, c, me, sibling, chips = _place()
        cps = []
        for w in range(n):
            cps.append(_remote(outs[w].at[c], outs[w].at[c], send_sems, recv_sems, w, sibling))
            cps[-1].start()
        for w in range(n):
            _remote(outs[w].at[1 - c], outs[w].at[1 - c], send_sems, recv_sems, w, sibling).wait_recv()
        for cp in cps:
            cp.wait_send()

    return pl.pallas_call(
        body, in_specs=[ANY] * n, out_specs=[ANY] * n,
        out_shape=[jax.ShapeDtypeStruct(h.shape, h.dtype) for h in halves],
        input_output_aliases={i: i for i in range(n)},
        scratch_shapes=_dma_sems(n), name=name,
    )(*halves)


def _rs_begin(names, grads, core, fractions, tag):
    slotted = [_to_slots(n, grads[n]) for n in names]
    got = _rs_swap(slotted, f"grads_sibling_swap_{tag}")
    pairs = [_rs_pair_sum(s, t, core.reshape(1), f"grads_pair_sum_{n}") for n, s, t in zip(names, slotted, got)]
    lands = [lax.empty((N_CHIP - 1,) + p.shape[1:], p.dtype) for p in pairs]
    return pairs, _Pipe("reduce", pairs, lands, fractions)


def _rs_end(names, pairs, pipe, place, shapes, tag):
    _pipe_flush(pipe, f"grads_chip_exchange_rest_{tag}")
    halves = [_rs_chip_sum(p, o, place, f"grads_chip_sum_{n}") for n, p, o in zip(names, pairs, pipe.lands)]
    both = _rs_share(halves, f"grads_sibling_share_{tag}")
    return {n: b.reshape(s) for n, b, s in zip(names, both, shapes)}


def _all_reduce_small(s, name):
    flips = [(dx, dy, dc) for dx in (0, 1) for dy in (0, 1) for dc in (0, 1) if (dx, dy, dc) != (0, 0, 0)]

    def body(s_ref, o_ref, buf, send_sems, recv_sems):
        x, y, c = (lax.axis_index(n) for n in COMM_AXES)
        me = 4 * x + 2 * y + c
        buf[me] = s_ref[...]
        peers = [((1 - x) if dx else x, (1 - y) if dy else y, (1 - c) if dc else c) for dx, dy, dc in flips]
        sent = [_remote(s_ref, buf.at[me], send_sems, recv_sems, j, p) for j, p in enumerate(peers)]
        for cp in sent:
            cp.start()
        for j, (px, py, pc) in enumerate(peers):
            _remote(s_ref, buf.at[4 * px + 2 * py + pc], send_sems, recv_sems, j, (px, py, pc)).wait_recv()
        for cp in sent:
            cp.wait_send()
        acc = buf[0]
        for d in range(1, N_DEV):
            acc = acc + buf[d]
        o_ref[...] = acc

    vmem = pl.BlockSpec(memory_space=pltpu.VMEM)
    return pl.pallas_call(
        body, in_specs=[vmem], out_specs=vmem, out_shape=jax.ShapeDtypeStruct(s.shape, F32),
        scratch_shapes=[pltpu.VMEM((N_DEV,) + s.shape, F32), pltpu.SemaphoreType.DMA((7,)), pltpu.SemaphoreType.DMA((7,))],
        name=name,
    )(s)


BIG = {
    "ffn1_w_gate": 1, "ffn1_w_up": 1, "ffn1_w_down": 0, "w_in": 1, "mla_w_q_up": 1, "mla_w_kv_up": 1,
    "w_branch_a": 0, "w_branch_b": 0, "w_out": 0, "xa_w_q": 0, "xa_w_k": 0, "xa_w_v": 0, "xa_w_o": 1,
    "ffn2_w_gate": 1, "ffn2_w_up": 1, "ffn2_w_down": 0,
}
WEIGHTS = [
    "hgrn_lb_logits", "ffn1_pre_g", "ffn1_w_gate", "ffn1_w_up", "ffn1_w_down", "ffn1_post_g", "mix_pre_g", "w_in",
    "hg_norm_g", "mla_q_norm_g", "mla_w_q_up", "mla_kv_norm_g", "mla_w_kv_up", "w_branch_a", "w_branch_b", "w_out",
    "mix_post_g", "xa_pre_g", "xa_mem_g", "xa_w_q", "xa_w_k", "xa_w_v", "xa_w_o", "xa_post_g", "ffn2_pre_g",
    "ffn2_w_gate", "ffn2_w_up", "ffn2_w_down", "ffn2_post_g",
]
SMALL = [n for n in WEIGHTS if n not in BIG]
SLOTTED = ("ffn1_w_gate", "ffn1_w_up", "ffn2_w_gate", "ffn2_w_up", "xa_w_o")
MIX_EARLY = ("w_out", "w_branch_a", "w_branch_b")


def _from_slots(name, g):
    S, r, cw = g.shape
    if BIG[name] == 0:
        return g.reshape(S * r, cw)
    return g if name in SLOTTED else g.transpose(1, 0, 2).reshape(r, S * cw)


def _to_slots(name, g):
    if g.ndim == 3:
        return g
    if BIG[name] == 0:
        return g.reshape(N_CHIP, g.shape[0] // N_CHIP, g.shape[1])
    return g.reshape(g.shape[0], N_CHIP, g.shape[1] // N_CHIP).transpose(1, 0, 2)


def _pack_small(vals, width):
    rows = [jnp.pad(v, ((0, 0), (0, width - v.shape[1]))) for v in vals]
    s = jnp.concatenate(rows, axis=0)
    return jnp.pad(s, ((0, -s.shape[0] % 8), (0, 0)))


def _unpack_small(s, shapes):
    out, o = [], 0
    for r, w in shapes:
        out.append(s[o:o + r, :w])
        o += r
    return out


def kernel(x, mem, positions, hgrn_lb_logits, ffn1_pre_g, ffn1_w_gate, ffn1_w_up, ffn1_w_down, ffn1_post_g, mix_pre_g, w_in, hg_norm_g, mla_q_norm_g, mla_w_q_up, mla_kv_norm_g, mla_w_kv_up, w_branch_a, w_branch_b, w_out, mix_post_g, xa_pre_g, xa_mem_g, xa_w_q, xa_w_k, xa_w_v, xa_w_o, xa_post_g, ffn2_pre_g, ffn2_w_gate, ffn2_w_up, ffn2_w_down, ffn2_post_g, loss_target, m_hgrn_lb_logits, m_ffn1_pre_g, m_ffn1_w_gate, m_ffn1_w_up, m_ffn1_w_down, m_ffn1_post_g, m_mix_pre_g, m_w_in, m_hg_norm_g, m_mla_q_norm_g, m_mla_w_q_up, m_mla_kv_norm_g, m_mla_w_kv_up, m_w_branch_a, m_w_branch_b, m_w_out, m_mix_post_g, m_xa_pre_g, m_xa_mem_g, m_xa_w_q, m_xa_w_k, m_xa_w_v, m_xa_w_o, m_xa_post_g, m_ffn2_pre_g, m_ffn2_w_gate, m_ffn2_w_up, m_ffn2_w_down, m_ffn2_post_g, v_hgrn_lb_logits, v_ffn1_pre_g, v_ffn1_w_gate, v_ffn1_w_up, v_ffn1_w_down, v_ffn1_post_g, v_mix_pre_g, v_w_in, v_hg_norm_g, v_mla_q_norm_g, v_mla_w_q_up, v_mla_kv_norm_g, v_mla_w_kv_up, v_w_branch_a, v_w_branch_b, v_w_out, v_mix_post_g, v_xa_pre_g, v_xa_mem_g, v_xa_w_q, v_xa_w_k, v_xa_w_v, v_xa_w_o, v_xa_post_g, v_ffn2_pre_g, v_ffn2_w_gate, v_ffn2_w_up, v_ffn2_w_down, v_ffn2_post_g):
    a = dict(locals())
    big = list(BIG)
    small = {n: a[n] for n in SMALL}
    core = lax.axis_index("c").astype(jnp.int32)
    place = jnp.stack([(2 * lax.axis_index("x") + lax.axis_index("y")).astype(jnp.int32), core])

    shards = {n: a[n][0].astype(BF16) for n in big}
    whole = [n in BLOCK_WEIGHTS["ffn1"] for n in big]
    lands = dict(zip(big, _all_gather([shards[n] for n in big], whole, "weights_all_gather")))
    late = BLOCK_WEIGHTS["xa"] + BLOCK_WEIGHTS["ffn2"]
    gather = lambda names, fr: _Pipe("gather", [shards[n] for n in names], [lands[n] for n in names], fr)
    pipe_mix, pipe_late = gather(BLOCK_WEIGHTS["mix"], (2, 10, 10, 6, 10, 3)), gather(late, (6, 2, 2, 18, 10))
    carry = {f"ffn1_{k}": pipe_mix for k in ("pre_norm", "gate", "up", "swiglu", "down", "post_norm")}
    carry.update({k: pipe_late for k in ("mix_in_hg", "mix_in_ga", "mix_in_gb", "hg_scan", "mla_attn")})

    def params_of(block):
        for first, names, pipe in (("mix", BLOCK_WEIGHTS["mix"], pipe_mix), ("xa", late, pipe_late)):
            if block == first:
                _pipe_flush(pipe, f"weights_gather_rest_{block}")
                lands.update(zip(names, pipe.lands))
        return _block_params(block, {n: _from_slots(n, lands[n]) for n in BLOCK_WEIGHTS[block]}, small)

    g_small, g_big, open_groups, held = {}, {}, {}, {}

    def begin(tag, names, fractions, carriers):
        pairs, pipe = _rs_begin(names, held, core, fractions, tag)
        open_groups[tag] = (names, pairs, pipe)
        carry.update({c: pipe for c in carriers})

    def end(tag):
        names, pairs, pipe = open_groups.pop(tag)
        g_big.update(_rs_end(names, pairs, pipe, place, [a[n].shape[1:] for n in names], tag))

    def on_grads(block, g):
        for n, v in _block_grads(block, g).items():
            (held if n in BIG else g_small)[n] = v
        if block == "ffn2":
            begin("ffn2", BLOCK_WEIGHTS["ffn2"], (1,), ("hg_scan_bwd",))
        elif block == "mix_early":
            begin("early", BLOCK_WEIGHTS["xa"] + MIX_EARLY, (1,), ("mla_attn_bwd",))
        elif block == "mix_late":
            end("ffn2")
            end("early")
            begin("mid", [n for n in BLOCK_WEIGHTS["mix"] if n not in MIX_EARLY], (10, 3, 3, 3, 2, 6, 6),
                  ("mix_in_w_hg_dx", "mix_in_w_ga_dx", "mix_in_w_gb_dx", "mix_pre_norm_bwd",
                   "ffn1_post_norm_bwd", "ffn1_down_dw", "ffn1_down_dx"))
        elif block == "ffn1_dw_down":
            begin("ffn1_down", ["ffn1_w_down"], (1, 1, 1), [f"ffn1_{k}" for k in ("swiglu_bwd", "gate_dw", "up_dw")])
        elif block == "ffn1_dw_gate_up":
            end("mid")
            begin("ffn1_gate_up", ["ffn1_w_gate", "ffn1_w_up"], (8, 9, 3), [f"ffn1_{k}" for k in ("gate_dx", "up_dx", "pre_norm_bwd")])
        elif block == "ffn1":
            end("ffn1_down")
            end("ffn1_gate_up")

    cos, sin = _rope_tables(positions[0])
    loss_part, grad_x = _local_step(x[0], mem[0], cos, sin, loss_target[0], params_of, carry, on_grads)

    small_shapes = [a[n].shape for n in SMALL]
    width = max(s[1] for s in small_shapes)
    gs = _all_reduce_small(_pack_small([g_small[n] for n in SMALL], width), "small_grads_all_reduce")

    out_g, out_d, out_m, out_v = {}, {}, {}, {}
    for n in big:
        flip = (lambda t: jnp.swapaxes(t, 0, 1)) if a[n].shape[2] % 128 else (lambda t: t)
        res = _adamw(*(flip(t) for t in (a[n][0], g_big[n], a["m_" + n][0], a["v_" + n][0])), f"adamw_{n}")
        out_g[n], out_d[n], out_m[n], out_v[n] = (flip(t).reshape(a[n].shape) for t in res)
    sw, sm, sv = (_pack_small([a[p + n] for n in SMALL], width) for p in ("", "m_", "v_"))
    for t, dst in zip(_adamw(sw, gs, sm, sv, "adamw_small"), (out_g, out_d, out_m, out_v)):
        dst.update(zip(SMALL, _unpack_small(t, small_shapes)))

    loss = lax.psum(loss_part, COMM_AXES)
    return (loss, grad_x[None], *[out_g[n] for n in WEIGHTS], *[out_d[n] for n in WEIGHTS],
            *[out_m[n] for n in WEIGHTS], *[out_v[n] for n in WEIGHTS])
```

```python
import functools

import jax
import jax.numpy as jnp
from jax import lax
from jax.experimental import pallas as pl
from jax.experimental.pallas import tpu as pltpu

F32 = jnp.float32
BF16 = jnp.bfloat16
EPS = 1e-6
HEAD = 128
ROPE = 64
CHUNK = 64
SUB = 16
HG_BLOCK = 256
ROPE_THETA = 10000.0
FFN_RESIDUAL_WEIGHT = 0.5
ADAM_LR, ADAM_B1, ADAM_B2, ADAM_EPS, ADAM_WD, ADAM_STEP = 0.001, 0.9, 0.999, 1e-08, 0.01, 10
VMEM_LIMIT = 56 * 2**20
ROW_BUDGET = 20 * 2**20
NEG = -1e30
MESH = pl.DeviceIdType.MESH
N_CHIP = 4
N_DEV = 8


def _params(sem):
    return pltpu.CompilerParams(dimension_semantics=sem, vmem_limit_bytes=VMEM_LIMIT)


def _tile(n, cap, mult):
    if n <= cap:
        return n
    t = (cap // mult) * mult
    while t >= mult:
        if n % t == 0:
            return t
        t -= mult
    raise ValueError(f"no tile for {n} under {cap}")


def _split_rows(n, fractions):
    if n % 16:
        return [(0, n)] + [(n, 0)] * (len(fractions) - 1)
    units, total, acc, cuts = n // 16, sum(fractions), 0, [0]
    for f in fractions[:-1]:
        acc += f
        cuts.append(round(units * acc / total))
    cuts.append(units)
    return [(16 * lo, 16 * (hi - lo)) for lo, hi in zip(cuts, cuts[1:])]


class _Pipe:
    def __init__(self, kind, srcs, lands, fractions):
        self.kind, self.srcs, self.lands = kind, list(srcs), list(lands)
        per_w = [_split_rows(s.shape[0] // 2 if kind == "gather" else s.shape[1], fractions) for s in srcs]
        self.parts = [[pw[i] for pw in per_w] for i in range(len(fractions))]
        self.taken = 0
        self.sems = (6 if kind == "gather" else 3) * len(srcs)

    def take(self):
        self.taken += 1
        return self.parts[self.taken - 1]

    def rest(self):
        left = self.parts[self.taken:]
        self.taken = len(self.parts)
        return [(left[0][w][0], sum(p[w][1] for p in left)) for w in range(len(self.srcs))] if left else None


def _pipe_copies(kind, rows, lands, srcs, send_sems, recv_sems):
    x, y, c, me, sibling, chips = _place()
    out = []
    for w, (r0, nr) in enumerate(rows):
        for j, (cx, cy) in enumerate(chips if nr else []):
            k, to = 2 * cx + cy, (cx, cy, c)
            if kind == "gather":
                rs = pl.ds(c * (srcs[w].shape[0] // 2) + r0, nr)
                src, dst, got, j0 = srcs[w].at[rs], lands[w].at[me, rs], lands[w].at[k, rs], 6 * w + j
            else:
                rs = pl.ds(r0, nr)
                src, dst, got, j0 = srcs[w].at[k, rs], lands[w].at[j, rs], lands[w].at[j, rs], 3 * w + j
            out.append((_remote(src, dst, send_sems, recv_sems, j0, to), _remote(src, got, send_sems, recv_sems, j0, to), w, j, k))
    return out


def _pipe_start(kind, rows, lands, srcs, send_sems, recv_sems):
    for send, _, _, _, _ in _pipe_copies(kind, rows, lands, srcs, send_sems, recv_sems):
        send.start()


def _pipe_finish(kind, rows, lands, srcs, send_sems, recv_sems):
    x, y, c, me, sibling, chips = _place()
    copies = _pipe_copies(kind, rows, lands, srcs, send_sems, recv_sems)
    passed = []
    for _, arrival, _, _, _ in copies:
        arrival.wait_recv()
    if kind == "gather":
        for w, (r0, nr) in enumerate(rows):
            hr = srcs[w].shape[0] // 2
            for j, (cx, cy) in enumerate(chips if nr else []):
                blk = lands[w].at[2 * cx + cy, pl.ds(c * hr + r0, nr)]
                passed.append(_remote(blk, blk, send_sems, recv_sems, 6 * w + 3 + j, sibling))
                passed[-1].start()
        for w, (r0, nr) in enumerate(rows):
            hr = srcs[w].shape[0] // 2
            for j, (cx, cy) in enumerate(chips if nr else []):
                blk = lands[w].at[2 * cx + cy, pl.ds((1 - c) * hr + r0, nr)]
                _remote(blk, blk, send_sems, recv_sems, 6 * w + 3 + j, sibling).wait_recv()
    for send, _, _, _, _ in copies:
        send.wait_send()
    for cp in passed:
        cp.wait_send()


def _carry_call(body, *, grid, in_specs, out_specs, out_shape, scratch_shapes, sem, name, args, pipe=None):
    single = not isinstance(out_shape, (list, tuple))
    if single:
        out_specs, out_shape = [out_specs], [out_shape]
    if pipe is None or pipe.taken >= len(pipe.parts):
        res = pl.pallas_call(body, grid=grid, in_specs=in_specs, out_specs=out_specs, out_shape=out_shape,
                             scratch_shapes=scratch_shapes, compiler_params=_params(sem), name=name)(*args)
        return res[0] if single else res
    rows, kind = pipe.take(), pipe.kind
    n_in, n_out, n_l, n_s, n_scr = len(args), len(out_shape), len(pipe.lands), len(pipe.srcs), len(scratch_shapes)

    def wrapped(*refs):
        ins, srcs = refs[:n_in], refs[n_in + n_l:n_in + n_l + n_s]
        o0 = n_in + n_l + n_s
        outs, lands = refs[o0:o0 + n_out], refs[o0 + n_out:o0 + n_out + n_l]
        scr = refs[o0 + n_out + n_l:o0 + n_out + n_l + n_scr]
        send_sems, recv_sems = refs[-2], refs[-1]
        ids = [pl.program_id(ax) for ax in range(len(grid))]
        first = functools.reduce(jnp.logical_and, [i == 0 for i in ids])
        last = functools.reduce(jnp.logical_and, [i == g - 1 for i, g in zip(ids, grid)])

        @pl.when(first)
        def _():
            _pipe_start(kind, rows, lands, srcs, send_sems, recv_sems)

        body(*ins, *outs, *scr)

        @pl.when(last)
        def _():
            _pipe_finish(kind, rows, lands, srcs, send_sems, recv_sems)

    res = pl.pallas_call(
        wrapped, grid=grid, in_specs=list(in_specs) + [ANY] * (n_l + n_s), out_specs=list(out_specs) + [ANY] * n_l,
        out_shape=list(out_shape) + [jax.ShapeDtypeStruct(l.shape, l.dtype) for l in pipe.lands],
        input_output_aliases={n_in + i: n_out + i for i in range(n_l)},
        scratch_shapes=list(scratch_shapes) + _dma_sems(pipe.sems),
        compiler_params=_params(("arbitrary",) * len(grid)), name=name,
    )(*args, *pipe.lands, *pipe.srcs)
    pipe.lands = list(res[n_out:])
    return res[0] if single else list(res[:n_out])


def _pipe_flush(pipe, name):
    rows = pipe.rest()
    if rows is None:
        return
    n_l, n_s, kind = len(pipe.lands), len(pipe.srcs), pipe.kind

    def body(*refs):
        srcs, lands = refs[n_l:n_l + n_s], refs[n_l + n_s:2 * n_l + n_s]
        _pipe_start(kind, rows, lands, srcs, refs[-2], refs[-1])
        _pipe_finish(kind, rows, lands, srcs, refs[-2], refs[-1])

    pipe.lands = list(pl.pallas_call(
        body, in_specs=[ANY] * (n_l + n_s), out_specs=[ANY] * n_l,
        out_shape=[jax.ShapeDtypeStruct(l.shape, l.dtype) for l in pipe.lands],
        input_output_aliases={i: i for i in range(n_l)}, scratch_shapes=_dma_sems(pipe.sems), name=name,
    )(*pipe.lands, *pipe.srcs))


def _sigmoid(x):
    return 1.0 / (1.0 + jnp.exp(-x))


def _dot(a, b, dims):
    return lax.dot_general(a, b, (dims, ((), ())), preferred_element_type=F32)


NN = ((1,), (0,))
NT = ((1,), (1,))
TN = ((0,), (0,))


def _mm(a, b, mode, out_dtype, name, add=None, out_slots=0, pipe=None, tm_cap=1024, tn_cap=512, tk_cap=2816):
    slot_cap = 1408
    b_slots = b.shape[0] if b.ndim == 3 else 0
    bs = (b.shape[1], b_slots * b.shape[2]) if b_slots else b.shape
    if mode == "nn":
        (M, K), (K2, N) = a.shape, bs
    elif mode == "nt":
        (M, K), (N, K2) = a.shape, bs
    else:
        (K, M), (K2, N) = a.shape, bs
    assert K == K2, (a.shape, b.shape, mode)
    tm = _tile(M, tm_cap, 128)
    tn = _tile(N // (b_slots or out_slots), slot_cap, 128) if (out_slots or (b_slots and mode == "nn")) else _tile(N, tn_cap, 128)
    tk = _tile(K // b_slots, slot_cap, 128) if (b_slots and mode == "nt") else _tile(K, tk_cap, 128)
    nk = K // tk
    a_spec = pl.BlockSpec((tk, tm), lambda i, j, k: (k, i)) if mode == "tn" else pl.BlockSpec((tm, tk), lambda i, j, k: (i, k))
    if b_slots and mode == "nn":
        per = b.shape[2] // tn
        b_spec = pl.BlockSpec((None, tk, tn), lambda i, j, k: (j // per, k, j % per))
    elif b_slots:
        per = b.shape[2] // tk
        b_spec = pl.BlockSpec((None, tn, tk), lambda i, j, k: (k // per, j, k % per))
    else:
        b_spec = pl.BlockSpec((tn, tk), lambda i, j, k: (j, k)) if mode == "nt" else pl.BlockSpec((tk, tn), lambda i, j, k: (k, j))
    if out_slots:
        per_o = N // out_slots // tn
        o_spec = pl.BlockSpec((None, tm, tn), lambda i, j, k: (j // per_o, i, j % per_o))
        o_shape = (out_slots, M, N // out_slots)
    else:
        o_spec = pl.BlockSpec((tm, tn), lambda i, j, k: (i, j))
        o_shape = (M, N)
    dims = {"nn": NN, "nt": NT, "tn": TN}[mode]
    has_add = add is not None

    def body(*refs):
        a_ref, b_ref = refs[0], refs[1]
        add_ref = refs[2] if has_add else None
        o_ref = refs[3] if has_add else refs[2]
        p = _dot(a_ref[...].astype(BF16), b_ref[...].astype(BF16), dims)

        def finish(val):
            if has_add:
                val = val + add_ref[...]
            o_ref[...] = val.astype(out_dtype)

        if nk == 1:
            finish(p)
        else:
            acc_ref = refs[-1]
            k = pl.program_id(2)

            @pl.when(k == 0)
            def _():
                acc_ref[...] = p

            @pl.when(k > 0)
            def _():
                acc_ref[...] += p

            @pl.when(k == nk - 1)
            def _():
                finish(acc_ref[...])

    assert not (has_add and out_slots)
    in_specs = [a_spec, b_spec] + ([o_spec] if has_add else [])
    args = (a, b) + ((add,) if has_add else ())
    return _carry_call(
        body, grid=(M // tm, N // tn, nk), in_specs=in_specs, out_specs=o_spec,
        out_shape=jax.ShapeDtypeStruct(o_shape, out_dtype),
        scratch_shapes=[pltpu.VMEM((tm, tn), F32)] if nk > 1 else [],
        sem=("parallel", "parallel", "arbitrary"), name=name, args=args, pipe=pipe,
    )


def _rows(body, ins, outs, name, pipe=None):
    T = next(a.shape[0] for a, k in ins if k == "row")
    per_row = sum(a.shape[1] * a.dtype.itemsize for a, k in ins if k == "row")
    per_row += sum(s[1] * jnp.dtype(d).itemsize for s, d, k in outs if k == "row")
    tr = next(t for t in range(512, 0, -8) if T % t == 0 and 2 * t * per_row <= ROW_BUDGET)
    in_specs = [
        pl.BlockSpec((tr, a.shape[1]), lambda i: (i, 0)) if k == "row" else pl.BlockSpec(a.shape, lambda i: (0, 0))
        for a, k in ins
    ]
    out_specs = [
        pl.BlockSpec((tr, s[1]), lambda i: (i, 0)) if k == "row" else pl.BlockSpec(s, lambda i: (0, 0))
        for s, d, k in outs
    ]
    has_acc = any(k == "acc" for _, _, k in outs)
    return _carry_call(
        body, grid=(T // tr,), in_specs=in_specs, out_specs=out_specs,
        out_shape=[jax.ShapeDtypeStruct(s, d) for s, d, k in outs], scratch_shapes=[],
        sem=("arbitrary",) if has_acc else ("parallel",), name=name, args=[a for a, _ in ins], pipe=pipe,
    )


def _rstd(x):
    return lax.rsqrt(jnp.mean(x * x, axis=-1, keepdims=True) + EPS)


def _norm_fwd(x, g, name, pipe=None):
    def body(x_ref, g_ref, o_ref):
        x = x_ref[...]
        o_ref[...] = (x * _rstd(x) * g_ref[...]).astype(BF16)

    return _rows(body, [(x, "row"), (g, "vec")], [(x.shape, BF16, "row")], name, pipe=pipe)[0]


def _postnorm_fwd(x, y, g, weight, name, pipe=None):
    def body(x_ref, y_ref, g_ref, o_ref):
        y = y_ref[...]
        o_ref[...] = x_ref[...] + weight * (y * _rstd(y) * g_ref[...])

    return _rows(body, [(x, "row"), (y, "row"), (g, "vec")], [(x.shape, F32, "row")], name, pipe=pipe)[0]


def _norm_bwd(x, g, dy, weight, out_dtype, name, res=None, pipe=None):
    has_res = res is not None

    def body(*refs):
        x_ref, g_ref, dy_ref = refs[:3]
        res_ref = refs[3] if has_res else None
        dx_ref, dg_ref = refs[-2], refs[-1]

        @pl.when(pl.program_id(0) == 0)
        def _():
            dg_ref[...] = jnp.zeros_like(dg_ref)

        x = x_ref[...]
        dn = dy_ref[...].astype(F32) * weight
        r = _rstd(x)
        xhat = x * r
        dg_ref[...] += jnp.sum(dn * xhat, axis=0, keepdims=True)
        dxh = dn * g_ref[...]
        dx = r * (dxh - xhat * jnp.mean(dxh * xhat, axis=-1, keepdims=True))
        if has_res:
            dx = dx + res_ref[...]
        dx_ref[...] = dx.astype(out_dtype)

    ins = [(x, "row"), (g, "vec"), (dy, "row")] + ([(res, "row")] if has_res else [])
    return _rows(body, ins, [(x.shape, out_dtype, "row"), (g.shape, F32, "acc")], name, pipe=pipe)


def _swiglu_fwd(a, b, name, pipe=None):
    def body(a_ref, b_ref, o_ref):
        a = a_ref[...]
        o_ref[...] = (a * _sigmoid(a) * b_ref[...]).astype(BF16)

    return _rows(body, [(a, "row"), (b, "row")], [(a.shape, BF16, "row")], name, pipe=pipe)[0]


def _swiglu_bwd(a, b, ds, name, pipe=None):
    def body(a_ref, b_ref, ds_ref, da_ref, db_ref):
        a, ds = a_ref[...], ds_ref[...]
        sg = _sigmoid(a)
        da_ref[...] = (ds * b_ref[...] * (sg * (1.0 + a * (1.0 - sg)))).astype(BF16)
        db_ref[...] = (ds * (a * sg)).astype(BF16)

    return _rows(body, [(a, "row"), (b, "row"), (ds, "row")], [(a.shape, BF16, "row"), (a.shape, BF16, "row")], name, pipe=pipe)


def _merge_fwd(ga, gb, ya, yb, name):
    def body(ga_ref, gb_ref, ya_ref, yb_ref, o_ref):
        o_ref[...] = (_sigmoid(ga_ref[...]) * ya_ref[...] + _sigmoid(gb_ref[...]) * yb_ref[...]).astype(BF16)

    return _rows(body, [(ga, "row"), (gb, "row"), (ya, "row"), (yb, "row")], [(ga.shape, BF16, "row")], name)[0]


def _merge_bwd(ga, gb, ya, yb, dy, name):
    def body(ga_ref, gb_ref, ya_ref, yb_ref, dy_ref, dya_ref, dyb_ref, dga_ref, dgb_ref):
        dy = dy_ref[...]
        sa, sb = _sigmoid(ga_ref[...]), _sigmoid(gb_ref[...])
        dya_ref[...] = (dy * sa).astype(BF16)
        dyb_ref[...] = (dy * sb).astype(BF16)
        dga_ref[...] = (dy * ya_ref[...] * (sa * (1.0 - sa))).astype(BF16)
        dgb_ref[...] = (dy * yb_ref[...] * (sb * (1.0 - sb))).astype(BF16)

    ins = [(ga, "row"), (gb, "row"), (ya, "row"), (yb, "row"), (dy, "row")]
    return _rows(body, ins, [(ga.shape, BF16, "row")] * 4, name)


def _loss_head(y, target, name):
    D = y.shape[1]

    def body(y_ref, t_ref, dy_ref, acc_ref):
        @pl.when(pl.program_id(0) == 0)
        def _():
            acc_ref[...] = jnp.zeros_like(acc_ref)

        err = y_ref[...] - t_ref[...]
        dy_ref[...] = err * (1.0 / D)
        acc_ref[...] += jnp.sum(err * err, axis=0, keepdims=True)

    return _rows(body, [(y, "row"), (target, "row")], [(y.shape, F32, "row"), ((1, D), F32, "acc")], name)


def _rot(x):
    lane = lax.broadcasted_iota(jnp.int32, x.shape, 1)
    return jnp.where((lane % ROPE) < ROPE // 2, -pltpu.roll(x, 128 - ROPE // 2, 1), pltpu.roll(x, ROPE // 2, 1))


def _rope_q_fwd(qpe, cos, sin, name):
    T, W = qpe.shape
    tr = min(T, 512)
    blk = pl.BlockSpec((tr, 128), lambda i, j: (i, j))
    tab = pl.BlockSpec((tr, 128), lambda i, j: (i, 0))

    def body(x_ref, c_ref, s_ref, o_ref):
        x = x_ref[...]
        o_ref[...] = (x * c_ref[...] + _rot(x) * s_ref[...]).astype(BF16)

    return pl.pallas_call(
        body, grid=(T // tr, W // 128), in_specs=[blk, tab, tab], out_specs=blk,
        out_shape=jax.ShapeDtypeStruct((T, W), BF16), compiler_params=_params(("parallel", "parallel")), name=name,
    )(qpe, cos, sin)


def _rope_q_bwd(dq_heads, cos, sin, name):
    T, W = dq_heads.shape
    tr = min(T, 512)
    even = pl.BlockSpec((tr, 128), lambda i, j: (i, 2 * j))
    odd = pl.BlockSpec((tr, 128), lambda i, j: (i, 2 * j + 1))
    tab = pl.BlockSpec((tr, 128), lambda i, j: (i, 0))

    def body(a_ref, b_ref, c_ref, s_ref, o_ref):
        d = a_ref[...] + b_ref[...]
        o_ref[...] = (d * c_ref[...] - _rot(d * s_ref[...])).astype(BF16)

    return pl.pallas_call(
        body, grid=(T // tr, W // 256), in_specs=[even, odd, tab, tab],
        out_specs=pl.BlockSpec((tr, 128), lambda i, j: (i, j)),
        out_shape=jax.ShapeDtypeStruct((T, W // 2), BF16), compiler_params=_params(("parallel", "parallel")), name=name,
    )(dq_heads, dq_heads, cos, sin)


def _rope_k_fwd(kpe, cos, sin, name):
    def body(x_ref, c_ref, s_ref, o_ref):
        x = x_ref[...]
        y = x * c_ref[...] + _rot(x) * s_ref[...]
        o_ref[...] = (y + pltpu.roll(y, ROPE, 1)).astype(BF16)

    return _rows(body, [(kpe, "row"), (cos, "row"), (sin, "row")], [(kpe.shape, BF16, "row")], name)[0]


def _rope_k_bwd(dk_heads, cos, sin, name):
    T, W = dk_heads.shape

    def body(d_ref, c_ref, s_ref, o_ref):
        d = d_ref[:, 0:128]
        for h in range(1, W // 128):
            d = d + d_ref[:, h * 128:(h + 1) * 128]
        d = d + pltpu.roll(d, ROPE, 1)
        dx = d * c_ref[...] - _rot(d * s_ref[...])
        lane = lax.broadcasted_iota(jnp.int32, dx.shape, 1)
        o_ref[...] = jnp.where(lane < ROPE, dx, 0.0).astype(BF16)

    return _rows(body, [(dk_heads, "row"), (cos, "row"), (sin, "row")], [((T, 128), BF16, "row")], name)[0]


def _attn_probs(q, k, qpe, kpe, scale, causal, q0):
    s = _dot(q, k, NT)
    if qpe is not None:
        s = s + _dot(qpe, kpe, NT)
    s = s * scale
    if causal:
        row = q0 + lax.broadcasted_iota(jnp.int32, s.shape, 0)
        col = lax.broadcasted_iota(jnp.int32, s.shape, 1)
        s = jnp.where((col // CHUNK) <= (row // CHUNK), s, NEG)
    p = jnp.exp(s - jnp.max(s, axis=-1, keepdims=True))
    return p / jnp.sum(p, axis=-1, keepdims=True)


def _pe_mask(x, h):
    lane = lax.broadcasted_iota(jnp.int32, x.shape, 1)
    return jnp.where((lane // ROPE) == (h % 2), x, jnp.zeros_like(x))


def _attn_fwd(q, k, v, scale, name, qpe=None, kpe=None, causal=False, pipe=None):
    T, W = q.shape
    Tk = k.shape[0]
    H = W // HEAD
    tq = min(T, 256)
    nq = T // tq
    has_pe = qpe is not None
    qs = pl.BlockSpec((tq, HEAD), lambda h, i: (i, h))
    ks = pl.BlockSpec((Tk, HEAD), lambda h, i: (0, h))
    in_specs, args = [qs, ks, ks], [q, k, v]
    if has_pe:
        in_specs += [pl.BlockSpec((tq, HEAD), lambda h, i: (i, h // 2)), pl.BlockSpec((Tk, HEAD), lambda h, i: (0, 0))]
        args += [qpe, kpe]

    def body(*refs):
        q_ref, k_ref, v_ref = refs[:3]
        o_ref = refs[-1]
        h, i = pl.program_id(0), pl.program_id(1)

        def compute(klen):
            qp = _pe_mask(refs[3][...], h) if has_pe else None
            kp = refs[4][0:klen, :] if has_pe else None
            p = _attn_probs(q_ref[...], k_ref[0:klen, :], qp, kp, scale, causal, i * tq)
            o_ref[...] = _dot(p.astype(BF16), v_ref[0:klen, :], NN).astype(BF16)

        if causal:
            for qi in range(nq):
                pl.when(i == qi)(functools.partial(compute, (qi + 1) * tq))
        else:
            compute(Tk)

    return _carry_call(
        body, grid=(H, nq), in_specs=in_specs, out_specs=qs, out_shape=jax.ShapeDtypeStruct((T, W), BF16),
        scratch_shapes=[], sem=("parallel", "parallel"), name=name, args=args, pipe=pipe,
    )


def _attn_bwd(q, k, v, do, scale, name, qpe=None, kpe=None, causal=False, pipe=None):
    T, W = q.shape
    Tk = k.shape[0]
    H = W // HEAD
    tq = min(T, 256)
    nq = T // tq
    has_pe = qpe is not None
    qs = pl.BlockSpec((tq, HEAD), lambda h, i: (i, h))
    ks = pl.BlockSpec((Tk, HEAD), lambda h, i: (0, h))
    in_specs, args = [qs, ks, ks, qs], [q, k, v, do]
    out_specs = [qs, ks, ks]
    out_shape = [jax.ShapeDtypeStruct((T, W), BF16), jax.ShapeDtypeStruct((Tk, W), BF16), jax.ShapeDtypeStruct((Tk, W), BF16)]
    scratch = [pltpu.VMEM((Tk, HEAD), F32), pltpu.VMEM((Tk, HEAD), F32)]
    if has_pe:
        in_specs += [pl.BlockSpec((tq, HEAD), lambda h, i: (i, h // 2)), pl.BlockSpec((Tk, HEAD), lambda h, i: (0, 0))]
        args += [qpe, kpe]
        out_specs += [qs, ks]
        out_shape += [jax.ShapeDtypeStruct((T, W), F32), jax.ShapeDtypeStruct((Tk, W), F32)]
        scratch += [pltpu.VMEM((Tk, HEAD), F32)]
    n_in = len(in_specs)

    def body(*refs):
        q_ref, k_ref, v_ref, do_ref = refs[:4]
        outs = refs[n_in:n_in + len(out_specs)]
        accs = refs[n_in + len(out_specs):]
        dq_ref, dk_ref, dv_ref = outs[:3]
        h, i = pl.program_id(0), pl.program_id(1)

        @pl.when(i == 0)
        def _():
            for acc in accs:
                acc[...] = jnp.zeros_like(acc)

        def compute(klen):
            qp = _pe_mask(refs[4][...], h) if has_pe else None
            kp = refs[5][0:klen, :] if has_pe else None
            qv, kv, vv, dov = q_ref[...], k_ref[0:klen, :], v_ref[0:klen, :], do_ref[...]
            p = _attn_probs(qv, kv, qp, kp, scale, causal, i * tq)
            dp = _dot(dov, vv, NT)
            ds = (p * (dp - jnp.sum(p * dp, axis=-1, keepdims=True)) * scale).astype(BF16)
            dq_ref[...] = _dot(ds, kv, NN).astype(BF16)
            accs[0][0:klen, :] += _dot(ds, qv, TN)
            accs[1][0:klen, :] += _dot(p.astype(BF16), dov, TN)
            if has_pe:
                outs[3][...] = _pe_mask(_dot(ds, kp, NN), h)
                accs[2][0:klen, :] += _dot(ds, qp, TN)

        if causal:
            for qi in range(nq):
                pl.when(i == qi)(functools.partial(compute, (qi + 1) * tq))
        else:
            compute(Tk)

        @pl.when(i == nq - 1)
        def _():
            dk_ref[...] = accs[0][...].astype(BF16)
            dv_ref[...] = accs[1][...].astype(BF16)
            if has_pe:
                outs[4][...] = accs[2][...]

    return _carry_call(
        body, grid=(H, nq), in_specs=in_specs, out_specs=out_specs, out_shape=out_shape, scratch_shapes=scratch,
        sem=("parallel", "arbitrary"), name=name, args=args, pipe=pipe,
    )


def _split3(x):
    hi = x.astype(BF16)
    r1 = x - hi.astype(F32)
    mid = r1.astype(BF16)
    lo = (r1 - mid.astype(F32)).astype(BF16)
    return hi, mid, lo


def _tri_dot(tri, x):
    hi, mid, lo = _split3(x)
    return _dot(tri, hi, NN) + _dot(tri, mid, NN) + _dot(tri, lo, NN)


def _hg_gates(u, lb):
    q, fr, v = u[:, 0:HEAD], u[:, HEAD:2 * HEAD], u[:, 2 * HEAD:3 * HEAD]
    sg = 1.0 / (1.0 + jnp.exp(-fr))
    sgm = 1.0 / (1.0 + jnp.exp(fr))
    f = lb + (1.0 - lb) * sg
    kin = (1.0 - lb) * sgm
    sq = _sigmoid(q)
    return q, v, sg, sgm, f, kin, sq, q * sq


def _hg_block_mats(blk):
    t = lax.broadcasted_iota(jnp.int32, (blk, blk), 0)
    s = lax.broadcasted_iota(jnp.int32, (blk, blk), 1)
    same = (t // SUB) == (s // SUB)
    one = lambda m: jnp.where(m, 1.0, 0.0).astype(BF16)
    return one(same & (s <= t)), one(same), one(same & (s >= t))


def _hg_stage(pairs, blk):
    for sc, val in pairs:
        sc[0:SUB, :] = jnp.zeros((SUB, HEAD), F32)
        sc[SUB:SUB + blk, :] = val


def _hg_scan_fwd(u, lb, name, pipe=None):
    T, W = u.shape
    H = W // (4 * HEAD)
    blk = min(T, HG_BLOCK)
    nb, nsb = T // blk, blk // SUB

    def body(u_ref, lb_ref, o_ref, st_ref, state, k_sc, b_sc, v_sc):
        @pl.when(pl.program_id(1) == 0)
        def _():
            state[...] = jnp.zeros_like(state)

        q, v, sg, sgm, f, kin, sq, qin = _hg_gates(u_ref[...], lb_ref[...])
        tri, ones, _ = _hg_block_mats(blk)
        logf = jnp.log(f)
        brel = _tri_dot(tri, logf)
        btot = _tri_dot(ones, logf)
        _hg_stage(((k_sc, kin), (b_sc, brel), (v_sc, v)), blk)
        sub_row = lax.broadcasted_iota(jnp.int32, (blk, HEAD), 0) % SUB
        o = jnp.zeros((blk, HEAD), F32)
        for d in range(SUB):
            win = slice(SUB - d, SUB - d + blk)
            e = jnp.exp(jnp.where(sub_row >= d, brel - b_sc[win, :], NEG))
            o = o + jnp.sum(qin * e * k_sc[win, :], axis=-1, keepdims=True) * v_sc[win, :]
        ab = (qin * jnp.exp(brel)).astype(BF16)
        kdb = (kin * jnp.exp(btot - brel)).astype(BF16)
        vb = v.astype(BF16)
        ebt = jnp.exp(btot)
        st = state[...]
        st_ref[...] = st
        for i in range(nsb):
            sl = slice(i * SUB, (i + 1) * SUB)
            o_ref[sl, :] = o[sl] + _dot(ab[sl], st.astype(BF16), NT)
            st = ebt[i * SUB:i * SUB + 1, :] * st + _dot(vb[sl], kdb[sl], TN)
        state[...] = st

    return _carry_call(
        body, grid=(H, nb),
        in_specs=[pl.BlockSpec((blk, 4 * HEAD), lambda h, c: (c, h)), pl.BlockSpec((1, HEAD), lambda h, c: (0, h))],
        out_specs=[pl.BlockSpec((blk, HEAD), lambda h, c: (c, h)), pl.BlockSpec((None, None, HEAD, HEAD), lambda h, c: (h, c, 0, 0))],
        out_shape=[jax.ShapeDtypeStruct((T, H * HEAD), F32), jax.ShapeDtypeStruct((H, nb, HEAD, HEAD), F32)],
        scratch_shapes=[pltpu.VMEM((HEAD, HEAD), F32)] + [pltpu.VMEM((SUB + blk, HEAD), F32)] * 3,
        sem=("parallel", "arbitrary"), name=name, args=(u, lb), pipe=pipe,
    )


def _hg_scan_bwd(u, lb, do, dog, states, name, pipe=None):
    T, W = u.shape
    H = W // (4 * HEAD)
    blk = min(T, HG_BLOCK)
    NC, nsb = T // blk, blk // SUB

    def body(u_ref, lb_ref, do_ref, dog_ref, st_ref, du_ref, dlb_ref, dstate, s_all, k_sc, b_sc, v_sc, dk_sc, dbn_sc,
             dv_sc, da_sc, dkd_sc, dvs_sc, dbt_sc):
        @pl.when(pl.program_id(1) == 0)
        def _():
            dstate[...] = jnp.zeros_like(dstate)
            dlb_ref[...] = jnp.zeros_like(dlb_ref)

        lb = lb_ref[...]
        q, v, sg, sgm, f, kin, sq, qin = _hg_gates(u_ref[...], lb)
        tri, ones, tri_t = _hg_block_mats(blk)
        logf = jnp.log(f)
        brel = _tri_dot(tri, logf)
        btot = _tri_dot(ones, logf)
        eb, ekd, ebt = jnp.exp(brel), jnp.exp(btot - brel), jnp.exp(btot)
        a, kd = qin * eb, kin * ekd
        ab, kdb, vb = a.astype(BF16), kd.astype(BF16), v.astype(BF16)
        do = do_ref[...]
        dob = do.astype(BF16)
        st = st_ref[...]
        for i in range(nsb):
            sl = slice(i * SUB, (i + 1) * SUB)
            s_all[i] = st
            st = ebt[i * SUB:i * SUB + 1, :] * st + _dot(vb[sl], kdb[sl], TN)
        ds = dstate[...]
        for i in reversed(range(nsb)):
            sl = slice(i * SUB, (i + 1) * SUB)
            st_i = s_all[i]
            dsb = ds.astype(BF16)
            e_i = ebt[i * SUB:i * SUB + 1, :]
            da_sc[sl, :] = _dot(dob[sl], st_i.astype(BF16), NN)
            dvs_sc[sl, :] = _dot(kdb[sl], dsb, NT)
            dkd_sc[sl, :] = _dot(vb[sl], dsb, NN)
            dbt_sc[sl, :] = jnp.broadcast_to(jnp.sum(ds * st_i, axis=0, keepdims=True) * e_i, (SUB, HEAD))
            ds = e_i * ds + _dot(dob[sl], ab[sl], TN)
        dstate[...] = ds
        da, dkd = da_sc[...], dkd_sc[...]
        t1 = dkd * kd
        dqin = da * eb
        dbrel = da * a - t1
        dkin = dkd * ekd
        dbtot = dbt_sc[...] + _tri_dot(ones, t1)
        _hg_stage(((k_sc, kin), (b_sc, brel), (v_sc, v)), blk)
        for sc in (dk_sc, dbn_sc, dv_sc):
            sc[...] = jnp.zeros_like(sc)
        sub_row = lax.broadcasted_iota(jnp.int32, (blk, HEAD), 0) % SUB
        for d in range(SUB):
            win = slice(SUB - d, SUB - d + blk)
            ks = k_sc[win, :]
            e = jnp.exp(jnp.where(sub_row >= d, brel - b_sc[win, :], NEG))
            qe = qin * e
            col = jnp.sum(qe * ks, axis=-1, keepdims=True)
            dcol = jnp.sum(do * v_sc[win, :], axis=-1, keepdims=True)
            dqe = dcol * qe
            g = dqe * ks
            dqin = dqin + dcol * (e * ks)
            dbrel = dbrel + g
            dk_sc[win, :] += dqe
            dbn_sc[win, :] += g
            dv_sc[win, :] += col * do
        dkin = dkin + dk_sc[SUB:SUB + blk, :]
        dbrel = dbrel - dbn_sc[SUB:SUB + blk, :]
        dv = dvs_sc[...] + dv_sc[SUB:SUB + blk, :]
        dlogf = _tri_dot(tri_t, dbrel) + dbtot
        diff = dlogf / f - dkin
        dlb_ref[...] += jnp.sum(sgm * diff, axis=0, keepdims=True)
        du_ref[:, 0:HEAD] = (dqin * (sq * (1.0 + q * (1.0 - sq)))).astype(BF16)
        du_ref[:, HEAD:2 * HEAD] = ((1.0 - lb) * sg * sgm * diff).astype(BF16)
        du_ref[:, 2 * HEAD:3 * HEAD] = dv.astype(BF16)
        du_ref[:, 3 * HEAD:4 * HEAD] = dog_ref[...]

    rev = lambda h, c: (NC - 1 - c, h)
    return _carry_call(
        body, grid=(H, NC),
        in_specs=[
            pl.BlockSpec((blk, 4 * HEAD), rev), pl.BlockSpec((1, HEAD), lambda h, c: (0, h)),
            pl.BlockSpec((blk, HEAD), rev), pl.BlockSpec((blk, HEAD), rev),
            pl.BlockSpec((None, None, HEAD, HEAD), lambda h, c: (h, NC - 1 - c, 0, 0)),
        ],
        out_specs=[pl.BlockSpec((blk, 4 * HEAD), rev), pl.BlockSpec((1, HEAD), lambda h, c: (0, h))],
        out_shape=[jax.ShapeDtypeStruct((T, W), BF16), jax.ShapeDtypeStruct((1, H * HEAD), F32)],
        scratch_shapes=[pltpu.VMEM((HEAD, HEAD), F32), pltpu.VMEM((nsb, HEAD, HEAD), F32)]
        + [pltpu.VMEM((SUB + blk, HEAD), F32)] * 6 + [pltpu.VMEM((blk, HEAD), F32)] * 4,
        sem=("parallel", "arbitrary"), name=name, args=(u, lb, do, dog, states), pipe=pipe,
    )


def _hg_tail_fwd(o_raw, u, g, name):
    T, D = o_raw.shape
    H = D // HEAD
    tr = min(T, 512)
    blk = pl.BlockSpec((tr, HEAD), lambda h, i: (i, h))

    def body(o_ref, og_ref, g_ref, out_ref):
        o, og = o_ref[...], og_ref[...]
        out_ref[...] = (o * _rstd(o) * g_ref[...] * (og * _sigmoid(og))).astype(BF16)

    return pl.pallas_call(
        body, grid=(H, T // tr),
        in_specs=[blk, pl.BlockSpec((tr, HEAD), lambda h, i: (i, 4 * h + 3)), pl.BlockSpec((1, HEAD), lambda h, i: (0, h))],
        out_specs=blk, out_shape=jax.ShapeDtypeStruct((T, D), BF16),
        compiler_params=_params(("parallel", "parallel")), name=name,
    )(o_raw, u, g)


def _hg_tail_bwd(o_raw, u, g, doa, name):
    T, D = o_raw.shape
    H = D // HEAD
    tr = min(T, 512)
    blk = pl.BlockSpec((tr, HEAD), lambda h, i: (i, h))
    vec = pl.BlockSpec((1, HEAD), lambda h, i: (0, h))

    def body(o_ref, og_ref, g_ref, doa_ref, do_ref, dog_ref, dg_ref):
        @pl.when(pl.program_id(1) == 0)
        def _():
            dg_ref[...] = jnp.zeros_like(dg_ref)

        o, og, doa, g = o_ref[...], og_ref[...], doa_ref[...], g_ref[...]
        sg = _sigmoid(og)
        r = _rstd(o)
        xhat = o * r
        dog_ref[...] = (doa * (xhat * g) * (sg * (1.0 + og * (1.0 - sg)))).astype(BF16)
        dn = doa * (og * sg)
        dg_ref[...] += jnp.sum(dn * xhat, axis=0, keepdims=True)
        dxh = dn * g
        do_ref[...] = r * (dxh - xhat * jnp.mean(dxh * xhat, axis=-1, keepdims=True))

    return pl.pallas_call(
        body, grid=(H, T // tr),
        in_specs=[blk, pl.BlockSpec((tr, HEAD), lambda h, i: (i, 4 * h + 3)), vec, blk],
        out_specs=[blk, blk, vec],
        out_shape=[jax.ShapeDtypeStruct((T, D), F32), jax.ShapeDtypeStruct((T, D), BF16), jax.ShapeDtypeStruct((1, D), F32)],
        compiler_params=_params(("parallel", "arbitrary")), name=name,
    )(o_raw, u, g, doa)


def _lb_fwd(logits, name):
    def body(l_ref, o_ref):
        l0, l1 = l_ref[0:1, :], l_ref[1:2, :]
        m = jnp.maximum(l0, l1)
        e0, e1 = jnp.exp(l0 - m), jnp.exp(l1 - m)
        o_ref[...] = e0 / (e0 + e1)

    D = logits.shape[1]
    return pl.pallas_call(body, out_shape=jax.ShapeDtypeStruct((1, D), F32), name=name)(logits)


def _lb_bwd(lb, dlb, name):
    def body(lb_ref, d_ref, o_ref):
        lb = lb_ref[...]
        d0 = d_ref[...] * lb * (1.0 - lb)
        o_ref[0:1, :] = d0
        o_ref[1:2, :] = -d0

    D = lb.shape[1]
    return pl.pallas_call(body, out_shape=jax.ShapeDtypeStruct((2, D), F32), name=name)(lb, dlb)


def _slots(w):
    return w.shape[0] if w.ndim == 3 else 0


def _ffn_fwd(x, p, tag, carry):
    mm = lambda a, b, mode, dt, name, **kw: _mm(a, b, mode, dt, name, pipe=carry.get(name), **kw)
    hb = _norm_fwd(x, p["pre_g"], f"{tag}_pre_norm", pipe=carry.get(f"{tag}_pre_norm"))
    a = mm(hb, p["w_gate"], "nn", F32, f"{tag}_gate")
    b = mm(hb, p["w_up"], "nn", F32, f"{tag}_up")
    sb = _swiglu_fwd(a, b, f"{tag}_swiglu", pipe=carry.get(f"{tag}_swiglu"))
    y = mm(sb, p["w_down"], "nn", F32, f"{tag}_down")
    xo = _postnorm_fwd(x, y, p["post_g"], FFN_RESIDUAL_WEIGHT, f"{tag}_post_norm", pipe=carry.get(f"{tag}_post_norm"))
    return xo, (x, hb, a, b, sb, y)


def _ffn_bwd(dxo, p, saved, tag, carry, on_dw):
    mm = lambda a, b, mode, dt, name, **kw: _mm(a, b, mode, dt, name, pipe=carry.get(name), **kw)
    x, hb, a, b, sb, y = saved
    dyb, dpost = _norm_bwd(y, p["post_g"], dxo, FFN_RESIDUAL_WEIGHT, BF16, f"{tag}_post_norm_bwd",
                           pipe=carry.get(f"{tag}_post_norm_bwd"))
    dw_down = mm(sb, dyb, "tn", BF16, f"{tag}_down_dw")
    on_dw("down", {"w_down": dw_down})
    ds = mm(dyb, p["w_down"], "nt", F32, f"{tag}_down_dx")
    dab, dbb = _swiglu_bwd(a, b, ds, f"{tag}_swiglu_bwd", pipe=carry.get(f"{tag}_swiglu_bwd"))
    dw_gate = mm(hb, dab, "tn", BF16, f"{tag}_gate_dw", out_slots=_slots(p["w_gate"]))
    dw_up = mm(hb, dbb, "tn", BF16, f"{tag}_up_dw", out_slots=_slots(p["w_up"]))
    on_dw("gate_up", {"w_gate": dw_gate, "w_up": dw_up})
    dh = mm(dab, p["w_gate"], "nt", F32, f"{tag}_gate_dx")
    dh = mm(dbb, p["w_up"], "nt", F32, f"{tag}_up_dx", add=dh)
    dx, dpre = _norm_bwd(x, p["pre_g"], dh, 1.0, F32, f"{tag}_pre_norm_bwd", res=dxo, pipe=carry.get(f"{tag}_pre_norm_bwd"))
    return dx, {"pre_g": dpre, "w_gate": dw_gate, "w_up": dw_up, "w_down": dw_down, "post_g": dpost}


def _mixer_fwd(x, cos, sin, p, carry, rest_of):
    mm = lambda a, b, mode, dt, name, **kw: _mm(a, b, mode, dt, name, pipe=carry.get(name), **kw)
    scale = (HEAD + ROPE) ** -0.5
    hb = _norm_fwd(x, p["pre_g"], "mix_pre_norm")
    u = mm(hb, p["w_hg"], "nn", F32, "mix_in_hg")
    cq = mm(hb, p["w_cq"], "nn", F32, "mix_in_cq")
    ckv = mm(hb, p["w_ckv"], "nn", F32, "mix_in_ckv")
    kpe = mm(hb, p["w_kpe"], "nn", F32, "mix_in_kpe")
    ga = mm(hb, p["w_ga"], "nn", F32, "mix_in_ga")
    gb = mm(hb, p["w_gb"], "nn", F32, "mix_in_gb")
    lb = _lb_fwd(p["lb_logits"], "hg_lb")
    o_raw, states = _hg_scan_fwd(u, lb, "hg_scan", pipe=carry.get("hg_scan"))
    oa = _hg_tail_fwd(o_raw, u, p["hg_norm_g"], "hg_tail")
    p = {**p, **rest_of()}
    ya = mm(oa, p["w_branch_a"], "nn", F32, "mix_branch_a")
    cqn = _norm_fwd(cq, p["q_norm_g"], "mla_q_norm")
    qn = mm(cqn, p["w_qn"], "nn", BF16, "mla_q_up_nope")
    qpe = _rope_q_fwd(mm(cqn, p["w_qpe"], "nn", F32, "mla_q_up_pe"), cos, sin, "mla_rope_q")
    ckvn = _norm_fwd(ckv, p["kv_norm_g"], "mla_kv_norm")
    kn = mm(ckvn, p["w_kn"], "nn", BF16, "mla_k_up")
    vv = mm(ckvn, p["w_vv"], "nn", BF16, "mla_v_up")
    kpe2 = _rope_k_fwd(kpe, cos, sin, "mla_rope_k")
    ob = _attn_fwd(qn, kn, vv, scale, "mla_attn", qpe=qpe, kpe=kpe2, causal=True, pipe=carry.get("mla_attn"))
    yb = mm(ob, p["w_branch_b"], "nn", F32, "mix_branch_b")
    ym = _merge_fwd(ga, gb, ya, yb, "mix_merge")
    z = mm(ym, p["w_out"], "nn", F32, "mix_out")
    xo = _postnorm_fwd(x, z, p["post_g"], 1.0, "mix_post_norm")
    saved = (x, hb, u, cq, ckv, ga, gb, lb, o_raw, states, oa, ya, cqn, qn, qpe, ckvn, kn, vv, kpe2, ob, yb, ym, z)
    return xo, saved, p


def _mixer_bwd(dxo, cos, sin, p, saved, carry, on_early, on_late):
    x, hb, u, cq, ckv, ga, gb, lb, o_raw, states, oa, ya, cqn, qn, qpe, ckvn, kn, vv, kpe2, ob, yb, ym, z = saved
    scale = (HEAD + ROPE) ** -0.5
    g = {}
    dzb, g["post_g"] = _norm_bwd(z, p["post_g"], dxo, 1.0, BF16, "mix_post_norm_bwd")
    g["w_out"] = _mm(ym, dzb, "tn", BF16, "mix_out_dw")
    dym = _mm(dzb, p["w_out"], "nt", F32, "mix_out_dx")
    dya, dyb, dga, dgb = _merge_bwd(ga, gb, ya, yb, dym, "mix_merge_bwd")
    g["w_branch_a"] = _mm(oa, dya, "tn", BF16, "mix_branch_a_dw")
    doa = _mm(dya, p["w_branch_a"], "nt", F32, "mix_branch_a_dx")
    do_raw, dog, g["hg_norm_g"] = _hg_tail_bwd(o_raw, u, p["hg_norm_g"], doa, "hg_tail_bwd")
    du, dlb = _hg_scan_bwd(u, lb, do_raw, dog, states, "hg_scan_bwd", pipe=carry.get("hg_scan_bwd"))
    g["lb_logits"] = _lb_bwd(lb, dlb, "hg_lb_bwd")
    g["w_branch_b"] = _mm(ob, dyb, "tn", BF16, "mix_branch_b_dw")
    on_early({k: g[k] for k in ("w_out", "w_branch_a", "w_branch_b")})
    dob = _mm(dyb, p["w_branch_b"], "nt", BF16, "mix_branch_b_dx")
    dqn, dkn, dvv, dqpe_h, dkpe_h = _attn_bwd(qn, kn, vv, dob, scale, "mla_attn_bwd", qpe=qpe, kpe=kpe2, causal=True,
                                              pipe=carry.get("mla_attn_bwd"))
    dqpe = _rope_q_bwd(dqpe_h, cos, sin, "mla_rope_q_bwd")
    dkpe = _rope_k_bwd(dkpe_h, cos, sin, "mla_rope_k_bwd")
    g["w_qn"] = _mm(cqn, dqn, "tn", BF16, "mla_q_up_nope_dw")
    g["w_qpe"] = _mm(cqn, dqpe, "tn", BF16, "mla_q_up_pe_dw")
    dcqn = _mm(dqn, p["w_qn"], "nt", F32, "mla_q_up_nope_dx")
    dcqn = _mm(dqpe, p["w_qpe"], "nt", F32, "mla_q_up_pe_dx", add=dcqn)
    dcq, g["q_norm_g"] = _norm_bwd(cq, p["q_norm_g"], dcqn, 1.0, BF16, "mla_q_norm_bwd")
    g["w_kn"] = _mm(ckvn, dkn, "tn", BF16, "mla_k_up_dw")
    g["w_vv"] = _mm(ckvn, dvv, "tn", BF16, "mla_v_up_dw")
    dckvn = _mm(dkn, p["w_kn"], "nt", F32, "mla_k_up_dx")
    dckvn = _mm(dvv, p["w_vv"], "nt", F32, "mla_v_up_dx", add=dckvn)
    dckv, g["kv_norm_g"] = _norm_bwd(ckv, p["kv_norm_g"], dckvn, 1.0, BF16, "mla_kv_norm_bwd")
    parts = (("w_hg", du), ("w_cq", dcq), ("w_ckv", dckv), ("w_kpe", dkpe), ("w_ga", dga), ("w_gb", dgb))
    for key, d in parts:
        g[key] = _mm(hb, d, "tn", BF16, f"mix_in_{key}_dw")
    on_late(g)
    dh = None
    for key, d in parts:
        dh = _mm(d, p[key], "nt", F32, f"mix_in_{key}_dx", add=dh, pipe=carry.get(f"mix_in_{key}_dx"))
    dx, g["pre_g"] = _norm_bwd(x, p["pre_g"], dh, 1.0, F32, "mix_pre_norm_bwd", res=dxo, pipe=carry.get("mix_pre_norm_bwd"))
    return dx, g


def _xa_fwd(x, mem, p, carry):
    mm = lambda a, b, mode, dt, name, **kw: _mm(a, b, mode, dt, name, pipe=carry.get(name), **kw)
    scale = HEAD ** -0.5
    hb = _norm_fwd(x, p["pre_g"], "xa_pre_norm")
    mb = _norm_fwd(mem, p["mem_g"], "xa_mem_norm")
    q = mm(hb, p["w_q"], "nn", BF16, "xa_q")
    k = mm(mb, p["w_k"], "nn", BF16, "xa_k")
    v = mm(mb, p["w_v"], "nn", BF16, "xa_v")
    o = _attn_fwd(q, k, v, scale, "xa_attn", pipe=carry.get("xa_attn"))
    z = mm(o, p["w_o"], "nn", F32, "xa_o")
    xo = _postnorm_fwd(x, z, p["post_g"], 1.0, "xa_post_norm")
    return xo, (x, mem, hb, mb, q, k, v, o, z)


def _xa_bwd(dxo, p, saved):
    x, mem, hb, mb, q, k, v, o, z = saved
    scale = HEAD ** -0.5
    g = {}
    dzb, g["post_g"] = _norm_bwd(z, p["post_g"], dxo, 1.0, BF16, "xa_post_norm_bwd")
    g["w_o"] = _mm(o, dzb, "tn", BF16, "xa_o_dw", out_slots=_slots(p["w_o"]))
    do = _mm(dzb, p["w_o"], "nt", BF16, "xa_o_dx")
    dq, dk, dv = _attn_bwd(q, k, v, do, scale, "xa_attn_bwd")
    g["w_q"] = _mm(hb, dq, "tn", BF16, "xa_q_dw")
    g["w_k"] = _mm(mb, dk, "tn", BF16, "xa_k_dw")
    g["w_v"] = _mm(mb, dv, "tn", BF16, "xa_v_dw")
    dh = _mm(dq, p["w_q"], "nt", F32, "xa_q_dx")
    dm = _mm(dk, p["w_k"], "nt", F32, "xa_k_dx")
    dm = _mm(dv, p["w_v"], "nt", F32, "xa_v_dx", add=dm)
    _, g["mem_g"] = _norm_bwd(mem, p["mem_g"], dm, 1.0, BF16, "xa_mem_norm_bwd")
    dx, g["pre_g"] = _norm_bwd(x, p["pre_g"], dh, 1.0, F32, "xa_pre_norm_bwd", res=dxo)
    return dx, g


def _local_step(x, mem, cos, sin, target, params_of, carry, on_grads):
    p1 = params_of("ffn1")
    x1, s1 = _ffn_fwd(x, p1, "ffn1", carry)
    x2, s2, p2 = _mixer_fwd(x1, cos, sin, params_of("mix"), carry, lambda: params_of("mix_rest"))
    p3 = params_of("xa")
    x3, s3 = _xa_fwd(x2, mem, p3, carry)
    p4 = params_of("ffn2")
    x4, s4 = _ffn_fwd(x3, p4, "ffn2", carry)
    dy, sq_err = _loss_head(x4, target, "loss_head")
    loss = 0.5 / x.shape[1] * jnp.sum(sq_err)
    dx, g4 = _ffn_bwd(dy, p4, s4, "ffn2", carry, lambda stage, g: on_grads(f"ffn2_dw_{stage}", g))
    on_grads("ffn2", g4)
    dx, g3 = _xa_bwd(dx, p3, s3)
    on_grads("xa", g3)
    dx, g2 = _mixer_bwd(dx, cos, sin, p2, s2, carry, lambda g: on_grads("mix_early", g), lambda g: on_grads("mix_late", g))
    on_grads("mix", g2)
    dx, g1 = _ffn_bwd(dx, p1, s1, "ffn1", carry, lambda stage, g: on_grads(f"ffn1_dw_{stage}", g))
    on_grads("ffn1", g1)
    return loss, dx


def _split_w_in(w_in):
    D = w_in.shape[0]
    H = D // HEAD
    lora = (w_in.shape[1] - 6 * D - ROPE) // 2
    o = 4 * D
    w_hg = w_in[:, :o].reshape(D, 4, H, HEAD).transpose(0, 2, 1, 3).reshape(D, 4 * D)
    w_cq, w_ckv = w_in[:, o:o + lora], w_in[:, o + lora:o + 2 * lora]
    o += 2 * lora
    w_kpe = jnp.pad(w_in[:, o:o + ROPE], ((0, 0), (0, HEAD - ROPE)))
    o += ROPE
    return {"w_hg": w_hg, "w_cq": w_cq, "w_ckv": w_ckv, "w_kpe": w_kpe, "w_ga": w_in[:, o:o + D], "w_gb": w_in[:, o + D:o + 2 * D]}


def _merge_w_in(g):
    D = g["w_ga"].shape[0]
    H = D // HEAD
    hg = g["w_hg"].reshape(D, H, 4, HEAD).transpose(0, 2, 1, 3).reshape(D, 4 * D)
    return jnp.concatenate([hg, g["w_cq"], g["w_ckv"], g["w_kpe"][:, :ROPE], g["w_ga"], g["w_gb"]], axis=1)


def _split_heads(w, rest):
    K, N = w.shape
    w3 = w.reshape(K, N // (HEAD + rest), HEAD + rest)
    return w3[:, :, :HEAD].reshape(K, -1), w3[:, :, HEAD:].reshape(K, -1)


def _merge_heads(a, b, rest):
    K = a.shape[0]
    H = a.shape[1] // HEAD
    return jnp.concatenate([a.reshape(K, H, HEAD), b.reshape(K, H, rest)], axis=2).reshape(K, H * (HEAD + rest))


BLOCK_WEIGHTS = {
    "ffn1": ("ffn1_w_gate", "ffn1_w_up", "ffn1_w_down"),
    "mix": ("w_in",),
    "mix_rest": ("mla_w_q_up", "mla_w_kv_up", "w_branch_a", "w_branch_b", "w_out"),
    "xa": ("xa_w_q", "xa_w_k", "xa_w_v", "xa_w_o"),
    "ffn2": ("ffn2_w_gate", "ffn2_w_up", "ffn2_w_down"),
}


def _block_params(block, w, small):
    if block in ("ffn1", "ffn2"):
        return {"pre_g": small[f"{block}_pre_g"], "w_gate": w[f"{block}_w_gate"], "w_up": w[f"{block}_w_up"],
                "w_down": w[f"{block}_w_down"], "post_g": small[f"{block}_post_g"]}
    if block == "xa":
        return {"pre_g": small["xa_pre_g"], "mem_g": small["xa_mem_g"], "post_g": small["xa_post_g"],
                "w_q": w["xa_w_q"], "w_k": w["xa_w_k"], "w_v": w["xa_w_v"], "w_o": w["xa_w_o"]}
    if block == "mix_rest":
        (w_qn, w_qpe), (w_kn, w_vv) = _split_heads(w["mla_w_q_up"], ROPE), _split_heads(w["mla_w_kv_up"], HEAD)
        return dict(w_qn=w_qn, w_qpe=w_qpe, w_kn=w_kn, w_vv=w_vv, w_branch_a=w["w_branch_a"],
                    w_branch_b=w["w_branch_b"], w_out=w["w_out"])
    mix = _split_w_in(w["w_in"])
    mix.update(pre_g=small["mix_pre_g"], post_g=small["mix_post_g"], hg_norm_g=small["hg_norm_g"],
               q_norm_g=small["mla_q_norm_g"], kv_norm_g=small["mla_kv_norm_g"], lb_logits=small["hgrn_lb_logits"])
    return mix


def _block_grads(block, g):
    if block in ("ffn1", "ffn2"):
        return {f"{block}_{k}": g[k] for k in ("pre_g", "w_gate", "w_up", "w_down", "post_g")}
    if block == "xa":
        return {f"xa_{k}": g[k] for k in ("pre_g", "mem_g", "post_g", "w_q", "w_k", "w_v", "w_o")}
    if block[4:8] == "_dw_":
        return {f"{block[:4]}_{k}": v for k, v in g.items()}
    if block == "mix_early":
        return dict(g)
    if block == "mix_late":
        return dict(w_in=_merge_w_in(g), mla_w_q_up=_merge_heads(g["w_qn"], g["w_qpe"], ROPE),
                    mla_w_kv_up=_merge_heads(g["w_kn"], g["w_vv"], HEAD))
    return dict(mix_pre_g=g["pre_g"], mix_post_g=g["post_g"],
                hg_norm_g=g["hg_norm_g"], mla_q_norm_g=g["q_norm_g"], mla_kv_norm_g=g["kv_norm_g"],
                hgrn_lb_logits=g["lb_logits"])


def _rope_tables(positions):
    inv_freq = 1.0 / (ROPE_THETA ** (jnp.arange(0, ROPE, 2, dtype=F32) / ROPE))
    ang = positions.astype(F32)[:, None] * inv_freq
    return jnp.tile(jnp.cos(ang), (1, 4)), jnp.tile(jnp.sin(ang), (1, 4))


def _adamw(w, g, m, v, name):
    bc1 = 1.0 - ADAM_B1 ** ADAM_STEP
    bc2 = 1.0 - ADAM_B2 ** ADAM_STEP

    def body(w_ref, g_ref, m_ref, v_ref, go_ref, d_ref, mo_ref, vo_ref):
        g = g_ref[...]
        m = ADAM_B1 * m_ref[...] + (1.0 - ADAM_B1) * g
        v = ADAM_B2 * v_ref[...] + (1.0 - ADAM_B2) * (g * g)
        go_ref[...] = g
        mo_ref[...] = m
        vo_ref[...] = v
        d_ref[...] = -ADAM_LR * ((m / bc1) / (jnp.sqrt(v / bc2) + ADAM_EPS) + ADAM_WD * w_ref[...])

    return _rows(body, [(w, "row"), (g, "row"), (m, "row"), (v, "row")], [(w.shape, F32, "row")] * 4, name)


ANY = pl.BlockSpec(memory_space=pl.ANY)
COMM_AXES = ("x", "y", "c")


def _place():
    x, y, c = (lax.axis_index(n) for n in COMM_AXES)
    chips = [(1 - x, y), (x, 1 - y), (1 - x, 1 - y)]
    return x, y, c, 2 * x + y, (x, y, 1 - c), chips


def _remote(src, dst, send_sems, recv_sems, j, to):
    return pltpu.make_async_remote_copy(src_ref=src, dst_ref=dst, send_sem=send_sems.at[j], recv_sem=recv_sems.at[j],
                                        device_id=to, device_id_type=MESH)


def _dma_sems(n):
    return [pltpu.SemaphoreType.DMA((n,)), pltpu.SemaphoreType.DMA((n,))]


def _all_gather(shards, whole, name):
    n = len(shards)

    def body(*refs):
        srcs, outs, send_sems, recv_sems = refs[:n], refs[n:2 * n], refs[2 * n], refs[2 * n + 1]
        x, y, c, me, sibling, chips = _place()
        sent = []

        def start(cp):
            cp.start()
            sent.append(cp)

        def rows(w, h):
            hr = srcs[w].shape[0] // 2
            return pl.ds(h * hr, hr)

        gathered = [w for w in range(n) if whole[w]]
        for w in gathered:
            for j, (cx, cy) in enumerate(chips):
                start(_remote(srcs[w].at[rows(w, c)], outs[w].at[me, rows(w, c)], send_sems, recv_sems, 7 * w + j, (cx, cy, c)))
        for w in range(n):
            start(_remote(srcs[w], outs[w].at[me], send_sems, recv_sems, 7 * w + 6, sibling))
        for w in gathered:
            for j, (cx, cy) in enumerate(chips):
                blk = outs[w].at[2 * cx + cy, rows(w, c)]
                _remote(srcs[w].at[rows(w, c)], blk, send_sems, recv_sems, 7 * w + j, (cx, cy, c)).wait_recv()
                start(_remote(blk, blk, send_sems, recv_sems, 7 * w + 3 + j, sibling))
        for w in gathered:
            for j, (cx, cy) in enumerate(chips):
                blk = outs[w].at[2 * cx + cy, rows(w, 1 - c)]
                _remote(blk, blk, send_sems, recv_sems, 7 * w + 3 + j, sibling).wait_recv()
        for w in range(n):
            _remote(srcs[w], outs[w].at[me], send_sems, recv_sems, 7 * w + 6, sibling).wait_recv()
        for cp in sent:
            cp.wait_send()

    return pl.pallas_call(
        body, in_specs=[ANY] * n, out_specs=[ANY] * n,
        out_shape=[jax.ShapeDtypeStruct((N_CHIP,) + s.shape, s.dtype) for s in shards],
        scratch_shapes=_dma_sems(7 * n), name=name,
    )(*shards)


def _rs_swap(grads, name):
    n = len(grads)

    def body(*refs):
        gs, outs, send_sems, recv_sems = refs[:n], refs[n:2 * n], refs[2 * n], refs[2 * n + 1]
        x, y, c, me, sibling, chips = _place()
        cps = []
        for w in range(n):
            hr = gs[w].shape[1] // 2
            cps.append(_remote(gs[w].at[:, pl.ds((1 - c) * hr, hr)], outs[w], send_sems, recv_sems, w, sibling))
            cps[-1].start()
        for cp in cps:
            cp.wait()

    return pl.pallas_call(
        body, in_specs=[ANY] * n, out_specs=[ANY] * n,
        out_shape=[jax.ShapeDtypeStruct((g.shape[0], g.shape[1] // 2, g.shape[2]), g.dtype) for g in grads],
        scratch_shapes=_dma_sems(n), name=name,
    )(*grads)


def _sum_rows(hr, row_bytes):
    return _tile(hr, max(16, ROW_BUDGET // (2 * row_bytes) // 16 * 16), 16)


def _rs_pair_sum(g, got, c, name):
    S, r, cw = g.shape
    hr = r // 2
    tr = _sum_rows(hr, 3 * cw * 2)
    nrb = hr // tr

    def body(c_ref, a_ref, b_ref, o_ref):
        o_ref[...] = (a_ref[...].astype(F32) + b_ref[...].astype(F32)).astype(BF16)

    return pl.pallas_call(
        body,
        grid_spec=pltpu.PrefetchScalarGridSpec(
            num_scalar_prefetch=1, grid=(S, nrb),
            in_specs=[pl.BlockSpec((None, tr, cw), lambda k, i, c_ref: (k, c_ref[0] * nrb + i, 0)),
                      pl.BlockSpec((None, tr, cw), lambda k, i, c_ref: (k, i, 0))],
            out_specs=pl.BlockSpec((None, tr, cw), lambda k, i, c_ref: (k, i, 0)),
        ),
        out_shape=jax.ShapeDtypeStruct((S, hr, cw), BF16),
        compiler_params=_params(("parallel", "parallel")), name=name,
    )(c, g, got)


def _rs_chip_sum(pair, got, place, name):
    S, hr, cw = pair.shape
    tr = _sum_rows(hr, cw * (4 * 2 + 4))

    def body(p_ref, a_ref, z_ref, o_ref):
        o_ref[...] = a_ref[...].astype(F32) + z_ref[0].astype(F32) + z_ref[1].astype(F32) + z_ref[2].astype(F32)

    return pl.pallas_call(
        body,
        grid_spec=pltpu.PrefetchScalarGridSpec(
            num_scalar_prefetch=1, grid=(hr // tr,),
            in_specs=[pl.BlockSpec((None, tr, cw), lambda i, p_ref: (p_ref[0], i, 0)),
                      pl.BlockSpec((3, tr, cw), lambda i, p_ref: (0, i, 0))],
            out_specs=pl.BlockSpec((None, tr, cw), lambda i, p_ref: (p_ref[1], i, 0)),
        ),
        out_shape=jax.ShapeDtypeStruct((2, hr, cw), F32),
        compiler_params=_params(("parallel",)), name=name,
    )(place, pair, got)


def _rs_share(halves, name):
    n = len(halves)

    def body(*refs):
        outs, send_sems, recv_sems = refs[n:2 * n], refs[2 * n], refs[2 * n + 1]
        x, y, c, me, sibling, chips = _place()
        cps = []
        for w in range(n):
            cps.append(_remote(outs[w].at[c], outs[w].at[c], send_sems, recv_sems, w, sibling))
            cps[-1].start()
        for w in range(n):
            _remote(outs[w].at[1 - c], outs[w].at[1 - c], send_sems, recv_sems, w, sibling).wait_recv()
        for cp in cps:
            cp.wait_send()

    return pl.pallas_call(
        body, in_specs=[ANY] * n, out_specs=[ANY] * n,
        out_shape=[jax.ShapeDtypeStruct(h.shape, h.dtype) for h in halves],
        input_output_aliases={i: i for i in range(n)},
        scratch_shapes=_dma_sems(n), name=name,
    )(*halves)


def _rs_begin(names, grads, core, fractions, tag):
    slotted = [_to_slots(n, grads[n]) for n in names]
    got = _rs_swap(slotted, f"grads_sibling_swap_{tag}")
    pairs = [_rs_pair_sum(s, t, core.reshape(1), f"grads_pair_sum_{n}") for n, s, t in zip(names, slotted, got)]
    lands = [lax.empty((N_CHIP - 1,) + p.shape[1:], p.dtype) for p in pairs]
    return pairs, _Pipe("reduce", pairs, lands, fractions)


def _rs_end(names, pairs, pipe, place, shapes, tag):
    _pipe_flush(pipe, f"grads_chip_exchange_rest_{tag}")
    halves = [_rs_chip_sum(p, o, place, f"grads_chip_sum_{n}") for n, p, o in zip(names, pairs, pipe.lands)]
    both = _rs_share(halves, f"grads_sibling_share_{tag}")
    return {n: b.reshape(s) for n, b, s in zip(names, both, shapes)}


def _all_reduce_small(s, name):
    flips = [(dx, dy, dc) for dx in (0, 1) for dy in (0, 1) for dc in (0, 1) if (dx, dy, dc) != (0, 0, 0)]

    def body(s_ref, o_ref, buf, send_sems, recv_sems):
        x, y, c = (lax.axis_index(n) for n in COMM_AXES)
        me = 4 * x + 2 * y + c
        buf[me] = s_ref[...]
        peers = [((1 - x) if dx else x, (1 - y) if dy else y, (1 - c) if dc else c) for dx, dy, dc in flips]
        sent = [_remote(s_ref, buf.at[me], send_sems, recv_sems, j, p) for j, p in enumerate(peers)]
        for cp in sent:
            cp.start()
        for j, (px, py, pc) in enumerate(peers):
            _remote(s_ref, buf.at[4 * px + 2 * py + pc], send_sems, recv_sems, j, (px, py, pc)).wait_recv()
        for cp in sent:
            cp.wait_send()
        acc = buf[0]
        for d in range(1, N_DEV):
            acc = acc + buf[d]
        o_ref[...] = acc

    vmem = pl.BlockSpec(memory_space=pltpu.VMEM)
    return pl.pallas_call(
        body, in_specs=[vmem], out_specs=vmem, out_shape=jax.ShapeDtypeStruct(s.shape, F32),
        scratch_shapes=[pltpu.VMEM((N_DEV,) + s.shape, F32), pltpu.SemaphoreType.DMA((7,)), pltpu.SemaphoreType.DMA((7,))],
        name=name,
    )(s)


BIG = {
    "ffn1_w_gate": 1, "ffn1_w_up": 1, "ffn1_w_down": 0, "w_in": 1, "mla_w_q_up": 1, "mla_w_kv_up": 1,
    "w_branch_a": 0, "w_branch_b": 0, "w_out": 0, "xa_w_q": 0, "xa_w_k": 0, "xa_w_v": 0, "xa_w_o": 1,
    "ffn2_w_gate": 1, "ffn2_w_up": 1, "ffn2_w_down": 0,
}
WEIGHTS = [
    "hgrn_lb_logits", "ffn1_pre_g", "ffn1_w_gate", "ffn1_w_up", "ffn1_w_down", "ffn1_post_g", "mix_pre_g", "w_in",
    "hg_norm_g", "mla_q_norm_g", "mla_w_q_up", "mla_kv_norm_g", "mla_w_kv_up", "w_branch_a", "w_branch_b", "w_out",
    "mix_post_g", "xa_pre_g", "xa_mem_g", "xa_w_q", "xa_w_k", "xa_w_v", "xa_w_o", "xa_post_g", "ffn2_pre_g",
    "ffn2_w_gate", "ffn2_w_up", "ffn2_w_down", "ffn2_post_g",
]
SMALL = [n for n in WEIGHTS if n not in BIG]
SLOTTED = ("ffn1_w_gate", "ffn1_w_up", "ffn2_w_gate", "ffn2_w_up", "xa_w_o")
MIX_EARLY = ("w_out", "w_branch_a", "w_branch_b")


def _from_slots(name, g):
    S, r, cw = g.shape
    if BIG[name] == 0:
        return g.reshape(S * r, cw)
    return g if name in SLOTTED else g.transpose(1, 0, 2).reshape(r, S * cw)


def _to_slots(name, g):
    if g.ndim == 3:
        return g
    if BIG[name] == 0:
        return g.reshape(N_CHIP, g.shape[0] // N_CHIP, g.shape[1])
    return g.reshape(g.shape[0], N_CHIP, g.shape[1] // N_CHIP).transpose(1, 0, 2)


def _pack_small(vals, width):
    rows = [jnp.pad(v, ((0, 0), (0, width - v.shape[1]))) for v in vals]
    s = jnp.concatenate(rows, axis=0)
    return jnp.pad(s, ((0, -s.shape[0] % 8), (0, 0)))


def _unpack_small(s, shapes):
    out, o = [], 0
    for r, w in shapes:
        out.append(s[o:o + r, :w])
        o += r
    return out


def kernel(x, mem, positions, hgrn_lb_logits, ffn1_pre_g, ffn1_w_gate, ffn1_w_up, ffn1_w_down, ffn1_post_g, mix_pre_g, w_in, hg_norm_g, mla_q_norm_g, mla_w_q_up, mla_kv_norm_g, mla_w_kv_up, w_branch_a, w_branch_b, w_out, mix_post_g, xa_pre_g, xa_mem_g, xa_w_q, xa_w_k, xa_w_v, xa_w_o, xa_post_g, ffn2_pre_g, ffn2_w_gate, ffn2_w_up, ffn2_w_down, ffn2_post_g, loss_target, m_hgrn_lb_logits, m_ffn1_pre_g, m_ffn1_w_gate, m_ffn1_w_up, m_ffn1_w_down, m_ffn1_post_g, m_mix_pre_g, m_w_in, m_hg_norm_g, m_mla_q_norm_g, m_mla_w_q_up, m_mla_kv_norm_g, m_mla_w_kv_up, m_w_branch_a, m_w_branch_b, m_w_out, m_mix_post_g, m_xa_pre_g, m_xa_mem_g, m_xa_w_q, m_xa_w_k, m_xa_w_v, m_xa_w_o, m_xa_post_g, m_ffn2_pre_g, m_ffn2_w_gate, m_ffn2_w_up, m_ffn2_w_down, m_ffn2_post_g, v_hgrn_lb_logits, v_ffn1_pre_g, v_ffn1_w_gate, v_ffn1_w_up, v_ffn1_w_down, v_ffn1_post_g, v_mix_pre_g, v_w_in, v_hg_norm_g, v_mla_q_norm_g, v_mla_w_q_up, v_mla_kv_norm_g, v_mla_w_kv_up, v_w_branch_a, v_w_branch_b, v_w_out, v_mix_post_g, v_xa_pre_g, v_xa_mem_g, v_xa_w_q, v_xa_w_k, v_xa_w_v, v_xa_w_o, v_xa_post_g, v_ffn2_pre_g, v_ffn2_w_gate, v_ffn2_w_up, v_ffn2_w_down, v_ffn2_post_g):
    a = dict(locals())
    big = list(BIG)
    small = {n: a[n] for n in SMALL}
    core = lax.axis_index("c").astype(jnp.int32)
    place = jnp.stack([(2 * lax.axis_index("x") + lax.axis_index("y")).astype(jnp.int32), core])

    shards = {n: a[n][0].astype(BF16) for n in big}
    whole = [n in BLOCK_WEIGHTS["ffn1"] for n in big]
    lands = dict(zip(big, _all_gather([shards[n] for n in big], whole, "weights_all_gather")))
    plan = {
        "mix": ((2, 10, 10, 6, 10, 3), [f"ffn1_{k}" for k in ("pre_norm", "gate", "up", "swiglu", "down", "post_norm")]),
        "mix_rest": ((1,), ["hg_scan"]),
        "xa": ((1, 1), ["mix_in_ga", "mix_in_gb"]),
        "ffn2": ((8, 18, 3, 3, 3, 1), ["mix_in_hg", "mla_attn", "mix_branch_b", "mix_out", "xa_attn", "xa_o"]),
    }
    carry, pipes = {}, {}
    for block, (fractions, carriers) in plan.items():
        names = BLOCK_WEIGHTS[block]
        pipes[block] = _Pipe("gather", [shards[n] for n in names], [lands[n] for n in names], fractions)
        carry.update({c: pipes[block] for c in carriers})

    def params_of(block):
        if block in pipes:
            _pipe_flush(pipes[block], f"weights_gather_rest_{block}")
            lands.update(zip(BLOCK_WEIGHTS[block], pipes[block].lands))
        return _block_params(block, {n: _from_slots(n, lands[n]) for n in BLOCK_WEIGHTS[block]}, small)

    g_small, g_big, open_groups, held = {}, {}, {}, {}

    def begin(tag, names, fractions, carriers):
        pairs, pipe = _rs_begin(names, held, core, fractions, tag)
        open_groups[tag] = (names, pairs, pipe)
        carry.update({c: pipe for c in carriers})

    def end(tag):
        names, pairs, pipe = open_groups.pop(tag)
        g_big.update(_rs_end(names, pairs, pipe, place, [a[n].shape[1:] for n in names], tag))

    def on_grads(block, g):
        for n, v in _block_grads(block, g).items():
            (held if n in BIG else g_small)[n] = v
        if block == "ffn2":
            begin("ffn2", BLOCK_WEIGHTS["ffn2"], (1,), ("hg_scan_bwd",))
        elif block == "mix_early":
            begin("early", BLOCK_WEIGHTS["xa"] + MIX_EARLY, (1,), ("mla_attn_bwd",))
        elif block == "mix_late":
            end("ffn2")
            end("early")
            begin("mid", ["w_in", "mla_w_q_up", "mla_w_kv_up"], (10, 3, 3, 3, 2, 6, 6),
                  ("mix_in_w_hg_dx", "mix_in_w_ga_dx", "mix_in_w_gb_dx", "mix_pre_norm_bwd",
                   "ffn1_post_norm_bwd", "ffn1_down_dw", "ffn1_down_dx"))
        elif block == "ffn1_dw_down":
            begin("ffn1_down", ["ffn1_w_down"], (1, 1, 1), [f"ffn1_{k}" for k in ("swiglu_bwd", "gate_dw", "up_dw")])
        elif block == "ffn1_dw_gate_up":
            end("mid")
            begin("ffn1_gate_up", ["ffn1_w_gate", "ffn1_w_up"], (8, 9, 3), [f"ffn1_{k}" for k in ("gate_dx", "up_dx", "pre_norm_bwd")])
        elif block == "ffn1":
            end("ffn1_down")
            end("ffn1_gate_up")

    cos, sin = _rope_tables(positions[0])
    loss_part, grad_x = _local_step(x[0], mem[0], cos, sin, loss_target[0], params_of, carry, on_grads)

    small_shapes = [a[n].shape for n in SMALL]
    width = max(s[1] for s in small_shapes)
    gs = _all_reduce_small(_pack_small([g_small[n] for n in SMALL], width), "small_grads_all_reduce")

    out_g, out_d, out_m, out_v = {}, {}, {}, {}
    for n in big:
        flip = (lambda t: jnp.swapaxes(t, 0, 1)) if a[n].shape[2] % 128 else (lambda t: t)
        res = _adamw(*(flip(t) for t in (a[n][0], g_big[n], a["m_" + n][0], a["v_" + n][0])), f"adamw_{n}")
        out_g[n], out_d[n], out_m[n], out_v[n] = (flip(t).reshape(a[n].shape) for t in res)
    sw, sm, sv = (_pack_small([a[p + n] for n in SMALL], width) for p in ("", "m_", "v_"))
    for t, dst in zip(_adamw(sw, gs, sm, sv, "adamw_small"), (out_g, out_d, out_m, out_v)):
        dst.update(zip(SMALL, _unpack_small(t, small_shapes)))

    loss = lax.psum(loss_part, COMM_AXES)
    return (loss, grad_x[None], *[out_g[n] for n in WEIGHTS], *[out_d[n] for n in WEIGHTS],
            *[out_m[n] for n in WEIGHTS], *[out_v[n] for n in WEIGHTS])
```

```python
import functools

import jax
import jax.numpy as jnp
from jax import lax
from jax.experimental import pallas as pl
from jax.experimental.pallas import tpu as pltpu

F32 = jnp.float32
BF16 = jnp.bfloat16
EPS = 1e-6
HEAD = 128
ROPE = 64
CHUNK = 64
SUB = 16
HG_BLOCK = 256
ROPE_THETA = 10000.0
FFN_RESIDUAL_WEIGHT = 0.5
ADAM_LR, ADAM_B1, ADAM_B2, ADAM_EPS, ADAM_WD, ADAM_STEP = 0.001, 0.9, 0.999, 1e-08, 0.01, 10
VMEM_LIMIT = 56 * 2**20
ROW_BUDGET = 20 * 2**20
NEG = -1e30
MESH = pl.DeviceIdType.MESH
N_CHIP = 4
N_DEV = 8


def _params(sem):
    return pltpu.CompilerParams(dimension_semantics=sem, vmem_limit_bytes=VMEM_LIMIT)


def _tile(n, cap, mult):
    if n <= cap:
        return n
    t = (cap // mult) * mult
    while t >= mult:
        if n % t == 0:
            return t
        t -= mult
    raise ValueError(f"no tile for {n} under {cap}")


def _split_rows(n, fractions):
    if n % 16:
        return [(0, n)] + [(n, 0)] * (len(fractions) - 1)
    units, total, acc, cuts = n // 16, sum(fractions), 0, [0]
    for f in fractions[:-1]:
        acc += f
        cuts.append(round(units * acc / total))
    cuts.append(units)
    return [(16 * lo, 16 * (hi - lo)) for lo, hi in zip(cuts, cuts[1:])]


class _Pipe:
    def __init__(self, kind, srcs, lands, fractions):
        self.kind, self.srcs, self.lands = kind, list(srcs), list(lands)
        per_w = [_split_rows(s.shape[0] // 2 if kind == "gather" else s.shape[1], fractions) for s in srcs]
        self.parts = [[pw[i] for pw in per_w] for i in range(len(fractions))]
        self.taken = 0
        self.sems = (6 if kind == "gather" else 3) * len(srcs)

    def take(self):
        self.taken += 1
        return self.parts[self.taken - 1]

    def rest(self):
        left = self.parts[self.taken:]
        self.taken = len(self.parts)
        return [(left[0][w][0], sum(p[w][1] for p in left)) for w in range(len(self.srcs))] if left else None


def _pipe_copies(kind, rows, lands, srcs, send_sems, recv_sems):
    x, y, c, me, sibling, chips = _place()
    out = []
    for w, (r0, nr) in enumerate(rows):
        for j, (cx, cy) in enumerate(chips if nr else []):
            k, to = 2 * cx + cy, (cx, cy, c)
            if kind == "gather":
                rs = pl.ds(c * (srcs[w].shape[0] // 2) + r0, nr)
                src, dst, got, j0 = srcs[w].at[rs], lands[w].at[me, rs], lands[w].at[k, rs], 6 * w + j
            else:
                rs = pl.ds(r0, nr)
                src, dst, got, j0 = srcs[w].at[k, rs], lands[w].at[j, rs], lands[w].at[j, rs], 3 * w + j
            out.append((_remote(src, dst, send_sems, recv_sems, j0, to), _remote(src, got, send_sems, recv_sems, j0, to), w, j, k))
    return out


def _pipe_start(kind, rows, lands, srcs, send_sems, recv_sems):
    for send, _, _, _, _ in _pipe_copies(kind, rows, lands, srcs, send_sems, recv_sems):
        send.start()


def _pipe_finish(kind, rows, lands, srcs, send_sems, recv_sems):
    x, y, c, me, sibling, chips = _place()
    copies = _pipe_copies(kind, rows, lands, srcs, send_sems, recv_sems)
    passed = []
    for _, arrival, _, _, _ in copies:
        arrival.wait_recv()
    if kind == "gather":
        for w, (r0, nr) in enumerate(rows):
            hr = srcs[w].shape[0] // 2
            for j, (cx, cy) in enumerate(chips if nr else []):
                blk = lands[w].at[2 * cx + cy, pl.ds(c * hr + r0, nr)]
                passed.append(_remote(blk, blk, send_sems, recv_sems, 6 * w + 3 + j, sibling))
                passed[-1].start()
        for w, (r0, nr) in enumerate(rows):
            hr = srcs[w].shape[0] // 2
            for j, (cx, cy) in enumerate(chips if nr else []):
                blk = lands[w].at[2 * cx + cy, pl.ds((1 - c) * hr + r0, nr)]
                _remote(blk, blk, send_sems, recv_sems, 6 * w + 3 + j, sibling).wait_recv()
    for send, _, _, _, _ in copies:
        send.wait_send()
    for cp in passed:
        cp.wait_send()


def _carry_call(body, *, grid, in_specs, out_specs, out_shape, scratch_shapes, sem, name, args, pipe=None):
    single = not isinstance(out_shape, (list, tuple))
    if single:
        out_specs, out_shape = [out_specs], [out_shape]
    if pipe is None or pipe.taken >= len(pipe.parts):
        res = pl.pallas_call(body, grid=grid, in_specs=in_specs, out_specs=out_specs, out_shape=out_shape,
                             scratch_shapes=scratch_shapes, compiler_params=_params(sem), name=name)(*args)
        return res[0] if single else res
    rows, kind = pipe.take(), pipe.kind
    n_in, n_out, n_l, n_s, n_scr = len(args), len(out_shape), len(pipe.lands), len(pipe.srcs), len(scratch_shapes)

    def wrapped(*refs):
        ins, srcs = refs[:n_in], refs[n_in + n_l:n_in + n_l + n_s]
        o0 = n_in + n_l + n_s
        outs, lands = refs[o0:o0 + n_out], refs[o0 + n_out:o0 + n_out + n_l]
        scr = refs[o0 + n_out + n_l:o0 + n_out + n_l + n_scr]
        send_sems, recv_sems = refs[-2], refs[-1]
        ids = [pl.program_id(ax) for ax in range(len(grid))]
        first = functools.reduce(jnp.logical_and, [i == 0 for i in ids])
        last = functools.reduce(jnp.logical_and, [i == g - 1 for i, g in zip(ids, grid)])

        @pl.when(first)
        def _():
            _pipe_start(kind, rows, lands, srcs, send_sems, recv_sems)

        body(*ins, *outs, *scr)

        @pl.when(last)
        def _():
            _pipe_finish(kind, rows, lands, srcs, send_sems, recv_sems)

    res = pl.pallas_call(
        wrapped, grid=grid, in_specs=list(in_specs) + [ANY] * (n_l + n_s), out_specs=list(out_specs) + [ANY] * n_l,
        out_shape=list(out_shape) + [jax.ShapeDtypeStruct(l.shape, l.dtype) for l in pipe.lands],
        input_output_aliases={n_in + i: n_out + i for i in range(n_l)},
        scratch_shapes=list(scratch_shapes) + _dma_sems(pipe.sems),
        compiler_params=_params(("arbitrary",) * len(grid)), name=name,
    )(*args, *pipe.lands, *pipe.srcs)
    pipe.lands = list(res[n_out:])
    return res[0] if single else list(res[:n_out])


def _pipe_flush(pipe, name):
    rows = pipe.rest()
    if rows is None:
        return
    n_l, n_s, kind = len(pipe.lands), len(pipe.srcs), pipe.kind

    def body(*refs):
        srcs, lands = refs[n_l:n_l + n_s], refs[n_l + n_s:2 * n_l + n_s]
        _pipe_start(kind, rows, lands, srcs, refs[-2], refs[-1])
        _pipe_finish(kind, rows, lands, srcs, refs[-2], refs[-1])

    pipe.lands = list(pl.pallas_call(
        body, in_specs=[ANY] * (n_l + n_s), out_specs=[ANY] * n_l,
        out_shape=[jax.ShapeDtypeStruct(l.shape, l.dtype) for l in pipe.lands],
        input_output_aliases={i: i for i in range(n_l)}, scratch_shapes=_dma_sems(pipe.sems), name=name,
    )(*pipe.lands, *pipe.srcs))


def _sigmoid(x):
    return 1.0 / (1.0 + jnp.exp(-x))


def _dot(a, b, dims):
    return lax.dot_general(a, b, (dims, ((), ())), preferred_element_type=F32)


NN = ((1,), (0,))
NT = ((1,), (1,))
TN = ((0,), (0,))


def _mm(a, b, mode, out_dtype, name, add=None, out_slots=0, pipe=None, tm_cap=1024, tn_cap=512, tk_cap=2816):
    slot_cap = 1408
    b_slots = b.shape[0] if b.ndim == 3 else 0
    bs = (b.shape[1], b_slots * b.shape[2]) if b_slots else b.shape
    if mode == "nn":
        (M, K), (K2, N) = a.shape, bs
    elif mode == "nt":
        (M, K), (N, K2) = a.shape, bs
    else:
        (K, M), (K2, N) = a.shape, bs
    assert K == K2, (a.shape, b.shape, mode)
    tm = _tile(M, tm_cap, 128)
    tn = _tile(N // (b_slots or out_slots), slot_cap, 128) if (out_slots or (b_slots and mode == "nn")) else _tile(N, tn_cap, 128)
    tk = _tile(K // b_slots, slot_cap, 128) if (b_slots and mode == "nt") else _tile(K, tk_cap, 128)
    nk = K // tk
    a_spec = pl.BlockSpec((tk, tm), lambda i, j, k: (k, i)) if mode == "tn" else pl.BlockSpec((tm, tk), lambda i, j, k: (i, k))
    if b_slots and mode == "nn":
        per = b.shape[2] // tn
        b_spec = pl.BlockSpec((None, tk, tn), lambda i, j, k: (j // per, k, j % per))
    elif b_slots:
        per = b.shape[2] // tk
        b_spec = pl.BlockSpec((None, tn, tk), lambda i, j, k: (k // per, j, k % per))
    else:
        b_spec = pl.BlockSpec((tn, tk), lambda i, j, k: (j, k)) if mode == "nt" else pl.BlockSpec((tk, tn), lambda i, j, k: (k, j))
    if out_slots:
        per_o = N // out_slots // tn
        o_spec = pl.BlockSpec((None, tm, tn), lambda i, j, k: (j // per_o, i, j % per_o))
        o_shape = (out_slots, M, N // out_slots)
    else:
        o_spec = pl.BlockSpec((tm, tn), lambda i, j, k: (i, j))
        o_shape = (M, N)
    dims = {"nn": NN, "nt": NT, "tn": TN}[mode]
    has_add = add is not None

    def body(*refs):
        a_ref, b_ref = refs[0], refs[1]
        add_ref = refs[2] if has_add else None
        o_ref = refs[3] if has_add else refs[2]
        p = _dot(a_ref[...].astype(BF16), b_ref[...].astype(BF16), dims)

        def finish(val):
            if has_add:
                val = val + add_ref[...]
            o_ref[...] = val.astype(out_dtype)

        if nk == 1:
            finish(p)
        else:
            acc_ref = refs[-1]
            k = pl.program_id(2)

            @pl.when(k == 0)
            def _():
                acc_ref[...] = p

            @pl.when(k > 0)
            def _():
                acc_ref[...] += p

            @pl.when(k == nk - 1)
            def _():
                finish(acc_ref[...])

    assert not (has_add and out_slots)
    in_specs = [a_spec, b_spec] + ([o_spec] if has_add else [])
    args = (a, b) + ((add,) if has_add else ())
    return _carry_call(
        body, grid=(M // tm, N // tn, nk), in_specs=in_specs, out_specs=o_spec,
        out_shape=jax.ShapeDtypeStruct(o_shape, out_dtype),
        scratch_shapes=[pltpu.VMEM((tm, tn), F32)] if nk > 1 else [],
        sem=("parallel", "parallel", "arbitrary"), name=name, args=args, pipe=pipe,
    )


def _rows(body, ins, outs, name, pipe=None):
    T = next(a.shape[0] for a, k in ins if k == "row")
    per_row = sum(a.shape[1] * a.dtype.itemsize for a, k in ins if k == "row")
    per_row += sum(s[1] * jnp.dtype(d).itemsize for s, d, k in outs if k == "row")
    tr = next(t for t in range(512, 0, -8) if T % t == 0 and 2 * t * per_row <= ROW_BUDGET)
    in_specs = [
        pl.BlockSpec((tr, a.shape[1]), lambda i: (i, 0)) if k == "row" else pl.BlockSpec(a.shape, lambda i: (0, 0))
        for a, k in ins
    ]
    out_specs = [
        pl.BlockSpec((tr, s[1]), lambda i: (i, 0)) if k == "row" else pl.BlockSpec(s, lambda i: (0, 0))
        for s, d, k in outs
    ]
    has_acc = any(k == "acc" for _, _, k in outs)
    return _carry_call(
        body, grid=(T // tr,), in_specs=in_specs, out_specs=out_specs,
        out_shape=[jax.ShapeDtypeStruct(s, d) for s, d, k in outs], scratch_shapes=[],
        sem=("arbitrary",) if has_acc else ("parallel",), name=name, args=[a for a, _ in ins], pipe=pipe,
    )


def _rstd(x):
    return lax.rsqrt(jnp.mean(x * x, axis=-1, keepdims=True) + EPS)


def _norm_fwd(x, g, name, pipe=None):
    def body(x_ref, g_ref, o_ref):
        x = x_ref[...]
        o_ref[...] = (x * _rstd(x) * g_ref[...]).astype(BF16)

    return _rows(body, [(x, "row"), (g, "vec")], [(x.shape, BF16, "row")], name, pipe=pipe)[0]


def _postnorm_fwd(x, y, g, weight, name, pipe=None):
    def body(x_ref, y_ref, g_ref, o_ref):
        y = y_ref[...]
        o_ref[...] = x_ref[...] + weight * (y * _rstd(y) * g_ref[...])

    return _rows(body, [(x, "row"), (y, "row"), (g, "vec")], [(x.shape, F32, "row")], name, pipe=pipe)[0]


def _norm_bwd(x, g, dy, weight, out_dtype, name, res=None, pipe=None):
    has_res = res is not None

    def body(*refs):
        x_ref, g_ref, dy_ref = refs[:3]
        res_ref = refs[3] if has_res else None
        dx_ref, dg_ref = refs[-2], refs[-1]

        @pl.when(pl.program_id(0) == 0)
        def _():
            dg_ref[...] = jnp.zeros_like(dg_ref)

        x = x_ref[...]
        dn = dy_ref[...].astype(F32) * weight
        r = _rstd(x)
        xhat = x * r
        dg_ref[...] += jnp.sum(dn * xhat, axis=0, keepdims=True)
        dxh = dn * g_ref[...]
        dx = r * (dxh - xhat * jnp.mean(dxh * xhat, axis=-1, keepdims=True))
        if has_res:
            dx = dx + res_ref[...]
        dx_ref[...] = dx.astype(out_dtype)

    ins = [(x, "row"), (g, "vec"), (dy, "row")] + ([(res, "row")] if has_res else [])
    return _rows(body, ins, [(x.shape, out_dtype, "row"), (g.shape, F32, "acc")], name, pipe=pipe)


def _swiglu_fwd(a, b, name, pipe=None):
    def body(a_ref, b_ref, o_ref):
        a = a_ref[...]
        o_ref[...] = (a * _sigmoid(a) * b_ref[...]).astype(BF16)

    return _rows(body, [(a, "row"), (b, "row")], [(a.shape, BF16, "row")], name, pipe=pipe)[0]


def _swiglu_bwd(a, b, ds, name, pipe=None):
    def body(a_ref, b_ref, ds_ref, da_ref, db_ref):
        a, ds = a_ref[...], ds_ref[...]
        sg = _sigmoid(a)
        da_ref[...] = (ds * b_ref[...] * (sg * (1.0 + a * (1.0 - sg)))).astype(BF16)
        db_ref[...] = (ds * (a * sg)).astype(BF16)

    return _rows(body, [(a, "row"), (b, "row"), (ds, "row")], [(a.shape, BF16, "row"), (a.shape, BF16, "row")], name, pipe=pipe)


def _merge_fwd(ga, gb, ya, yb, name):
    def body(ga_ref, gb_ref, ya_ref, yb_ref, o_ref):
        o_ref[...] = (_sigmoid(ga_ref[...]) * ya_ref[...] + _sigmoid(gb_ref[...]) * yb_ref[...]).astype(BF16)

    return _rows(body, [(ga, "row"), (gb, "row"), (ya, "row"), (yb, "row")], [(ga.shape, BF16, "row")], name)[0]


def _merge_bwd(ga, gb, ya, yb, dy, name):
    def body(ga_ref, gb_ref, ya_ref, yb_ref, dy_ref, dya_ref, dyb_ref, dga_ref, dgb_ref):
        dy = dy_ref[...]
        sa, sb = _sigmoid(ga_ref[...]), _sigmoid(gb_ref[...])
        dya_ref[...] = (dy * sa).astype(BF16)
        dyb_ref[...] = (dy * sb).astype(BF16)
        dga_ref[...] = (dy * ya_ref[...] * (sa * (1.0 - sa))).astype(BF16)
        dgb_ref[...] = (dy * yb_ref[...] * (sb * (1.0 - sb))).astype(BF16)

    ins = [(ga, "row"), (gb, "row"), (ya, "row"), (yb, "row"), (dy, "row")]
    return _rows(body, ins, [(ga.shape, BF16, "row")] * 4, name)


def _loss_head(y, target, name):
    D = y.shape[1]

    def body(y_ref, t_ref, dy_ref, acc_ref):
        @pl.when(pl.program_id(0) == 0)
        def _():
            acc_ref[...] = jnp.zeros_like(acc_ref)

        err = y_ref[...] - t_ref[...]
        dy_ref[...] = err * (1.0 / D)
        acc_ref[...] += jnp.sum(err * err, axis=0, keepdims=True)

    return _rows(body, [(y, "row"), (target, "row")], [(y.shape, F32, "row"), ((1, D), F32, "acc")], name)


def _rot(x):
    lane = lax.broadcasted_iota(jnp.int32, x.shape, 1)
    return jnp.where((lane % ROPE) < ROPE // 2, -pltpu.roll(x, 128 - ROPE // 2, 1), pltpu.roll(x, ROPE // 2, 1))


def _rope_q_fwd(qpe, cos, sin, name):
    T, W = qpe.shape
    tr = min(T, 512)
    blk = pl.BlockSpec((tr, 128), lambda i, j: (i, j))
    tab = pl.BlockSpec((tr, 128), lambda i, j: (i, 0))

    def body(x_ref, c_ref, s_ref, o_ref):
        x = x_ref[...]
        o_ref[...] = (x * c_ref[...] + _rot(x) * s_ref[...]).astype(BF16)

    return pl.pallas_call(
        body, grid=(T // tr, W // 128), in_specs=[blk, tab, tab], out_specs=blk,
        out_shape=jax.ShapeDtypeStruct((T, W), BF16), compiler_params=_params(("parallel", "parallel")), name=name,
    )(qpe, cos, sin)


def _rope_q_bwd(dq_heads, cos, sin, name):
    T, W = dq_heads.shape
    tr = min(T, 512)
    even = pl.BlockSpec((tr, 128), lambda i, j: (i, 2 * j))
    odd = pl.BlockSpec((tr, 128), lambda i, j: (i, 2 * j + 1))
    tab = pl.BlockSpec((tr, 128), lambda i, j: (i, 0))

    def body(a_ref, b_ref, c_ref, s_ref, o_ref):
        d = a_ref[...] + b_ref[...]
        o_ref[...] = (d * c_ref[...] - _rot(d * s_ref[...])).astype(BF16)

    return pl.pallas_call(
        body, grid=(T // tr, W // 256), in_specs=[even, odd, tab, tab],
        out_specs=pl.BlockSpec((tr, 128), lambda i, j: (i, j)),
        out_shape=jax.ShapeDtypeStruct((T, W // 2), BF16), compiler_params=_params(("parallel", "parallel")), name=name,
    )(dq_heads, dq_heads, cos, sin)


def _rope_k_fwd(kpe, cos, sin, name):
    def body(x_ref, c_ref, s_ref, o_ref):
        x = x_ref[...]
        y = x * c_ref[...] + _rot(x) * s_ref[...]
        o_ref[...] = (y + pltpu.roll(y, ROPE, 1)).astype(BF16)

    return _rows(body, [(kpe, "row"), (cos, "row"), (sin, "row")], [(kpe.shape, BF16, "row")], name)[0]


def _rope_k_bwd(dk_heads, cos, sin, name):
    T, W = dk_heads.shape

    def body(d_ref, c_ref, s_ref, o_ref):
        d = d_ref[:, 0:128]
        for h in range(1, W // 128):
            d = d + d_ref[:, h * 128:(h + 1) * 128]
        d = d + pltpu.roll(d, ROPE, 1)
        dx = d * c_ref[...] - _rot(d * s_ref[...])
        lane = lax.broadcasted_iota(jnp.int32, dx.shape, 1)
        o_ref[...] = jnp.where(lane < ROPE, dx, 0.0).astype(BF16)

    return _rows(body, [(dk_heads, "row"), (cos, "row"), (sin, "row")], [((T, 128), BF16, "row")], name)[0]


def _attn_probs(q, k, qpe, kpe, scale, causal, q0):
    s = _dot(q, k, NT)
    if qpe is not None:
        s = s + _dot(qpe, kpe, NT)
    s = s * scale
    if causal:
        row = q0 + lax.broadcasted_iota(jnp.int32, s.shape, 0)
        col = lax.broadcasted_iota(jnp.int32, s.shape, 1)
        s = jnp.where((col // CHUNK) <= (row // CHUNK), s, NEG)
    p = jnp.exp(s - jnp.max(s, axis=-1, keepdims=True))
    return p / jnp.sum(p, axis=-1, keepdims=True)


def _pe_mask(x, h):
    lane = lax.broadcasted_iota(jnp.int32, x.shape, 1)
    return jnp.where((lane // ROPE) == (h % 2), x, jnp.zeros_like(x))


def _attn_fwd(q, k, v, scale, name, qpe=None, kpe=None, causal=False, pipe=None):
    T, W = q.shape
    Tk = k.shape[0]
    H = W // HEAD
    tq = min(T, 256)
    nq = T // tq
    has_pe = qpe is not None
    qs = pl.BlockSpec((tq, HEAD), lambda h, i: (i, h))
    ks = pl.BlockSpec((Tk, HEAD), lambda h, i: (0, h))
    in_specs, args = [qs, ks, ks], [q, k, v]
    if has_pe:
        in_specs += [pl.BlockSpec((tq, HEAD), lambda h, i: (i, h // 2)), pl.BlockSpec((Tk, HEAD), lambda h, i: (0, 0))]
        args += [qpe, kpe]

    def body(*refs):
        q_ref, k_ref, v_ref = refs[:3]
        o_ref = refs[-1]
        h, i = pl.program_id(0), pl.program_id(1)

        def compute(klen):
            qp = _pe_mask(refs[3][...], h) if has_pe else None
            kp = refs[4][0:klen, :] if has_pe else None
            p = _attn_probs(q_ref[...], k_ref[0:klen, :], qp, kp, scale, causal, i * tq)
            o_ref[...] = _dot(p.astype(BF16), v_ref[0:klen, :], NN).astype(BF16)

        if causal:
            for qi in range(nq):
                pl.when(i == qi)(functools.partial(compute, (qi + 1) * tq))
        else:
            compute(Tk)

    return _carry_call(
        body, grid=(H, nq), in_specs=in_specs, out_specs=qs, out_shape=jax.ShapeDtypeStruct((T, W), BF16),
        scratch_shapes=[], sem=("parallel", "parallel"), name=name, args=args, pipe=pipe,
    )


def _attn_bwd(q, k, v, do, scale, name, qpe=None, kpe=None, causal=False, pipe=None):
    T, W = q.shape
    Tk = k.shape[0]
    H = W // HEAD
    tq = min(T, 256)
    nq = T // tq
    has_pe = qpe is not None
    qs = pl.BlockSpec((tq, HEAD), lambda h, i: (i, h))
    ks = pl.BlockSpec((Tk, HEAD), lambda h, i: (0, h))
    in_specs, args = [qs, ks, ks, qs], [q, k, v, do]
    out_specs = [qs, ks, ks]
    out_shape = [jax.ShapeDtypeStruct((T, W), BF16), jax.ShapeDtypeStruct((Tk, W), BF16), jax.ShapeDtypeStruct((Tk, W), BF16)]
    scratch = [pltpu.VMEM((Tk, HEAD), F32), pltpu.VMEM((Tk, HEAD), F32)]
    if has_pe:
        in_specs += [pl.BlockSpec((tq, HEAD), lambda h, i: (i, h // 2)), pl.BlockSpec((Tk, HEAD), lambda h, i: (0, 0))]
        args += [qpe, kpe]
        out_specs += [qs, ks]
        out_shape += [jax.ShapeDtypeStruct((T, W), F32), jax.ShapeDtypeStruct((Tk, W), F32)]
        scratch += [pltpu.VMEM((Tk, HEAD), F32)]
    n_in = len(in_specs)

    def body(*refs):
        q_ref, k_ref, v_ref, do_ref = refs[:4]
        outs = refs[n_in:n_in + len(out_specs)]
        accs = refs[n_in + len(out_specs):]
        dq_ref, dk_ref, dv_ref = outs[:3]
        h, i = pl.program_id(0), pl.program_id(1)

        @pl.when(i == 0)
        def _():
            for acc in accs:
                acc[...] = jnp.zeros_like(acc)

        def compute(klen):
            qp = _pe_mask(refs[4][...], h) if has_pe else None
            kp = refs[5][0:klen, :] if has_pe else None
            qv, kv, vv, dov = q_ref[...], k_ref[0:klen, :], v_ref[0:klen, :], do_ref[...]
            p = _attn_probs(qv, kv, qp, kp, scale, causal, i * tq)
            dp = _dot(dov, vv, NT)
            ds = (p * (dp - jnp.sum(p * dp, axis=-1, keepdims=True)) * scale).astype(BF16)
            dq_ref[...] = _dot(ds, kv, NN).astype(BF16)
            accs[0][0:klen, :] += _dot(ds, qv, TN)
            accs[1][0:klen, :] += _dot(p.astype(BF16), dov, TN)
            if has_pe:
                outs[3][...] = _pe_mask(_dot(ds, kp, NN), h)
                accs[2][0:klen, :] += _dot(ds, qp, TN)

        if causal:
            for qi in range(nq):
                pl.when(i == qi)(functools.partial(compute, (qi + 1) * tq))
        else:
            compute(Tk)

        @pl.when(i == nq - 1)
        def _():
            dk_ref[...] = accs[0][...].astype(BF16)
            dv_ref[...] = accs[1][...].astype(BF16)
            if has_pe:
                outs[4][...] = accs[2][...]

    return _carry_call(
        body, grid=(H, nq), in_specs=in_specs, out_specs=out_specs, out_shape=out_shape, scratch_shapes=scratch,
        sem=("parallel", "arbitrary"), name=name, args=args, pipe=pipe,
    )


def _split3(x):
    hi = x.astype(BF16)
    r1 = x - hi.astype(F32)
    mid = r1.astype(BF16)
    lo = (r1 - mid.astype(F32)).astype(BF16)
    return hi, mid, lo


def _tri_dot(tri, x):
    hi, mid, lo = _split3(x)
    return _dot(tri, hi, NN) + _dot(tri, mid, NN) + _dot(tri, lo, NN)


def _hg_gates(u, lb):
    q, fr, v = u[:, 0:HEAD], u[:, HEAD:2 * HEAD], u[:, 2 * HEAD:3 * HEAD]
    sg = 1.0 / (1.0 + jnp.exp(-fr))
    sgm = 1.0 / (1.0 + jnp.exp(fr))
    f = lb + (1.0 - lb) * sg
    kin = (1.0 - lb) * sgm
    sq = _sigmoid(q)
    return q, v, sg, sgm, f, kin, sq, q * sq


def _hg_block_mats(blk):
    t = lax.broadcasted_iota(jnp.int32, (blk, blk), 0)
    s = lax.broadcasted_iota(jnp.int32, (blk, blk), 1)
    same = (t // SUB) == (s // SUB)
    one = lambda m: jnp.where(m, 1.0, 0.0).astype(BF16)
    return one(same & (s <= t)), one(same), one(same & (s >= t))


def _hg_stage(pairs, blk):
    for sc, val in pairs:
        sc[0:SUB, :] = jnp.zeros((SUB, HEAD), F32)
        sc[SUB:SUB + blk, :] = val


def _hg_scan_fwd(u, lb, name, pipe=None):
    T, W = u.shape
    H = W // (4 * HEAD)
    blk = min(T, HG_BLOCK)
    nb, nsb = T // blk, blk // SUB

    def body(u_ref, lb_ref, o_ref, st_ref, state, k_sc, b_sc, v_sc):
        @pl.when(pl.program_id(1) == 0)
        def _():
            state[...] = jnp.zeros_like(state)

        q, v, sg, sgm, f, kin, sq, qin = _hg_gates(u_ref[...], lb_ref[...])
        tri, ones, _ = _hg_block_mats(blk)
        logf = jnp.log(f)
        brel = _tri_dot(tri, logf)
        btot = _tri_dot(ones, logf)
        _hg_stage(((k_sc, kin), (b_sc, brel), (v_sc, v)), blk)
        sub_row = lax.broadcasted_iota(jnp.int32, (blk, HEAD), 0) % SUB
        o = jnp.zeros((blk, HEAD), F32)
        for d in range(SUB):
            win = slice(SUB - d, SUB - d + blk)
            e = jnp.exp(jnp.where(sub_row >= d, brel - b_sc[win, :], NEG))
            o = o + jnp.sum(qin * e * k_sc[win, :], axis=-1, keepdims=True) * v_sc[win, :]
        ab = (qin * jnp.exp(brel)).astype(BF16)
        kdb = (kin * jnp.exp(btot - brel)).astype(BF16)
        vb = v.astype(BF16)
        ebt = jnp.exp(btot)
        st = state[...]
        st_ref[...] = st
        for i in range(nsb):
            sl = slice(i * SUB, (i + 1) * SUB)
            o_ref[sl, :] = o[sl] + _dot(ab[sl], st.astype(BF16), NT)
            st = ebt[i * SUB:i * SUB + 1, :] * st + _dot(vb[sl], kdb[sl], TN)
        state[...] = st

    return _carry_call(
        body, grid=(H, nb),
        in_specs=[pl.BlockSpec((blk, 4 * HEAD), lambda h, c: (c, h)), pl.BlockSpec((1, HEAD), lambda h, c: (0, h))],
        out_specs=[pl.BlockSpec((blk, HEAD), lambda h, c: (c, h)), pl.BlockSpec((None, None, HEAD, HEAD), lambda h, c: (h, c, 0, 0))],
        out_shape=[jax.ShapeDtypeStruct((T, H * HEAD), F32), jax.ShapeDtypeStruct((H, nb, HEAD, HEAD), F32)],
        scratch_shapes=[pltpu.VMEM((HEAD, HEAD), F32)] + [pltpu.VMEM((SUB + blk, HEAD), F32)] * 3,
        sem=("parallel", "arbitrary"), name=name, args=(u, lb), pipe=pipe,
    )


def _hg_scan_bwd(u, lb, do, dog, states, name, pipe=None):
    T, W = u.shape
    H = W // (4 * HEAD)
    blk = min(T, HG_BLOCK)
    NC, nsb = T // blk, blk // SUB

    def body(u_ref, lb_ref, do_ref, dog_ref, st_ref, du_ref, dlb_ref, dstate, s_all, k_sc, b_sc, v_sc, dk_sc, dbn_sc,
             dv_sc, da_sc, dkd_sc, dvs_sc, dbt_sc):
        @pl.when(pl.program_id(1) == 0)
        def _():
            dstate[...] = jnp.zeros_like(dstate)
            dlb_ref[...] = jnp.zeros_like(dlb_ref)

        lb = lb_ref[...]
        q, v, sg, sgm, f, kin, sq, qin = _hg_gates(u_ref[...], lb)
        tri, ones, tri_t = _hg_block_mats(blk)
        logf = jnp.log(f)
        brel = _tri_dot(tri, logf)
        btot = _tri_dot(ones, logf)
        eb, ekd, ebt = jnp.exp(brel), jnp.exp(btot - brel), jnp.exp(btot)
        a, kd = qin * eb, kin * ekd
        ab, kdb, vb = a.astype(BF16), kd.astype(BF16), v.astype(BF16)
        do = do_ref[...]
        dob = do.astype(BF16)
        st = st_ref[...]
        for i in range(nsb):
            sl = slice(i * SUB, (i + 1) * SUB)
            s_all[i] = st
            st = ebt[i * SUB:i * SUB + 1, :] * st + _dot(vb[sl], kdb[sl], TN)
        ds = dstate[...]
        for i in reversed(range(nsb)):
            sl = slice(i * SUB, (i + 1) * SUB)
            st_i = s_all[i]
            dsb = ds.astype(BF16)
            e_i = ebt[i * SUB:i * SUB + 1, :]
            da_sc[sl, :] = _dot(dob[sl], st_i.astype(BF16), NN)
            dvs_sc[sl, :] = _dot(kdb[sl], dsb, NT)
            dkd_sc[sl, :] = _dot(vb[sl], dsb, NN)
            dbt_sc[sl, :] = jnp.broadcast_to(jnp.sum(ds * st_i, axis=0, keepdims=True) * e_i, (SUB, HEAD))
            ds = e_i * ds + _dot(dob[sl], ab[sl], TN)
        dstate[...] = ds
        da, dkd = da_sc[...], dkd_sc[...]
        t1 = dkd * kd
        dqin = da * eb
        dbrel = da * a - t1
        dkin = dkd * ekd
        dbtot = dbt_sc[...] + _tri_dot(ones, t1)
        _hg_stage(((k_sc, kin), (b_sc, brel), (v_sc, v)), blk)
        for sc in (dk_sc, dbn_sc, dv_sc):
            sc[...] = jnp.zeros_like(sc)
        sub_row = lax.broadcasted_iota(jnp.int32, (blk, HEAD), 0) % SUB
        for d in range(SUB):
            win = slice(SUB - d, SUB - d + blk)
            ks = k_sc[win, :]
            e = jnp.exp(jnp.where(sub_row >= d, brel - b_sc[win, :], NEG))
            qe = qin * e
            col = jnp.sum(qe * ks, axis=-1, keepdims=True)
            dcol = jnp.sum(do * v_sc[win, :], axis=-1, keepdims=True)
            dqe = dcol * qe
            g = dqe * ks
            dqin = dqin + dcol * (e * ks)
            dbrel = dbrel + g
            dk_sc[win, :] += dqe
            dbn_sc[win, :] += g
            dv_sc[win, :] += col * do
        dkin = dkin + dk_sc[SUB:SUB + blk, :]
        dbrel = dbrel - dbn_sc[SUB:SUB + blk, :]
        dv = dvs_sc[...] + dv_sc[SUB:SUB + blk, :]
        dlogf = _tri_dot(tri_t, dbrel) + dbtot
        diff = dlogf / f - dkin
        dlb_ref[...] += jnp.sum(sgm * diff, axis=0, keepdims=True)
        du_ref[:, 0:HEAD] = (dqin * (sq * (1.0 + q * (1.0 - sq)))).astype(BF16)
        du_ref[:, HEAD:2 * HEAD] = ((1.0 - lb) * sg * sgm * diff).astype(BF16)
        du_ref[:, 2 * HEAD:3 * HEAD] = dv.astype(BF16)
        du_ref[:, 3 * HEAD:4 * HEAD] = dog_ref[...]

    rev = lambda h, c: (NC - 1 - c, h)
    return _carry_call(
        body, grid=(H, NC),
        in_specs=[
            pl.BlockSpec((blk, 4 * HEAD), rev), pl.BlockSpec((1, HEAD), lambda h, c: (0, h)),
            pl.BlockSpec((blk, HEAD), rev), pl.BlockSpec((blk, HEAD), rev),
            pl.BlockSpec((None, None, HEAD, HEAD), lambda h, c: (h, NC - 1 - c, 0, 0)),
        ],
        out_specs=[pl.BlockSpec((blk, 4 * HEAD), rev), pl.BlockSpec((1, HEAD), lambda h, c: (0, h))],
        out_shape=[jax.ShapeDtypeStruct((T, W), BF16), jax.ShapeDtypeStruct((1, H * HEAD), F32)],
        scratch_shapes=[pltpu.VMEM((HEAD, HEAD), F32), pltpu.VMEM((nsb, HEAD, HEAD), F32)]
        + [pltpu.VMEM((SUB + blk, HEAD), F32)] * 6 + [pltpu.VMEM((blk, HEAD), F32)] * 4,
        sem=("parallel", "arbitrary"), name=name, args=(u, lb, do, dog, states), pipe=pipe,
    )


def _hg_tail_fwd(o_raw, u, g, name):
    T, D = o_raw.shape
    H = D // HEAD
    tr = min(T, 512)
    blk = pl.BlockSpec((tr, HEAD), lambda h, i: (i, h))

    def body(o_ref, og_ref, g_ref, out_ref):
        o, og = o_ref[...], og_ref[...]
        out_ref[...] = (o * _rstd(o) * g_ref[...] * (og * _sigmoid(og))).astype(BF16)

    return pl.pallas_call(
        body, grid=(H, T // tr),
        in_specs=[blk, pl.BlockSpec((tr, HEAD), lambda h, i: (i, 4 * h + 3)), pl.BlockSpec((1, HEAD), lambda h, i: (0, h))],
        out_specs=blk, out_shape=jax.ShapeDtypeStruct((T, D), BF16),
        compiler_params=_params(("parallel", "parallel")), name=name,
    )(o_raw, u, g)


def _hg_tail_bwd(o_raw, u, g, doa, name):
    T, D = o_raw.shape
    H = D // HEAD
    tr = min(T, 512)
    blk = pl.BlockSpec((tr, HEAD), lambda h, i: (i, h))
    vec = pl.BlockSpec((1, HEAD), lambda h, i: (0, h))

    def body(o_ref, og_ref, g_ref, doa_ref, do_ref, dog_ref, dg_ref):
        @pl.when(pl.program_id(1) == 0)
        def _():
            dg_ref[...] = jnp.zeros_like(dg_ref)

        o, og, doa, g = o_ref[...], og_ref[...], doa_ref[...], g_ref[...]
        sg = _sigmoid(og)
        r = _rstd(o)
        xhat = o * r
        dog_ref[...] = (doa * (xhat * g) * (sg * (1.0 + og * (1.0 - sg)))).astype(BF16)
        dn = doa * (og * sg)
        dg_ref[...] += jnp.sum(dn * xhat, axis=0, keepdims=True)
        dxh = dn * g
        do_ref[...] = r * (dxh - xhat * jnp.mean(dxh * xhat, axis=-1, keepdims=True))

    return pl.pallas_call(
        body, grid=(H, T // tr),
        in_specs=[blk, pl.BlockSpec((tr, HEAD), lambda h, i: (i, 4 * h + 3)), vec, blk],
        out_specs=[blk, blk, vec],
        out_shape=[jax.ShapeDtypeStruct((T, D), F32), jax.ShapeDtypeStruct((T, D), BF16), jax.ShapeDtypeStruct((1, D), F32)],
        compiler_params=_params(("parallel", "arbitrary")), name=name,
    )(o_raw, u, g, doa)


def _lb_fwd(logits, name):
    def body(l_ref, o_ref):
        l0, l1 = l_ref[0:1, :], l_ref[1:2, :]
        m = jnp.maximum(l0, l1)
        e0, e1 = jnp.exp(l0 - m), jnp.exp(l1 - m)
        o_ref[...] = e0 / (e0 + e1)

    D = logits.shape[1]
    return pl.pallas_call(body, out_shape=jax.ShapeDtypeStruct((1, D), F32), name=name)(logits)


def _lb_bwd(lb, dlb, name):
    def body(lb_ref, d_ref, o_ref):
        lb = lb_ref[...]
        d0 = d_ref[...] * lb * (1.0 - lb)
        o_ref[0:1, :] = d0
        o_ref[1:2, :] = -d0

    D = lb.shape[1]
    return pl.pallas_call(body, out_shape=jax.ShapeDtypeStruct((2, D), F32), name=name)(lb, dlb)


def _slots(w):
    return w.shape[0] if w.ndim == 3 else 0


def _ffn_fwd(x, p, tag, carry):
    mm = lambda a, b, mode, dt, name, **kw: _mm(a, b, mode, dt, name, pipe=carry.get(name), **kw)
    hb = _norm_fwd(x, p["pre_g"], f"{tag}_pre_norm", pipe=carry.get(f"{tag}_pre_norm"))
    a = mm(hb, p["w_gate"], "nn", F32, f"{tag}_gate")
    b = mm(hb, p["w_up"], "nn", F32, f"{tag}_up")
    sb = _swiglu_fwd(a, b, f"{tag}_swiglu", pipe=carry.get(f"{tag}_swiglu"))
    y = mm(sb, p["w_down"], "nn", F32, f"{tag}_down")
    xo = _postnorm_fwd(x, y, p["post_g"], FFN_RESIDUAL_WEIGHT, f"{tag}_post_norm", pipe=carry.get(f"{tag}_post_norm"))
    return xo, (x, hb, a, b, sb, y)


def _ffn_bwd(dxo, p, saved, tag, carry, on_dw):
    mm = lambda a, b, mode, dt, name, **kw: _mm(a, b, mode, dt, name, pipe=carry.get(name), **kw)
    x, hb, a, b, sb, y = saved
    dyb, dpost = _norm_bwd(y, p["post_g"], dxo, FFN_RESIDUAL_WEIGHT, BF16, f"{tag}_post_norm_bwd",
                           pipe=carry.get(f"{tag}_post_norm_bwd"))
    dw_down = mm(sb, dyb, "tn", BF16, f"{tag}_down_dw")
    on_dw("down", {"w_down": dw_down})
    ds = mm(dyb, p["w_down"], "nt", F32, f"{tag}_down_dx")
    dab, dbb = _swiglu_bwd(a, b, ds, f"{tag}_swiglu_bwd", pipe=carry.get(f"{tag}_swiglu_bwd"))
    dw_gate = mm(hb, dab, "tn", BF16, f"{tag}_gate_dw", out_slots=_slots(p["w_gate"]))
    dw_up = mm(hb, dbb, "tn", BF16, f"{tag}_up_dw", out_slots=_slots(p["w_up"]))
    on_dw("gate_up", {"w_gate": dw_gate, "w_up": dw_up})
    dh = mm(dab, p["w_gate"], "nt", F32, f"{tag}_gate_dx")
    dh = mm(dbb, p["w_up"], "nt", F32, f"{tag}_up_dx", add=dh)
    dx, dpre = _norm_bwd(x, p["pre_g"], dh, 1.0, F32, f"{tag}_pre_norm_bwd", res=dxo, pipe=carry.get(f"{tag}_pre_norm_bwd"))
    return dx, {"pre_g": dpre, "w_gate": dw_gate, "w_up": dw_up, "w_down": dw_down, "post_g": dpost}


def _mixer_fwd(x, cos, sin, p, carry, rest_of):
    mm = lambda a, b, mode, dt, name, **kw: _mm(a, b, mode, dt, name, pipe=carry.get(name), **kw)
    scale = (HEAD + ROPE) ** -0.5
    hb = _norm_fwd(x, p["pre_g"], "mix_pre_norm")
    u = mm(hb, p["w_hg"], "nn", F32, "mix_in_hg")
    cq = mm(hb, p["w_cq"], "nn", F32, "mix_in_cq")
    ckv = mm(hb, p["w_ckv"], "nn", F32, "mix_in_ckv")
    kpe = mm(hb, p["w_kpe"], "nn", F32, "mix_in_kpe")
    ga = mm(hb, p["w_ga"], "nn", F32, "mix_in_ga")
    gb = mm(hb, p["w_gb"], "nn", F32, "mix_in_gb")
    lb = _lb_fwd(p["lb_logits"], "hg_lb")
    o_raw, states = _hg_scan_fwd(u, lb, "hg_scan", pipe=carry.get("hg_scan"))
    oa = _hg_tail_fwd(o_raw, u, p["hg_norm_g"], "hg_tail")
    p = {**p, **rest_of()}
    ya = mm(oa, p["w_branch_a"], "nn", F32, "mix_branch_a")
    cqn = _norm_fwd(cq, p["q_norm_g"], "mla_q_norm")
    qn = mm(cqn, p["w_qn"], "nn", BF16, "mla_q_up_nope")
    qpe = _rope_q_fwd(mm(cqn, p["w_qpe"], "nn", F32, "mla_q_up_pe"), cos, sin, "mla_rope_q")
    ckvn = _norm_fwd(ckv, p["kv_norm_g"], "mla_kv_norm")
    kn = mm(ckvn, p["w_kn"], "nn", BF16, "mla_k_up")
    vv = mm(ckvn, p["w_vv"], "nn", BF16, "mla_v_up")
    kpe2 = _rope_k_fwd(kpe, cos, sin, "mla_rope_k")
    ob = _attn_fwd(qn, kn, vv, scale, "mla_attn", qpe=qpe, kpe=kpe2, causal=True, pipe=carry.get("mla_attn"))
    yb = mm(ob, p["w_branch_b"], "nn", F32, "mix_branch_b")
    ym = _merge_fwd(ga, gb, ya, yb, "mix_merge")
    z = mm(ym, p["w_out"], "nn", F32, "mix_out")
    xo = _postnorm_fwd(x, z, p["post_g"], 1.0, "mix_post_norm")
    saved = (x, hb, u, cq, ckv, ga, gb, lb, o_raw, states, oa, ya, cqn, qn, qpe, ckvn, kn, vv, kpe2, ob, yb, ym, z)
    return xo, saved, p


def _mixer_bwd(dxo, cos, sin, p, saved, carry, on_early, on_late):
    x, hb, u, cq, ckv, ga, gb, lb, o_raw, states, oa, ya, cqn, qn, qpe, ckvn, kn, vv, kpe2, ob, yb, ym, z = saved
    scale = (HEAD + ROPE) ** -0.5
    g = {}
    dzb, g["post_g"] = _norm_bwd(z, p["post_g"], dxo, 1.0, BF16, "mix_post_norm_bwd")
    g["w_out"] = _mm(ym, dzb, "tn", BF16, "mix_out_dw")
    dym = _mm(dzb, p["w_out"], "nt", F32, "mix_out_dx")
    dya, dyb, dga, dgb = _merge_bwd(ga, gb, ya, yb, dym, "mix_merge_bwd")
    g["w_branch_a"] = _mm(oa, dya, "tn", BF16, "mix_branch_a_dw")
    doa = _mm(dya, p["w_branch_a"], "nt", F32, "mix_branch_a_dx")
    do_raw, dog, g["hg_norm_g"] = _hg_tail_bwd(o_raw, u, p["hg_norm_g"], doa, "hg_tail_bwd")
    du, dlb = _hg_scan_bwd(u, lb, do_raw, dog, states, "hg_scan_bwd", pipe=carry.get("hg_scan_bwd"))
    g["lb_logits"] = _lb_bwd(lb, dlb, "hg_lb_bwd")
    g["w_branch_b"] = _mm(ob, dyb, "tn", BF16, "mix_branch_b_dw")
    on_early({k: g[k] for k in ("w_out", "w_branch_a", "w_branch_b")})
    dob = _mm(dyb, p["w_branch_b"], "nt", BF16, "mix_branch_b_dx")
    dqn, dkn, dvv, dqpe_h, dkpe_h = _attn_bwd(qn, kn, vv, dob, scale, "mla_attn_bwd", qpe=qpe, kpe=kpe2, causal=True,
                                              pipe=carry.get("mla_attn_bwd"))
    dqpe = _rope_q_bwd(dqpe_h, cos, sin, "mla_rope_q_bwd")
    dkpe = _rope_k_bwd(dkpe_h, cos, sin, "mla_rope_k_bwd")
    g["w_qn"] = _mm(cqn, dqn, "tn", BF16, "mla_q_up_nope_dw")
    g["w_qpe"] = _mm(cqn, dqpe, "tn", BF16, "mla_q_up_pe_dw")
    dcqn = _mm(dqn, p["w_qn"], "nt", F32, "mla_q_up_nope_dx")
    dcqn = _mm(dqpe, p["w_qpe"], "nt", F32, "mla_q_up_pe_dx", add=dcqn)
    dcq, g["q_norm_g"] = _norm_bwd(cq, p["q_norm_g"], dcqn, 1.0, BF16, "mla_q_norm_bwd")
    g["w_kn"] = _mm(ckvn, dkn, "tn", BF16, "mla_k_up_dw")
    g["w_vv"] = _mm(ckvn, dvv, "tn", BF16, "mla_v_up_dw")
    dckvn = _mm(dkn, p["w_kn"], "nt", F32, "mla_k_up_dx")
    dckvn = _mm(dvv, p["w_vv"], "nt", F32, "mla_v_up_dx", add=dckvn)
    dckv, g["kv_norm_g"] = _norm_bwd(ckv, p["kv_norm_g"], dckvn, 1.0, BF16, "mla_kv_norm_bwd")
    parts = (("w_hg", du), ("w_cq", dcq), ("w_ckv", dckv), ("w_kpe", dkpe), ("w_ga", dga), ("w_gb", dgb))
    for key, d in parts:
        g[key] = _mm(hb, d, "tn", BF16, f"mix_in_{key}_dw")
    on_late(g)
    dh = None
    for key, d in parts:
        dh = _mm(d, p[key], "nt", F32, f"mix_in_{key}_dx", add=dh, pipe=carry.get(f"mix_in_{key}_dx"))
    dx, g["pre_g"] = _norm_bwd(x, p["pre_g"], dh, 1.0, F32, "mix_pre_norm_bwd", res=dxo, pipe=carry.get("mix_pre_norm_bwd"))
    return dx, g


def _xa_fwd(x, mem, p, carry):
    mm = lambda a, b, mode, dt, name, **kw: _mm(a, b, mode, dt, name, pipe=carry.get(name), **kw)
    scale = HEAD ** -0.5
    hb = _norm_fwd(x, p["pre_g"], "xa_pre_norm")
    mb = _norm_fwd(mem, p["mem_g"], "xa_mem_norm")
    q = mm(hb, p["w_q"], "nn", BF16, "xa_q")
    k = mm(mb, p["w_k"], "nn", BF16, "xa_k")
    v = mm(mb, p["w_v"], "nn", BF16, "xa_v")
    o = _attn_fwd(q, k, v, scale, "xa_attn", pipe=carry.get("xa_attn"))
    z = mm(o, p["w_o"], "nn", F32, "xa_o")
    xo = _postnorm_fwd(x, z, p["post_g"], 1.0, "xa_post_norm")
    return xo, (x, mem, hb, mb, q, k, v, o, z)


def _xa_bwd(dxo, p, saved):
    x, mem, hb, mb, q, k, v, o, z = saved
    scale = HEAD ** -0.5
    g = {}
    dzb, g["post_g"] = _norm_bwd(z, p["post_g"], dxo, 1.0, BF16, "xa_post_norm_bwd")
    g["w_o"] = _mm(o, dzb, "tn", BF16, "xa_o_dw", out_slots=_slots(p["w_o"]))
    do = _mm(dzb, p["w_o"], "nt", BF16, "xa_o_dx")
    dq, dk, dv = _attn_bwd(q, k, v, do, scale, "xa_attn_bwd")
    g["w_q"] = _mm(hb, dq, "tn", BF16, "xa_q_dw")
    g["w_k"] = _mm(mb, dk, "tn", BF16, "xa_k_dw")
    g["w_v"] = _mm(mb, dv, "tn", BF16, "xa_v_dw")
    dh = _mm(dq, p["w_q"], "nt", F32, "xa_q_dx")
    dm = _mm(dk, p["w_k"], "nt", F32, "xa_k_dx")
    dm = _mm(dv, p["w_v"], "nt", F32, "xa_v_dx", add=dm)
    _, g["mem_g"] = _norm_bwd(mem, p["mem_g"], dm, 1.0, BF16, "xa_mem_norm_bwd")
    dx, g["pre_g"] = _norm_bwd(x, p["pre_g"], dh, 1.0, F32, "xa_pre_norm_bwd", res=dxo)
    return dx, g


def _local_step(x, mem, cos, sin, target, params_of, carry, on_grads):
    p1 = params_of("ffn1")
    x1, s1 = _ffn_fwd(x, p1, "ffn1", carry)
    x2, s2, p2 = _mixer_fwd(x1, cos, sin, params_of("mix"), carry, lambda: params_of("mix_rest"))
    p3 = params_of("xa")
    x3, s3 = _xa_fwd(x2, mem, p3, carry)
    p4 = params_of("ffn2")
    x4, s4 = _ffn_fwd(x3, p4, "ffn2", carry)
    dy, sq_err = _loss_head(x4, target, "loss_head")
    loss = 0.5 / x.shape[1] * jnp.sum(sq_err)
    dx, g4 = _ffn_bwd(dy, p4, s4, "ffn2", carry, lambda stage, g: on_grads(f"ffn2_dw_{stage}", g))
    on_grads("ffn2", g4)
    dx, g3 = _xa_bwd(dx, p3, s3)
    on_grads("xa", g3)
    dx, g2 = _mixer_bwd(dx, cos, sin, p2, s2, carry, lambda g: on_grads("mix_early", g), lambda g: on_grads("mix_late", g))
    on_grads("mix", g2)
    dx, g1 = _ffn_bwd(dx, p1, s1, "ffn1", carry, lambda stage, g: on_grads(f"ffn1_dw_{stage}", g))
    on_grads("ffn1", g1)
    return loss, dx


def _split_w_in(w_in):
    D = w_in.shape[0]
    H = D // HEAD
    lora = (w_in.shape[1] - 6 * D - ROPE) // 2
    o = 4 * D
    w_hg = w_in[:, :o].reshape(D, 4, H, HEAD).transpose(0, 2, 1, 3).reshape(D, 4 * D)
    w_cq, w_ckv = w_in[:, o:o + lora], w_in[:, o + lora:o + 2 * lora]
    o += 2 * lora
    w_kpe = jnp.pad(w_in[:, o:o + ROPE], ((0, 0), (0, HEAD - ROPE)))
    o += ROPE
    return {"w_hg": w_hg, "w_cq": w_cq, "w_ckv": w_ckv, "w_kpe": w_kpe, "w_ga": w_in[:, o:o + D], "w_gb": w_in[:, o + D:o + 2 * D]}


def _merge_w_in(g):
    D = g["w_ga"].shape[0]
    H = D // HEAD
    hg = g["w_hg"].reshape(D, H, 4, HEAD).transpose(0, 2, 1, 3).reshape(D, 4 * D)
    return jnp.concatenate([hg, g["w_cq"], g["w_ckv"], g["w_kpe"][:, :ROPE], g["w_ga"], g["w_gb"]], axis=1)


def _split_heads(w, rest):
    K, N = w.shape
    w3 = w.reshape(K, N // (HEAD + rest), HEAD + rest)
    return w3[:, :, :HEAD].reshape(K, -1), w3[:, :, HEAD:].reshape(K, -1)


def _merge_heads(a, b, rest):
    K = a.shape[0]
    H = a.shape[1] // HEAD
    return jnp.concatenate([a.reshape(K, H, HEAD), b.reshape(K, H, rest)], axis=2).reshape(K, H * (HEAD + rest))


BLOCK_WEIGHTS = {
    "ffn1": ("ffn1_w_gate", "ffn1_w_up", "ffn1_w_down"),
    "mix": ("w_in",),
    "mix_rest": ("mla_w_q_up", "mla_w_kv_up", "w_branch_a", "w_branch_b", "w_out"),
    "xa": ("xa_w_q", "xa_w_k", "xa_w_v", "xa_w_o"),
    "ffn2": ("ffn2_w_gate", "ffn2_w_up", "ffn2_w_down"),
}


def _block_params(block, w, small):
    if block in ("ffn1", "ffn2"):
        return {"pre_g": small[f"{block}_pre_g"], "w_gate": w[f"{block}_w_gate"], "w_up": w[f"{block}_w_up"],
                "w_down": w[f"{block}_w_down"], "post_g": small[f"{block}_post_g"]}
    if block == "xa":
        return {"pre_g": small["xa_pre_g"], "mem_g": small["xa_mem_g"], "post_g": small["xa_post_g"],
                "w_q": w["xa_w_q"], "w_k": w["xa_w_k"], "w_v": w["xa_w_v"], "w_o": w["xa_w_o"]}
    if block == "mix_rest":
        (w_qn, w_qpe), (w_kn, w_vv) = _split_heads(w["mla_w_q_up"], ROPE), _split_heads(w["mla_w_kv_up"], HEAD)
        return dict(w_qn=w_qn, w_qpe=w_qpe, w_kn=w_kn, w_vv=w_vv, w_branch_a=w["w_branch_a"],
                    w_branch_b=w["w_branch_b"], w_out=w["w_out"])
    mix = _split_w_in(w["w_in"])
    mix.update(pre_g=small["mix_pre_g"], post_g=small["mix_post_g"], hg_norm_g=small["hg_norm_g"],
               q_norm_g=small["mla_q_norm_g"], kv_norm_g=small["mla_kv_norm_g"], lb_logits=small["hgrn_lb_logits"])
    return mix


def _block_grads(block, g):
    if block in ("ffn1", "ffn2"):
        return {f"{block}_{k}": g[k] for k in ("pre_g", "w_gate", "w_up", "w_down", "post_g")}
    if block == "xa":
        return {f"xa_{k}": g[k] for k in ("pre_g", "mem_g", "post_g", "w_q", "w_k", "w_v", "w_o")}
    if block[4:8] == "_dw_":
        return {f"{block[:4]}_{k}": v for k, v in g.items()}
    if block == "mix_early":
        return dict(g)
    if block == "mix_late":
        return dict(w_in=_merge_w_in(g), mla_w_q_up=_merge_heads(g["w_qn"], g["w_qpe"], ROPE),
                    mla_w_kv_up=_merge_heads(g["w_kn"], g["w_vv"], HEAD))
    return dict(mix_pre_g=g["pre_g"], mix_post_g=g["post_g"],
                hg_norm_g=g["hg_norm_g"], mla_q_norm_g=g["q_norm_g"], mla_kv_norm_g=g["kv_norm_g"],
                hgrn_lb_logits=g["lb_logits"])


def _rope_tables(positions):
    inv_freq = 1.0 / (ROPE_THETA ** (jnp.arange(0, ROPE, 2, dtype=F32) / ROPE))
    ang = positions.astype(F32)[:, None] * inv_freq
    return jnp.tile(jnp.cos(ang), (1, 4)), jnp.tile(jnp.sin(ang), (1, 4))


def _adamw(w, g, m, v, name):
    bc1 = 1.0 - ADAM_B1 ** ADAM_STEP
    bc2 = 1.0 - ADAM_B2 ** ADAM_STEP

    def body(w_ref, g_ref, m_ref, v_ref, go_ref, d_ref, mo_ref, vo_ref):
        g = g_ref[...]
        m = ADAM_B1 * m_ref[...] + (1.0 - ADAM_B1) * g
        v = ADAM_B2 * v_ref[...] + (1.0 - ADAM_B2) * (g * g)
        go_ref[...] = g
        mo_ref[...] = m
        vo_ref[...] = v
        d_ref[...] = -ADAM_LR * ((m / bc1) / (jnp.sqrt(v / bc2) + ADAM_EPS) + ADAM_WD * w_ref[...])

    return _rows(body, [(w, "row"), (g, "row"), (m, "row"), (v, "row")], [(w.shape, F32, "row")] * 4, name)


ANY = pl.BlockSpec(memory_space=pl.ANY)
COMM_AXES = ("x", "y", "c")


def _place():
    x, y, c = (lax.axis_index(n) for n in COMM_AXES)
    chips = [(1 - x, y), (x, 1 - y), (1 - x, 1 - y)]
    return x, y, c, 2 * x + y, (x, y, 1 - c), chips


def _remote(src, dst, send_sems, recv_sems, j, to):
    return pltpu.make_async_remote_copy(src_ref=src, dst_ref=dst, send_sem=send_sems.at[j], recv_sem=recv_sems.at[j],
                                        device_id=to, device_id_type=MESH)


def _dma_sems(n):
    return [pltpu.SemaphoreType.DMA((n,)), pltpu.SemaphoreType.DMA((n,))]


def _all_gather(shards, whole, name):
    n = len(shards)

    def body(*refs):
        srcs, outs, send_sems, recv_sems = refs[:n], refs[n:2 * n], refs[2 * n], refs[2 * n + 1]
        x, y, c, me, sibling, chips = _place()
        sent = []

        def start(cp):
            cp.start()
            sent.append(cp)

        def rows(w, h):
            hr = srcs[w].shape[0] // 2
            return pl.ds(h * hr, hr)

        gathered = [w for w in range(n) if whole[w]]
        for w in gathered:
            for j, (cx, cy) in enumerate(chips):
                start(_remote(srcs[w].at[rows(w, c)], outs[w].at[me, rows(w, c)], send_sems, recv_sems, 7 * w + j, (cx, cy, c)))
        for w in range(n):
            start(_remote(srcs[w], outs[w].at[me], send_sems, recv_sems, 7 * w + 6, sibling))
        for w in gathered:
            for j, (cx, cy) in enumerate(chips):
                blk = outs[w].at[2 * cx + cy, rows(w, c)]
                _remote(srcs[w].at[rows(w, c)], blk, send_sems, recv_sems, 7 * w + j, (cx, cy, c)).wait_recv()
                start(_remote(blk, blk, send_sems, recv_sems, 7 * w + 3 + j, sibling))
        for w in gathered:
            for j, (cx, cy) in enumerate(chips):
                blk = outs[w].at[2 * cx + cy, rows(w, 1 - c)]
                _remote(blk, blk, send_sems, recv_sems, 7 * w + 3 + j, sibling).wait_recv()
        for w in range(n):
            _remote(srcs[w], outs[w].at[me], send_sems, recv_sems, 7 * w + 6, sibling).wait_recv()
        for cp in sent:
            cp.wait_send()

    return pl.pallas_call(
        body, in_specs=[ANY] * n, out_specs=[ANY] * n,
        out_shape=[jax.ShapeDtypeStruct((N_CHIP,) + s.shape, s.dtype) for s in shards],
        scratch_shapes=_dma_sems(7 * n), name=name,
    )(*shards)


def _rs_swap(grads, name):
    n = len(grads)

    def body(*refs):
        gs, outs, send_sems, recv_sems = refs[:n], refs[n:2 * n], refs[2 * n], refs[2 * n + 1]
        x, y, c, me, sibling, chips = _place()
        cps = []
        for w in range(n):
            hr = gs[w].shape[1] // 2
            cps.append(_remote(gs[w].at[:, pl.ds((1 - c) * hr, hr)], outs[w], send_sems, recv_sems, w, sibling))
            cps[-1].start()
        for cp in cps:
            cp.wait()

    return pl.pallas_call(
        body, in_specs=[ANY] * n, out_specs=[ANY] * n,
        out_shape=[jax.ShapeDtypeStruct((g.shape[0], g.shape[1] // 2, g.shape[2]), g.dtype) for g in grads],
        scratch_shapes=_dma_sems(n), name=name,
    )(*grads)


def _sum_rows(hr, row_bytes):
    return _tile(hr, max(16, ROW_BUDGET // (2 * row_bytes) // 16 * 16), 16)


def _rs_pair_sum(g, got, c, name):
    S, r, cw = g.shape
    hr = r // 2
    tr = _sum_rows(hr, 3 * cw * 2)
    nrb = hr // tr

    def body(c_ref, a_ref, b_ref, o_ref):
        o_ref[...] = (a_ref[...].astype(F32) + b_ref[...].astype(F32)).astype(BF16)

    return pl.pallas_call(
        body,
        grid_spec=pltpu.PrefetchScalarGridSpec(
            num_scalar_prefetch=1, grid=(S, nrb),
            in_specs=[pl.BlockSpec((None, tr, cw), lambda k, i, c_ref: (k, c_ref[0] * nrb + i, 0)),
                      pl.BlockSpec((None, tr, cw), lambda k, i, c_ref: (k, i, 0))],
            out_specs=pl.BlockSpec((None, tr, cw), lambda k, i, c_ref: (k, i, 0)),
        ),
        out_shape=jax.ShapeDtypeStruct((S, hr, cw), BF16),
        compiler_params=_params(("parallel", "parallel")), name=name,
    )(c, g, got)


def _rs_chip_sum(pair, got, place, name):
    S, hr, cw = pair.shape
    tr = _sum_rows(hr, cw * (4 * 2 + 4))

    def body(p_ref, a_ref, z_ref, o_ref):
        o_ref[...] = a_ref[...].astype(F32) + z_ref[0].astype(F32) + z_ref[1].astype(F32) + z_ref[2].astype(F32)

    return pl.pallas_call(
        body,
        grid_spec=pltpu.PrefetchScalarGridSpec(
            num_scalar_prefetch=1, grid=(hr // tr,),
            in_specs=[pl.BlockSpec((None, tr, cw), lambda i, p_ref: (p_ref[0], i, 0)),
                      pl.BlockSpec((3, tr, cw), lambda i, p_ref: (0, i, 0))],
            out_specs=pl.BlockSpec((None, tr, cw), lambda i, p_ref: (p_ref[1], i, 0)),
        ),
        out_shape=jax.ShapeDtypeStruct((2, hr, cw), F32),
        compiler_params=_params(("parallel",)), name=name,
    )(place, pair, got)


def _rs_share(halves, name):
    n = len(halves)

    def body(*refs):
        outs, send_sems, recv_sems = refs[n:2 * n], refs[2 * n], refs[2 * n + 1]
        x, y, c, me, sibling, chips = _place()
        cps = []
        for w in range(n):
            cps.append(_remote(outs[w].at[c], outs[w].at[c], send_sems, recv_sems, w, sibling))
            cps[-1].start()
        for w in range(n):
            _remote(outs[w].at[1 - c], outs[w].at[1 - c], send_sems, recv_sems, w, sibling).wait_recv()
        for cp in cps:
            cp.wait_send()

    return pl.pallas_call(
        body, in_specs=[ANY] * n, out_specs=[ANY] * n,
        out_shape=[jax.ShapeDtypeStruct(h.shape, h.dtype) for h in halves],
        input_output_aliases={i: i for i in range(n)},
        scratch_shapes=_dma_sems(n), name=name,
    )(*halves)


def _rs_begin(names, grads, core, fractions, tag):
    slotted = [_to_slots(n, grads[n]) for n in names]
    got = _rs_swap(slotted, f"grads_sibling_swap_{tag}")
    pairs = [_rs_pair_sum(s, t, core.reshape(1), f"grads_pair_sum_{n}") for n, s, t in zip(names, slotted, got)]
    lands = [lax.empty((N_CHIP - 1,) + p.shape[1:], p.dtype) for p in pairs]
    return pairs, _Pipe("reduce", pairs, lands, fractions)


def _rs_end(names, pairs, pipe, place, shapes, tag):
    _pipe_flush(pipe, f"grads_chip_exchange_rest_{tag}")
    halves = [_rs_chip_sum(p, o, place, f"grads_chip_sum_{n}") for n, p, o in zip(names, pairs, pipe.lands)]
    both = _rs_share(halves, f"grads_sibling_share_{tag}")
    return {n: b.reshape(s) for n, b, s in zip(names, both, shapes)}


def _all_reduce_small(s, name):
    flips = [(dx, dy, dc) for dx in (0, 1) for dy in (0, 1) for dc in (0, 1) if (dx, dy, dc) != (0, 0, 0)]

    def body(s_ref, o_ref, buf, send_sems, recv_sems):
        x, y, c = (lax.axis_index(n) for n in COMM_AXES)
        me = 4 * x + 2 * y + c
        buf[me] = s_ref[...]
        peers = [((1 - x) if dx else x, (1 - y) if dy else y, (1 - c) if dc else c) for dx, dy, dc in flips]
        sent = [_remote(s_ref, buf.at[me], send_sems, recv_sems, j, p) for j, p in enumerate(peers)]
        for cp in sent:
            cp.start()
        for j, (px, py, pc) in enumerate(peers):
            _remote(s_ref, buf.at[4 * px + 2 * py + pc], send_sems, recv_sems, j, (px, py, pc)).wait_recv()
        for cp in sent:
            cp.wait_send()
        acc = buf[0]
        for d in range(1, N_DEV):
            acc = acc + buf[d]
        o_ref[...] = acc

    vmem = pl.BlockSpec(memory_space=pltpu.VMEM)
    return pl.pallas_call(
        body, in_specs=[vmem], out_specs=vmem, out_shape=jax.ShapeDtypeStruct(s.shape, F32),
        scratch_shapes=[pltpu.VMEM((N_DEV,) + s.shape, F32), pltpu.SemaphoreType.DMA((7,)), pltpu.SemaphoreType.DMA((7,))],
        name=name,
    )(s)


BIG = {
    "ffn1_w_gate": 1, "ffn1_w_up": 1, "ffn1_w_down": 0, "w_in": 1, "mla_w_q_up": 1, "mla_w_kv_up": 1,
    "w_branch_a": 0, "w_branch_b": 0, "w_out": 0, "xa_w_q": 0, "xa_w_k": 0, "xa_w_v": 0, "xa_w_o": 1,
    "ffn2_w_gate": 1, "ffn2_w_up": 1, "ffn2_w_down": 0,
}
WEIGHTS = [
    "hgrn_lb_logits", "ffn1_pre_g", "ffn1_w_gate", "ffn1_w_up", "ffn1_w_down", "ffn1_post_g", "mix_pre_g", "w_in",
    "hg_norm_g", "mla_q_norm_g", "mla_w_q_up", "mla_kv_norm_g", "mla_w_kv_up", "w_branch_a", "w_branch_b", "w_out",
    "mix_post_g", "xa_pre_g", "xa_mem_g", "xa_w_q", "xa_w_k", "xa_w_v", "xa_w_o", "xa_post_g", "ffn2_pre_g",
    "ffn2_w_gate", "ffn2_w_up", "ffn2_w_down", "ffn2_post_g",
]
SMALL = [n for n in WEIGHTS if n not in BIG]
SLOTTED = ("ffn1_w_gate", "ffn1_w_up", "ffn2_w_gate", "ffn2_w_up", "xa_w_o")
MIX_EARLY = ("w_out", "w_branch_a", "w_branch_b")


def _from_slots(name, g):
    S, r, cw = g.shape
    if BIG[name] == 0:
        return g.reshape(S * r, cw)
    return g if name in SLOTTED else g.transpose(1, 0, 2).reshape(r, S * cw)


def _to_slots(name, g):
    if g.ndim == 3:
        return g
    if BIG[name] == 0:
        return g.reshape(N_CHIP, g.shape[0] // N_CHIP, g.shape[1])
    return g.reshape(g.shape[0], N_CHIP, g.shape[1] // N_CHIP).transpose(1, 0, 2)


def _pack_small(vals, width):
    rows = [jnp.pad(v, ((0, 0), (0, width - v.shape[1]))) for v in vals]
    s = jnp.concatenate(rows, axis=0)
    return jnp.pad(s, ((0, -s.shape[0] % 8), (0, 0)))


def _unpack_small(s, shapes):
    out, o = [], 0
    for r, w in shapes:
        out.append(s[o:o + r, :w])
        o += r
    return out


def kernel(x, mem, positions, hgrn_lb_logits, ffn1_pre_g, ffn1_w_gate, ffn1_w_up, ffn1_w_down, ffn1_post_g, mix_pre_g, w_in, hg_norm_g, mla_q_norm_g, mla_w_q_up, mla_kv_norm_g, mla_w_kv_up, w_branch_a, w_branch_b, w_out, mix_post_g, xa_pre_g, xa_mem_g, xa_w_q, xa_w_k, xa_w_v, xa_w_o, xa_post_g, ffn2_pre_g, ffn2_w_gate, ffn2_w_up, ffn2_w_down, ffn2_post_g, loss_target, m_hgrn_lb_logits, m_ffn1_pre_g, m_ffn1_w_gate, m_ffn1_w_up, m_ffn1_w_down, m_ffn1_post_g, m_mix_pre_g, m_w_in, m_hg_norm_g, m_mla_q_norm_g, m_mla_w_q_up, m_mla_kv_norm_g, m_mla_w_kv_up, m_w_branch_a, m_w_branch_b, m_w_out, m_mix_post_g, m_xa_pre_g, m_xa_mem_g, m_xa_w_q, m_xa_w_k, m_xa_w_v, m_xa_w_o, m_xa_post_g, m_ffn2_pre_g, m_ffn2_w_gate, m_ffn2_w_up, m_ffn2_w_down, m_ffn2_post_g, v_hgrn_lb_logits, v_ffn1_pre_g, v_ffn1_w_gate, v_ffn1_w_up, v_ffn1_w_down, v_ffn1_post_g, v_mix_pre_g, v_w_in, v_hg_norm_g, v_mla_q_norm_g, v_mla_w_q_up, v_mla_kv_norm_g, v_mla_w_kv_up, v_w_branch_a, v_w_branch_b, v_w_out, v_mix_post_g, v_xa_pre_g, v_xa_mem_g, v_xa_w_q, v_xa_w_k, v_xa_w_v, v_xa_w_o, v_xa_post_g, v_ffn2_pre_g, v_ffn2_w_gate, v_ffn2_w_up, v_ffn2_w_down, v_ffn2_post_g):
    a = dict(locals())
    big = list(BIG)
    small = {n: a[n] for n in SMALL}
    core = lax.axis_index("c").astype(jnp.int32)
    place = jnp.stack([(2 * lax.axis_index("x") + lax.axis_index("y")).astype(jnp.int32), core])

    shards = {n: a[n][0].astype(BF16) for n in big}
    whole = [n in BLOCK_WEIGHTS["ffn1"] for n in big]
    lands = dict(zip(big, _all_gather([shards[n] for n in big], whole, "weights_all_gather")))
    plan = {
        "mix": ((2, 10, 10, 6, 10, 3), [f"ffn1_{k}" for k in ("pre_norm", "gate", "up", "swiglu", "down", "post_norm")]),
        "mix_rest": ((8, 1, 1, 3, 3), ["mix_in_hg", "mix_in_cq", "mix_in_ckv", "mix_in_ga", "mix_in_gb"]),
        "ffn2": ((4, 3), ["hg_scan", "mla_attn"]),
        "xa": ((1, 1), ["mix_branch_b", "mix_out"]),
    }
    carry, pipes = {}, {}
    for block, (fractions, carriers) in plan.items():
        names = BLOCK_WEIGHTS[block]
        pipes[block] = _Pipe("gather", [shards[n] for n in names], [lands[n] for n in names], fractions)
        carry.update({c: pipes[block] for c in carriers})

    def params_of(block):
        if block in pipes:
            _pipe_flush(pipes[block], f"weights_gather_rest_{block}")
            lands.update(zip(BLOCK_WEIGHTS[block], pipes[block].lands))
        return _block_params(block, {n: _from_slots(n, lands[n]) for n in BLOCK_WEIGHTS[block]}, small)

    g_small, g_big, open_groups, held = {}, {}, {}, {}

    def begin(tag, names, fractions, carriers):
        pairs, pipe = _rs_begin(names, held, core, fractions, tag)
        open_groups[tag] = (names, pairs, pipe)
        carry.update({c: pipe for c in carriers})

    def end(tag):
        names, pairs, pipe = open_groups.pop(tag)
        g_big.update(_rs_end(names, pairs, pipe, place, [a[n].shape[1:] for n in names], tag))

    def on_grads(block, g):
        for n, v in _block_grads(block, g).items():
            (held if n in BIG else g_small)[n] = v
        if block == "ffn2":
            begin("ffn2", BLOCK_WEIGHTS["ffn2"], (1,), ("hg_scan_bwd",))
        elif block == "mix_early":
            begin("early", BLOCK_WEIGHTS["xa"] + MIX_EARLY, (1,), ("mla_attn_bwd",))
        elif block == "mix_late":
            end("ffn2")
            end("early")
            begin("mid", ["w_in", "mla_w_q_up", "mla_w_kv_up"], (10, 3, 3, 3, 2, 6, 6),
                  ("mix_in_w_hg_dx", "mix_in_w_ga_dx", "mix_in_w_gb_dx", "mix_pre_norm_bwd",
                   "ffn1_post_norm_bwd", "ffn1_down_dw", "ffn1_down_dx"))
        elif block == "ffn1_dw_down":
            begin("ffn1_down", ["ffn1_w_down"], (1, 1, 1), [f"ffn1_{k}" for k in ("swiglu_bwd", "gate_dw", "up_dw")])
        elif block == "ffn1_dw_gate_up":
            end("mid")
            begin("ffn1_gate_up", ["ffn1_w_gate", "ffn1_w_up"], (8, 9, 3), [f"ffn1_{k}" for k in ("gate_dx", "up_dx", "pre_norm_bwd")])
        elif block == "ffn1":
            end("ffn1_down")
            end("ffn1_gate_up")

    cos, sin = _rope_tables(positions[0])
    loss_part, grad_x = _local_step(x[0], mem[0], cos, sin, loss_target[0], params_of, carry, on_grads)

    small_shapes = [a[n].shape for n in SMALL]
    width = max(s[1] for s in small_shapes)
    gs = _all_reduce_small(_pack_small([g_small[n] for n in SMALL], width), "small_grads_all_reduce")

    out_g, out_d, out_m, out_v = {}, {}, {}, {}
    for n in big:
        flip = (lambda t: jnp.swapaxes(t, 0, 1)) if a[n].shape[2] % 128 else (lambda t: t)
        res = _adamw(*(flip(t) for t in (a[n][0], g_big[n], a["m_" + n][0], a["v_" + n][0])), f"adamw_{n}")
        out_g[n], out_d[n], out_m[n], out_v[n] = (flip(t).reshape(a[n].shape) for t in res)
    sw, sm, sv = (_pack_small([a[p + n] for n in SMALL], width) for p in ("", "m_", "v_"))
    for t, dst in zip(_adamw(sw, gs, sm, sv, "adamw_small"), (out_g, out_d, out_m, out_v)):
        dst.update(zip(SMALL, _unpack_small(t, small_shapes)))

    loss = lax.psum(loss_part, COMM_AXES)
    return (loss, grad_x[None], *[out_g[n] for n in WEIGHTS], *[out_d[n] for n in WEIGHTS],
            *[out_m[n] for n in WEIGHTS], *[out_v[n] for n in WEIGHTS])
```

```python
import functools

import jax
import jax.numpy as jnp
from jax import lax
from jax.experimental import pallas as pl
from jax.experimental.pallas import tpu as pltpu

F32 = jnp.float32
BF16 = jnp.bfloat16
EPS = 1e-6
HEAD = 128
ROPE = 64
CHUNK = 64
SUB = 16
SUB_FWD = 32
HG_BLOCK = 256
ROPE_THETA = 10000.0
FFN_RESIDUAL_WEIGHT = 0.5
ADAM_LR, ADAM_B1, ADAM_B2, ADAM_EPS, ADAM_WD, ADAM_STEP = 0.001, 0.9, 0.999, 1e-08, 0.01, 10
VMEM_LIMIT = 56 * 2**20
ROW_BUDGET = 20 * 2**20
NEG = -1e30
MESH = pl.DeviceIdType.MESH
N_CHIP = 4
N_DEV = 8


def _params(sem):
    return pltpu.CompilerParams(dimension_semantics=sem, vmem_limit_bytes=VMEM_LIMIT)


def _tile(n, cap, mult):
    if n <= cap:
        return n
    t = (cap // mult) * mult
    while t >= mult:
        if n % t == 0:
            return t
        t -= mult
    raise ValueError(f"no tile for {n} under {cap}")


def _split_rows(n, fractions):
    if n % 16:
        return [(0, n)] + [(n, 0)] * (len(fractions) - 1)
    units, total, acc, cuts = n // 16, sum(fractions), 0, [0]
    for f in fractions[:-1]:
        acc += f
        cuts.append(round(units * acc / total))
    cuts.append(units)
    return [(16 * lo, 16 * (hi - lo)) for lo, hi in zip(cuts, cuts[1:])]


class _Pipe:
    def __init__(self, kind, srcs, lands, fractions):
        self.kind, self.srcs, self.lands = kind, list(srcs), list(lands)
        per_w = [_split_rows(s.shape[0] // 2 if kind == "gather" else s.shape[1], fractions) for s in srcs]
        self.parts = [[pw[i] for pw in per_w] for i in range(len(fractions))]
        self.taken = 0
        self.pending = None
        self.sems = (9 if kind == "gather" else 3) * len(srcs)

    def take(self):
        self.taken += 1
        rows, prev = self.parts[self.taken - 1], self.pending
        self.pending = rows if self.kind == "gather" else None
        return rows, prev

    def rest(self):
        left, prev = self.parts[self.taken:], self.pending
        self.taken, self.pending = len(self.parts), None
        n = len(self.srcs)
        rows = [(left[0][w][0], sum(p[w][1] for p in left)) for w in range(n)] if left else [(0, 0)] * n
        return (rows, prev) if (left or prev) else None


def _pipe_copies(kind, rows, lands, srcs, send_sems, recv_sems):
    x, y, c, me, sibling, chips = _place()
    out = []
    per = 9 if kind == "gather" else 3
    for w, (r0, nr) in enumerate(rows):
        for j, (cx, cy) in enumerate(chips if nr else []):
            k, to = 2 * cx + cy, (cx, cy, c)
            if kind == "gather":
                rs = pl.ds(c * (srcs[w].shape[0] // 2) + r0, nr)
                src, dst, got = srcs[w].at[rs], lands[w].at[me, rs], lands[w].at[k, rs]
            else:
                rs = pl.ds(r0, nr)
                src, dst, got = srcs[w].at[k, rs], lands[w].at[j, rs], lands[w].at[j, rs]
            out.append((_remote(src, dst, send_sems, recv_sems, per * w + j, to), _remote(src, got, send_sems, recv_sems, per * w + j, to)))
    return out


def _pipe_handover(rows, bank, lands, srcs, send_sems, recv_sems):
    x, y, c, me, sibling, chips = _place()
    out = []
    for w, (r0, nr) in enumerate(rows or []):
        hr = srcs[w].shape[0] // 2
        for j, (cx, cy) in enumerate(chips if nr else []):
            mine = lands[w].at[2 * cx + cy, pl.ds(c * hr + r0, nr)]
            theirs = lands[w].at[2 * cx + cy, pl.ds((1 - c) * hr + r0, nr)]
            sem = 9 * w + 3 + 3 * bank + j
            out.append((_remote(mine, mine, send_sems, recv_sems, sem, sibling), _remote(theirs, theirs, send_sems, recv_sems, sem, sibling)))
    return out


def _pipe_start(kind, rows, prev, lands, srcs, send_sems, recv_sems):
    for send, _ in _pipe_copies(kind, rows, lands, srcs, send_sems, recv_sems):
        send.start()
    for send, _ in _pipe_handover(prev, 0, lands, srcs, send_sems, recv_sems):
        send.start()


def _pipe_finish(kind, rows, prev, last, lands, srcs, send_sems, recv_sems):
    copies = _pipe_copies(kind, rows, lands, srcs, send_sems, recv_sems)
    over = _pipe_handover(prev, 0, lands, srcs, send_sems, recv_sems)
    for _, arrival in copies + over:
        arrival.wait_recv()
    if kind == "gather" and last:
        final = _pipe_handover(rows, 1, lands, srcs, send_sems, recv_sems)
        for send, _ in final:
            send.start()
        for _, arrival in final:
            arrival.wait_recv()
        over = over + final
    for send, _ in copies + over:
        send.wait_send()


def _carry_call(body, *, grid, in_specs, out_specs, out_shape, scratch_shapes, sem, name, args, pipe=None):
    single = not isinstance(out_shape, (list, tuple))
    if single:
        out_specs, out_shape = [out_specs], [out_shape]
    if pipe is None or pipe.taken >= len(pipe.parts):
        res = pl.pallas_call(body, grid=grid, in_specs=in_specs, out_specs=out_specs, out_shape=out_shape,
                             scratch_shapes=scratch_shapes, compiler_params=_params(sem), name=name)(*args)
        return res[0] if single else res
    (rows, prev), kind = pipe.take(), pipe.kind
    n_in, n_out, n_l, n_s, n_scr = len(args), len(out_shape), len(pipe.lands), len(pipe.srcs), len(scratch_shapes)

    def wrapped(*refs):
        ins, srcs = refs[:n_in], refs[n_in + n_l:n_in + n_l + n_s]
        o0 = n_in + n_l + n_s
        outs, lands = refs[o0:o0 + n_out], refs[o0 + n_out:o0 + n_out + n_l]
        scr = refs[o0 + n_out + n_l:o0 + n_out + n_l + n_scr]
        send_sems, recv_sems = refs[-2], refs[-1]
        ids = [pl.program_id(ax) for ax in range(len(grid))]
        first = functools.reduce(jnp.logical_and, [i == 0 for i in ids])
        last = functools.reduce(jnp.logical_and, [i == g - 1 for i, g in zip(ids, grid)])

        @pl.when(first)
        def _():
            _pipe_start(kind, rows, prev, lands, srcs, send_sems, recv_sems)

        body(*ins, *outs, *scr)

        @pl.when(last)
        def _():
            _pipe_finish(kind, rows, prev, False, lands, srcs, send_sems, recv_sems)

    res = pl.pallas_call(
        wrapped, grid=grid, in_specs=list(in_specs) + [ANY] * (n_l + n_s), out_specs=list(out_specs) + [ANY] * n_l,
        out_shape=list(out_shape) + [jax.ShapeDtypeStruct(l.shape, l.dtype) for l in pipe.lands],
        input_output_aliases={n_in + i: n_out + i for i in range(n_l)},
        scratch_shapes=list(scratch_shapes) + _dma_sems(pipe.sems),
        compiler_params=_params(("arbitrary",) * len(grid)), name=name,
    )(*args, *pipe.lands, *pipe.srcs)
    pipe.lands = list(res[n_out:])
    return res[0] if single else list(res[:n_out])


def _pipe_flush(pipe, name):
    todo = pipe.rest()
    if todo is None:
        return
    rows, prev = todo
    n_l, n_s, kind = len(pipe.lands), len(pipe.srcs), pipe.kind

    def body(*refs):
        srcs, lands = refs[n_l:n_l + n_s], refs[n_l + n_s:2 * n_l + n_s]
        _pipe_start(kind, rows, prev, lands, srcs, refs[-2], refs[-1])
        _pipe_finish(kind, rows, prev, True, lands, srcs, refs[-2], refs[-1])

    pipe.lands = list(pl.pallas_call(
        body, in_specs=[ANY] * (n_l + n_s), out_specs=[ANY] * n_l,
        out_shape=[jax.ShapeDtypeStruct(l.shape, l.dtype) for l in pipe.lands],
        input_output_aliases={i: i for i in range(n_l)}, scratch_shapes=_dma_sems(pipe.sems), name=name,
    )(*pipe.lands, *pipe.srcs))


def _sigmoid(x):
    return 1.0 / (1.0 + jnp.exp(-x))


def _dot(a, b, dims):
    return lax.dot_general(a, b, (dims, ((), ())), preferred_element_type=F32)


NN = ((1,), (0,))
NT = ((1,), (1,))
TN = ((0,), (0,))


def _mm(a, b, mode, out_dtype, name, add=None, out_slots=0, pipe=None, tm_cap=1024, tn_cap=512, tk_cap=2816):
    slot_cap = 1408
    b_slots = b.shape[0] if b.ndim == 3 else 0
    bs = (b.shape[1], b_slots * b.shape[2]) if b_slots else b.shape
    if mode == "nn":
        (M, K), (K2, N) = a.shape, bs
    elif mode == "nt":
        (M, K), (N, K2) = a.shape, bs
    else:
        (K, M), (K2, N) = a.shape, bs
    assert K == K2, (a.shape, b.shape, mode)
    tm = _tile(M, tm_cap, 128)
    tn = _tile(N // (b_slots or out_slots), slot_cap, 128) if (out_slots or (b_slots and mode == "nn")) else _tile(N, tn_cap, 128)
    tk = _tile(K // b_slots, slot_cap, 128) if (b_slots and mode == "nt") else _tile(K, tk_cap, 128)
    nk = K // tk
    a_spec = pl.BlockSpec((tk, tm), lambda i, j, k: (k, i)) if mode == "tn" else pl.BlockSpec((tm, tk), lambda i, j, k: (i, k))
    if b_slots and mode == "nn":
        per = b.shape[2] // tn
        b_spec = pl.BlockSpec((None, tk, tn), lambda i, j, k: (j // per, k, j % per))
    elif b_slots:
        per = b.shape[2] // tk
        b_spec = pl.BlockSpec((None, tn, tk), lambda i, j, k: (k // per, j, k % per))
    else:
        b_spec = pl.BlockSpec((tn, tk), lambda i, j, k: (j, k)) if mode == "nt" else pl.BlockSpec((tk, tn), lambda i, j, k: (k, j))
    if out_slots:
        per_o = N // out_slots // tn
        o_spec = pl.BlockSpec((None, tm, tn), lambda i, j, k: (j // per_o, i, j % per_o))
        o_shape = (out_slots, M, N // out_slots)
    else:
        o_spec = pl.BlockSpec((tm, tn), lambda i, j, k: (i, j))
        o_shape = (M, N)
    dims = {"nn": NN, "nt": NT, "tn": TN}[mode]
    has_add = add is not None

    def body(*refs):
        a_ref, b_ref = refs[0], refs[1]
        add_ref = refs[2] if has_add else None
        o_ref = refs[3] if has_add else refs[2]
        p = _dot(a_ref[...].astype(BF16), b_ref[...].astype(BF16), dims)

        def finish(val):
            if has_add:
                val = val + add_ref[...]
            o_ref[...] = val.astype(out_dtype)

        if nk == 1:
            finish(p)
        else:
            acc_ref = refs[-1]
            k = pl.program_id(2)

            @pl.when(k == 0)
            def _():
                acc_ref[...] = p

            @pl.when(k > 0)
            def _():
                acc_ref[...] += p

            @pl.when(k == nk - 1)
            def _():
                finish(acc_ref[...])

    assert not (has_add and out_slots)
    in_specs = [a_spec, b_spec] + ([o_spec] if has_add else [])
    args = (a, b) + ((add,) if has_add else ())
    return _carry_call(
        body, grid=(M // tm, N // tn, nk), in_specs=in_specs, out_specs=o_spec,
        out_shape=jax.ShapeDtypeStruct(o_shape, out_dtype),
        scratch_shapes=[pltpu.VMEM((tm, tn), F32)] if nk > 1 else [],
        sem=("parallel", "parallel", "arbitrary"), name=name, args=args, pipe=pipe,
    )


def _rows(body, ins, outs, name, pipe=None):
    T = next(a.shape[0] for a, k in ins if k == "row")
    per_row = sum(a.shape[1] * a.dtype.itemsize for a, k in ins if k == "row")
    per_row += sum(s[1] * jnp.dtype(d).itemsize for s, d, k in outs if k == "row")
    tr = next(t for t in range(512, 0, -8) if T % t == 0 and 2 * t * per_row <= ROW_BUDGET)
    in_specs = [
        pl.BlockSpec((tr, a.shape[1]), lambda i: (i, 0)) if k == "row" else pl.BlockSpec(a.shape, lambda i: (0, 0))
        for a, k in ins
    ]
    out_specs = [
        pl.BlockSpec((tr, s[1]), lambda i: (i, 0)) if k == "row" else pl.BlockSpec(s, lambda i: (0, 0))
        for s, d, k in outs
    ]
    has_acc = any(k == "acc" for _, _, k in outs)
    return _carry_call(
        body, grid=(T // tr,), in_specs=in_specs, out_specs=out_specs,
        out_shape=[jax.ShapeDtypeStruct(s, d) for s, d, k in outs], scratch_shapes=[],
        sem=("arbitrary",) if has_acc else ("parallel",), name=name, args=[a for a, _ in ins], pipe=pipe,
    )


def _rstd(x):
    return lax.rsqrt(jnp.mean(x * x, axis=-1, keepdims=True) + EPS)


def _norm_fwd(x, g, name, pipe=None):
    def body(x_ref, g_ref, o_ref):
        x = x_ref[...]
        o_ref[...] = (x * _rstd(x) * g_ref[...]).astype(BF16)

    return _rows(body, [(x, "row"), (g, "vec")], [(x.shape, BF16, "row")], name, pipe=pipe)[0]


def _postnorm_fwd(x, y, g, weight, name, pipe=None):
    def body(x_ref, y_ref, g_ref, o_ref):
        y = y_ref[...]
        o_ref[...] = x_ref[...] + weight * (y * _rstd(y) * g_ref[...])

    return _rows(body, [(x, "row"), (y, "row"), (g, "vec")], [(x.shape, F32, "row")], name, pipe=pipe)[0]


def _norm_bwd(x, g, dy, weight, out_dtype, name, res=None, pipe=None):
    has_res = res is not None

    def body(*refs):
        x_ref, g_ref, dy_ref = refs[:3]
        res_ref = refs[3] if has_res else None
        dx_ref, dg_ref = refs[-2], refs[-1]

        @pl.when(pl.program_id(0) == 0)
        def _():
            dg_ref[...] = jnp.zeros_like(dg_ref)

        x = x_ref[...]
        dn = dy_ref[...].astype(F32) * weight
        r = _rstd(x)
        xhat = x * r
        dg_ref[...] += jnp.sum(dn * xhat, axis=0, keepdims=True)
        dxh = dn * g_ref[...]
        dx = r * (dxh - xhat * jnp.mean(dxh * xhat, axis=-1, keepdims=True))
        if has_res:
            dx = dx + res_ref[...]
        dx_ref[...] = dx.astype(out_dtype)

    ins = [(x, "row"), (g, "vec"), (dy, "row")] + ([(res, "row")] if has_res else [])
    return _rows(body, ins, [(x.shape, out_dtype, "row"), (g.shape, F32, "acc")], name, pipe=pipe)


def _swiglu_fwd(a, b, name, pipe=None):
    def body(a_ref, b_ref, o_ref):
        a = a_ref[...]
        o_ref[...] = (a * _sigmoid(a) * b_ref[...]).astype(BF16)

    return _rows(body, [(a, "row"), (b, "row")], [(a.shape, BF16, "row")], name, pipe=pipe)[0]


def _swiglu_bwd(a, b, ds, name, pipe=None):
    def body(a_ref, b_ref, ds_ref, da_ref, db_ref):
        a, ds = a_ref[...], ds_ref[...]
        sg = _sigmoid(a)
        da_ref[...] = (ds * b_ref[...] * (sg * (1.0 + a * (1.0 - sg)))).astype(BF16)
        db_ref[...] = (ds * (a * sg)).astype(BF16)

    return _rows(body, [(a, "row"), (b, "row"), (ds, "row")], [(a.shape, BF16, "row"), (a.shape, BF16, "row")], name, pipe=pipe)


def _merge_fwd(ga, gb, ya, yb, name):
    def body(ga_ref, gb_ref, ya_ref, yb_ref, o_ref):
        o_ref[...] = (_sigmoid(ga_ref[...]) * ya_ref[...] + _sigmoid(gb_ref[...]) * yb_ref[...]).astype(BF16)

    return _rows(body, [(ga, "row"), (gb, "row"), (ya, "row"), (yb, "row")], [(ga.shape, BF16, "row")], name)[0]


def _merge_bwd(ga, gb, ya, yb, dy, name):
    def body(ga_ref, gb_ref, ya_ref, yb_ref, dy_ref, dya_ref, dyb_ref, dga_ref, dgb_ref):
        dy = dy_ref[...]
        sa, sb = _sigmoid(ga_ref[...]), _sigmoid(gb_ref[...])
        dya_ref[...] = (dy * sa).astype(BF16)
        dyb_ref[...] = (dy * sb).astype(BF16)
        dga_ref[...] = (dy * ya_ref[...] * (sa * (1.0 - sa))).astype(BF16)
        dgb_ref[...] = (dy * yb_ref[...] * (sb * (1.0 - sb))).astype(BF16)

    ins = [(ga, "row"), (gb, "row"), (ya, "row"), (yb, "row"), (dy, "row")]
    return _rows(body, ins, [(ga.shape, BF16, "row")] * 4, name)


def _loss_head(y, target, name):
    D = y.shape[1]

    def body(y_ref, t_ref, dy_ref, acc_ref):
        @pl.when(pl.program_id(0) == 0)
        def _():
            acc_ref[...] = jnp.zeros_like(acc_ref)

        err = y_ref[...] - t_ref[...]
        dy_ref[...] = err * (1.0 / D)
        acc_ref[...] += jnp.sum(err * err, axis=0, keepdims=True)

    return _rows(body, [(y, "row"), (target, "row")], [(y.shape, F32, "row"), ((1, D), F32, "acc")], name)


def _rot(x):
    lane = lax.broadcasted_iota(jnp.int32, x.shape, 1)
    return jnp.where((lane % ROPE) < ROPE // 2, -pltpu.roll(x, 128 - ROPE // 2, 1), pltpu.roll(x, ROPE // 2, 1))


def _rope_q_fwd(qpe, cos, sin, name):
    T, W = qpe.shape
    tr = min(T, 512)
    blk = pl.BlockSpec((tr, 128), lambda i, j: (i, j))
    tab = pl.BlockSpec((tr, 128), lambda i, j: (i, 0))

    def body(x_ref, c_ref, s_ref, o_ref):
        x = x_ref[...]
        o_ref[...] = (x * c_ref[...] + _rot(x) * s_ref[...]).astype(BF16)

    return pl.pallas_call(
        body, grid=(T // tr, W // 128), in_specs=[blk, tab, tab], out_specs=blk,
        out_shape=jax.ShapeDtypeStruct((T, W), BF16), compiler_params=_params(("parallel", "parallel")), name=name,
    )(qpe, cos, sin)


def _rope_q_bwd(dq_heads, cos, sin, name):
    T, W = dq_heads.shape
    tr = min(T, 512)
    even = pl.BlockSpec((tr, 128), lambda i, j: (i, 2 * j))
    odd = pl.BlockSpec((tr, 128), lambda i, j: (i, 2 * j + 1))
    tab = pl.BlockSpec((tr, 128), lambda i, j: (i, 0))

    def body(a_ref, b_ref, c_ref, s_ref, o_ref):
        d = a_ref[...] + b_ref[...]
        o_ref[...] = (d * c_ref[...] - _rot(d * s_ref[...])).astype(BF16)

    return pl.pallas_call(
        body, grid=(T // tr, W // 256), in_specs=[even, odd, tab, tab],
        out_specs=pl.BlockSpec((tr, 128), lambda i, j: (i, j)),
        out_shape=jax.ShapeDtypeStruct((T, W // 2), BF16), compiler_params=_params(("parallel", "parallel")), name=name,
    )(dq_heads, dq_heads, cos, sin)


def _rope_k_fwd(kpe, cos, sin, name):
    def body(x_ref, c_ref, s_ref, o_ref):
        x = x_ref[...]
        y = x * c_ref[...] + _rot(x) * s_ref[...]
        o_ref[...] = (y + pltpu.roll(y, ROPE, 1)).astype(BF16)

    return _rows(body, [(kpe, "row"), (cos, "row"), (sin, "row")], [(kpe.shape, BF16, "row")], name)[0]


def _rope_k_bwd(dk_heads, cos, sin, name):
    T, W = dk_heads.shape

    def body(d_ref, c_ref, s_ref, o_ref):
        d = d_ref[:, 0:128]
        for h in range(1, W // 128):
            d = d + d_ref[:, h * 128:(h + 1) * 128]
        d = d + pltpu.roll(d, ROPE, 1)
        dx = d * c_ref[...] - _rot(d * s_ref[...])
        lane = lax.broadcasted_iota(jnp.int32, dx.shape, 1)
        o_ref[...] = jnp.where(lane < ROPE, dx, 0.0).astype(BF16)

    return _rows(body, [(dk_heads, "row"), (cos, "row"), (sin, "row")], [((T, 128), BF16, "row")], name)[0]


def _attn_probs(q, k, qpe, kpe, scale, causal, q0):
    s = _dot(q, k, NT)
    if qpe is not None:
        s = s + _dot(qpe, kpe, NT)
    s = s * scale
    if causal:
        row = q0 + lax.broadcasted_iota(jnp.int32, s.shape, 0)
        col = lax.broadcasted_iota(jnp.int32, s.shape, 1)
        s = jnp.where((col // CHUNK) <= (row // CHUNK), s, NEG)
    p = jnp.exp(s - jnp.max(s, axis=-1, keepdims=True))
    return p / jnp.sum(p, axis=-1, keepdims=True)


def _pe_mask(x, h):
    lane = lax.broadcasted_iota(jnp.int32, x.shape, 1)
    return jnp.where((lane // ROPE) == (h % 2), x, jnp.zeros_like(x))


def _attn_fwd(q, k, v, scale, name, qpe=None, kpe=None, causal=False, pipe=None):
    T, W = q.shape
    Tk = k.shape[0]
    H = W // HEAD
    tq = min(T, 256)
    nq = T // tq
    has_pe = qpe is not None
    qs = pl.BlockSpec((tq, HEAD), lambda h, i: (i, h))
    ks = pl.BlockSpec((Tk, HEAD), lambda h, i: (0, h))
    in_specs, args = [qs, ks, ks], [q, k, v]
    if has_pe:
        in_specs += [pl.BlockSpec((tq, HEAD), lambda h, i: (i, h // 2)), pl.BlockSpec((Tk, HEAD), lambda h, i: (0, 0))]
        args += [qpe, kpe]

    def body(*refs):
        q_ref, k_ref, v_ref = refs[:3]
        o_ref = refs[-1]
        h, i = pl.program_id(0), pl.program_id(1)

        def compute(klen):
            qp = _pe_mask(refs[3][...], h) if has_pe else None
            kp = refs[4][0:klen, :] if has_pe else None
            p = _attn_probs(q_ref[...], k_ref[0:klen, :], qp, kp, scale, causal, i * tq)
            o_ref[...] = _dot(p.astype(BF16), v_ref[0:klen, :], NN).astype(BF16)

        if causal:
            for qi in range(nq):
                pl.when(i == qi)(functools.partial(compute, (qi + 1) * tq))
        else:
            compute(Tk)

    return _carry_call(
        body, grid=(H, nq), in_specs=in_specs, out_specs=qs, out_shape=jax.ShapeDtypeStruct((T, W), BF16),
        scratch_shapes=[], sem=("parallel", "parallel"), name=name, args=args, pipe=pipe,
    )


def _attn_bwd(q, k, v, do, scale, name, qpe=None, kpe=None, causal=False, pipe=None):
    T, W = q.shape
    Tk = k.shape[0]
    H = W // HEAD
    tq = min(T, 256)
    nq = T // tq
    has_pe = qpe is not None
    qs = pl.BlockSpec((tq, HEAD), lambda h, i: (i, h))
    ks = pl.BlockSpec((Tk, HEAD), lambda h, i: (0, h))
    in_specs, args = [qs, ks, ks, qs], [q, k, v, do]
    out_specs = [qs, ks, ks]
    out_shape = [jax.ShapeDtypeStruct((T, W), BF16), jax.ShapeDtypeStruct((Tk, W), BF16), jax.ShapeDtypeStruct((Tk, W), BF16)]
    scratch = [pltpu.VMEM((Tk, HEAD), F32), pltpu.VMEM((Tk, HEAD), F32)]
    if has_pe:
        in_specs += [pl.BlockSpec((tq, HEAD), lambda h, i: (i, h // 2)), pl.BlockSpec((Tk, HEAD), lambda h, i: (0, 0))]
        args += [qpe, kpe]
        out_specs += [qs, ks]
        out_shape += [jax.ShapeDtypeStruct((T, W), F32), jax.ShapeDtypeStruct((Tk, W), F32)]
        scratch += [pltpu.VMEM((Tk, HEAD), F32)]
    n_in = len(in_specs)

    def body(*refs):
        q_ref, k_ref, v_ref, do_ref = refs[:4]
        outs = refs[n_in:n_in + len(out_specs)]
        accs = refs[n_in + len(out_specs):]
        dq_ref, dk_ref, dv_ref = outs[:3]
        h, i = pl.program_id(0), pl.program_id(1)

        @pl.when(i == 0)
        def _():
            for acc in accs:
                acc[...] = jnp.zeros_like(acc)

        def compute(klen):
            qp = _pe_mask(refs[4][...], h) if has_pe else None
            kp = refs[5][0:klen, :] if has_pe else None
            qv, kv, vv, dov = q_ref[...], k_ref[0:klen, :], v_ref[0:klen, :], do_ref[...]
            p = _attn_probs(qv, kv, qp, kp, scale, causal, i * tq)
            dp = _dot(dov, vv, NT)
            ds = (p * (dp - jnp.sum(p * dp, axis=-1, keepdims=True)) * scale).astype(BF16)
            dq_ref[...] = _dot(ds, kv, NN).astype(BF16)
            accs[0][0:klen, :] += _dot(ds, qv, TN)
            accs[1][0:klen, :] += _dot(p.astype(BF16), dov, TN)
            if has_pe:
                outs[3][...] = _pe_mask(_dot(ds, kp, NN), h)
                accs[2][0:klen, :] += _dot(ds, qp, TN)

        if causal:
            for qi in range(nq):
                pl.when(i == qi)(functools.partial(compute, (qi + 1) * tq))
        else:
            compute(Tk)

        @pl.when(i == nq - 1)
        def _():
            dk_ref[...] = accs[0][...].astype(BF16)
            dv_ref[...] = accs[1][...].astype(BF16)
            if has_pe:
                outs[4][...] = accs[2][...]

    return _carry_call(
        body, grid=(H, nq), in_specs=in_specs, out_specs=out_specs, out_shape=out_shape, scratch_shapes=scratch,
        sem=("parallel", "arbitrary"), name=name, args=args, pipe=pipe,
    )


def _split3(x):
    hi = x.astype(BF16)
    r1 = x - hi.astype(F32)
    mid = r1.astype(BF16)
    lo = (r1 - mid.astype(F32)).astype(BF16)
    return hi, mid, lo


def _tri_dot(tri, x):
    hi, mid, lo = _split3(x)
    return _dot(tri, hi, NN) + _dot(tri, mid, NN) + _dot(tri, lo, NN)


def _hg_gates(u, lb):
    q, fr, v = u[:, 0:HEAD], u[:, HEAD:2 * HEAD], u[:, 2 * HEAD:3 * HEAD]
    sg = 1.0 / (1.0 + jnp.exp(-fr))
    sgm = 1.0 / (1.0 + jnp.exp(fr))
    f = lb + (1.0 - lb) * sg
    kin = (1.0 - lb) * sgm
    sq = _sigmoid(q)
    return q, v, sg, sgm, f, kin, sq, q * sq


def _hg_block_mats(blk, sub=SUB):
    t = lax.broadcasted_iota(jnp.int32, (blk, blk), 0)
    s = lax.broadcasted_iota(jnp.int32, (blk, blk), 1)
    same = (t // sub) == (s // sub)
    one = lambda m: jnp.where(m, 1.0, 0.0).astype(BF16)
    return one(same & (s <= t)), one(same), one(same & (s >= t))


def _hg_stage(pairs, blk, sub=SUB):
    for sc, val in pairs:
        sc[0:sub, :] = jnp.zeros((sub, HEAD), F32)
        sc[sub:sub + blk, :] = val


def _hg_scan_fwd(u, lb, name, pipe=None):
    T, W = u.shape
    H = W // (4 * HEAD)
    blk = min(T, HG_BLOCK)
    sub = min(SUB_FWD, blk)
    nb, nsb = T // blk, blk // sub

    def body(u_ref, lb_ref, o_ref, st_ref, state, k_sc, b_sc, v_sc):
        @pl.when(pl.program_id(1) == 0)
        def _():
            state[...] = jnp.zeros_like(state)

        q, v, sg, sgm, f, kin, sq, qin = _hg_gates(u_ref[...], lb_ref[...])
        tri, ones, _ = _hg_block_mats(blk, sub)
        logf = jnp.log(f)
        brel = _tri_dot(tri, logf)
        btot = _tri_dot(ones, logf)
        _hg_stage(((k_sc, kin), (b_sc, brel), (v_sc, v)), blk, sub)
        sub_row = lax.broadcasted_iota(jnp.int32, (blk, HEAD), 0) % sub
        o = jnp.zeros((blk, HEAD), F32)
        for d in range(sub):
            win = slice(sub - d, sub - d + blk)
            e = jnp.exp(jnp.where(sub_row >= d, brel - b_sc[win, :], NEG))
            o = o + jnp.sum(qin * e * k_sc[win, :], axis=-1, keepdims=True) * v_sc[win, :]
        ab = (qin * jnp.exp(brel)).astype(BF16)
        kdb = (kin * jnp.exp(btot - brel)).astype(BF16)
        vb = v.astype(BF16)
        ebt = jnp.exp(btot)
        st = state[...]
        st_ref[...] = st
        for i in range(nsb):
            sl = slice(i * sub, (i + 1) * sub)
            o_ref[sl, :] = o[sl] + _dot(ab[sl], st.astype(BF16), NT)
            st = ebt[i * sub:i * sub + 1, :] * st + _dot(vb[sl], kdb[sl], TN)
        state[...] = st

    return _carry_call(
        body, grid=(H, nb),
        in_specs=[pl.BlockSpec((blk, 4 * HEAD), lambda h, c: (c, h)), pl.BlockSpec((1, HEAD), lambda h, c: (0, h))],
        out_specs=[pl.BlockSpec((blk, HEAD), lambda h, c: (c, h)), pl.BlockSpec((None, None, HEAD, HEAD), lambda h, c: (h, c, 0, 0))],
        out_shape=[jax.ShapeDtypeStruct((T, H * HEAD), F32), jax.ShapeDtypeStruct((H, nb, HEAD, HEAD), F32)],
        scratch_shapes=[pltpu.VMEM((HEAD, HEAD), F32)] + [pltpu.VMEM((sub + blk, HEAD), F32)] * 3,
        sem=("parallel", "arbitrary"), name=name, args=(u, lb), pipe=pipe,
    )


def _hg_scan_bwd(u, lb, do, dog, states, name, pipe=None):
    T, W = u.shape
    H = W // (4 * HEAD)
    blk = min(T, HG_BLOCK)
    NC, nsb = T // blk, blk // SUB

    def body(u_ref, lb_ref, do_ref, dog_ref, st_ref, du_ref, dlb_ref, dstate, s_all, k_sc, b_sc, v_sc, dk_sc, dbn_sc,
             dv_sc, da_sc, dkd_sc, dvs_sc, dbt_sc):
        @pl.when(pl.program_id(1) == 0)
        def _():
            dstate[...] = jnp.zeros_like(dstate)
            dlb_ref[...] = jnp.zeros_like(dlb_ref)

        lb = lb_ref[...]
        q, v, sg, sgm, f, kin, sq, qin = _hg_gates(u_ref[...], lb)
        tri, ones, tri_t = _hg_block_mats(blk)
        logf = jnp.log(f)
        brel = _tri_dot(tri, logf)
        btot = _tri_dot(ones, logf)
        eb, ekd, ebt = jnp.exp(brel), jnp.exp(btot - brel), jnp.exp(btot)
        a, kd = qin * eb, kin * ekd
        ab, kdb, vb = a.astype(BF16), kd.astype(BF16), v.astype(BF16)
        do = do_ref[...]
        dob = do.astype(BF16)
        st = st_ref[...]
        for i in range(nsb):
            sl = slice(i * SUB, (i + 1) * SUB)
            s_all[i] = st
            st = ebt[i * SUB:i * SUB + 1, :] * st + _dot(vb[sl], kdb[sl], TN)
        ds = dstate[...]
        for i in reversed(range(nsb)):
            sl = slice(i * SUB, (i + 1) * SUB)
            st_i = s_all[i]
            dsb = ds.astype(BF16)
            e_i = ebt[i * SUB:i * SUB + 1, :]
            da_sc[sl, :] = _dot(dob[sl], st_i.astype(BF16), NN)
            dvs_sc[sl, :] = _dot(kdb[sl], dsb, NT)
            dkd_sc[sl, :] = _dot(vb[sl], dsb, NN)
            dbt_sc[sl, :] = jnp.broadcast_to(jnp.sum(ds * st_i, axis=0, keepdims=True) * e_i, (SUB, HEAD))
            ds = e_i * ds + _dot(dob[sl], ab[sl], TN)
        dstate[...] = ds
        da, dkd = da_sc[...], dkd_sc[...]
        t1 = dkd * kd
        dqin = da * eb
        dbrel = da * a - t1
        dkin = dkd * ekd
        dbtot = dbt_sc[...] + _tri_dot(ones, t1)
        _hg_stage(((k_sc, kin), (b_sc, brel), (v_sc, v)), blk)
        for sc in (dk_sc, dbn_sc, dv_sc):
            sc[...] = jnp.zeros_like(sc)
        sub_row = lax.broadcasted_iota(jnp.int32, (blk, HEAD), 0) % SUB
        for d in range(SUB):
            win = slice(SUB - d, SUB - d + blk)
            ks = k_sc[win, :]
            e = jnp.exp(jnp.where(sub_row >= d, brel - b_sc[win, :], NEG))
            qe = qin * e
            col = jnp.sum(qe * ks, axis=-1, keepdims=True)
            dcol = jnp.sum(do * v_sc[win, :], axis=-1, keepdims=True)
            dqe = dcol * qe
            g = dqe * ks
            dqin = dqin + dcol * (e * ks)
            dbrel = dbrel + g
            dk_sc[win, :] += dqe
            dbn_sc[win, :] += g
            dv_sc[win, :] += col * do
        dkin = dkin + dk_sc[SUB:SUB + blk, :]
        dbrel = dbrel - dbn_sc[SUB:SUB + blk, :]
        dv = dvs_sc[...] + dv_sc[SUB:SUB + blk, :]
        dlogf = _tri_dot(tri_t, dbrel) + dbtot
        diff = dlogf / f - dkin
        dlb_ref[...] += jnp.sum(sgm * diff, axis=0, keepdims=True)
        du_ref[:, 0:HEAD] = (dqin * (sq * (1.0 + q * (1.0 - sq)))).astype(BF16)
        du_ref[:, HEAD:2 * HEAD] = ((1.0 - lb) * sg * sgm * diff).astype(BF16)
        du_ref[:, 2 * HEAD:3 * HEAD] = dv.astype(BF16)
        du_ref[:, 3 * HEAD:4 * HEAD] = dog_ref[...]

    rev = lambda h, c: (NC - 1 - c, h)
    return _carry_call(
        body, grid=(H, NC),
        in_specs=[
            pl.BlockSpec((blk, 4 * HEAD), rev), pl.BlockSpec((1, HEAD), lambda h, c: (0, h)),
            pl.BlockSpec((blk, HEAD), rev), pl.BlockSpec((blk, HEAD), rev),
            pl.BlockSpec((None, None, HEAD, HEAD), lambda h, c: (h, NC - 1 - c, 0, 0)),
        ],
        out_specs=[pl.BlockSpec((blk, 4 * HEAD), rev), pl.BlockSpec((1, HEAD), lambda h, c: (0, h))],
        out_shape=[jax.ShapeDtypeStruct((T, W), BF16), jax.ShapeDtypeStruct((1, H * HEAD), F32)],
        scratch_shapes=[pltpu.VMEM((HEAD, HEAD), F32), pltpu.VMEM((nsb, HEAD, HEAD), F32)]
        + [pltpu.VMEM((SUB + blk, HEAD), F32)] * 6 + [pltpu.VMEM((blk, HEAD), F32)] * 4,
        sem=("parallel", "arbitrary"), name=name, args=(u, lb, do, dog, states), pipe=pipe,
    )


def _hg_tail_fwd(o_raw, u, g, name):
    T, D = o_raw.shape
    H = D // HEAD
    tr = min(T, 512)
    blk = pl.BlockSpec((tr, HEAD), lambda h, i: (i, h))

    def body(o_ref, og_ref, g_ref, out_ref):
        o, og = o_ref[...], og_ref[...]
        out_ref[...] = (o * _rstd(o) * g_ref[...] * (og * _sigmoid(og))).astype(BF16)

    return pl.pallas_call(
        body, grid=(H, T // tr),
        in_specs=[blk, pl.BlockSpec((tr, HEAD), lambda h, i: (i, 4 * h + 3)), pl.BlockSpec((1, HEAD), lambda h, i: (0, h))],
        out_specs=blk, out_shape=jax.ShapeDtypeStruct((T, D), BF16),
        compiler_params=_params(("parallel", "parallel")), name=name,
    )(o_raw, u, g)


def _hg_tail_bwd(o_raw, u, g, doa, name):
    T, D = o_raw.shape
    H = D // HEAD
    tr = min(T, 512)
    blk = pl.BlockSpec((tr, HEAD), lambda h, i: (i, h))
    vec = pl.BlockSpec((1, HEAD), lambda h, i: (0, h))

    def body(o_ref, og_ref, g_ref, doa_ref, do_ref, dog_ref, dg_ref):
        @pl.when(pl.program_id(1) == 0)
        def _():
            dg_ref[...] = jnp.zeros_like(dg_ref)

        o, og, doa, g = o_ref[...], og_ref[...], doa_ref[...], g_ref[...]
        sg = _sigmoid(og)
        r = _rstd(o)
        xhat = o * r
        dog_ref[...] = (doa * (xhat * g) * (sg * (1.0 + og * (1.0 - sg)))).astype(BF16)
        dn = doa * (og * sg)
        dg_ref[...] += jnp.sum(dn * xhat, axis=0, keepdims=True)
        dxh = dn * g
        do_ref[...] = r * (dxh - xhat * jnp.mean(dxh * xhat, axis=-1, keepdims=True))

    return pl.pallas_call(
        body, grid=(H, T // tr),
        in_specs=[blk, pl.BlockSpec((tr, HEAD), lambda h, i: (i, 4 * h + 3)), vec, blk],
        out_specs=[blk, blk, vec],
        out_shape=[jax.ShapeDtypeStruct((T, D), F32), jax.ShapeDtypeStruct((T, D), BF16), jax.ShapeDtypeStruct((1, D), F32)],
        compiler_params=_params(("parallel", "arbitrary")), name=name,
    )(o_raw, u, g, doa)


def _lb_fwd(logits, name):
    def body(l_ref, o_ref):
        l0, l1 = l_ref[0:1, :], l_ref[1:2, :]
        m = jnp.maximum(l0, l1)
        e0, e1 = jnp.exp(l0 - m), jnp.exp(l1 - m)
        o_ref[...] = e0 / (e0 + e1)

    D = logits.shape[1]
    return pl.pallas_call(body, out_shape=jax.ShapeDtypeStruct((1, D), F32), name=name)(logits)


def _lb_bwd(lb, dlb, name):
    def body(lb_ref, d_ref, o_ref):
        lb = lb_ref[...]
        d0 = d_ref[...] * lb * (1.0 - lb)
        o_ref[0:1, :] = d0
        o_ref[1:2, :] = -d0

    D = lb.shape[1]
    return pl.pallas_call(body, out_shape=jax.ShapeDtypeStruct((2, D), F32), name=name)(lb, dlb)


def _slots(w):
    return w.shape[0] if w.ndim == 3 else 0


def _ffn_fwd(x, p, tag, carry):
    mm = lambda a, b, mode, dt, name, **kw: _mm(a, b, mode, dt, name, pipe=carry.get(name), **kw)
    hb = _norm_fwd(x, p["pre_g"], f"{tag}_pre_norm", pipe=carry.get(f"{tag}_pre_norm"))
    a = mm(hb, p["w_gate"], "nn", F32, f"{tag}_gate")
    b = mm(hb, p["w_up"], "nn", F32, f"{tag}_up")
    sb = _swiglu_fwd(a, b, f"{tag}_swiglu", pipe=carry.get(f"{tag}_swiglu"))
    y = mm(sb, p["w_down"], "nn", F32, f"{tag}_down")
    xo = _postnorm_fwd(x, y, p["post_g"], FFN_RESIDUAL_WEIGHT, f"{tag}_post_norm", pipe=carry.get(f"{tag}_post_norm"))
    return xo, (x, hb, a, b, sb, y)


def _ffn_bwd(dxo, p, saved, tag, carry, on_dw):
    mm = lambda a, b, mode, dt, name, **kw: _mm(a, b, mode, dt, name, pipe=carry.get(name), **kw)
    x, hb, a, b, sb, y = saved
    dyb, dpost = _norm_bwd(y, p["post_g"], dxo, FFN_RESIDUAL_WEIGHT, BF16, f"{tag}_post_norm_bwd",
                           pipe=carry.get(f"{tag}_post_norm_bwd"))
    dw_down = mm(sb, dyb, "tn", BF16, f"{tag}_down_dw")
    on_dw("down", {"w_down": dw_down})
    ds = mm(dyb, p["w_down"], "nt", F32, f"{tag}_down_dx")
    dab, dbb = _swiglu_bwd(a, b, ds, f"{tag}_swiglu_bwd", pipe=carry.get(f"{tag}_swiglu_bwd"))
    dw_gate = mm(hb, dab, "tn", BF16, f"{tag}_gate_dw", out_slots=_slots(p["w_gate"]))
    dw_up = mm(hb, dbb, "tn", BF16, f"{tag}_up_dw", out_slots=_slots(p["w_up"]))
    on_dw("gate_up", {"w_gate": dw_gate, "w_up": dw_up})
    dh = mm(dab, p["w_gate"], "nt", F32, f"{tag}_gate_dx")
    dh = mm(dbb, p["w_up"], "nt", F32, f"{tag}_up_dx", add=dh)
    dx, dpre = _norm_bwd(x, p["pre_g"], dh, 1.0, F32, f"{tag}_pre_norm_bwd", res=dxo, pipe=carry.get(f"{tag}_pre_norm_bwd"))
    return dx, {"pre_g": dpre, "w_gate": dw_gate, "w_up": dw_up, "w_down": dw_down, "post_g": dpost}


def _mixer_fwd(x, cos, sin, p, carry, rest_of):
    mm = lambda a, b, mode, dt, name, **kw: _mm(a, b, mode, dt, name, pipe=carry.get(name), **kw)
    scale = (HEAD + ROPE) ** -0.5
    hb = _norm_fwd(x, p["pre_g"], "mix_pre_norm")
    u = mm(hb, p["w_hg"], "nn", F32, "mix_in_hg")
    cq = mm(hb, p["w_cq"], "nn", F32, "mix_in_cq")
    ckv = mm(hb, p["w_ckv"], "nn", F32, "mix_in_ckv")
    kpe = mm(hb, p["w_kpe"], "nn", F32, "mix_in_kpe")
    ga = mm(hb, p["w_ga"], "nn", F32, "mix_in_ga")
    gb = mm(hb, p["w_gb"], "nn", F32, "mix_in_gb")
    lb = _lb_fwd(p["lb_logits"], "hg_lb")
    o_raw, states = _hg_scan_fwd(u, lb, "hg_scan", pipe=carry.get("hg_scan"))
    oa = _hg_tail_fwd(o_raw, u, p["hg_norm_g"], "hg_tail")
    p = {**p, **rest_of()}
    ya = mm(oa, p["w_branch_a"], "nn", F32, "mix_branch_a")
    cqn = _norm_fwd(cq, p["q_norm_g"], "mla_q_norm")
    qn = mm(cqn, p["w_qn"], "nn", BF16, "mla_q_up_nope")
    qpe = _rope_q_fwd(mm(cqn, p["w_qpe"], "nn", F32, "mla_q_up_pe"), cos, sin, "mla_rope_q")
    ckvn = _norm_fwd(ckv, p["kv_norm_g"], "mla_kv_norm")
    kn = mm(ckvn, p["w_kn"], "nn", BF16, "mla_k_up")
    vv = mm(ckvn, p["w_vv"], "nn", BF16, "mla_v_up")
    kpe2 = _rope_k_fwd(kpe, cos, sin, "mla_rope_k")
    ob = _attn_fwd(qn, kn, vv, scale, "mla_attn", qpe=qpe, kpe=kpe2, causal=True, pipe=carry.get("mla_attn"))
    yb = mm(ob, p["w_branch_b"], "nn", F32, "mix_branch_b")
    ym = _merge_fwd(ga, gb, ya, yb, "mix_merge")
    z = mm(ym, p["w_out"], "nn", F32, "mix_out")
    xo = _postnorm_fwd(x, z, p["post_g"], 1.0, "mix_post_norm")
    saved = (x, hb, u, cq, ckv, ga, gb, lb, o_raw, states, oa, ya, cqn, qn, qpe, ckvn, kn, vv, kpe2, ob, yb, ym, z)
    return xo, saved, p


def _mixer_bwd(dxo, cos, sin, p, saved, carry, on_early, on_late):
    x, hb, u, cq, ckv, ga, gb, lb, o_raw, states, oa, ya, cqn, qn, qpe, ckvn, kn, vv, kpe2, ob, yb, ym, z = saved
    scale = (HEAD + ROPE) ** -0.5
    g = {}
    dzb, g["post_g"] = _norm_bwd(z, p["post_g"], dxo, 1.0, BF16, "mix_post_norm_bwd")
    g["w_out"] = _mm(ym, dzb, "tn", BF16, "mix_out_dw")
    dym = _mm(dzb, p["w_out"], "nt", F32, "mix_out_dx")
    dya, dyb, dga, dgb = _merge_bwd(ga, gb, ya, yb, dym, "mix_merge_bwd")
    g["w_branch_a"] = _mm(oa, dya, "tn", BF16, "mix_branch_a_dw")
    doa = _mm(dya, p["w_branch_a"], "nt", F32, "mix_branch_a_dx")
    do_raw, dog, g["hg_norm_g"] = _hg_tail_bwd(o_raw, u, p["hg_norm_g"], doa, "hg_tail_bwd")
    du, dlb = _hg_scan_bwd(u, lb, do_raw, dog, states, "hg_scan_bwd", pipe=carry.get("hg_scan_bwd"))
    g["lb_logits"] = _lb_bwd(lb, dlb, "hg_lb_bwd")
    g["w_branch_b"] = _mm(ob, dyb, "tn", BF16, "mix_branch_b_dw")
    on_early({k: g[k] for k in ("w_out", "w_branch_a", "w_branch_b")})
    dob = _mm(dyb, p["w_branch_b"], "nt", BF16, "mix_branch_b_dx")
    dqn, dkn, dvv, dqpe_h, dkpe_h = _attn_bwd(qn, kn, vv, dob, scale, "mla_attn_bwd", qpe=qpe, kpe=kpe2, causal=True,
                                              pipe=carry.get("mla_attn_bwd"))
    dqpe = _rope_q_bwd(dqpe_h, cos, sin, "mla_rope_q_bwd")
    dkpe = _rope_k_bwd(dkpe_h, cos, sin, "mla_rope_k_bwd")
    g["w_qn"] = _mm(cqn, dqn, "tn", BF16, "mla_q_up_nope_dw")
    g["w_qpe"] = _mm(cqn, dqpe, "tn", BF16, "mla_q_up_pe_dw")
    dcqn = _mm(dqn, p["w_qn"], "nt", F32, "mla_q_up_nope_dx")
    dcqn = _mm(dqpe, p["w_qpe"], "nt", F32, "mla_q_up_pe_dx", add=dcqn)
    dcq, g["q_norm_g"] = _norm_bwd(cq, p["q_norm_g"], dcqn, 1.0, BF16, "mla_q_norm_bwd")
    g["w_kn"] = _mm(ckvn, dkn, "tn", BF16, "mla_k_up_dw")
    g["w_vv"] = _mm(ckvn, dvv, "tn", BF16, "mla_v_up_dw")
    dckvn = _mm(dkn, p["w_kn"], "nt", F32, "mla_k_up_dx")
    dckvn = _mm(dvv, p["w_vv"], "nt", F32, "mla_v_up_dx", add=dckvn)
    dckv, g["kv_norm_g"] = _norm_bwd(ckv, p["kv_norm_g"], dckvn, 1.0, BF16, "mla_kv_norm_bwd")
    parts = (("w_hg", du), ("w_cq", dcq), ("w_ckv", dckv), ("w_kpe", dkpe), ("w_ga", dga), ("w_gb", dgb))
    for key, d in parts:
        g[key] = _mm(hb, d, "tn", BF16, f"mix_in_{key}_dw")
    on_late(g)
    dh = None
    for key, d in parts:
        dh = _mm(d, p[key], "nt", F32, f"mix_in_{key}_dx", add=dh, pipe=carry.get(f"mix_in_{key}_dx"))
    dx, g["pre_g"] = _norm_bwd(x, p["pre_g"], dh, 1.0, F32, "mix_pre_norm_bwd", res=dxo, pipe=carry.get("mix_pre_norm_bwd"))
    return dx, g


def _xa_fwd(x, mem, p, carry):
    mm = lambda a, b, mode, dt, name, **kw: _mm(a, b, mode, dt, name, pipe=carry.get(name), **kw)
    scale = HEAD ** -0.5
    hb = _norm_fwd(x, p["pre_g"], "xa_pre_norm")
    mb = _norm_fwd(mem, p["mem_g"], "xa_mem_norm")
    q = mm(hb, p["w_q"], "nn", BF16, "xa_q")
    k = mm(mb, p["w_k"], "nn", BF16, "xa_k")
    v = mm(mb, p["w_v"], "nn", BF16, "xa_v")
    o = _attn_fwd(q, k, v, scale, "xa_attn", pipe=carry.get("xa_attn"))
    z = mm(o, p["w_o"], "nn", F32, "xa_o")
    xo = _postnorm_fwd(x, z, p["post_g"], 1.0, "xa_post_norm")
    return xo, (x, mem, hb, mb, q, k, v, o, z)


def _xa_bwd(dxo, p, saved):
    x, mem, hb, mb, q, k, v, o, z = saved
    scale = HEAD ** -0.5
    g = {}
    dzb, g["post_g"] = _norm_bwd(z, p["post_g"], dxo, 1.0, BF16, "xa_post_norm_bwd")
    g["w_o"] = _mm(o, dzb, "tn", BF16, "xa_o_dw", out_slots=_slots(p["w_o"]))
    do = _mm(dzb, p["w_o"], "nt", BF16, "xa_o_dx")
    dq, dk, dv = _attn_bwd(q, k, v, do, scale, "xa_attn_bwd")
    g["w_q"] = _mm(hb, dq, "tn", BF16, "xa_q_dw")
    g["w_k"] = _mm(mb, dk, "tn", BF16, "xa_k_dw")
    g["w_v"] = _mm(mb, dv, "tn", BF16, "xa_v_dw")
    dh = _mm(dq, p["w_q"], "nt", F32, "xa_q_dx")
    dm = _mm(dk, p["w_k"], "nt", F32, "xa_k_dx")
    dm = _mm(dv, p["w_v"], "nt", F32, "xa_v_dx", add=dm)
    _, g["mem_g"] = _norm_bwd(mem, p["mem_g"], dm, 1.0, BF16, "xa_mem_norm_bwd")
    dx, g["pre_g"] = _norm_bwd(x, p["pre_g"], dh, 1.0, F32, "xa_pre_norm_bwd", res=dxo)
    return dx, g


def _local_step(x, mem, cos, sin, target, params_of, carry, on_grads):
    p1 = params_of("ffn1")
    x1, s1 = _ffn_fwd(x, p1, "ffn1", carry)
    x2, s2, p2 = _mixer_fwd(x1, cos, sin, params_of("mix"), carry, lambda: params_of("mix_rest"))
    p3 = params_of("xa")
    x3, s3 = _xa_fwd(x2, mem, p3, carry)
    p4 = params_of("ffn2")
    x4, s4 = _ffn_fwd(x3, p4, "ffn2", carry)
    dy, sq_err = _loss_head(x4, target, "loss_head")
    loss = 0.5 / x.shape[1] * jnp.sum(sq_err)
    dx, g4 = _ffn_bwd(dy, p4, s4, "ffn2", carry, lambda stage, g: on_grads(f"ffn2_dw_{stage}", g))
    on_grads("ffn2", g4)
    dx, g3 = _xa_bwd(dx, p3, s3)
    on_grads("xa", g3)
    dx, g2 = _mixer_bwd(dx, cos, sin, p2, s2, carry, lambda g: on_grads("mix_early", g), lambda g: on_grads("mix_late", g))
    on_grads("mix", g2)
    dx, g1 = _ffn_bwd(dx, p1, s1, "ffn1", carry, lambda stage, g: on_grads(f"ffn1_dw_{stage}", g))
    on_grads("ffn1", g1)
    return loss, dx


def _split_w_in(w_in):
    D = w_in.shape[0]
    H = D // HEAD
    lora = (w_in.shape[1] - 6 * D - ROPE) // 2
    o = 4 * D
    w_hg = w_in[:, :o].reshape(D, 4, H, HEAD).transpose(0, 2, 1, 3).reshape(D, 4 * D)
    w_cq, w_ckv = w_in[:, o:o + lora], w_in[:, o + lora:o + 2 * lora]
    o += 2 * lora
    w_kpe = jnp.pad(w_in[:, o:o + ROPE], ((0, 0), (0, HEAD - ROPE)))
    o += ROPE
    return {"w_hg": w_hg, "w_cq": w_cq, "w_ckv": w_ckv, "w_kpe": w_kpe, "w_ga": w_in[:, o:o + D], "w_gb": w_in[:, o + D:o + 2 * D]}


def _merge_w_in(g):
    D = g["w_ga"].shape[0]
    H = D // HEAD
    hg = g["w_hg"].reshape(D, H, 4, HEAD).transpose(0, 2, 1, 3).reshape(D, 4 * D)
    return jnp.concatenate([hg, g["w_cq"], g["w_ckv"], g["w_kpe"][:, :ROPE], g["w_ga"], g["w_gb"]], axis=1)


def _split_heads(w, rest):
    K, N = w.shape
    w3 = w.reshape(K, N // (HEAD + rest), HEAD + rest)
    return w3[:, :, :HEAD].reshape(K, -1), w3[:, :, HEAD:].reshape(K, -1)


def _merge_heads(a, b, rest):
    K = a.shape[0]
    H = a.shape[1] // HEAD
    return jnp.concatenate([a.reshape(K, H, HEAD), b.reshape(K, H, rest)], axis=2).reshape(K, H * (HEAD + rest))


BLOCK_WEIGHTS = {
    "ffn1": ("ffn1_w_gate", "ffn1_w_up", "ffn1_w_down"),
    "mix": ("w_in",),
    "mix_rest": ("mla_w_q_up", "mla_w_kv_up", "w_branch_a", "w_branch_b", "w_out"),
    "xa": ("xa_w_q", "xa_w_k", "xa_w_v", "xa_w_o"),
    "ffn2": ("ffn2_w_gate", "ffn2_w_up", "ffn2_w_down"),
}


def _block_params(block, w, small):
    if block in ("ffn1", "ffn2"):
        return {"pre_g": small[f"{block}_pre_g"], "w_gate": w[f"{block}_w_gate"], "w_up": w[f"{block}_w_up"],
                "w_down": w[f"{block}_w_down"], "post_g": small[f"{block}_post_g"]}
    if block == "xa":
        return {"pre_g": small["xa_pre_g"], "mem_g": small["xa_mem_g"], "post_g": small["xa_post_g"],
                "w_q": w["xa_w_q"], "w_k": w["xa_w_k"], "w_v": w["xa_w_v"], "w_o": w["xa_w_o"]}
    if block == "mix_rest":
        (w_qn, w_qpe), (w_kn, w_vv) = _split_heads(w["mla_w_q_up"], ROPE), _split_heads(w["mla_w_kv_up"], HEAD)
        return dict(w_qn=w_qn, w_qpe=w_qpe, w_kn=w_kn, w_vv=w_vv, w_branch_a=w["w_branch_a"],
                    w_branch_b=w["w_branch_b"], w_out=w["w_out"])
    mix = _split_w_in(w["w_in"])
    mix.update(pre_g=small["mix_pre_g"], post_g=small["mix_post_g"], hg_norm_g=small["hg_norm_g"],
               q_norm_g=small["mla_q_norm_g"], kv_norm_g=small["mla_kv_norm_g"], lb_logits=small["hgrn_lb_logits"])
    return mix


def _block_grads(block, g):
    if block in ("ffn1", "ffn2"):
        return {f"{block}_{k}": g[k] for k in ("pre_g", "w_gate", "w_up", "w_down", "post_g")}
    if block == "xa":
        return {f"xa_{k}": g[k] for k in ("pre_g", "mem_g", "post_g", "w_q", "w_k", "w_v", "w_o")}
    if block[4:8] == "_dw_":
        return {f"{block[:4]}_{k}": v for k, v in g.items()}
    if block == "mix_early":
        return dict(g)
    if block == "mix_late":
        return dict(w_in=_merge_w_in(g), mla_w_q_up=_merge_heads(g["w_qn"], g["w_qpe"], ROPE),
                    mla_w_kv_up=_merge_heads(g["w_kn"], g["w_vv"], HEAD))
    return dict(mix_pre_g=g["pre_g"], mix_post_g=g["post_g"],
                hg_norm_g=g["hg_norm_g"], mla_q_norm_g=g["q_norm_g"], mla_kv_norm_g=g["kv_norm_g"],
                hgrn_lb_logits=g["lb_logits"])


def _rope_tables(positions):
    inv_freq = 1.0 / (ROPE_THETA ** (jnp.arange(0, ROPE, 2, dtype=F32) / ROPE))
    ang = positions.astype(F32)[:, None] * inv_freq
    return jnp.tile(jnp.cos(ang), (1, 4)), jnp.tile(jnp.sin(ang), (1, 4))


def _adamw(w, g, m, v, name):
    bc1 = 1.0 - ADAM_B1 ** ADAM_STEP
    bc2 = 1.0 - ADAM_B2 ** ADAM_STEP

    def body(w_ref, g_ref, m_ref, v_ref, go_ref, d_ref, mo_ref, vo_ref):
        g = g_ref[...]
        m = ADAM_B1 * m_ref[...] + (1.0 - ADAM_B1) * g
        v = ADAM_B2 * v_ref[...] + (1.0 - ADAM_B2) * (g * g)
        go_ref[...] = g
        mo_ref[...] = m
        vo_ref[...] = v
        d_ref[...] = -ADAM_LR * ((m / bc1) / (jnp.sqrt(v / bc2) + ADAM_EPS) + ADAM_WD * w_ref[...])

    return _rows(body, [(w, "row"), (g, "row"), (m, "row"), (v, "row")], [(w.shape, F32, "row")] * 4, name)


ANY = pl.BlockSpec(memory_space=pl.ANY)
COMM_AXES = ("x", "y", "c")


def _place():
    x, y, c = (lax.axis_index(n) for n in COMM_AXES)
    chips = [(1 - x, y), (x, 1 - y), (1 - x, 1 - y)]
    return x, y, c, 2 * x + y, (x, y, 1 - c), chips


def _remote(src, dst, send_sems, recv_sems, j, to):
    return pltpu.make_async_remote_copy(src_ref=src, dst_ref=dst, send_sem=send_sems.at[j], recv_sem=recv_sems.at[j],
                                        device_id=to, device_id_type=MESH)


def _dma_sems(n):
    return [pltpu.SemaphoreType.DMA((n,)), pltpu.SemaphoreType.DMA((n,))]


def _all_gather(shards, whole, name):
    n = len(shards)

    def body(*refs):
        srcs, outs, send_sems, recv_sems = refs[:n], refs[n:2 * n], refs[2 * n], refs[2 * n + 1]
        x, y, c, me, sibling, chips = _place()
        sent = []

        def start(cp):
            cp.start()
            sent.append(cp)

        def rows(w, h):
            hr = srcs[w].shape[0] // 2
            return pl.ds(h * hr, hr)

        gathered = [w for w in range(n) if whole[w]]
        for w in gathered:
            for j, (cx, cy) in enumerate(chips):
                start(_remote(srcs[w].at[rows(w, c)], outs[w].at[me, rows(w, c)], send_sems, recv_sems, 7 * w + j, (cx, cy, c)))
        for w in range(n):
            start(_remote(srcs[w], outs[w].at[me], send_sems, recv_sems, 7 * w + 6, sibling))
        for w in gathered:
            for j, (cx, cy) in enumerate(chips):
                blk = outs[w].at[2 * cx + cy, rows(w, c)]
                _remote(srcs[w].at[rows(w, c)], blk, send_sems, recv_sems, 7 * w + j, (cx, cy, c)).wait_recv()
                start(_remote(blk, blk, send_sems, recv_sems, 7 * w + 3 + j, sibling))
        for w in gathered:
            for j, (cx, cy) in enumerate(chips):
                blk = outs[w].at[2 * cx + cy, rows(w, 1 - c)]
                _remote(blk, blk, send_sems, recv_sems, 7 * w + 3 + j, sibling).wait_recv()
        for w in range(n):
            _remote(srcs[w], outs[w].at[me], send_sems, recv_sems, 7 * w + 6, sibling).wait_recv()
        for cp in sent:
            cp.wait_send()

    return pl.pallas_call(
        body, in_specs=[ANY] * n, out_specs=[ANY] * n,
        out_shape=[jax.ShapeDtypeStruct((N_CHIP,) + s.shape, s.dtype) for s in shards],
        scratch_shapes=_dma_sems(7 * n), name=name,
    )(*shards)


def _rs_swap(grads, name):
    n = len(grads)

    def body(*refs):
        gs, outs, send_sems, recv_sems = refs[:n], refs[n:2 * n], refs[2 * n], refs[2 * n + 1]
        x, y, c, me, sibling, chips = _place()
        cps = []
        for w in range(n):
            hr = gs[w].shape[1] // 2
            cps.append(_remote(gs[w].at[:, pl.ds((1 - c) * hr, hr)], outs[w], send_sems, recv_sems, w, sibling))
            cps[-1].start()
        for cp in cps:
            cp.wait()

    return pl.pallas_call(
        body, in_specs=[ANY] * n, out_specs=[ANY] * n,
        out_shape=[jax.ShapeDtypeStruct((g.shape[0], g.shape[1] // 2, g.shape[2]), g.dtype) for g in grads],
        scratch_shapes=_dma_sems(n), name=name,
    )(*grads)


def _sum_rows(hr, row_bytes):
    return _tile(hr, max(16, ROW_BUDGET // (2 * row_bytes) // 16 * 16), 16)


def _rs_pair_sum(g, got, c, name):
    S, r, cw = g.shape
    hr = r // 2
    tr = _sum_rows(hr, 3 * cw * 2)
    nrb = hr // tr

    def body(c_ref, a_ref, b_ref, o_ref):
        o_ref[...] = (a_ref[...].astype(F32) + b_ref[...].astype(F32)).astype(BF16)

    return pl.pallas_call(
        body,
        grid_spec=pltpu.PrefetchScalarGridSpec(
            num_scalar_prefetch=1, grid=(S, nrb),
            in_specs=[pl.BlockSpec((None, tr, cw), lambda k, i, c_ref: (k, c_ref[0] * nrb + i, 0)),
                      pl.BlockSpec((None, tr, cw), lambda k, i, c_ref: (k, i, 0))],
            out_specs=pl.BlockSpec((None, tr, cw), lambda k, i, c_ref: (k, i, 0)),
        ),
        out_shape=jax.ShapeDtypeStruct((S, hr, cw), BF16),
        compiler_params=_params(("parallel", "parallel")), name=name,
    )(c, g, got)


def _rs_chip_sum(pair, got, place, name):
    S, hr, cw = pair.shape
    tr = _sum_rows(hr, cw * (4 * 2 + 4))

    def body(p_ref, a_ref, z_ref, o_ref):
        o_ref[...] = a_ref[...].astype(F32) + z_ref[0].astype(F32) + z_ref[1].astype(F32) + z_ref[2].astype(F32)

    return pl.pallas_call(
        body,
        grid_spec=pltpu.PrefetchScalarGridSpec(
            num_scalar_prefetch=1, grid=(hr // tr,),
            in_specs=[pl.BlockSpec((None, tr, cw), lambda i, p_ref: (p_ref[0], i, 0)),
                      pl.BlockSpec((3, tr, cw), lambda i, p_ref: (0, i, 0))],
            out_specs=pl.BlockSpec((None, tr, cw), lambda i, p_ref: (p_ref[1], i, 0)),
        ),
        out_shape=jax.ShapeDtypeStruct((2, hr, cw), F32),
        compiler_params=_params(("parallel",)), name=name,
    )(place, pair, got)


def _rs_share(halves, name):
    n = len(halves)

    def body(*refs):
        outs, send_sems, recv_sems = refs[n:2 * n], refs[2 * n], refs[2 * n + 1]
        x, y, c, me, sibling, chips = _place()
        cps = []
        for w in range(n):
            cps.append(_remote(outs[w].at[c], outs[w].at[c], send_sems, recv_sems, w, sibling))
            cps[-1].start()
        for w in range(n):
            _remote(outs[w].at[1 - c], outs[w].at[1 - c], send_sems, recv_sems, w, sibling).wait_recv()
        for cp in cps:
            cp.wait_send()

    return pl.pallas_call(
        body, in_specs=[ANY] * n, out_specs=[ANY] * n,
        out_shape=[jax.ShapeDtypeStruct(h.shape, h.dtype) for h in halves],
        input_output_aliases={i: i for i in range(n)},
        scratch_shapes=_dma_sems(n), name=name,
    )(*halves)


def _rs_begin(names, grads, core, fractions, tag):
    slotted = [_to_slots(n, grads[n]) for n in names]
    got = _rs_swap(slotted, f"grads_sibling_swap_{tag}")
    pairs = [_rs_pair_sum(s, t, core.reshape(1), f"grads_pair_sum_{n}") for n, s, t in zip(names, slotted, got)]
    lands = [lax.empty((N_CHIP - 1,) + p.shape[1:], p.dtype) for p in pairs]
    return pairs, _Pipe("reduce", pairs, lands, fractions)


def _rs_end(names, pairs, pipe, place, shapes, tag):
    _pipe_flush(pipe, f"grads_chip_exchange_rest_{tag}")
    halves = [_rs_chip_sum(p, o, place, f"grads_chip_sum_{n}") for n, p, o in zip(names, pairs, pipe.lands)]
    both = _rs_share(halves, f"grads_sibling_share_{tag}")
    return {n: b.reshape(s) for n, b, s in zip(names, both, shapes)}


def _all_reduce_small(s, name):
    flips = [(dx, dy, dc) for dx in (0, 1) for dy in (0, 1) for dc in (0, 1) if (dx, dy, dc) != (0, 0, 0)]

    def body(s_ref, o_ref, buf, send_sems, recv_sems):
        x, y, c = (lax.axis_index(n) for n in COMM_AXES)
        me = 4 * x + 2 * y + c
        buf[me] = s_ref[...]
        peers = [((1 - x) if dx else x, (1 - y) if dy else y, (1 - c) if dc else c) for dx, dy, dc in flips]
        sent = [_remote(s_ref, buf.at[me], send_sems, recv_sems, j, p) for j, p in enumerate(peers)]
        for cp in sent:
            cp.start()
        for j, (px, py, pc) in enumerate(peers):
            _remote(s_ref, buf.at[4 * px + 2 * py + pc], send_sems, recv_sems, j, (px, py, pc)).wait_recv()
        for cp in sent:
            cp.wait_send()
        acc = buf[0]
        for d in range(1, N_DEV):
            acc = acc + buf[d]
        o_ref[...] = acc

    vmem = pl.BlockSpec(memory_space=pltpu.VMEM)
    return pl.pallas_call(
        body, in_specs=[vmem], out_specs=vmem, out_shape=jax.ShapeDtypeStruct(s.shape, F32),
        scratch_shapes=[pltpu.VMEM((N_DEV,) + s.shape, F32), pltpu.SemaphoreType.DMA((7,)), pltpu.SemaphoreType.DMA((7,))],
        name=name,
    )(s)


BIG = {
    "ffn1_w_gate": 1, "ffn1_w_up": 1, "ffn1_w_down": 0, "w_in": 1, "mla_w_q_up": 1, "mla_w_kv_up": 1,
    "w_branch_a": 0, "w_branch_b": 0, "w_out": 0, "xa_w_q": 0, "xa_w_k": 0, "xa_w_v": 0, "xa_w_o": 1,
    "ffn2_w_gate": 1, "ffn2_w_up": 1, "ffn2_w_down": 0,
}
WEIGHTS = [
    "hgrn_lb_logits", "ffn1_pre_g", "ffn1_w_gate", "ffn1_w_up", "ffn1_w_down", "ffn1_post_g", "mix_pre_g", "w_in",
    "hg_norm_g", "mla_q_norm_g", "mla_w_q_up", "mla_kv_norm_g", "mla_w_kv_up", "w_branch_a", "w_branch_b", "w_out",
    "mix_post_g", "xa_pre_g", "xa_mem_g", "xa_w_q", "xa_w_k", "xa_w_v", "xa_w_o", "xa_post_g", "ffn2_pre_g",
    "ffn2_w_gate", "ffn2_w_up", "ffn2_w_down", "ffn2_post_g",
]
SMALL = [n for n in WEIGHTS if n not in BIG]
SLOTTED = ("ffn1_w_gate", "ffn1_w_up", "ffn2_w_gate", "ffn2_w_up", "xa_w_o")
MIX_EARLY = ("w_out", "w_branch_a", "w_branch_b")


def _from_slots(name, g):
    S, r, cw = g.shape
    if BIG[name] == 0:
        return g.reshape(S * r, cw)
    return g if name in SLOTTED else g.transpose(1, 0, 2).reshape(r, S * cw)


def _to_slots(name, g):
    if g.ndim == 3:
        return g
    if BIG[name] == 0:
        return g.reshape(N_CHIP, g.shape[0] // N_CHIP, g.shape[1])
    return g.reshape(g.shape[0], N_CHIP, g.shape[1] // N_CHIP).transpose(1, 0, 2)


def _pack_small(vals, width):
    rows = [jnp.pad(v, ((0, 0), (0, width - v.shape[1]))) for v in vals]
    s = jnp.concatenate(rows, axis=0)
    return jnp.pad(s, ((0, -s.shape[0] % 8), (0, 0)))


def _unpack_small(s, shapes):
    out, o = [], 0
    for r, w in shapes:
        out.append(s[o:o + r, :w])
        o += r
    return out


def kernel(x, mem, positions, hgrn_lb_logits, ffn1_pre_g, ffn1_w_gate, ffn1_w_up, ffn1_w_down, ffn1_post_g, mix_pre_g, w_in, hg_norm_g, mla_q_norm_g, mla_w_q_up, mla_kv_norm_g, mla_w_kv_up, w_branch_a, w_branch_b, w_out, mix_post_g, xa_pre_g, xa_mem_g, xa_w_q, xa_w_k, xa_w_v, xa_w_o, xa_post_g, ffn2_pre_g, ffn2_w_gate, ffn2_w_up, ffn2_w_down, ffn2_post_g, loss_target, m_hgrn_lb_logits, m_ffn1_pre_g, m_ffn1_w_gate, m_ffn1_w_up, m_ffn1_w_down, m_ffn1_post_g, m_mix_pre_g, m_w_in, m_hg_norm_g, m_mla_q_norm_g, m_mla_w_q_up, m_mla_kv_norm_g, m_mla_w_kv_up, m_w_branch_a, m_w_branch_b, m_w_out, m_mix_post_g, m_xa_pre_g, m_xa_mem_g, m_xa_w_q, m_xa_w_k, m_xa_w_v, m_xa_w_o, m_xa_post_g, m_ffn2_pre_g, m_ffn2_w_gate, m_ffn2_w_up, m_ffn2_w_down, m_ffn2_post_g, v_hgrn_lb_logits, v_ffn1_pre_g, v_ffn1_w_gate, v_ffn1_w_up, v_ffn1_w_down, v_ffn1_post_g, v_mix_pre_g, v_w_in, v_hg_norm_g, v_mla_q_norm_g, v_mla_w_q_up, v_mla_kv_norm_g, v_mla_w_kv_up, v_w_branch_a, v_w_branch_b, v_w_out, v_mix_post_g, v_xa_pre_g, v_xa_mem_g, v_xa_w_q, v_xa_w_k, v_xa_w_v, v_xa_w_o, v_xa_post_g, v_ffn2_pre_g, v_ffn2_w_gate, v_ffn2_w_up, v_ffn2_w_down, v_ffn2_post_g):
    a = dict(locals())
    big = list(BIG)
    small = {n: a[n] for n in SMALL}
    core = lax.axis_index("c").astype(jnp.int32)
    place = jnp.stack([(2 * lax.axis_index("x") + lax.axis_index("y")).astype(jnp.int32), core])

    shards = {n: a[n][0].astype(BF16) for n in big}
    whole = [n in BLOCK_WEIGHTS["ffn1"] for n in big]
    lands = dict(zip(big, _all_gather([shards[n] for n in big], whole, "weights_all_gather")))
    plan = {
        "mix": ((2, 10, 10, 6, 10, 3), [f"ffn1_{k}" for k in ("pre_norm", "gate", "up", "swiglu", "down", "post_norm")]),
        "mix_rest": ((8, 1, 1, 3, 3), ["mix_in_hg", "mix_in_cq", "mix_in_ckv", "mix_in_ga", "mix_in_gb"]),
        "ffn2": ((4, 3), ["hg_scan", "mla_attn"]),
        "xa": ((1, 1), ["mix_branch_b", "mix_out"]),
    }
    carry, pipes = {}, {}
    for block, (fractions, carriers) in plan.items():
        names = BLOCK_WEIGHTS[block]
        pipes[block] = _Pipe("gather", [shards[n] for n in names], [lands[n] for n in names], fractions)
        carry.update({c: pipes[block] for c in carriers})

    def params_of(block):
        if block in pipes:
            _pipe_flush(pipes[block], f"weights_gather_rest_{block}")
            lands.update(zip(BLOCK_WEIGHTS[block], pipes[block].lands))
        return _block_params(block, {n: _from_slots(n, lands[n]) for n in BLOCK_WEIGHTS[block]}, small)

    g_small, g_big, open_groups, held = {}, {}, {}, {}

    def begin(tag, names, fractions, carriers):
        pairs, pipe = _rs_begin(names, held, core, fractions, tag)
        open_groups[tag] = (names, pairs, pipe)
        carry.update({c: pipe for c in carriers})

    def end(tag):
        names, pairs, pipe = open_groups.pop(tag)
        g_big.update(_rs_end(names, pairs, pipe, place, [a[n].shape[1:] for n in names], tag))

    def on_grads(block, g):
        for n, v in _block_grads(block, g).items():
            (held if n in BIG else g_small)[n] = v
        if block == "ffn2":
            begin("ffn2", BLOCK_WEIGHTS["ffn2"], (1,), ("hg_scan_bwd",))
        elif block == "mix_early":
            begin("early", BLOCK_WEIGHTS["xa"] + MIX_EARLY, (1,), ("mla_attn_bwd",))
        elif block == "mix_late":
            end("ffn2")
            end("early")
            begin("mid", ["w_in", "mla_w_q_up", "mla_w_kv_up"], (10, 3, 3, 3, 2, 6, 6),
                  ("mix_in_w_hg_dx", "mix_in_w_ga_dx", "mix_in_w_gb_dx", "mix_pre_norm_bwd",
                   "ffn1_post_norm_bwd", "ffn1_down_dw", "ffn1_down_dx"))
        elif block == "ffn1_dw_down":
            begin("ffn1_down", ["ffn1_w_down"], (1, 1, 1), [f"ffn1_{k}" for k in ("swiglu_bwd", "gate_dw", "up_dw")])
        elif block == "ffn1_dw_gate_up":
            end("mid")
            begin("ffn1_gate_up", ["ffn1_w_gate", "ffn1_w_up"], (8, 9, 3), [f"ffn1_{k}" for k in ("gate_dx", "up_dx", "pre_norm_bwd")])
        elif block == "ffn1":
            end("ffn1_down")
            end("ffn1_gate_up")

    cos, sin = _rope_tables(positions[0])
    loss_part, grad_x = _local_step(x[0], mem[0], cos, sin, loss_target[0], params_of, carry, on_grads)

    small_shapes = [a[n].shape for n in SMALL]
    width = max(s[1] for s in small_shapes)
    gs = _all_reduce_small(_pack_small([g_small[n] for n in SMALL], width), "small_grads_all_reduce")

    out_g, out_d, out_m, out_v = {}, {}, {}, {}
    for n in big:
        flip = (lambda t: jnp.swapaxes(t, 0, 1)) if a[n].shape[2] % 128 else (lambda t: t)
        res = _adamw(*(flip(t) for t in (a[n][0], g_big[n], a["m_" + n][0], a["v_" + n][0])), f"adamw_{n}")
        out_g[n], out_d[n], out_m[n], out_v[n] = (flip(t).reshape(a[n].shape) for t in res)
    sw, sm, sv = (_pack_small([a[p + n] for n in SMALL], width) for p in ("", "m_", "v_"))
    for t, dst in zip(_adamw(sw, gs, sm, sv, "adamw_small"), (out_g, out_d, out_m, out_v)):
        dst.update(zip(SMALL, _unpack_small(t, small_shapes)))

    loss = lax.psum(loss_part, COMM_AXES)
    return (loss, grad_x[None], *[out_g[n] for n in WEIGHTS], *[out_d[n] for n in WEIGHTS],
            *[out_m[n] for n in WEIGHTS], *[out_v[n] for n in WEIGHTS])
```

```python
import functools

import jax
import jax.numpy as jnp
from jax import lax
from jax.experimental import pallas as pl
from jax.experimental.pallas import tpu as pltpu

F32 = jnp.float32
BF16 = jnp.bfloat16
EPS = 1e-6
HEAD = 128
ROPE = 64
CHUNK = 64
SUB = 16
SUB_FWD = 32
HG_BLOCK = 256
ROPE_THETA = 10000.0
FFN_RESIDUAL_WEIGHT = 0.5
ADAM_LR, ADAM_B1, ADAM_B2, ADAM_EPS, ADAM_WD, ADAM_STEP = 0.001, 0.9, 0.999, 1e-08, 0.01, 10
VMEM_LIMIT = 56 * 2**20
ROW_BUDGET = 20 * 2**20
NEG = -1e30
MESH = pl.DeviceIdType.MESH
N_CHIP = 4
N_DEV = 8


def _params(sem):
    return pltpu.CompilerParams(dimension_semantics=sem, vmem_limit_bytes=VMEM_LIMIT)


def _tile(n, cap, mult):
    if n <= cap:
        return n
    t = (cap // mult) * mult
    while t >= mult:
        if n % t == 0:
            return t
        t -= mult
    raise ValueError(f"no tile for {n} under {cap}")


def _split_rows(n, fractions):
    if n % 16:
        return [(0, n)] + [(n, 0)] * (len(fractions) - 1)
    units, total, acc, cuts = n // 16, sum(fractions), 0, [0]
    for f in fractions[:-1]:
        acc += f
        cuts.append(round(units * acc / total))
    cuts.append(units)
    return [(16 * lo, 16 * (hi - lo)) for lo, hi in zip(cuts, cuts[1:])]


class _Pipe:
    def __init__(self, kind, srcs, lands, fractions):
        self.kind, self.srcs, self.lands = kind, list(srcs), list(lands)
        per_w = [_split_rows(s.shape[0] // 2 if kind == "gather" else s.shape[1], fractions) for s in srcs]
        self.parts = [[pw[i] for pw in per_w] for i in range(len(fractions))]
        self.taken = 0
        self.pending = None
        self.sems = (9 if kind == "gather" else 3) * len(srcs)

    def take(self):
        self.taken += 1
        rows, prev = self.parts[self.taken - 1], self.pending
        self.pending = rows if self.kind == "gather" else None
        return rows, prev

    def rest(self):
        left, prev = self.parts[self.taken:], self.pending
        self.taken, self.pending = len(self.parts), None
        n = len(self.srcs)
        rows = [(left[0][w][0], sum(p[w][1] for p in left)) for w in range(n)] if left else [(0, 0)] * n
        return (rows, prev) if (left or prev) else None


def _pipe_copies(kind, rows, lands, srcs, send_sems, recv_sems):
    x, y, c, me, sibling, chips = _place()
    out = []
    per = 9 if kind == "gather" else 3
    for w, (r0, nr) in enumerate(rows):
        for j, (cx, cy) in enumerate(chips if nr else []):
            k, to = 2 * cx + cy, (cx, cy, c)
            if kind == "gather":
                rs = pl.ds(c * (srcs[w].shape[0] // 2) + r0, nr)
                src, dst, got = srcs[w].at[rs], lands[w].at[me, rs], lands[w].at[k, rs]
            else:
                rs = pl.ds(r0, nr)
                src, dst, got = srcs[w].at[k, rs], lands[w].at[j, rs], lands[w].at[j, rs]
            out.append((_remote(src, dst, send_sems, recv_sems, per * w + j, to), _remote(src, got, send_sems, recv_sems, per * w + j, to)))
    return out


def _pipe_handover(rows, bank, lands, srcs, send_sems, recv_sems):
    x, y, c, me, sibling, chips = _place()
    out = []
    for w, (r0, nr) in enumerate(rows or []):
        hr = srcs[w].shape[0] // 2
        for j, (cx, cy) in enumerate(chips if nr else []):
            mine = lands[w].at[2 * cx + cy, pl.ds(c * hr + r0, nr)]
            theirs = lands[w].at[2 * cx + cy, pl.ds((1 - c) * hr + r0, nr)]
            sem = 9 * w + 3 + 3 * bank + j
            out.append((_remote(mine, mine, send_sems, recv_sems, sem, sibling), _remote(theirs, theirs, send_sems, recv_sems, sem, sibling)))
    return out


def _pipe_start(kind, rows, prev, lands, srcs, send_sems, recv_sems):
    for send, _ in _pipe_copies(kind, rows, lands, srcs, send_sems, recv_sems):
        send.start()
    for send, _ in _pipe_handover(prev, 0, lands, srcs, send_sems, recv_sems):
        send.start()


def _pipe_finish(kind, rows, prev, last, lands, srcs, send_sems, recv_sems):
    copies = _pipe_copies(kind, rows, lands, srcs, send_sems, recv_sems)
    over = _pipe_handover(prev, 0, lands, srcs, send_sems, recv_sems)
    for _, arrival in copies + over:
        arrival.wait_recv()
    if kind == "gather" and last:
        final = _pipe_handover(rows, 1, lands, srcs, send_sems, recv_sems)
        for send, _ in final:
            send.start()
        for _, arrival in final:
            arrival.wait_recv()
        over = over + final
    for send, _ in copies + over:
        send.wait_send()


def _carry_call(body, *, grid, in_specs, out_specs, out_shape, scratch_shapes, sem, name, args, pipe=None):
    single = not isinstance(out_shape, (list, tuple))
    if single:
        out_specs, out_shape = [out_specs], [out_shape]
    if pipe is None or pipe.taken >= len(pipe.parts):
        res = pl.pallas_call(body, grid=grid, in_specs=in_specs, out_specs=out_specs, out_shape=out_shape,
                             scratch_shapes=scratch_shapes, compiler_params=_params(sem), name=name)(*args)
        return res[0] if single else res
    (rows, prev), kind = pipe.take(), pipe.kind
    n_in, n_out, n_l, n_s, n_scr = len(args), len(out_shape), len(pipe.lands), len(pipe.srcs), len(scratch_shapes)

    def wrapped(*refs):
        ins, srcs = refs[:n_in], refs[n_in + n_l:n_in + n_l + n_s]
        o0 = n_in + n_l + n_s
        outs, lands = refs[o0:o0 + n_out], refs[o0 + n_out:o0 + n_out + n_l]
        scr = refs[o0 + n_out + n_l:o0 + n_out + n_l + n_scr]
        send_sems, recv_sems = refs[-2], refs[-1]
        ids = [pl.program_id(ax) for ax in range(len(grid))]
        first = functools.reduce(jnp.logical_and, [i == 0 for i in ids])
        last = functools.reduce(jnp.logical_and, [i == g - 1 for i, g in zip(ids, grid)])

        @pl.when(first)
        def _():
            _pipe_start(kind, rows, prev, lands, srcs, send_sems, recv_sems)

        body(*ins, *outs, *scr)

        @pl.when(last)
        def _():
            _pipe_finish(kind, rows, prev, False, lands, srcs, send_sems, recv_sems)

    res = pl.pallas_call(
        wrapped, grid=grid, in_specs=list(in_specs) + [ANY] * (n_l + n_s), out_specs=list(out_specs) + [ANY] * n_l,
        out_shape=list(out_shape) + [jax.ShapeDtypeStruct(l.shape, l.dtype) for l in pipe.lands],
        input_output_aliases={n_in + i: n_out + i for i in range(n_l)},
        scratch_shapes=list(scratch_shapes) + _dma_sems(pipe.sems),
        compiler_params=_params(("arbitrary",) * len(grid)), name=name,
    )(*args, *pipe.lands, *pipe.srcs)
    pipe.lands = list(res[n_out:])
    return res[0] if single else list(res[:n_out])


def _pipe_flush(pipe, name):
    todo = pipe.rest()
    if todo is None:
        return
    rows, prev = todo
    n_l, n_s, kind = len(pipe.lands), len(pipe.srcs), pipe.kind

    def body(*refs):
        srcs, lands = refs[n_l:n_l + n_s], refs[n_l + n_s:2 * n_l + n_s]
        _pipe_start(kind, rows, prev, lands, srcs, refs[-2], refs[-1])
        _pipe_finish(kind, rows, prev, True, lands, srcs, refs[-2], refs[-1])

    pipe.lands = list(pl.pallas_call(
        body, in_specs=[ANY] * (n_l + n_s), out_specs=[ANY] * n_l,
        out_shape=[jax.ShapeDtypeStruct(l.shape, l.dtype) for l in pipe.lands],
        input_output_aliases={i: i for i in range(n_l)}, scratch_shapes=_dma_sems(pipe.sems), name=name,
    )(*pipe.lands, *pipe.srcs))


def _sigmoid(x):
    return 1.0 / (1.0 + jnp.exp(-x))


def _dot(a, b, dims):
    return lax.dot_general(a, b, (dims, ((), ())), preferred_element_type=F32)


NN = ((1,), (0,))
NT = ((1,), (1,))
TN = ((0,), (0,))


def _mm(a, b, mode, out_dtype, name, add=None, out_slots=0, pipe=None, tm_cap=1024, tn_cap=512, tk_cap=2816):
    slot_cap = 1408
    b_slots = b.shape[0] if b.ndim == 3 else 0
    bs = (b.shape[1], b_slots * b.shape[2]) if b_slots else b.shape
    if mode == "nn":
        (M, K), (K2, N) = a.shape, bs
    elif mode == "nt":
        (M, K), (N, K2) = a.shape, bs
    else:
        (K, M), (K2, N) = a.shape, bs
    assert K == K2, (a.shape, b.shape, mode)
    tm = _tile(M, tm_cap, 128)
    tn = _tile(N // (b_slots or out_slots), slot_cap, 128) if (out_slots or (b_slots and mode == "nn")) else _tile(N, tn_cap, 128)
    tk = _tile(K // b_slots, slot_cap, 128) if (b_slots and mode == "nt") else _tile(K, tk_cap, 128)
    nk = K // tk
    a_spec = pl.BlockSpec((tk, tm), lambda i, j, k: (k, i)) if mode == "tn" else pl.BlockSpec((tm, tk), lambda i, j, k: (i, k))
    if b_slots and mode == "nn":
        per = b.shape[2] // tn
        b_spec = pl.BlockSpec((None, tk, tn), lambda i, j, k: (j // per, k, j % per))
    elif b_slots:
        per = b.shape[2] // tk
        b_spec = pl.BlockSpec((None, tn, tk), lambda i, j, k: (k // per, j, k % per))
    else:
        b_spec = pl.BlockSpec((tn, tk), lambda i, j, k: (j, k)) if mode == "nt" else pl.BlockSpec((tk, tn), lambda i, j, k: (k, j))
    if out_slots:
        per_o = N // out_slots // tn
        o_spec = pl.BlockSpec((None, tm, tn), lambda i, j, k: (j // per_o, i, j % per_o))
        o_shape = (out_slots, M, N // out_slots)
    else:
        o_spec = pl.BlockSpec((tm, tn), lambda i, j, k: (i, j))
        o_shape = (M, N)
    dims = {"nn": NN, "nt": NT, "tn": TN}[mode]
    has_add = add is not None

    def body(*refs):
        a_ref, b_ref = refs[0], refs[1]
        add_ref = refs[2] if has_add else None
        o_ref = refs[3] if has_add else refs[2]
        p = _dot(a_ref[...].astype(BF16), b_ref[...].astype(BF16), dims)

        def finish(val):
            if has_add:
                val = val + add_ref[...]
            o_ref[...] = val.astype(out_dtype)

        if nk == 1:
            finish(p)
        else:
            acc_ref = refs[-1]
            k = pl.program_id(2)

            @pl.when(k == 0)
            def _():
                acc_ref[...] = p

            @pl.when(k > 0)
            def _():
                acc_ref[...] += p

            @pl.when(k == nk - 1)
            def _():
                finish(acc_ref[...])

    assert not (has_add and out_slots)
    in_specs = [a_spec, b_spec] + ([o_spec] if has_add else [])
    args = (a, b) + ((add,) if has_add else ())
    return _carry_call(
        body, grid=(M // tm, N // tn, nk), in_specs=in_specs, out_specs=o_spec,
        out_shape=jax.ShapeDtypeStruct(o_shape, out_dtype),
        scratch_shapes=[pltpu.VMEM((tm, tn), F32)] if nk > 1 else [],
        sem=("parallel", "parallel", "arbitrary"), name=name, args=args, pipe=pipe,
    )


def _rows(body, ins, outs, name, pipe=None):
    T = next(a.shape[0] for a, k in ins if k == "row")
    per_row = sum(a.shape[1] * a.dtype.itemsize for a, k in ins if k == "row")
    per_row += sum(s[1] * jnp.dtype(d).itemsize for s, d, k in outs if k == "row")
    tr = next(t for t in range(512, 0, -8) if T % t == 0 and 2 * t * per_row <= ROW_BUDGET)
    in_specs = [
        pl.BlockSpec((tr, a.shape[1]), lambda i: (i, 0)) if k == "row" else pl.BlockSpec(a.shape, lambda i: (0, 0))
        for a, k in ins
    ]
    out_specs = [
        pl.BlockSpec((tr, s[1]), lambda i: (i, 0)) if k == "row" else pl.BlockSpec(s, lambda i: (0, 0))
        for s, d, k in outs
    ]
    has_acc = any(k == "acc" for _, _, k in outs)
    return _carry_call(
        body, grid=(T // tr,), in_specs=in_specs, out_specs=out_specs,
        out_shape=[jax.ShapeDtypeStruct(s, d) for s, d, k in outs], scratch_shapes=[],
        sem=("arbitrary",) if has_acc else ("parallel",), name=name, args=[a for a, _ in ins], pipe=pipe,
    )


def _rstd(x):
    return lax.rsqrt(jnp.mean(x * x, axis=-1, keepdims=True) + EPS)


def _norm_fwd(x, g, name, pipe=None):
    def body(x_ref, g_ref, o_ref):
        x = x_ref[...]
        o_ref[...] = (x * _rstd(x) * g_ref[...]).astype(BF16)

    return _rows(body, [(x, "row"), (g, "vec")], [(x.shape, BF16, "row")], name, pipe=pipe)[0]


def _postnorm_fwd(x, y, g, weight, name, pipe=None):
    def body(x_ref, y_ref, g_ref, o_ref):
        y = y_ref[...]
        o_ref[...] = x_ref[...] + weight * (y * _rstd(y) * g_ref[...])

    return _rows(body, [(x, "row"), (y, "row"), (g, "vec")], [(x.shape, F32, "row")], name, pipe=pipe)[0]


def _norm_bwd(x, g, dy, weight, out_dtype, name, res=None, pipe=None):
    has_res = res is not None

    def body(*refs):
        x_ref, g_ref, dy_ref = refs[:3]
        res_ref = refs[3] if has_res else None
        dx_ref, dg_ref = refs[-2], refs[-1]

        @pl.when(pl.program_id(0) == 0)
        def _():
            dg_ref[...] = jnp.zeros_like(dg_ref)

        x = x_ref[...]
        dn = dy_ref[...].astype(F32) * weight
        r = _rstd(x)
        xhat = x * r
        dg_ref[...] += jnp.sum(dn * xhat, axis=0, keepdims=True)
        dxh = dn * g_ref[...]
        dx = r * (dxh - xhat * jnp.mean(dxh * xhat, axis=-1, keepdims=True))
        if has_res:
            dx = dx + res_ref[...]
        dx_ref[...] = dx.astype(out_dtype)

    ins = [(x, "row"), (g, "vec"), (dy, "row")] + ([(res, "row")] if has_res else [])
    return _rows(body, ins, [(x.shape, out_dtype, "row"), (g.shape, F32, "acc")], name, pipe=pipe)


def _swiglu_fwd(a, b, name, pipe=None):
    def body(a_ref, b_ref, o_ref):
        a = a_ref[...]
        o_ref[...] = (a * _sigmoid(a) * b_ref[...]).astype(BF16)

    return _rows(body, [(a, "row"), (b, "row")], [(a.shape, BF16, "row")], name, pipe=pipe)[0]


def _swiglu_bwd(a, b, ds, name, pipe=None):
    def body(a_ref, b_ref, ds_ref, da_ref, db_ref):
        a, ds = a_ref[...], ds_ref[...]
        sg = _sigmoid(a)
        da_ref[...] = (ds * b_ref[...] * (sg * (1.0 + a * (1.0 - sg)))).astype(BF16)
        db_ref[...] = (ds * (a * sg)).astype(BF16)

    return _rows(body, [(a, "row"), (b, "row"), (ds, "row")], [(a.shape, BF16, "row"), (a.shape, BF16, "row")], name, pipe=pipe)


def _merge_fwd(ga, gb, ya, yb, name):
    def body(ga_ref, gb_ref, ya_ref, yb_ref, o_ref):
        o_ref[...] = (_sigmoid(ga_ref[...]) * ya_ref[...] + _sigmoid(gb_ref[...]) * yb_ref[...]).astype(BF16)

    return _rows(body, [(ga, "row"), (gb, "row"), (ya, "row"), (yb, "row")], [(ga.shape, BF16, "row")], name)[0]


def _merge_bwd(ga, gb, ya, yb, dy, name):
    def body(ga_ref, gb_ref, ya_ref, yb_ref, dy_ref, dya_ref, dyb_ref, dga_ref, dgb_ref):
        dy = dy_ref[...]
        sa, sb = _sigmoid(ga_ref[...]), _sigmoid(gb_ref[...])
        dya_ref[...] = (dy * sa).astype(BF16)
        dyb_ref[...] = (dy * sb).astype(BF16)
        dga_ref[...] = (dy * ya_ref[...] * (sa * (1.0 - sa))).astype(BF16)
        dgb_ref[...] = (dy * yb_ref[...] * (sb * (1.0 - sb))).astype(BF16)

    ins = [(ga, "row"), (gb, "row"), (ya, "row"), (yb, "row"), (dy, "row")]
    return _rows(body, ins, [(ga.shape, BF16, "row")] * 4, name)


def _loss_head(y, target, name):
    D = y.shape[1]

    def body(y_ref, t_ref, dy_ref, acc_ref):
        @pl.when(pl.program_id(0) == 0)
        def _():
            acc_ref[...] = jnp.zeros_like(acc_ref)

        err = y_ref[...] - t_ref[...]
        dy_ref[...] = err * (1.0 / D)
        acc_ref[...] += jnp.sum(err * err, axis=0, keepdims=True)

    return _rows(body, [(y, "row"), (target, "row")], [(y.shape, F32, "row"), ((1, D), F32, "acc")], name)


def _rot(x):
    lane = lax.broadcasted_iota(jnp.int32, x.shape, 1)
    return jnp.where((lane % ROPE) < ROPE // 2, -pltpu.roll(x, 128 - ROPE // 2, 1), pltpu.roll(x, ROPE // 2, 1))


def _rope_q_fwd(qpe, cos, sin, name):
    T, W = qpe.shape
    tr = min(T, 512)
    blk = pl.BlockSpec((tr, 128), lambda i, j: (i, j))
    tab = pl.BlockSpec((tr, 128), lambda i, j: (i, 0))

    def body(x_ref, c_ref, s_ref, o_ref):
        x = x_ref[...]
        o_ref[...] = (x * c_ref[...] + _rot(x) * s_ref[...]).astype(BF16)

    return pl.pallas_call(
        body, grid=(T // tr, W // 128), in_specs=[blk, tab, tab], out_specs=blk,
        out_shape=jax.ShapeDtypeStruct((T, W), BF16), compiler_params=_params(("parallel", "parallel")), name=name,
    )(qpe, cos, sin)


def _rope_q_bwd(dq_heads, cos, sin, name):
    T, W = dq_heads.shape
    tr = min(T, 512)
    even = pl.BlockSpec((tr, 128), lambda i, j: (i, 2 * j))
    odd = pl.BlockSpec((tr, 128), lambda i, j: (i, 2 * j + 1))
    tab = pl.BlockSpec((tr, 128), lambda i, j: (i, 0))

    def body(a_ref, b_ref, c_ref, s_ref, o_ref):
        d = a_ref[...] + b_ref[...]
        o_ref[...] = (d * c_ref[...] - _rot(d * s_ref[...])).astype(BF16)

    return pl.pallas_call(
        body, grid=(T // tr, W // 256), in_specs=[even, odd, tab, tab],
        out_specs=pl.BlockSpec((tr, 128), lambda i, j: (i, j)),
        out_shape=jax.ShapeDtypeStruct((T, W // 2), BF16), compiler_params=_params(("parallel", "parallel")), name=name,
    )(dq_heads, dq_heads, cos, sin)


def _rope_k_fwd(kpe, cos, sin, name):
    def body(x_ref, c_ref, s_ref, o_ref):
        x = x_ref[...]
        y = x * c_ref[...] + _rot(x) * s_ref[...]
        o_ref[...] = (y + pltpu.roll(y, ROPE, 1)).astype(BF16)

    return _rows(body, [(kpe, "row"), (cos, "row"), (sin, "row")], [(kpe.shape, BF16, "row")], name)[0]


def _rope_k_bwd(dk_heads, cos, sin, name):
    T, W = dk_heads.shape

    def body(d_ref, c_ref, s_ref, o_ref):
        d = d_ref[:, 0:128]
        for h in range(1, W // 128):
            d = d + d_ref[:, h * 128:(h + 1) * 128]
        d = d + pltpu.roll(d, ROPE, 1)
        dx = d * c_ref[...] - _rot(d * s_ref[...])
        lane = lax.broadcasted_iota(jnp.int32, dx.shape, 1)
        o_ref[...] = jnp.where(lane < ROPE, dx, 0.0).astype(BF16)

    return _rows(body, [(dk_heads, "row"), (cos, "row"), (sin, "row")], [((T, 128), BF16, "row")], name)[0]


def _attn_probs(q, k, qpe, kpe, scale, causal, q0):
    s = _dot(q, k, NT)
    if qpe is not None:
        s = s + _dot(qpe, kpe, NT)
    s = s * scale
    if causal:
        row = q0 + lax.broadcasted_iota(jnp.int32, s.shape, 0)
        col = lax.broadcasted_iota(jnp.int32, s.shape, 1)
        s = jnp.where((col // CHUNK) <= (row // CHUNK), s, NEG)
    p = jnp.exp(s - jnp.max(s, axis=-1, keepdims=True))
    return p / jnp.sum(p, axis=-1, keepdims=True)


def _pe_mask(x, h):
    lane = lax.broadcasted_iota(jnp.int32, x.shape, 1)
    return jnp.where((lane // ROPE) == (h % 2), x, jnp.zeros_like(x))


def _attn_fwd(q, k, v, scale, name, qpe=None, kpe=None, causal=False, pipe=None):
    T, W = q.shape
    Tk = k.shape[0]
    H = W // HEAD
    tq = min(T, 256)
    nq = T // tq
    has_pe = qpe is not None
    qs = pl.BlockSpec((tq, HEAD), lambda h, i: (i, h))
    ks = pl.BlockSpec((Tk, HEAD), lambda h, i: (0, h))
    in_specs, args = [qs, ks, ks], [q, k, v]
    if has_pe:
        in_specs += [pl.BlockSpec((tq, HEAD), lambda h, i: (i, h // 2)), pl.BlockSpec((Tk, HEAD), lambda h, i: (0, 0))]
        args += [qpe, kpe]

    def body(*refs):
        q_ref, k_ref, v_ref = refs[:3]
        o_ref = refs[-1]
        h, i = pl.program_id(0), pl.program_id(1)

        def compute(klen):
            qp = _pe_mask(refs[3][...], h) if has_pe else None
            kp = refs[4][0:klen, :] if has_pe else None
            p = _attn_probs(q_ref[...], k_ref[0:klen, :], qp, kp, scale, causal, i * tq)
            o_ref[...] = _dot(p.astype(BF16), v_ref[0:klen, :], NN).astype(BF16)

        if causal:
            for qi in range(nq):
                pl.when(i == qi)(functools.partial(compute, (qi + 1) * tq))
        else:
            compute(Tk)

    return _carry_call(
        body, grid=(H, nq), in_specs=in_specs, out_specs=qs, out_shape=jax.ShapeDtypeStruct((T, W), BF16),
        scratch_shapes=[], sem=("parallel", "parallel"), name=name, args=args, pipe=pipe,
    )


def _attn_bwd(q, k, v, do, scale, name, qpe=None, kpe=None, causal=False, pipe=None):
    T, W = q.shape
    Tk = k.shape[0]
    H = W // HEAD
    tq = min(T, 256)
    nq = T // tq
    has_pe = qpe is not None
    qs = pl.BlockSpec((tq, HEAD), lambda h, i: (i, h))
    ks = pl.BlockSpec((Tk, HEAD), lambda h, i: (0, h))
    in_specs, args = [qs, ks, ks, qs], [q, k, v, do]
    out_specs = [qs, ks, ks]
    out_shape = [jax.ShapeDtypeStruct((T, W), BF16), jax.ShapeDtypeStruct((Tk, W), BF16), jax.ShapeDtypeStruct((Tk, W), BF16)]
    scratch = [pltpu.VMEM((Tk, HEAD), F32), pltpu.VMEM((Tk, HEAD), F32)]
    if has_pe:
        in_specs += [pl.BlockSpec((tq, HEAD), lambda h, i: (i, h // 2)), pl.BlockSpec((Tk, HEAD), lambda h, i: (0, 0))]
        args += [qpe, kpe]
        out_specs += [qs, ks]
        out_shape += [jax.ShapeDtypeStruct((T, W), F32), jax.ShapeDtypeStruct((Tk, W), F32)]
        scratch += [pltpu.VMEM((Tk, HEAD), F32)]
    n_in = len(in_specs)

    def body(*refs):
        q_ref, k_ref, v_ref, do_ref = refs[:4]
        outs = refs[n_in:n_in + len(out_specs)]
        accs = refs[n_in + len(out_specs):]
        dq_ref, dk_ref, dv_ref = outs[:3]
        h, i = pl.program_id(0), pl.program_id(1)

        @pl.when(i == 0)
        def _():
            for acc in accs:
                acc[...] = jnp.zeros_like(acc)

        def compute(klen):
            qp = _pe_mask(refs[4][...], h) if has_pe else None
            kp = refs[5][0:klen, :] if has_pe else None
            qv, kv, vv, dov = q_ref[...], k_ref[0:klen, :], v_ref[0:klen, :], do_ref[...]
            p = _attn_probs(qv, kv, qp, kp, scale, causal, i * tq)
            dp = _dot(dov, vv, NT)
            ds = (p * (dp - jnp.sum(p * dp, axis=-1, keepdims=True)) * scale).astype(BF16)
            dq_ref[...] = _dot(ds, kv, NN).astype(BF16)
            accs[0][0:klen, :] += _dot(ds, qv, TN)
            accs[1][0:klen, :] += _dot(p.astype(BF16), dov, TN)
            if has_pe:
                outs[3][...] = _pe_mask(_dot(ds, kp, NN), h)
                accs[2][0:klen, :] += _dot(ds, qp, TN)

        if causal:
            for qi in range(nq):
                pl.when(i == qi)(functools.partial(compute, (qi + 1) * tq))
        else:
            compute(Tk)

        @pl.when(i == nq - 1)
        def _():
            dk_ref[...] = accs[0][...].astype(BF16)
            dv_ref[...] = accs[1][...].astype(BF16)
            if has_pe:
                outs[4][...] = accs[2][...]

    return _carry_call(
        body, grid=(H, nq), in_specs=in_specs, out_specs=out_specs, out_shape=out_shape, scratch_shapes=scratch,
        sem=("parallel", "arbitrary"), name=name, args=args, pipe=pipe,
    )


def _split3(x):
    hi = x.astype(BF16)
    r1 = x - hi.astype(F32)
    mid = r1.astype(BF16)
    lo = (r1 - mid.astype(F32)).astype(BF16)
    return hi, mid, lo


def _tri_dot(tri, x):
    hi, mid, lo = _split3(x)
    return _dot(tri, hi, NN) + _dot(tri, mid, NN) + _dot(tri, lo, NN)


def _hg_gates(u, lb):
    q, fr, v = u[:, 0:HEAD], u[:, HEAD:2 * HEAD], u[:, 2 * HEAD:3 * HEAD]
    sg = 1.0 / (1.0 + jnp.exp(-fr))
    sgm = 1.0 / (1.0 + jnp.exp(fr))
    f = lb + (1.0 - lb) * sg
    kin = (1.0 - lb) * sgm
    sq = _sigmoid(q)
    return q, v, sg, sgm, f, kin, sq, q * sq


def _hg_block_mats(blk, sub=SUB):
    t = lax.broadcasted_iota(jnp.int32, (blk, blk), 0)
    s = lax.broadcasted_iota(jnp.int32, (blk, blk), 1)
    same = (t // sub) == (s // sub)
    one = lambda m: jnp.where(m, 1.0, 0.0).astype(BF16)
    return one(same & (s <= t)), one(same), one(same & (s >= t))


def _hg_stage(pairs, blk, sub=SUB):
    for sc, val in pairs:
        sc[0:sub, :] = jnp.zeros((sub, HEAD), F32)
        sc[sub:sub + blk, :] = val


def _hg_scan_fwd(u, lb, name, pipe=None):
    T, W = u.shape
    H = W // (4 * HEAD)
    blk = min(T, HG_BLOCK)
    sub = min(SUB_FWD, blk)
    nb, nsb = T // blk, blk // sub

    def body(u_ref, lb_ref, o_ref, st_ref, state, k_sc, b_sc, v_sc):
        @pl.when(pl.program_id(1) == 0)
        def _():
            state[...] = jnp.zeros_like(state)

        q, v, sg, sgm, f, kin, sq, qin = _hg_gates(u_ref[...], lb_ref[...])
        tri, ones, _ = _hg_block_mats(blk, sub)
        logf = jnp.log(f)
        brel = _tri_dot(tri, logf)
        btot = _tri_dot(ones, logf)
        _hg_stage(((k_sc, kin), (b_sc, brel), (v_sc, v)), blk, sub)
        sub_row = lax.broadcasted_iota(jnp.int32, (blk, HEAD), 0) % sub
        o = jnp.zeros((blk, HEAD), F32)
        for d in range(sub):
            win = slice(sub - d, sub - d + blk)
            e = jnp.exp(jnp.where(sub_row >= d, brel - b_sc[win, :], NEG))
            o = o + jnp.sum(qin * e * k_sc[win, :], axis=-1, keepdims=True) * v_sc[win, :]
        ab = (qin * jnp.exp(brel)).astype(BF16)
        kdb = (kin * jnp.exp(btot - brel)).astype(BF16)
        vb = v.astype(BF16)
        ebt = jnp.exp(btot)
        st = state[...]
        st_ref[...] = st
        for i in range(nsb):
            sl = slice(i * sub, (i + 1) * sub)
            o_ref[sl, :] = o[sl] + _dot(ab[sl], st.astype(BF16), NT)
            st = ebt[i * sub:i * sub + 1, :] * st + _dot(vb[sl], kdb[sl], TN)
        state[...] = st

    return _carry_call(
        body, grid=(H, nb),
        in_specs=[pl.BlockSpec((blk, 4 * HEAD), lambda h, c: (c, h)), pl.BlockSpec((1, HEAD), lambda h, c: (0, h))],
        out_specs=[pl.BlockSpec((blk, HEAD), lambda h, c: (c, h)), pl.BlockSpec((None, None, HEAD, HEAD), lambda h, c: (h, c, 0, 0))],
        out_shape=[jax.ShapeDtypeStruct((T, H * HEAD), F32), jax.ShapeDtypeStruct((H, nb, HEAD, HEAD), F32)],
        scratch_shapes=[pltpu.VMEM((HEAD, HEAD), F32)] + [pltpu.VMEM((sub + blk, HEAD), F32)] * 3,
        sem=("parallel", "arbitrary"), name=name, args=(u, lb), pipe=pipe,
    )


def _hg_scan_bwd(u, lb, do, dog, states, name, pipe=None):
    T, W = u.shape
    H = W // (4 * HEAD)
    blk = min(T, HG_BLOCK)
    NC, nsb = T // blk, blk // SUB

    def body(u_ref, lb_ref, do_ref, dog_ref, st_ref, du_ref, dlb_ref, dstate, s_all, k_sc, b_sc, v_sc, dk_sc, dbn_sc,
             dv_sc, da_sc, dkd_sc, dvs_sc, dbt_sc):
        @pl.when(pl.program_id(1) == 0)
        def _():
            dstate[...] = jnp.zeros_like(dstate)
            dlb_ref[...] = jnp.zeros_like(dlb_ref)

        lb = lb_ref[...]
        q, v, sg, sgm, f, kin, sq, qin = _hg_gates(u_ref[...], lb)
        tri, ones, tri_t = _hg_block_mats(blk)
        logf = jnp.log(f)
        brel = _tri_dot(tri, logf)
        btot = _tri_dot(ones, logf)
        eb, ekd, ebt = jnp.exp(brel), jnp.exp(btot - brel), jnp.exp(btot)
        a, kd = qin * eb, kin * ekd
        ab, kdb, vb = a.astype(BF16), kd.astype(BF16), v.astype(BF16)
        do = do_ref[...]
        dob = do.astype(BF16)
        st = st_ref[...]
        for i in range(nsb):
            sl = slice(i * SUB, (i + 1) * SUB)
            s_all[i] = st
            st = ebt[i * SUB:i * SUB + 1, :] * st + _dot(vb[sl], kdb[sl], TN)
        ds = dstate[...]
        for i in reversed(range(nsb)):
            sl = slice(i * SUB, (i + 1) * SUB)
            st_i = s_all[i]
            dsb = ds.astype(BF16)
            e_i = ebt[i * SUB:i * SUB + 1, :]
            da_sc[sl, :] = _dot(dob[sl], st_i.astype(BF16), NN)
            dvs_sc[sl, :] = _dot(kdb[sl], dsb, NT)
            dkd_sc[sl, :] = _dot(vb[sl], dsb, NN)
            dbt_sc[sl, :] = jnp.broadcast_to(jnp.sum(ds * st_i, axis=0, keepdims=True) * e_i, (SUB, HEAD))
            ds = e_i * ds + _dot(dob[sl], ab[sl], TN)
        dstate[...] = ds
        da, dkd = da_sc[...], dkd_sc[...]
        t1 = dkd * kd
        dqin = da * eb
        dbrel = da * a - t1
        dkin = dkd * ekd
        dbtot = dbt_sc[...] + _tri_dot(ones, t1)
        _hg_stage(((k_sc, kin), (b_sc, brel), (v_sc, v)), blk)
        for sc in (dk_sc, dbn_sc, dv_sc):
            sc[...] = jnp.zeros_like(sc)
        sub_row = lax.broadcasted_iota(jnp.int32, (blk, HEAD), 0) % SUB
        for d in range(SUB):
            win = slice(SUB - d, SUB - d + blk)
            ks = k_sc[win, :]
            e = jnp.exp(jnp.where(sub_row >= d, brel - b_sc[win, :], NEG))
            qe = qin * e
            col = jnp.sum(qe * ks, axis=-1, keepdims=True)
            dcol = jnp.sum(do * v_sc[win, :], axis=-1, keepdims=True)
            dqe = dcol * qe
            g = dqe * ks
            dqin = dqin + dcol * (e * ks)
            dbrel = dbrel + g
            dk_sc[win, :] += dqe
            dbn_sc[win, :] += g
            dv_sc[win, :] += col * do
        dkin = dkin + dk_sc[SUB:SUB + blk, :]
        dbrel = dbrel - dbn_sc[SUB:SUB + blk, :]
        dv = dvs_sc[...] + dv_sc[SUB:SUB + blk, :]
        dlogf = _tri_dot(tri_t, dbrel) + dbtot
        diff = dlogf / f - dkin
        dlb_ref[...] += jnp.sum(sgm * diff, axis=0, keepdims=True)
        du_ref[:, 0:HEAD] = (dqin * (sq * (1.0 + q * (1.0 - sq)))).astype(BF16)
        du_ref[:, HEAD:2 * HEAD] = ((1.0 - lb) * sg * sgm * diff).astype(BF16)
        du_ref[:, 2 * HEAD:3 * HEAD] = dv.astype(BF16)
        du_ref[:, 3 * HEAD:4 * HEAD] = dog_ref[...]

    rev = lambda h, c: (NC - 1 - c, h)
    return _carry_call(
        body, grid=(H, NC),
        in_specs=[
            pl.BlockSpec((blk, 4 * HEAD), rev), pl.BlockSpec((1, HEAD), lambda h, c: (0, h)),
            pl.BlockSpec((blk, HEAD), rev), pl.BlockSpec((blk, HEAD), rev),
            pl.BlockSpec((None, None, HEAD, HEAD), lambda h, c: (h, NC - 1 - c, 0, 0)),
        ],
        out_specs=[pl.BlockSpec((blk, 4 * HEAD), rev), pl.BlockSpec((1, HEAD), lambda h, c: (0, h))],
        out_shape=[jax.ShapeDtypeStruct((T, W), BF16), jax.ShapeDtypeStruct((1, H * HEAD), F32)],
        scratch_shapes=[pltpu.VMEM((HEAD, HEAD), F32), pltpu.VMEM((nsb, HEAD, HEAD), F32)]
        + [pltpu.VMEM((SUB + blk, HEAD), F32)] * 6 + [pltpu.VMEM((blk, HEAD), F32)] * 4,
        sem=("parallel", "arbitrary"), name=name, args=(u, lb, do, dog, states), pipe=pipe,
    )


def _hg_tail_fwd(o_raw, u, g, name):
    T, D = o_raw.shape
    H = D // HEAD
    tr = min(T, 512)
    blk = pl.BlockSpec((tr, HEAD), lambda h, i: (i, h))

    def body(o_ref, og_ref, g_ref, out_ref):
        o, og = o_ref[...], og_ref[...]
        out_ref[...] = (o * _rstd(o) * g_ref[...] * (og * _sigmoid(og))).astype(BF16)

    return pl.pallas_call(
        body, grid=(H, T // tr),
        in_specs=[blk, pl.BlockSpec((tr, HEAD), lambda h, i: (i, 4 * h + 3)), pl.BlockSpec((1, HEAD), lambda h, i: (0, h))],
        out_specs=blk, out_shape=jax.ShapeDtypeStruct((T, D), BF16),
        compiler_params=_params(("parallel", "parallel")), name=name,
    )(o_raw, u, g)


def _hg_tail_bwd(o_raw, u, g, doa, name):
    T, D = o_raw.shape
    H = D // HEAD
    tr = min(T, 512)
    blk = pl.BlockSpec((tr, HEAD), lambda h, i: (i, h))
    vec = pl.BlockSpec((1, HEAD), lambda h, i: (0, h))

    def body(o_ref, og_ref, g_ref, doa_ref, do_ref, dog_ref, dg_ref):
        @pl.when(pl.program_id(1) == 0)
        def _():
            dg_ref[...] = jnp.zeros_like(dg_ref)

        o, og, doa, g = o_ref[...], og_ref[...], doa_ref[...], g_ref[...]
        sg = _sigmoid(og)
        r = _rstd(o)
        xhat = o * r
        dog_ref[...] = (doa * (xhat * g) * (sg * (1.0 + og * (1.0 - sg)))).astype(BF16)
        dn = doa * (og * sg)
        dg_ref[...] += jnp.sum(dn * xhat, axis=0, keepdims=True)
        dxh = dn * g
        do_ref[...] = r * (dxh - xhat * jnp.mean(dxh * xhat, axis=-1, keepdims=True))

    return pl.pallas_call(
        body, grid=(H, T // tr),
        in_specs=[blk, pl.BlockSpec((tr, HEAD), lambda h, i: (i, 4 * h + 3)), vec, blk],
        out_specs=[blk, blk, vec],
        out_shape=[jax.ShapeDtypeStruct((T, D), F32), jax.ShapeDtypeStruct((T, D), BF16), jax.ShapeDtypeStruct((1, D), F32)],
        compiler_params=_params(("parallel", "arbitrary")), name=name,
    )(o_raw, u, g, doa)


def _lb_fwd(logits, name):
    def body(l_ref, o_ref):
        l0, l1 = l_ref[0:1, :], l_ref[1:2, :]
        m = jnp.maximum(l0, l1)
        e0, e1 = jnp.exp(l0 - m), jnp.exp(l1 - m)
        o_ref[...] = e0 / (e0 + e1)

    D = logits.shape[1]
    return pl.pallas_call(body, out_shape=jax.ShapeDtypeStruct((1, D), F32), name=name)(logits)


def _lb_bwd(lb, dlb, name):
    def body(lb_ref, d_ref, o_ref):
        lb = lb_ref[...]
        d0 = d_ref[...] * lb * (1.0 - lb)
        o_ref[0:1, :] = d0
        o_ref[1:2, :] = -d0

    D = lb.shape[1]
    return pl.pallas_call(body, out_shape=jax.ShapeDtypeStruct((2, D), F32), name=name)(lb, dlb)


def _slots(w):
    return w.shape[0] if w.ndim == 3 else 0


def _ffn_fwd(x, p, tag, carry):
    mm = lambda a, b, mode, dt, name, **kw: _mm(a, b, mode, dt, name, pipe=carry.get(name), **kw)
    hb = _norm_fwd(x, p["pre_g"], f"{tag}_pre_norm", pipe=carry.get(f"{tag}_pre_norm"))
    a = mm(hb, p["w_gate"], "nn", F32, f"{tag}_gate")
    b = mm(hb, p["w_up"], "nn", F32, f"{tag}_up")
    sb = _swiglu_fwd(a, b, f"{tag}_swiglu", pipe=carry.get(f"{tag}_swiglu"))
    y = mm(sb, p["w_down"], "nn", F32, f"{tag}_down")
    xo = _postnorm_fwd(x, y, p["post_g"], FFN_RESIDUAL_WEIGHT, f"{tag}_post_norm", pipe=carry.get(f"{tag}_post_norm"))
    return xo, (x, hb, a, b, sb, y)


def _ffn_bwd(dxo, p, saved, tag, carry, on_dw):
    mm = lambda a, b, mode, dt, name, **kw: _mm(a, b, mode, dt, name, pipe=carry.get(name), **kw)
    x, hb, a, b, sb, y = saved
    dyb, dpost = _norm_bwd(y, p["post_g"], dxo, FFN_RESIDUAL_WEIGHT, BF16, f"{tag}_post_norm_bwd",
                           pipe=carry.get(f"{tag}_post_norm_bwd"))
    dw_down = mm(sb, dyb, "tn", BF16, f"{tag}_down_dw")
    on_dw("down", {"w_down": dw_down})
    ds = mm(dyb, p["w_down"], "nt", F32, f"{tag}_down_dx")
    dab, dbb = _swiglu_bwd(a, b, ds, f"{tag}_swiglu_bwd", pipe=carry.get(f"{tag}_swiglu_bwd"))
    dw_gate = mm(hb, dab, "tn", BF16, f"{tag}_gate_dw", out_slots=_slots(p["w_gate"]))
    dw_up = mm(hb, dbb, "tn", BF16, f"{tag}_up_dw", out_slots=_slots(p["w_up"]))
    on_dw("gate_up", {"w_gate": dw_gate, "w_up": dw_up})
    dh = mm(dab, p["w_gate"], "nt", F32, f"{tag}_gate_dx")
    dh = mm(dbb, p["w_up"], "nt", F32, f"{tag}_up_dx", add=dh)
    dx, dpre = _norm_bwd(x, p["pre_g"], dh, 1.0, F32, f"{tag}_pre_norm_bwd", res=dxo, pipe=carry.get(f"{tag}_pre_norm_bwd"))
    return dx, {"pre_g": dpre, "w_gate": dw_gate, "w_up": dw_up, "w_down": dw_down, "post_g": dpost}


def _mixer_fwd(x, cos, sin, p, carry, rest_of):
    mm = lambda a, b, mode, dt, name, **kw: _mm(a, b, mode, dt, name, pipe=carry.get(name), **kw)
    scale = (HEAD + ROPE) ** -0.5
    hb = _norm_fwd(x, p["pre_g"], "mix_pre_norm")
    u = mm(hb, p["w_hg"], "nn", F32, "mix_in_hg")
    cq = mm(hb, p["w_cq"], "nn", F32, "mix_in_cq")
    ckv = mm(hb, p["w_ckv"], "nn", F32, "mix_in_ckv")
    kpe = mm(hb, p["w_kpe"], "nn", F32, "mix_in_kpe")
    ga = mm(hb, p["w_ga"], "nn", F32, "mix_in_ga")
    gb = mm(hb, p["w_gb"], "nn", F32, "mix_in_gb")
    lb = _lb_fwd(p["lb_logits"], "hg_lb")
    o_raw, states = _hg_scan_fwd(u, lb, "hg_scan", pipe=carry.get("hg_scan"))
    oa = _hg_tail_fwd(o_raw, u, p["hg_norm_g"], "hg_tail")
    p = {**p, **rest_of()}
    ya = mm(oa, p["w_branch_a"], "nn", F32, "mix_branch_a")
    cqn = _norm_fwd(cq, p["q_norm_g"], "mla_q_norm")
    qn = mm(cqn, p["w_qn"], "nn", BF16, "mla_q_up_nope")
    qpe = _rope_q_fwd(mm(cqn, p["w_qpe"], "nn", F32, "mla_q_up_pe"), cos, sin, "mla_rope_q")
    ckvn = _norm_fwd(ckv, p["kv_norm_g"], "mla_kv_norm")
    kn = mm(ckvn, p["w_kn"], "nn", BF16, "mla_k_up")
    vv = mm(ckvn, p["w_vv"], "nn", BF16, "mla_v_up")
    kpe2 = _rope_k_fwd(kpe, cos, sin, "mla_rope_k")
    ob = _attn_fwd(qn, kn, vv, scale, "mla_attn", qpe=qpe, kpe=kpe2, causal=True, pipe=carry.get("mla_attn"))
    yb = mm(ob, p["w_branch_b"], "nn", F32, "mix_branch_b")
    ym = _merge_fwd(ga, gb, ya, yb, "mix_merge")
    z = mm(ym, p["w_out"], "nn", F32, "mix_out")
    xo = _postnorm_fwd(x, z, p["post_g"], 1.0, "mix_post_norm")
    saved = (x, hb, u, cq, ckv, ga, gb, lb, o_raw, states, oa, ya, cqn, qn, qpe, ckvn, kn, vv, kpe2, ob, yb, ym, z)
    return xo, saved, p


def _mixer_bwd(dxo, cos, sin, p, saved, carry, on_early, on_late):
    x, hb, u, cq, ckv, ga, gb, lb, o_raw, states, oa, ya, cqn, qn, qpe, ckvn, kn, vv, kpe2, ob, yb, ym, z = saved
    scale = (HEAD + ROPE) ** -0.5
    g = {}
    dzb, g["post_g"] = _norm_bwd(z, p["post_g"], dxo, 1.0, BF16, "mix_post_norm_bwd")
    g["w_out"] = _mm(ym, dzb, "tn", BF16, "mix_out_dw")
    dym = _mm(dzb, p["w_out"], "nt", F32, "mix_out_dx")
    dya, dyb, dga, dgb = _merge_bwd(ga, gb, ya, yb, dym, "mix_merge_bwd")
    g["w_branch_a"] = _mm(oa, dya, "tn", BF16, "mix_branch_a_dw")
    doa = _mm(dya, p["w_branch_a"], "nt", F32, "mix_branch_a_dx")
    do_raw, dog, g["hg_norm_g"] = _hg_tail_bwd(o_raw, u, p["hg_norm_g"], doa, "hg_tail_bwd")
    du, dlb = _hg_scan_bwd(u, lb, do_raw, dog, states, "hg_scan_bwd", pipe=carry.get("hg_scan_bwd"))
    g["lb_logits"] = _lb_bwd(lb, dlb, "hg_lb_bwd")
    g["w_branch_b"] = _mm(ob, dyb, "tn", BF16, "mix_branch_b_dw")
    on_early({k: g[k] for k in ("w_out", "w_branch_a", "w_branch_b")})
    dob = _mm(dyb, p["w_branch_b"], "nt", BF16, "mix_branch_b_dx")
    dqn, dkn, dvv, dqpe_h, dkpe_h = _attn_bwd(qn, kn, vv, dob, scale, "mla_attn_bwd", qpe=qpe, kpe=kpe2, causal=True,
                                              pipe=carry.get("mla_attn_bwd"))
    dqpe = _rope_q_bwd(dqpe_h, cos, sin, "mla_rope_q_bwd")
    dkpe = _rope_k_bwd(dkpe_h, cos, sin, "mla_rope_k_bwd")
    g["w_qn"] = _mm(cqn, dqn, "tn", BF16, "mla_q_up_nope_dw")
    g["w_qpe"] = _mm(cqn, dqpe, "tn", BF16, "mla_q_up_pe_dw")
    dcqn = _mm(dqn, p["w_qn"], "nt", F32, "mla_q_up_nope_dx")
    dcqn = _mm(dqpe, p["w_qpe"], "nt", F32, "mla_q_up_pe_dx", add=dcqn)
    dcq, g["q_norm_g"] = _norm_bwd(cq, p["q_norm_g"], dcqn, 1.0, BF16, "mla_q_norm_bwd")
    g["w_kn"] = _mm(ckvn, dkn, "tn", BF16, "mla_k_up_dw")
    g["w_vv"] = _mm(ckvn, dvv, "tn", BF16, "mla_v_up_dw")
    dckvn = _mm(dkn, p["w_kn"], "nt", F32, "mla_k_up_dx")
    dckvn = _mm(dvv, p["w_vv"], "nt", F32, "mla_v_up_dx", add=dckvn)
    dckv, g["kv_norm_g"] = _norm_bwd(ckv, p["kv_norm_g"], dckvn, 1.0, BF16, "mla_kv_norm_bwd")
    parts = (("w_hg", du), ("w_cq", dcq), ("w_ckv", dckv), ("w_kpe", dkpe), ("w_ga", dga), ("w_gb", dgb))
    for key, d in parts:
        g[key] = _mm(hb, d, "tn", BF16, f"mix_in_{key}_dw")
    on_late(g)
    dh = None
    for key, d in parts:
        dh = _mm(d, p[key], "nt", F32, f"mix_in_{key}_dx", add=dh, pipe=carry.get(f"mix_in_{key}_dx"))
    dx, g["pre_g"] = _norm_bwd(x, p["pre_g"], dh, 1.0, F32, "mix_pre_norm_bwd", res=dxo, pipe=carry.get("mix_pre_norm_bwd"))
    return dx, g


def _xa_fwd(x, mem, p, carry):
    mm = lambda a, b, mode, dt, name, **kw: _mm(a, b, mode, dt, name, pipe=carry.get(name), **kw)
    scale = HEAD ** -0.5
    hb = _norm_fwd(x, p["pre_g"], "xa_pre_norm")
    mb = _norm_fwd(mem, p["mem_g"], "xa_mem_norm")
    q = mm(hb, p["w_q"], "nn", BF16, "xa_q")
    k = mm(mb, p["w_k"], "nn", BF16, "xa_k")
    v = mm(mb, p["w_v"], "nn", BF16, "xa_v")
    o = _attn_fwd(q, k, v, scale, "xa_attn", pipe=carry.get("xa_attn"))
    z = mm(o, p["w_o"], "nn", F32, "xa_o")
    xo = _postnorm_fwd(x, z, p["post_g"], 1.0, "xa_post_norm")
    return xo, (x, mem, hb, mb, q, k, v, o, z)


def _xa_bwd(dxo, p, saved):
    x, mem, hb, mb, q, k, v, o, z = saved
    scale = HEAD ** -0.5
    g = {}
    dzb, g["post_g"] = _norm_bwd(z, p["post_g"], dxo, 1.0, BF16, "xa_post_norm_bwd")
    g["w_o"] = _mm(o, dzb, "tn", BF16, "xa_o_dw", out_slots=_slots(p["w_o"]))
    do = _mm(dzb, p["w_o"], "nt", BF16, "xa_o_dx")
    dq, dk, dv = _attn_bwd(q, k, v, do, scale, "xa_attn_bwd")
    g["w_q"] = _mm(hb, dq, "tn", BF16, "xa_q_dw")
    g["w_k"] = _mm(mb, dk, "tn", BF16, "xa_k_dw")
    g["w_v"] = _mm(mb, dv, "tn", BF16, "xa_v_dw")
    dh = _mm(dq, p["w_q"], "nt", F32, "xa_q_dx")
    dm = _mm(dk, p["w_k"], "nt", F32, "xa_k_dx")
    dm = _mm(dv, p["w_v"], "nt", F32, "xa_v_dx", add=dm)
    _, g["mem_g"] = _norm_bwd(mem, p["mem_g"], dm, 1.0, BF16, "xa_mem_norm_bwd")
    dx, g["pre_g"] = _norm_bwd(x, p["pre_g"], dh, 1.0, F32, "xa_pre_norm_bwd", res=dxo)
    return dx, g


def _local_step(x, mem, cos, sin, target, params_of, carry, on_grads):
    p1 = params_of("ffn1")
    x1, s1 = _ffn_fwd(x, p1, "ffn1", carry)
    x2, s2, p2 = _mixer_fwd(x1, cos, sin, params_of("mix"), carry, lambda: params_of("mix_rest"))
    p3 = params_of("xa")
    x3, s3 = _xa_fwd(x2, mem, p3, carry)
    p4 = params_of("ffn2")
    x4, s4 = _ffn_fwd(x3, p4, "ffn2", carry)
    dy, sq_err = _loss_head(x4, target, "loss_head")
    loss = 0.5 / x.shape[1] * jnp.sum(sq_err)
    dx, g4 = _ffn_bwd(dy, p4, s4, "ffn2", carry, lambda stage, g: on_grads(f"ffn2_dw_{stage}", g))
    on_grads("ffn2", g4)
    dx, g3 = _xa_bwd(dx, p3, s3)
    on_grads("xa", g3)
    dx, g2 = _mixer_bwd(dx, cos, sin, p2, s2, carry, lambda g: on_grads("mix_early", g), lambda g: on_grads("mix_late", g))
    on_grads("mix", g2)
    dx, g1 = _ffn_bwd(dx, p1, s1, "ffn1", carry, lambda stage, g: on_grads(f"ffn1_dw_{stage}", g))
    on_grads("ffn1", g1)
    return loss, dx


def _split_w_in(w_in):
    D = w_in.shape[0]
    H = D // HEAD
    lora = (w_in.shape[1] - 6 * D - ROPE) // 2
    o = 4 * D
    w_hg = w_in[:, :o].reshape(D, 4, H, HEAD).transpose(0, 2, 1, 3).reshape(D, 4 * D)
    w_cq, w_ckv = w_in[:, o:o + lora], w_in[:, o + lora:o + 2 * lora]
    o += 2 * lora
    w_kpe = jnp.pad(w_in[:, o:o + ROPE], ((0, 0), (0, HEAD - ROPE)))
    o += ROPE
    return {"w_hg": w_hg, "w_cq": w_cq, "w_ckv": w_ckv, "w_kpe": w_kpe, "w_ga": w_in[:, o:o + D], "w_gb": w_in[:, o + D:o + 2 * D]}


def _merge_w_in(g):
    D = g["w_ga"].shape[0]
    H = D // HEAD
    hg = g["w_hg"].reshape(D, H, 4, HEAD).transpose(0, 2, 1, 3).reshape(D, 4 * D)
    return jnp.concatenate([hg, g["w_cq"], g["w_ckv"], g["w_kpe"][:, :ROPE], g["w_ga"], g["w_gb"]], axis=1)


def _split_heads(w, rest):
    K, N = w.shape
    w3 = w.reshape(K, N // (HEAD + rest), HEAD + rest)
    return w3[:, :, :HEAD].reshape(K, -1), w3[:, :, HEAD:].reshape(K, -1)


def _merge_heads(a, b, rest):
    K = a.shape[0]
    H = a.shape[1] // HEAD
    return jnp.concatenate([a.reshape(K, H, HEAD), b.reshape(K, H, rest)], axis=2).reshape(K, H * (HEAD + rest))


BLOCK_WEIGHTS = {
    "ffn1": ("ffn1_w_gate", "ffn1_w_up", "ffn1_w_down"),
    "mix": ("w_in",),
    "mix_rest": ("mla_w_q_up", "mla_w_kv_up", "w_branch_a", "w_branch_b", "w_out"),
    "xa": ("xa_w_q", "xa_w_k", "xa_w_v", "xa_w_o"),
    "ffn2": ("ffn2_w_gate", "ffn2_w_up", "ffn2_w_down"),
}


def _block_params(block, w, small):
    if block in ("ffn1", "ffn2"):
        return {"pre_g": small[f"{block}_pre_g"], "w_gate": w[f"{block}_w_gate"], "w_up": w[f"{block}_w_up"],
                "w_down": w[f"{block}_w_down"], "post_g": small[f"{block}_post_g"]}
    if block == "xa":
        return {"pre_g": small["xa_pre_g"], "mem_g": small["xa_mem_g"], "post_g": small["xa_post_g"],
                "w_q": w["xa_w_q"], "w_k": w["xa_w_k"], "w_v": w["xa_w_v"], "w_o": w["xa_w_o"]}
    if block == "mix_rest":
        (w_qn, w_qpe), (w_kn, w_vv) = _split_heads(w["mla_w_q_up"], ROPE), _split_heads(w["mla_w_kv_up"], HEAD)
        return dict(w_qn=w_qn, w_qpe=w_qpe, w_kn=w_kn, w_vv=w_vv, w_branch_a=w["w_branch_a"],
                    w_branch_b=w["w_branch_b"], w_out=w["w_out"])
    mix = _split_w_in(w["w_in"])
    mix.update(pre_g=small["mix_pre_g"], post_g=small["mix_post_g"], hg_norm_g=small["hg_norm_g"],
               q_norm_g=small["mla_q_norm_g"], kv_norm_g=small["mla_kv_norm_g"], lb_logits=small["hgrn_lb_logits"])
    return mix


def _block_grads(block, g):
    if block in ("ffn1", "ffn2"):
        return {f"{block}_{k}": g[k] for k in ("pre_g", "w_gate", "w_up", "w_down", "post_g")}
    if block == "xa":
        return {f"xa_{k}": g[k] for k in ("pre_g", "mem_g", "post_g", "w_q", "w_k", "w_v", "w_o")}
    if block[4:8] == "_dw_":
        return {f"{block[:4]}_{k}": v for k, v in g.items()}
    if block == "mix_early":
        return dict(g)
    if block == "mix_late":
        return dict(w_in=_merge_w_in(g), mla_w_q_up=_merge_heads(g["w_qn"], g["w_qpe"], ROPE),
                    mla_w_kv_up=_merge_heads(g["w_kn"], g["w_vv"], HEAD))
    return dict(mix_pre_g=g["pre_g"], mix_post_g=g["post_g"],
                hg_norm_g=g["hg_norm_g"], mla_q_norm_g=g["q_norm_g"], mla_kv_norm_g=g["kv_norm_g"],
                hgrn_lb_logits=g["lb_logits"])


def _rope_tables(positions):
    inv_freq = 1.0 / (ROPE_THETA ** (jnp.arange(0, ROPE, 2, dtype=F32) / ROPE))
    ang = positions.astype(F32)[:, None] * inv_freq
    return jnp.tile(jnp.cos(ang), (1, 4)), jnp.tile(jnp.sin(ang), (1, 4))


def _adamw(w, g, m, v, name):
    bc1 = 1.0 - ADAM_B1 ** ADAM_STEP
    bc2 = 1.0 - ADAM_B2 ** ADAM_STEP

    def body(w_ref, g_ref, m_ref, v_ref, go_ref, d_ref, mo_ref, vo_ref):
        g = g_ref[...]
        m = ADAM_B1 * m_ref[...] + (1.0 - ADAM_B1) * g
        v = ADAM_B2 * v_ref[...] + (1.0 - ADAM_B2) * (g * g)
        go_ref[...] = g
        mo_ref[...] = m
        vo_ref[...] = v
        d_ref[...] = -ADAM_LR * ((m / bc1) / (jnp.sqrt(v / bc2) + ADAM_EPS) + ADAM_WD * w_ref[...])

    return _rows(body, [(w, "row"), (g, "row"), (m, "row"), (v, "row")], [(w.shape, F32, "row")] * 4, name)


ANY = pl.BlockSpec(memory_space=pl.ANY)
COMM_AXES = ("x", "y", "c")


def _place():
    x, y, c = (lax.axis_index(n) for n in COMM_AXES)
    chips = [(1 - x, y), (x, 1 - y), (1 - x, 1 - y)]
    return x, y, c, 2 * x + y, (x, y, 1 - c), chips


def _remote(src, dst, send_sems, recv_sems, j, to):
    return pltpu.make_async_remote_copy(src_ref=src, dst_ref=dst, send_sem=send_sems.at[j], recv_sem=recv_sems.at[j],
                                        device_id=to, device_id_type=MESH)


def _dma_sems(n):
    return [pltpu.SemaphoreType.DMA((n,)), pltpu.SemaphoreType.DMA((n,))]


def _all_gather(shards, whole, name):
    n = len(shards)

    def body(*refs):
        srcs, outs, send_sems, recv_sems = refs[:n], refs[n:2 * n], refs[2 * n], refs[2 * n + 1]
        x, y, c, me, sibling, chips = _place()
        sent = []

        def start(cp):
            cp.start()
            sent.append(cp)

        def rows(w, h):
            hr = srcs[w].shape[0] // 2
            return pl.ds(h * hr, hr)

        gathered = [w for w in range(n) if whole[w]]
        for w in gathered:
            for j, (cx, cy) in enumerate(chips):
                start(_remote(srcs[w].at[rows(w, c)], outs[w].at[me, rows(w, c)], send_sems, recv_sems, 7 * w + j, (cx, cy, c)))
        for w in range(n):
            start(_remote(srcs[w], outs[w].at[me], send_sems, recv_sems, 7 * w + 6, sibling))
        for w in gathered:
            for j, (cx, cy) in enumerate(chips):
                blk = outs[w].at[2 * cx + cy, rows(w, c)]
                _remote(srcs[w].at[rows(w, c)], blk, send_sems, recv_sems, 7 * w + j, (cx, cy, c)).wait_recv()
                start(_remote(blk, blk, send_sems, recv_sems, 7 * w + 3 + j, sibling))
        for w in gathered:
            for j, (cx, cy) in enumerate(chips):
                blk = outs[w].at[2 * cx + cy, rows(w, 1 - c)]
                _remote(blk, blk, send_sems, recv_sems, 7 * w + 3 + j, sibling).wait_recv()
        for w in range(n):
            _remote(srcs[w], outs[w].at[me], send_sems, recv_sems, 7 * w + 6, sibling).wait_recv()
        for cp in sent:
            cp.wait_send()

    return pl.pallas_call(
        body, in_specs=[ANY] * n, out_specs=[ANY] * n,
        out_shape=[jax.ShapeDtypeStruct((N_CHIP,) + s.shape, s.dtype) for s in shards],
        scratch_shapes=_dma_sems(7 * n), name=name,
    )(*shards)


def _rs_swap(grads, name):
    n = len(grads)

    def body(*refs):
        gs, outs, send_sems, recv_sems = refs[:n], refs[n:2 * n], refs[2 * n], refs[2 * n + 1]
        x, y, c, me, sibling, chips = _place()
        cps = []
        for w in range(n):
            hr = gs[w].shape[1] // 2
            cps.append(_remote(gs[w].at[:, pl.ds((1 - c) * hr, hr)], outs[w], send_sems, recv_sems, w, sibling))
            cps[-1].start()
        for cp in cps:
            cp.wait()

    return pl.pallas_call(
        body, in_specs=[ANY] * n, out_specs=[ANY] * n,
        out_shape=[jax.ShapeDtypeStruct((g.shape[0], g.shape[1] // 2, g.shape[2]), g.dtype) for g in grads],
        scratch_shapes=_dma_sems(n), name=name,
    )(*grads)


def _sum_rows(hr, row_bytes):
    return _tile(hr, max(16, ROW_BUDGET // (2 * row_bytes) // 16 * 16), 16)


def _rs_pair_sum(g, got, c, name):
    S, r, cw = g.shape
    hr = r // 2
    tr = _sum_rows(hr, 3 * cw * 2)
    nrb = hr // tr

    def body(c_ref, a_ref, b_ref, o_ref):
        o_ref[...] = (a_ref[...].astype(F32) + b_ref[...].astype(F32)).astype(BF16)

    return pl.pallas_call(
        body,
        grid_spec=pltpu.PrefetchScalarGridSpec(
            num_scalar_prefetch=1, grid=(S, nrb),
            in_specs=[pl.BlockSpec((None, tr, cw), lambda k, i, c_ref: (k, c_ref[0] * nrb + i, 0)),
                      pl.BlockSpec((None, tr, cw), lambda k, i, c_ref: (k, i, 0))],
            out_specs=pl.BlockSpec((None, tr, cw), lambda k, i, c_ref: (k, i, 0)),
        ),
        out_shape=jax.ShapeDtypeStruct((S, hr, cw), BF16),
        compiler_params=_params(("parallel", "parallel")), name=name,
    )(c, g, got)


def _rs_chip_sum(pair, got, place, name):
    S, hr, cw = pair.shape
    tr = _sum_rows(hr, cw * (4 * 2 + 4))

    def body(p_ref, a_ref, z_ref, o_ref):
        o_ref[...] = a_ref[...].astype(F32) + z_ref[0].astype(F32) + z_ref[1].astype(F32) + z_ref[2].astype(F32)

    return pl.pallas_call(
        body,
        grid_spec=pltpu.PrefetchScalarGridSpec(
            num_scalar_prefetch=1, grid=(hr // tr,),
            in_specs=[pl.BlockSpec((None, tr, cw), lambda i, p_ref: (p_ref[0], i, 0)),
                      pl.BlockSpec((3, tr, cw), lambda i, p_ref: (0, i, 0))],
            out_specs=pl.BlockSpec((None, tr, cw), lambda i, p_ref: (p_ref[1], i, 0)),
        ),
        out_shape=jax.ShapeDtypeStruct((2, hr, cw), F32),
        compiler_params=_params(("parallel",)), name=name,
    )(place, pair, got)


def _rs_share(halves, name):
    n = len(halves)

    def body(*refs):
        outs, send_sems, recv_sems = refs[n:2 * n], refs[2 * n], refs[2 * n + 1]
        x, y, c, me, sibling, chips = _place()
        cps = []
        for w in range(n):
            cps.append(_remote(outs[w].at[c], outs[w].at[c], send_sems, recv_sems, w, sibling))
            cps[-1].start()
        for w in range(n):
            _remote(outs[w].at[1 - c], outs[w].at[1 - c], send_sems, recv_sems, w, sibling).wait_recv()
        for cp in cps:
            cp.wait_send()

    return pl.pallas_call(
        body, in_specs=[ANY] * n, out_specs=[ANY] * n,
        out_shape=[jax.ShapeDtypeStruct(h.shape, h.dtype) for h in halves],
        input_output_aliases={i: i for i in range(n)},
        scratch_shapes=_dma_sems(n), name=name,
    )(*halves)


def _rs_begin(names, grads, core, fractions, tag):
    slotted = [_to_slots(n, grads[n]) for n in names]
    got = _rs_swap(slotted, f"grads_sibling_swap_{tag}")
    pairs = [_rs_pair_sum(s, t, core.reshape(1), f"grads_pair_sum_{n}") for n, s, t in zip(names, slotted, got)]
    lands = [lax.empty((N_CHIP - 1,) + p.shape[1:], p.dtype) for p in pairs]
    return pairs, _Pipe("reduce", pairs, lands, fractions)


def _rs_end(names, pairs, pipe, place, shapes, tag):
    _pipe_flush(pipe, f"grads_chip_exchange_rest_{tag}")
    halves = [_rs_chip_sum(p, o, place, f"grads_chip_sum_{n}") for n, p, o in zip(names, pairs, pipe.lands)]
    both = _rs_share(halves, f"grads_sibling_share_{tag}")
    return {n: b.reshape(s) for n, b, s in zip(names, both, shapes)}


def _all_reduce_small(s, name):
    flips = [(dx, dy, dc) for dx in (0, 1) for dy in (0, 1) for dc in (0, 1) if (dx, dy, dc) != (0, 0, 0)]

    def body(s_ref, o_ref, buf, send_sems, recv_sems):
        x, y, c = (lax.axis_index(n) for n in COMM_AXES)
        me = 4 * x + 2 * y + c
        buf[me] = s_ref[...]
        peers = [((1 - x) if dx else x, (1 - y) if dy else y, (1 - c) if dc else c) for dx, dy, dc in flips]
        sent = [_remote(s_ref, buf.at[me], send_sems, recv_sems, j, p) for j, p in enumerate(peers)]
        for cp in sent:
            cp.start()
        for j, (px, py, pc) in enumerate(peers):
            _remote(s_ref, buf.at[4 * px + 2 * py + pc], send_sems, recv_sems, j, (px, py, pc)).wait_recv()
        for cp in sent:
            cp.wait_send()
        acc = buf[0]
        for d in range(1, N_DEV):
            acc = acc + buf[d]
        o_ref[...] = acc

    vmem = pl.BlockSpec(memory_space=pltpu.VMEM)
    return pl.pallas_call(
        body, in_specs=[vmem], out_specs=vmem, out_shape=jax.ShapeDtypeStruct(s.shape, F32),
        scratch_shapes=[pltpu.VMEM((N_DEV,) + s.shape, F32), pltpu.SemaphoreType.DMA((7,)), pltpu.SemaphoreType.DMA((7,))],
        name=name,
    )(s)


BIG = {
    "ffn1_w_gate": 1, "ffn1_w_up": 1, "ffn1_w_down": 0, "w_in": 1, "mla_w_q_up": 1, "mla_w_kv_up": 1,
    "w_branch_a": 0, "w_branch_b": 0, "w_out": 0, "xa_w_q": 0, "xa_w_k": 0, "xa_w_v": 0, "xa_w_o": 1,
    "ffn2_w_gate": 1, "ffn2_w_up": 1, "ffn2_w_down": 0,
}
WEIGHTS = [
    "hgrn_lb_logits", "ffn1_pre_g", "ffn1_w_gate", "ffn1_w_up", "ffn1_w_down", "ffn1_post_g", "mix_pre_g", "w_in",
    "hg_norm_g", "mla_q_norm_g", "mla_w_q_up", "mla_kv_norm_g", "mla_w_kv_up", "w_branch_a", "w_branch_b", "w_out",
    "mix_post_g", "xa_pre_g", "xa_mem_g", "xa_w_q", "xa_w_k", "xa_w_v", "xa_w_o", "xa_post_g", "ffn2_pre_g",
    "ffn2_w_gate", "ffn2_w_up", "ffn2_w_down", "ffn2_post_g",
]
SMALL = [n for n in WEIGHTS if n not in BIG]
SLOTTED = ("ffn1_w_gate", "ffn1_w_up", "ffn2_w_gate", "ffn2_w_up", "xa_w_o")
MIX_EARLY = ("w_out", "w_branch_a", "w_branch_b")


def _from_slots(name, g):
    S, r, cw = g.shape
    if BIG[name] == 0:
        return g.reshape(S * r, cw)
    return g if name in SLOTTED else g.transpose(1, 0, 2).reshape(r, S * cw)


def _to_slots(name, g):
    if g.ndim == 3:
        return g
    if BIG[name] == 0:
        return g.reshape(N_CHIP, g.shape[0] // N_CHIP, g.shape[1])
    return g.reshape(g.shape[0], N_CHIP, g.shape[1] // N_CHIP).transpose(1, 0, 2)


def _pack_small(vals, width):
    rows = [jnp.pad(v, ((0, 0), (0, width - v.shape[1]))) for v in vals]
    s = jnp.concatenate(rows, axis=0)
    return jnp.pad(s, ((0, -s.shape[0] % 8), (0, 0)))


def _unpack_small(s, shapes):
    out, o = [], 0
    for r, w in shapes:
        out.append(s[o:o + r, :w])
        o += r
    return out


def kernel(x, mem, positions, hgrn_lb_logits, ffn1_pre_g, ffn1_w_gate, ffn1_w_up, ffn1_w_down, ffn1_post_g, mix_pre_g, w_in, hg_norm_g, mla_q_norm_g, mla_w_q_up, mla_kv_norm_g, mla_w_kv_up, w_branch_a, w_branch_b, w_out, mix_post_g, xa_pre_g, xa_mem_g, xa_w_q, xa_w_k, xa_w_v, xa_w_o, xa_post_g, ffn2_pre_g, ffn2_w_gate, ffn2_w_up, ffn2_w_down, ffn2_post_g, loss_target, m_hgrn_lb_logits, m_ffn1_pre_g, m_ffn1_w_gate, m_ffn1_w_up, m_ffn1_w_down, m_ffn1_post_g, m_mix_pre_g, m_w_in, m_hg_norm_g, m_mla_q_norm_g, m_mla_w_q_up, m_mla_kv_norm_g, m_mla_w_kv_up, m_w_branch_a, m_w_branch_b, m_w_out, m_mix_post_g, m_xa_pre_g, m_xa_mem_g, m_xa_w_q, m_xa_w_k, m_xa_w_v, m_xa_w_o, m_xa_post_g, m_ffn2_pre_g, m_ffn2_w_gate, m_ffn2_w_up, m_ffn2_w_down, m_ffn2_post_g, v_hgrn_lb_logits, v_ffn1_pre_g, v_ffn1_w_gate, v_ffn1_w_up, v_ffn1_w_down, v_ffn1_post_g, v_mix_pre_g, v_w_in, v_hg_norm_g, v_mla_q_norm_g, v_mla_w_q_up, v_mla_kv_norm_g, v_mla_w_kv_up, v_w_branch_a, v_w_branch_b, v_w_out, v_mix_post_g, v_xa_pre_g, v_xa_mem_g, v_xa_w_q, v_xa_w_k, v_xa_w_v, v_xa_w_o, v_xa_post_g, v_ffn2_pre_g, v_ffn2_w_gate, v_ffn2_w_up, v_ffn2_w_down, v_ffn2_post_g):
    a = dict(locals())
    big = list(BIG)
    small = {n: a[n] for n in SMALL}
    core = lax.axis_index("c").astype(jnp.int32)
    place = jnp.stack([(2 * lax.axis_index("x") + lax.axis_index("y")).astype(jnp.int32), core])

    shards = {n: a[n][0].astype(BF16) for n in big}
    whole = [n in BLOCK_WEIGHTS["ffn1"] for n in big]
    lands = dict(zip(big, _all_gather([shards[n] for n in big], whole, "weights_all_gather")))
    plan = {
        "mix": ((2, 10, 10, 6, 10, 3), [f"ffn1_{k}" for k in ("pre_norm", "gate", "up", "swiglu", "down", "post_norm")]),
        "mix_rest": ((8, 1, 1, 3, 3), ["mix_in_hg", "mix_in_cq", "mix_in_ckv", "mix_in_ga", "mix_in_gb"]),
        "ffn2": ((4, 3), ["hg_scan", "mla_attn"]),
        "xa": ((1, 1), ["mix_branch_b", "mix_out"]),
    }
    carry, pipes = {}, {}
    for block, (fractions, carriers) in plan.items():
        names = BLOCK_WEIGHTS[block]
        pipes[block] = _Pipe("gather", [shards[n] for n in names], [lands[n] for n in names], fractions)
        carry.update({c: pipes[block] for c in carriers})

    def params_of(block):
        if block in pipes:
            _pipe_flush(pipes[block], f"weights_gather_rest_{block}")
            lands.update(zip(BLOCK_WEIGHTS[block], pipes[block].lands))
        return _block_params(block, {n: _from_slots(n, lands[n]) for n in BLOCK_WEIGHTS[block]}, small)

    g_small, g_big, open_groups, held = {}, {}, {}, {}

    def begin(tag, names, fractions, carriers):
        pairs, pipe = _rs_begin(names, held, core, fractions, tag)
        open_groups[tag] = (names, pairs, pipe)
        carry.update({c: pipe for c in carriers})

    def end(tag):
        names, pairs, pipe = open_groups.pop(tag)
        g_big.update(_rs_end(names, pairs, pipe, place, [a[n].shape[1:] for n in names], tag))

    def on_grads(block, g):
        for n, v in _block_grads(block, g).items():
            (held if n in BIG else g_small)[n] = v
        if block == "ffn2":
            begin("ffn2", BLOCK_WEIGHTS["ffn2"], (1,), ("hg_scan_bwd",))
        elif block == "mix_early":
            begin("early", BLOCK_WEIGHTS["xa"] + MIX_EARLY, (1,), ("mla_attn_bwd",))
        elif block == "mix_late":
            end("ffn2")
            end("early")
            begin("mid", ["w_in", "mla_w_q_up", "mla_w_kv_up"], (10, 3, 3, 3, 2, 6, 6, 6),
                  ("mix_in_w_hg_dx", "mix_in_w_ga_dx", "mix_in_w_gb_dx", "mix_pre_norm_bwd",
                   "ffn1_post_norm_bwd", "ffn1_down_dw", "ffn1_down_dx", "ffn1_swiglu_bwd"))
        elif block == "ffn1_dw_down":
            begin("ffn1_down", ["ffn1_w_down"], (1, 1), [f"ffn1_{k}" for k in ("gate_dw", "up_dw")])
        elif block == "ffn1_dw_gate_up":
            end("mid")
            begin("ffn1_gate_up", ["ffn1_w_gate", "ffn1_w_up"], (8, 9, 3), [f"ffn1_{k}" for k in ("gate_dx", "up_dx", "pre_norm_bwd")])
        elif block == "ffn1":
            end("ffn1_down")
            end("ffn1_gate_up")

    cos, sin = _rope_tables(positions[0])
    loss_part, grad_x = _local_step(x[0], mem[0], cos, sin, loss_target[0], params_of, carry, on_grads)

    small_shapes = [a[n].shape for n in SMALL]
    width = max(s[1] for s in small_shapes)
    gs = _all_reduce_small(_pack_small([g_small[n] for n in SMALL], width), "small_grads_all_reduce")

    out_g, out_d, out_m, out_v = {}, {}, {}, {}
    for n in big:
        flip = (lambda t: jnp.swapaxes(t, 0, 1)) if a[n].shape[2] % 128 else (lambda t: t)
        res = _adamw(*(flip(t) for t in (a[n][0], g_big[n], a["m_" + n][0], a["v_" + n][0])), f"adamw_{n}")
        out_g[n], out_d[n], out_m[n], out_v[n] = (flip(t).reshape(a[n].shape) for t in res)
    sw, sm, sv = (_pack_small([a[p + n] for n in SMALL], width) for p in ("", "m_", "v_"))
    for t, dst in zip(_adamw(sw, gs, sm, sv, "adamw_small"), (out_g, out_d, out_m, out_v)):
        dst.update(zip(SMALL, _unpack_small(t, small_shapes)))

    loss = lax.psum(loss_part, COMM_AXES)
    return (loss, grad_x[None], *[out_g[n] for n in WEIGHTS], *[out_d[n] for n in WEIGHTS],
            *[out_m[n] for n in WEIGHTS], *[out_v[n] for n in WEIGHTS])
```
